```python
import jax
import jax.numpy as jnp
from jax import lax
import numpy as np

D_MODEL = 1024
BATCH = 2
SEQ = 8192
DEPTH = 2
DEC_BATCH = 128
DEC_SEQ = 8
PAST_LEN = 2048
PAGE_SIZE = 128

N_A = DEPTH // 2
N_B = DEPTH - N_A

CONV_WIDTH = 3

WINDOWS = (128, 512, 2048)
DILATIONS = (1, 4, 16)
N_ATTN_GROUPS = 3
HEADS_PER_GROUP = 8
HEAD_DIM = D_MODEL // HEADS_PER_GROUP
N_KV_HEADS = HEADS_PER_GROUP
KV_WIDTH = N_KV_HEADS * HEAD_DIM
Q_WIDTH = N_ATTN_GROUPS * HEADS_PER_GROUP * HEAD_DIM
WINDOW_MAX = max(WINDOWS)
ROPE_THETA = 10000.0

MOE_GROUPS = 4
EXPERTS_PER_GROUP = 4
N_EXPERTS = MOE_GROUPS * EXPERTS_PER_GROUP
EXPERT_TOP_K = 2
D_EXPERT = D_MODEL // 2
EXPERT_BLOCK = 128

RMS_EPS = 1e-6
NEG_BIG = -1e30

kernel_name = 'yoco_shortconv_dilated_swa_hmoe_step'


def rmsnorm(x, g):
    xf = x.astype(jnp.float32)
    y = xf * lax.rsqrt(jnp.mean(xf * xf, axis=-1, keepdims=True) + RMS_EPS)
    return (y * g.astype(jnp.float32)).astype(x.dtype)


def rope(x, pos):
    half = HEAD_DIM // 2
    inv_freq = jnp.power(jnp.float32(ROPE_THETA), -jnp.arange(half, dtype=jnp.float32) / half)
    ang = pos.astype(jnp.float32)[:, None] * inv_freq[None, :]
    shape = (1, pos.shape[0]) + (1,) * (x.ndim - 3) + (half,)
    cos = jnp.cos(ang).reshape(shape)
    sin = jnp.sin(ang).reshape(shape)
    xf = x.astype(jnp.float32)
    x1, x2 = xf[..., :half], xf[..., half:]
    return jnp.concatenate([x1 * cos - x2 * sin, x2 * cos + x1 * sin], axis=-1).astype(x.dtype)


def short_conv_mixer(xn, prev, w_in, conv_k, w_out):
    s = xn.shape[1]
    c_gate, b_gate, hid = jnp.split(xn @ w_in, 3, axis=-1)
    u = c_gate * hid
    ucat = jnp.concatenate([prev.astype(u.dtype), u], axis=1)
    conv = conv_k[0] * ucat[:, 0:s]
    for i in range(1, CONV_WIDTH):
        conv = conv + conv_k[i] * ucat[:, i:i + s]
    y = (b_gate * conv) @ w_out
    return y, ucat[:, s:]


def shared_kv(h, g, w_kv, pos):
    b, s, _ = h.shape
    k, v = jnp.split(rmsnorm(h, g) @ w_kv, 2, axis=-1)
    k = rope(k.reshape(b, s, N_KV_HEADS, HEAD_DIM), pos)
    return k, v.reshape(b, s, N_KV_HEADS, HEAD_DIM)


def dilated_attention_prompt(q, k, v, dilation, window):
    b, s, h, c = q.shape
    span = window // dilation
    unit = dilation * span
    sp = -(-s // unit) * unit
    n_blk = sp // unit

    def to_blocks(t):
        t = jnp.pad(t, ((0, 0), (0, sp - s), (0, 0), (0, 0)))
        t = t.reshape(b, sp // dilation, dilation, h, c).transpose(0, 2, 1, 3, 4)
        return t.reshape(b, dilation, n_blk, span, h, c)

    def with_prev(t):
        prev = jnp.pad(t[:, :, :-1], ((0, 0), (0, 0), (1, 0), (0, 0), (0, 0), (0, 0)))
        return jnp.concatenate([prev, t], axis=3)

    qb = to_blocks(q)
    kc = with_prev(to_blocks(k))
    vc = with_prev(to_blocks(v))
    scores = jnp.einsum('brnqhc,brnkhc->brnhqk', qb, kc).astype(jnp.float32)
    qi = jnp.arange(span)[:, None] + span
    kj = jnp.arange(2 * span)[None, :]
    rel = qi - kj
    band = (rel >= 0) & (rel <= span)
    has_prev = (jnp.arange(n_blk)[:, None, None] > 0) | (kj[None] >= span)
    mask = band[None] & has_prev
    scores = jnp.where(mask[None, None, :, None], scores, NEG_BIG)
    lse = jax.nn.logsumexp(scores, axis=-1)
    p = jnp.exp(scores - lse[..., None])
    out = jnp.einsum('brnhqk,brnkhc->brnqhc', p.astype(v.dtype), vc)
    out = out.reshape(b, dilation, sp // dilation, h, c).transpose(0, 2, 1, 3, 4).reshape(b, sp, h, c)[:, :s]
    lse = lse.transpose(0, 1, 2, 4, 3).reshape(b, dilation, sp // dilation, h)
    lse = lse.transpose(0, 2, 1, 3).reshape(b, sp, h)[:, :s]
    return out, lse


def dilated_attention_sample(q, k_all, v_all, n_past, dilation, window):
    b, n, h, c = q.shape
    span = window // dilation
    idx = n_past + jnp.arange(n)[:, None] - dilation * jnp.arange(span + 1)[None, :]
    valid = idx >= 0
    flat = jnp.maximum(idx, 0).reshape(-1)
    kg = jnp.take(k_all, flat, axis=1).reshape(b, n, span + 1, h, c)
    vg = jnp.take(v_all, flat, axis=1).reshape(b, n, span + 1, h, c)
    scores = jnp.einsum('bnhc,bnjhc->bnhj', q, kg).astype(jnp.float32)
    scores = jnp.where(valid[None, :, None, :], scores, NEG_BIG)
    lse = jax.nn.logsumexp(scores, axis=-1)
    p = jnp.exp(scores - lse[..., None])
    out = jnp.einsum('bnhj,bnjhc->bnhc', p.astype(v_all.dtype), vg)
    return out, lse


def merge_by_denominator(outs, lses):
    w = jax.nn.softmax(jnp.stack(lses, axis=0), axis=0)
    return jnp.einsum('gbsh,gbshc->bshc', w.astype(outs[0].dtype), jnp.stack(outs, axis=0))


def routed_expert_ffn(t, expert, gate, w1, w3, w2):
    n_tok, d = t.shape
    n_exp = w1.shape[0]
    n_assign = expert.size
    flat_e = expert.reshape(-1)
    flat_tok = jnp.arange(n_assign, dtype=jnp.int32) // expert.shape[1]
    flat_gate = gate.reshape(-1)
    order = jnp.argsort(flat_e)
    se, stok, sgate = flat_e[order], flat_tok[order], flat_gate[order]
    counts = jax.ops.segment_sum(jnp.ones_like(flat_e), flat_e, num_segments=n_exp)
    padded = (counts + EXPERT_BLOCK - 1) // EXPERT_BLOCK * EXPERT_BLOCK
    pad_end = jnp.cumsum(padded)
    pad_start = pad_end - padded
    start = jnp.cumsum(counts) - counts
    slot = pad_start[se] + jnp.arange(n_assign, dtype=jnp.int32) - start[se]
    n_blocks = -(-(n_assign + n_exp * (EXPERT_BLOCK - 1)) // EXPERT_BLOCK)
    n_slots = n_blocks * EXPERT_BLOCK
    slot_tok = jnp.full((n_slots,), n_tok, jnp.int32).at[slot].set(stok)
    slot_gate = jnp.zeros((n_slots,), gate.dtype).at[slot].set(sgate)
    block_expert = jnp.minimum(
        jnp.searchsorted(pad_end, jnp.arange(n_blocks, dtype=jnp.int32) * EXPERT_BLOCK, side='right'),
        n_exp - 1)
    t_pad = jnp.concatenate([t, jnp.zeros((1, d), t.dtype)], axis=0)
    xs = t_pad[slot_tok].reshape(n_blocks, EXPERT_BLOCK, d)

    def expert_block(args):
        xb, e = args
        return (jax.nn.silu(xb @ w1[e]) * (xb @ w3[e])) @ w2[e]

    ys = lax.map(expert_block, (xs, block_expert)).reshape(n_slots, d)
    ys = ys * slot_gate[:, None].astype(ys.dtype)
    return jnp.zeros((n_tok + 1, d), t.dtype).at[slot_tok].add(ys)[:n_tok]


def hierarchical_moe(xn, wg, bg, we, be, w1, w3, w2):
    b, s, d = xn.shape
    t = xn.reshape(b * s, d)
    pg = jax.nn.softmax((t @ wg).astype(jnp.float32) + bg.astype(jnp.float32), axis=-1)
    gp, gi = lax.top_k(pg, 1)
    le = ((t @ we).astype(jnp.float32) + be.astype(jnp.float32)).reshape(b * s, MOE_GROUPS, EXPERTS_PER_GROUP)
    le = jnp.take_along_axis(le, gi[:, :, None], axis=1)[:, 0]
    ev, ei = lax.top_k(jax.nn.softmax(le, axis=-1), EXPERT_TOP_K)
    gate = gp * ev / jnp.sum(ev, axis=-1, keepdims=True)
    expert = gi * EXPERTS_PER_GROUP + ei
    return routed_expert_ffn(t, expert, gate, w1, w3, w2).reshape(b, s, d)


def setup_inputs(seed: int = 0) -> dict:
    key = jax.random.key(seed)
    ks = jax.random.split(key, 24)

    def nrm(k, shape, scale):
        return jax.random.normal(k, shape, jnp.float32) * scale

    w_buf = min(WINDOW_MAX, PAST_LEN)
    return {
        'x_prompt': nrm(ks[0], (BATCH, SEQ, D_MODEL), 1.0),
        'x_sample': nrm(ks[1], (DEC_BATCH, DEC_SEQ, D_MODEL), 1.0),
        'cache_k': nrm(ks[2], (DEC_BATCH, w_buf, N_KV_HEADS, HEAD_DIM), 1.0),
        'cache_v': nrm(ks[3], (DEC_BATCH, w_buf, N_KV_HEADS, HEAD_DIM), 1.0),
        'state_conv': nrm(ks[4], (N_A, DEC_BATCH, CONV_WIDTH - 1, D_MODEL), 1.0),
        'norm_mix': 1.0 + nrm(ks[5], (DEPTH, D_MODEL), 0.02),
        'norm_ffn': 1.0 + nrm(ks[6], (DEPTH, D_MODEL), 0.02),
        'norm_kv': 1.0 + nrm(ks[7], (D_MODEL,), 0.02),
        'norm_final': 1.0 + nrm(ks[8], (D_MODEL,), 0.02),
        'conv_w_in': nrm(ks[9], (N_A, D_MODEL, 3 * D_MODEL), D_MODEL ** -0.5),
        'conv_kernel': nrm(ks[10], (N_A, CONV_WIDTH, D_MODEL), CONV_WIDTH ** -0.5),
        'conv_w_out': nrm(ks[11], (N_A, D_MODEL, D_MODEL), D_MODEL ** -0.5),
        'attn_w_q': nrm(ks[12], (N_B, D_MODEL, Q_WIDTH), D_MODEL ** -0.5),
        'attn_w_kv': nrm(ks[13], (D_MODEL, 2 * KV_WIDTH), D_MODEL ** -0.5),
        'attn_w_o': nrm(ks[14], (N_B, KV_WIDTH, D_MODEL), KV_WIDTH ** -0.5),
        'router_group_w': nrm(ks[15], (DEPTH, D_MODEL, MOE_GROUPS), D_MODEL ** -0.5),
        'router_group_b': nrm(ks[16], (DEPTH, MOE_GROUPS), 0.01),
        'router_expert_w': nrm(ks[17], (DEPTH, D_MODEL, N_EXPERTS), D_MODEL ** -0.5),
        'router_expert_b': nrm(ks[18], (DEPTH, N_EXPERTS), 0.01),
        'expert_w1': nrm(ks[19], (DEPTH, N_EXPERTS, D_MODEL, D_EXPERT), D_MODEL ** -0.5),
        'expert_w3': nrm(ks[20], (DEPTH, N_EXPERTS, D_MODEL, D_EXPERT), D_MODEL ** -0.5),
        'expert_w2': nrm(ks[21], (DEPTH, N_EXPERTS, D_EXPERT, D_MODEL), D_EXPERT ** -0.5),
    }


def reference(x_prompt, x_sample, cache_k, cache_v, state_conv, norm_mix, norm_ffn, norm_kv,
              norm_final, conv_w_in, conv_kernel, conv_w_out, attn_w_q, attn_w_kv, attn_w_o,
              router_group_w, router_group_b, router_expert_w, router_expert_b,
              expert_w1, expert_w3, expert_w2):

    def forward(x, pos, conv_prev, attend):
        b, s, _ = x.shape
        h = x
        conv_states = []
        k = v = None
        for layer in range(DEPTH):
            xn = rmsnorm(h, norm_mix[layer])
            if layer < N_A:
                y, st = short_conv_mixer(xn, conv_prev[layer], conv_w_in[layer],
                                         conv_kernel[layer], conv_w_out[layer])
                conv_states.append(st)
            else:
                if layer == N_A:
                    k, v = shared_kv(h, norm_kv, attn_w_kv, pos)
                j = layer - N_A
                q = (xn @ attn_w_q[j]).reshape(b, s, N_ATTN_GROUPS, HEADS_PER_GROUP, HEAD_DIM)
                q = rope(q, pos) * (HEAD_DIM ** -0.5)
                o = attend(q, k, v)
                y = o.reshape(b, s, KV_WIDTH) @ attn_w_o[j]
            h = h + y
            h = h + hierarchical_moe(rmsnorm(h, norm_ffn[layer]), router_group_w[layer],
                                     router_group_b[layer], router_expert_w[layer],
                                     router_expert_b[layer], expert_w1[layer],
                                     expert_w3[layer], expert_w2[layer])
        return rmsnorm(h, norm_final), jnp.stack(conv_states, axis=0), k, v

    def attend_prompt(q, k, v):
        outs, lses = [], []
        for g in range(N_ATTN_GROUPS):
            o, l = dilated_attention_prompt(q[:, :, g], k, v, DILATIONS[g], WINDOWS[g])
            outs.append(o)
            lses.append(l)
        return merge_by_denominator(outs, lses)

    def attend_sample(q, k, v):
        n_past = cache_k.shape[1]
        k_all = jnp.concatenate([cache_k.astype(k.dtype), k], axis=1)
        v_all = jnp.concatenate([cache_v.astype(v.dtype), v], axis=1)
        outs, lses = [], []
        for g in range(N_ATTN_GROUPS):
            o, l = dilated_attention_sample(q[:, :, g], k_all, v_all, n_past, DILATIONS[g], WINDOWS[g])
            outs.append(o)
            lses.append(l)
        return merge_by_denominator(outs, lses)

    b_p, s_p, d_p = x_prompt.shape
    pos_p = jnp.arange(s_p, dtype=jnp.int32)
    conv_zero = jnp.zeros((N_A, b_p, CONV_WIDTH - 1, d_p), x_prompt.dtype)
    y_prompt, conv_p, k_p, v_p = forward(x_prompt, pos_p, conv_zero, attend_prompt)
    keep = min(WINDOW_MAX, s_p)
    k_p = k_p[:, s_p - keep:]
    v_p = v_p[:, s_p - keep:]

    pos_s = PAST_LEN + jnp.arange(x_sample.shape[1], dtype=jnp.int32)
    y_sample, conv_s, k_s, v_s = forward(x_sample, pos_s, state_conv, attend_sample)
    return (y_prompt, y_sample, k_p, v_p, conv_p, k_s, v_s, conv_s)
```

```python
import functools

import numpy as np
import jax
import jax.numpy as jnp
from jax import lax
from jax.experimental import pallas as pl
from jax.experimental.pallas import tpu as pltpu

D_MODEL = 1024
CONV_WIDTH = 3
WINDOWS = (128, 512, 2048)
DILATIONS = (1, 4, 16)
N_GROUPS = 3
N_HEADS = 8
HEAD_DIM = 128
SPAN = 128
ROPE_THETA = 10000.0
MOE_GROUPS = 4
EXPERTS_PER_GROUP = 4
N_EXPERTS = 16
D_EXPERT = 512
EXPERT_BLOCK = 128
RMS_EPS = 1e-6
NEG_BIG = -1e30
PAST_LEN = 2048

ROUTER_ROWS = 32
EXPERT_ROW0 = 8
VMEM_LIMIT = 56 * 1024 * 1024

_F32 = jnp.float32
_BF16 = jnp.bfloat16


def _rms(x, g):
    ms = jnp.mean(x * x, axis=-1, keepdims=True)
    return (x * lax.rsqrt(ms + RMS_EPS)) * g


def _dot(a, b):
    return jnp.dot(a, b, preferred_element_type=_F32)


def _dot_nt(a, b, precision=None):
    return lax.dot_general(a, b, (((1,), (1,)), ((), ())), precision=precision,
                           preferred_element_type=_F32)


def _route(xn, wr_ref, br_ref, eid_ref, gate_ref):
    logits = _dot_nt(wr_ref[...], xn, precision=lax.Precision.HIGHEST) + br_ref[...]
    lg = logits[0:MOE_GROUPS]
    row = lax.broadcasted_iota(jnp.int32, lg.shape, 0).astype(_F32)
    eg = jnp.exp(lg - jnp.max(lg, axis=0, keepdims=True))
    pg = eg / jnp.sum(eg, axis=0, keepdims=True)
    gp = jnp.max(pg, axis=0, keepdims=True)
    gi = jnp.min(jnp.where(pg == gp, row, float(MOE_GROUPS)), axis=0, keepdims=True)
    le = jnp.zeros_like(lg)
    for g in range(MOE_GROUPS):
        r0 = EXPERT_ROW0 + g * EXPERTS_PER_GROUP
        le = le + jnp.where(gi == float(g), logits[r0:r0 + EXPERTS_PER_GROUP], 0.0)
    ee = jnp.exp(le - jnp.max(le, axis=0, keepdims=True))
    ev = ee / jnp.sum(ee, axis=0, keepdims=True)
    v1 = jnp.max(ev, axis=0, keepdims=True)
    i1 = jnp.min(jnp.where(ev == v1, row, float(EXPERTS_PER_GROUP)), axis=0, keepdims=True)
    ev2 = jnp.where(row == i1, -1.0, ev)
    v2 = jnp.max(ev2, axis=0, keepdims=True)
    i2 = jnp.min(jnp.where(ev2 == v2, row, float(EXPERTS_PER_GROUP)), axis=0, keepdims=True)
    den = v1 + v2
    ids = jnp.concatenate([gi * EXPERTS_PER_GROUP + i1, gi * EXPERTS_PER_GROUP + i2], axis=0)
    eid_ref[...] = ids.astype(jnp.int32)
    gate_ref[...] = jnp.concatenate([gp * v1 / den, gp * v2 / den], axis=0)


def _conv_layer_kernel(x_ref, prev_ref, gmix_ref, win_ref, ck_ref, wout_ref, gffn_ref, wr_ref, br_ref,
                       h_ref, xn2_ref, eid_ref, gate_ref, st_ref, ubuf, *, shift, tm):
    i = pl.program_id(1)
    halo = ubuf.shape[0] - tm
    keep = (CONV_WIDTH - 1) * shift

    @pl.when(i == 0)
    def _():
        ubuf[halo - keep:halo, :] = prev_ref[0]

    @pl.when(i > 0)
    def _():
        ubuf[halo - keep:halo, :] = ubuf[halo + tm - keep:halo + tm, :]

    x = x_ref[0]
    xn = _rms(x, gmix_ref[...]).astype(_BF16)
    c_gate = _dot(xn, win_ref[:, 0:D_MODEL])
    hid = _dot(xn, win_ref[:, 2 * D_MODEL:3 * D_MODEL])
    u = c_gate * hid
    ubuf[halo:halo + tm, :] = u
    u1 = ubuf[halo - shift:halo - shift + tm, :]
    u2 = ubuf[halo - 2 * shift:halo - 2 * shift + tm, :]
    conv = ck_ref[0:1, :] * u2 + ck_ref[1:2, :] * u1 + ck_ref[2:3, :] * u
    b_gate = _dot(xn, win_ref[:, D_MODEL:2 * D_MODEL])
    y = _dot((b_gate * conv).astype(_BF16), wout_ref[...])
    h = x + y
    h_ref[0] = h
    st_ref[0] = ubuf[halo + tm - keep:halo + tm, :]
    xn2 = _rms(h, gffn_ref[...])
    xn2_ref[...] = xn2
    _route(xn2, wr_ref, br_ref, eid_ref, gate_ref)


def _conv_layer(x3, prev3, gmix, win, ck, wout, gffn, wr, br, *, shift, tm):
    nb, s, d = x3.shape
    nt = s // tm
    keep = (CONV_WIDTH - 1) * shift
    halo = -(-keep // 8) * 8
    t_all = nb * s
    const = lambda b, i: (0, 0)
    return pl.pallas_call(
        functools.partial(_conv_layer_kernel, shift=shift, tm=tm),
        grid=(nb, nt),
        in_specs=[
            pl.BlockSpec((1, tm, d), lambda b, i: (b, i, 0)),
            pl.BlockSpec((1, keep, d), lambda b, i: (b, 0, 0)),
            pl.BlockSpec((1, d), const),
            pl.BlockSpec((d, 3 * d), const),
            pl.BlockSpec((CONV_WIDTH, d), const),
            pl.BlockSpec((d, d), const),
            pl.BlockSpec((1, d), const),
            pl.BlockSpec((ROUTER_ROWS, d), const),
            pl.BlockSpec((ROUTER_ROWS, 1), const),
        ],
        out_specs=[
            pl.BlockSpec((1, tm, d), lambda b, i: (b, i, 0)),
            pl.BlockSpec((tm, d), lambda b, i: (b * nt + i, 0)),
            pl.BlockSpec((2, tm), lambda b, i: (0, b * nt + i)),
            pl.BlockSpec((2, tm), lambda b, i: (0, b * nt + i)),
            pl.BlockSpec((1, keep, d), lambda b, i: (b, 0, 0)),
        ],
        out_shape=[
            jax.ShapeDtypeStruct((nb, s, d), _F32),
            jax.ShapeDtypeStruct((t_all, d), _F32),
            jax.ShapeDtypeStruct((2, t_all), jnp.int32),
            jax.ShapeDtypeStruct((2, t_all), _F32),
            jax.ShapeDtypeStruct((nb, keep, d), _F32),
        ],
        scratch_shapes=[pltpu.VMEM((halo + tm, d), _F32)],
        compiler_params=pltpu.CompilerParams(
            dimension_semantics=("arbitrary", "arbitrary"), vmem_limit_bytes=VMEM_LIMIT),
        name="conv_layer",
    )(x3, prev3, gmix, win, ck, wout, gffn, wr, br)


def _moe_ffn_kernel(be_ref, nv_ref, tab_ref, x_hbm, gate_ref, w1_ref, w3_ref, w2_ref, y_hbm,
                    xbuf, obuf, gsem, ssem, *, n_tok, n_blocks):
    j = pl.program_id(0)
    slot = j % 2

    def gather_rows(blk, s):
        def body(k, carry):
            v = tab_ref[blk * EXPERT_BLOCK + k]
            tok = v - jnp.where(v >= n_tok, n_tok, 0)
            pltpu.make_async_copy(x_hbm.at[pl.ds(tok, 1)], xbuf.at[s, pl.ds(k, 1)], gsem.at[s]).start()
            return carry
        lax.fori_loop(0, EXPERT_BLOCK, body, 0, unroll=8)

    def scatter_rows(blk, s):
        n = nv_ref[blk]

        def body(k, carry):
            @pl.when(k < n)
            def _():
                v = tab_ref[blk * EXPERT_BLOCK + k]
                pltpu.make_async_copy(obuf.at[s, pl.ds(k, 1)], y_hbm.at[pl.ds(v, 1)], ssem.at[s]).start()
            return carry
        lax.fori_loop(0, EXPERT_BLOCK, body, 0, unroll=8)

    def wait_gather(s):
        pltpu.make_async_copy(x_hbm.at[pl.ds(0, EXPERT_BLOCK)], xbuf.at[s], gsem.at[s]).wait()

    def wait_scatter(blk, s):
        n = nv_ref[blk]

        n8 = pl.multiple_of((n >> 3) << 3, 8)

        @pl.when(n8 > 0)
        def _():
            pltpu.make_async_copy(obuf.at[s, pl.ds(0, n8)], y_hbm.at[pl.ds(0, n8)], ssem.at[s]).wait()

        for i in range(7):
            @pl.when((n & 7) > i)
            def _():
                pltpu.make_async_copy(obuf.at[s, pl.ds(0, 1)], y_hbm.at[pl.ds(0, 1)], ssem.at[s]).wait()

    @pl.when(j == 0)
    def _():
        gather_rows(0, 0)

    @pl.when(j + 1 < n_blocks)
    def _():
        gather_rows(j + 1, 1 - slot)

    wait_gather(slot)
    x = xbuf[slot].astype(_BF16)
    a = _dot(x, w1_ref[0])
    b = _dot(x, w3_ref[0])
    hmid = (jax.nn.silu(a) * b).astype(_BF16)
    y = _dot(hmid, w2_ref[0]) * gate_ref[...]

    @pl.when(j >= 2)
    def _():
        wait_scatter(j - 2, slot)

    obuf[slot] = y
    scatter_rows(j, slot)

    @pl.when(j == n_blocks - 1)
    def _():
        wait_scatter(j - 1, 1 - slot)
        wait_scatter(j, slot)


def _moe_ffn(block_expert, n_valid, table, gates, xn, w1, w3, w2):
    n_tok, d = xn.shape
    n_blocks = block_expert.shape[0]
    assert n_blocks >= 2
    wmap = lambda j, be, nv, tab: (be[j], 0, 0)
    return pl.pallas_call(
        functools.partial(_moe_ffn_kernel, n_tok=n_tok, n_blocks=n_blocks),
        grid_spec=pltpu.PrefetchScalarGridSpec(
            num_scalar_prefetch=3,
            grid=(n_blocks,),
            in_specs=[
                pl.BlockSpec(memory_space=pl.ANY),
                pl.BlockSpec((EXPERT_BLOCK, 1), lambda j, be, nv, tab: (j, 0)),
                pl.BlockSpec((1, d, D_EXPERT), wmap),
                pl.BlockSpec((1, d, D_EXPERT), wmap),
                pl.BlockSpec((1, D_EXPERT, d), wmap),
            ],
            out_specs=pl.BlockSpec(memory_space=pl.ANY),
            scratch_shapes=[
                pltpu.VMEM((2, EXPERT_BLOCK, d), _F32),
                pltpu.VMEM((2, EXPERT_BLOCK, d), _F32),
                pltpu.SemaphoreType.DMA((2,)),
                pltpu.SemaphoreType.DMA((2,)),
            ],
        ),
        out_shape=jax.ShapeDtypeStruct((2 * n_tok, d), _F32),
        compiler_params=pltpu.CompilerParams(
            dimension_semantics=("arbitrary",), vmem_limit_bytes=VMEM_LIMIT),
        name="moe_ffn",
    )(block_expert, n_valid, table, xn, gates, w1, w3, w2)


def _moe_plan(eid, gate, n_tok):
    n_assign = 2 * n_tok
    flat_e = eid.T.reshape(-1)
    flat_g = gate.T.reshape(-1)
    onehot = (flat_e[:, None] == jnp.arange(N_EXPERTS, dtype=jnp.int32)[None, :]).astype(jnp.int32)
    csum = jnp.cumsum(onehot, axis=0)
    counts = csum[-1]
    rank = jnp.sum(csum * onehot, axis=1) - 1
    padded = (counts + EXPERT_BLOCK - 1) // EXPERT_BLOCK * EXPERT_BLOCK
    pad_end = jnp.cumsum(padded)
    pad_start = pad_end - padded
    slot = pad_start[flat_e] + rank
    n_blocks = -(-(n_assign + N_EXPERTS * (EXPERT_BLOCK - 1)) // EXPERT_BLOCK)
    n_slots = n_blocks * EXPERT_BLOCK
    a = jnp.arange(n_assign, dtype=jnp.int32)
    value = (a % 2) * n_tok + a // 2
    table = jnp.zeros((n_slots,), jnp.int32).at[slot].set(value)
    slot_gate = jnp.zeros((n_slots,), _F32).at[slot].set(flat_g)
    block_start = jnp.arange(n_blocks, dtype=jnp.int32) * EXPERT_BLOCK
    block_expert = jnp.minimum(jnp.searchsorted(pad_end, block_start, side='right'),
                               N_EXPERTS - 1).astype(jnp.int32)
    n_valid = jnp.clip((pad_start + counts)[block_expert] - block_start, 0, EXPERT_BLOCK).astype(jnp.int32)
    return block_expert, n_valid, table, slot_gate.reshape(n_slots, 1)


def _rope_heads(x, cos, sin_signed, out_ref, scale):
    for h in range(N_HEADS):
        hs = slice(h * HEAD_DIM, (h + 1) * HEAD_DIM)
        xh = x[:, hs]
        r = xh * cos + pltpu.roll(xh, HEAD_DIM // 2, axis=1) * sin_signed
        if scale is not None:
            r = r * scale
        out_ref[:, hs] = r.astype(out_ref.dtype)


def _qkv_kernel(h_ref, y0_ref, y1_ref, gmix_ref, gkv_ref, wq_ref, wkv_ref, cos_ref, sin_ref,
                h2_ref, q0_ref, q1_ref, q2_ref, kb_ref, vb_ref, kf_ref, vf_ref, ktmp):
    h2 = h_ref[...] + (y0_ref[...] + y1_ref[...])
    h2_ref[...] = h2
    cos = cos_ref[...]
    sin = sin_ref[...]
    xn = _rms(h2, gmix_ref[...]).astype(_BF16)
    scale = HEAD_DIM ** -0.5
    for g, q_ref in enumerate((q0_ref, q1_ref, q2_ref)):
        q = _dot(xn, wq_ref[:, g * D_MODEL:(g + 1) * D_MODEL])
        _rope_heads(q, cos, sin, q_ref, scale)
    xkv = _rms(h2, gkv_ref[...]).astype(_BF16)
    k = _dot(xkv, wkv_ref[:, 0:D_MODEL])
    _rope_heads(k, cos, sin, ktmp, None)
    kr = ktmp[...]
    kf_ref[...] = kr
    kb_ref[...] = kr.astype(_BF16)
    v = _dot(xkv, wkv_ref[:, D_MODEL:2 * D_MODEL])
    vf_ref[...] = v
    vb_ref[...] = v.astype(_BF16)


def _qkv(h, ypair, gmix, gkv, wq, wkv, cos, sin, *, tm):
    t, d = h.shape
    nt = t // tm
    row = lambda i: (i, 0)
    const = lambda i: (0, 0)
    return pl.pallas_call(
        _qkv_kernel,
        grid=(nt,),
        in_specs=[
            pl.BlockSpec((tm, d), row),
            pl.BlockSpec((tm, d), row),
            pl.BlockSpec((tm, d), lambda i: (nt + i, 0)),
            pl.BlockSpec((1, d), const),
            pl.BlockSpec((1, d), const),
            pl.BlockSpec((d, 3 * d), const),
            pl.BlockSpec((d, 2 * d), const),
            pl.BlockSpec((tm, HEAD_DIM), row),
            pl.BlockSpec((tm, HEAD_DIM), row),
        ],
        out_specs=[pl.BlockSpec((tm, d), row)] * 8,
        out_shape=[jax.ShapeDtypeStruct((t, d), _F32)] + [jax.ShapeDtypeStruct((t, d), _BF16)] * 5
                  + [jax.ShapeDtypeStruct((t, d), _F32)] * 2,
        scratch_shapes=[pltpu.VMEM((tm, d), _F32)],
        compiler_params=pltpu.CompilerParams(
            dimension_semantics=("arbitrary",), vmem_limit_bytes=VMEM_LIMIT),
        name="qkv_proj",
    )(h, ypair, ypair, gmix, gkv, wq, wkv, cos, sin)


def _attn_prompt_kernel(q_ref, kp_ref, kc_ref, vp_ref, vc_ref, o_ref, st_ref):
    i = pl.program_id(1)
    qi = lax.broadcasted_iota(jnp.int32, (SPAN, SPAN), 0)
    kj = lax.broadcasted_iota(jnp.int32, (SPAN, SPAN), 1)
    mask_p = kj >= qi + jnp.where(i > 0, 0, SPAN)
    mask_c = kj <= qi
    lane = lax.broadcasted_iota(jnp.int32, (SPAN, HEAD_DIM), 1)
    st = jnp.zeros((SPAN, HEAD_DIM), _F32)
    for h in range(N_HEADS):
        hs = slice(h * HEAD_DIM, (h + 1) * HEAD_DIM)
        qh = q_ref[0, :, hs]
        sp = jnp.where(mask_p, _dot_nt(qh, kp_ref[0, :, hs]), NEG_BIG)
        sc = jnp.where(mask_c, _dot_nt(qh, kc_ref[0, :, hs]), NEG_BIG)
        m = jnp.maximum(jnp.max(sp, axis=1, keepdims=True), jnp.max(sc, axis=1, keepdims=True))
        pp = jnp.exp(sp - m)
        pc = jnp.exp(sc - m)
        l = jnp.sum(pp, axis=1, keepdims=True) + jnp.sum(pc, axis=1, keepdims=True)
        o = _dot(pp.astype(_BF16), vp_ref[0, :, hs]) + _dot(pc.astype(_BF16), vc_ref[0, :, hs])
        o_ref[0, :, hs] = o / l
        st = jnp.where(lane == h, m + jnp.log(l), st)
    st_ref[0] = st


def _attn_prompt(q, k, v, dilation):
    b, s, d = q.shape
    su = s // dilation
    nblk = su // SPAN
    qv = q.reshape(b, su, dilation * d)
    kv = k.reshape(b, su, dilation * d)
    vv = v.reshape(b, su, dilation * d)
    cur = lambda bb, i, r: (bb, i, r)
    prev = lambda bb, i, r: (bb, jnp.maximum(i - 1, 0), r)
    blk = (1, SPAN, d)
    o, st = pl.pallas_call(
        _attn_prompt_kernel,
        grid=(b, nblk, dilation),
        in_specs=[pl.BlockSpec(blk, cur), pl.BlockSpec(blk, prev), pl.BlockSpec(blk, cur),
                  pl.BlockSpec(blk, prev), pl.BlockSpec(blk, cur)],
        out_specs=[pl.BlockSpec(blk, cur), pl.BlockSpec((1, SPAN, HEAD_DIM), cur)],
        out_shape=[jax.ShapeDtypeStruct((b, su, dilation * d), _F32),
                   jax.ShapeDtypeStruct((b, su, dilation * HEAD_DIM), _F32)],
        compiler_params=pltpu.CompilerParams(
            dimension_semantics=("arbitrary", "arbitrary", "arbitrary"), vmem_limit_bytes=VMEM_LIMIT),
        name=f"attn_prompt_d{dilation}",
    )(qv, kv, kv, vv, vv)
    return o.reshape(b * s, d), st.reshape(b * s, HEAD_DIM)


N_NEW = 8
CACHE_A_GROUPS = (PAST_LEN - WINDOWS[1]) // DILATIONS[2]
CACHE_A_ROWS = CACHE_A_GROUPS * N_NEW
CACHE_B_ROWS = WINDOWS[1]
KEYS_REAL = CACHE_A_ROWS + CACHE_B_ROWS + N_NEW
KEYS_PAD = -(-KEYS_REAL // 128) * 128


def _sample_key_positions():
    pos = np.full((KEYS_PAD,), -1, np.int64)
    a = np.arange(CACHE_A_ROWS)
    pos[:CACHE_A_ROWS] = (a // N_NEW) * DILATIONS[2] + a % N_NEW
    pos[CACHE_A_ROWS:CACHE_A_ROWS + CACHE_B_ROWS] = PAST_LEN - CACHE_B_ROWS + np.arange(CACHE_B_ROWS)
    pos[CACHE_A_ROWS + CACHE_B_ROWS:KEYS_REAL] = PAST_LEN + np.arange(N_NEW)
    return pos


def _sample_bias():
    pos = _sample_key_positions()
    bias = np.full((N_GROUPS * N_NEW, KEYS_PAD), NEG_BIG, np.float32)
    for g in range(N_GROUPS):
        for n in range(N_NEW):
            delta = PAST_LEN + n - pos
            ok = (pos >= 0) & (delta >= 0) & (delta <= WINDOWS[g]) & (delta % DILATIONS[g] == 0)
            assert int(ok.sum()) == SPAN + 1
            bias[g * N_NEW + n, ok] = 0.0
    return bias


def _attn_sample_kernel(q0_ref, q1_ref, q2_ref, ka_ref, kb_ref, va_ref, vb_ref, kn_ref, vn_ref, bias_ref,
                        o_ref, kall, vall):
    @pl.when(pl.program_id(0) == 0)
    def _():
        kall[KEYS_REAL:KEYS_PAD, :] = jnp.zeros((KEYS_PAD - KEYS_REAL, D_MODEL), _BF16)
        vall[KEYS_REAL:KEYS_PAD, :] = jnp.zeros((KEYS_PAD - KEYS_REAL, D_MODEL), _BF16)

    nb0 = CACHE_A_ROWS + CACHE_B_ROWS
    for src_a, src_b, src_n, dst in ((ka_ref, kb_ref, kn_ref, kall), (va_ref, vb_ref, vn_ref, vall)):
        dst[0:CACHE_A_ROWS, :] = src_a[0].reshape(CACHE_A_ROWS, D_MODEL).astype(_BF16)
        dst[CACHE_A_ROWS:nb0, :] = src_b[0].reshape(CACHE_B_ROWS, D_MODEL).astype(_BF16)
        dst[nb0:KEYS_REAL, :] = src_n[0].astype(_BF16)

    bias = bias_ref[...]
    for h in range(N_HEADS):
        hs = slice(h * HEAD_DIM, (h + 1) * HEAD_DIM)
        qh = jnp.concatenate([q0_ref[0, :, hs].astype(_F32), q1_ref[0, :, hs].astype(_F32),
                              q2_ref[0, :, hs].astype(_F32)], axis=0).astype(_BF16)
        s = _dot_nt(qh, kall[:, hs]) + bias
        m = jnp.max(s, axis=1, keepdims=True)
        m8 = jnp.maximum(jnp.maximum(m[0:N_NEW], m[N_NEW:2 * N_NEW]), m[2 * N_NEW:3 * N_NEW])
        p = jnp.exp(s - jnp.concatenate([m8, m8, m8], axis=0))
        p8 = p[0:N_NEW] + p[N_NEW:2 * N_NEW] + p[2 * N_NEW:3 * N_NEW]
        l8 = jnp.sum(p8, axis=1, keepdims=True)
        o_ref[0, :, hs] = _dot(p8.astype(_BF16), vall[:, hs]) / l8


def _attn_sample(q0, q1, q2, cache_k, cache_v, k_new, v_new):
    nb, n_new, d = q0.shape
    past = cache_k.shape[1]
    assert past == PAST_LEN and n_new == N_NEW and d == D_MODEL
    ck = cache_k.reshape(nb, past // DILATIONS[2], DILATIONS[2], d)
    cv = cache_v.reshape(nb, past // DILATIONS[2], DILATIONS[2], d)
    bias = jnp.asarray(_sample_bias())
    seq = lambda b: (b, 0, 0)
    part_a = pl.BlockSpec((1, CACHE_A_GROUPS, N_NEW, d), lambda b: (b, 0, 0, 0))
    b_groups = CACHE_B_ROWS // DILATIONS[2]
    part_b = pl.BlockSpec((1, b_groups, DILATIONS[2], d), lambda b: (b, CACHE_A_GROUPS // b_groups, 0, 0))
    new = pl.BlockSpec((1, n_new, d), seq)
    return pl.pallas_call(
        _attn_sample_kernel,
        grid=(nb,),
        in_specs=[new, new, new, part_a, part_b, part_a, part_b, new, new,
                  pl.BlockSpec((N_GROUPS * N_NEW, KEYS_PAD), lambda b: (0, 0))],
        out_specs=new,
        out_shape=jax.ShapeDtypeStruct((nb, n_new, d), _F32),
        scratch_shapes=[pltpu.VMEM((KEYS_PAD, d), _BF16), pltpu.VMEM((KEYS_PAD, d), _BF16)],
        compiler_params=pltpu.CompilerParams(
            dimension_semantics=("arbitrary",), vmem_limit_bytes=VMEM_LIMIT),
        name="attn_sample",
    )(q0, q1, q2, ck, ck, cv, cv, k_new, v_new, bias)


def _attn_out_kernel(*refs, n_groups):
    o_refs = refs[:n_groups]
    st_refs = refs[n_groups:2 * n_groups] if n_groups > 1 else ()
    rest = refs[2 * n_groups:] if n_groups > 1 else refs[n_groups:]
    h_ref, wo_ref, gffn_ref, wr_ref, br_ref, h3_ref, xn2_ref, eid_ref, gate_ref, obuf = rest
    if n_groups == 1:
        o = o_refs[0][...].astype(_BF16)
    else:
        sts = [r[...] for r in st_refs]
        mx = sts[0]
        for s in sts[1:]:
            mx = jnp.maximum(mx, s)
        es = [jnp.exp(s - mx) for s in sts]
        den = es[0]
        for e in es[1:]:
            den = den + e
        ws = [e / den for e in es]
        for h in range(N_HEADS):
            hs = slice(h * HEAD_DIM, (h + 1) * HEAD_DIM)
            acc = ws[0][:, h:h + 1] * o_refs[0][:, hs]
            for g in range(1, n_groups):
                acc = acc + ws[g][:, h:h + 1] * o_refs[g][:, hs]
            obuf[:, hs] = acc.astype(_BF16)
        o = obuf[...]
    h3 = h_ref[...] + _dot(o, wo_ref[...])
    h3_ref[...] = h3
    xn2 = _rms(h3, gffn_ref[...])
    xn2_ref[...] = xn2
    _route(xn2, wr_ref, br_ref, eid_ref, gate_ref)


def _attn_out(os, sts, h, wo, gffn, wr, br, *, tm):
    t, d = h.shape
    n_groups = len(os)
    row = lambda i: (i, 0)
    const = lambda i: (0, 0)
    in_specs = [pl.BlockSpec((tm, d), row)] * n_groups
    if n_groups > 1:
        in_specs += [pl.BlockSpec((tm, HEAD_DIM), row)] * n_groups
    in_specs += [pl.BlockSpec((tm, d), row), pl.BlockSpec((d, d), const), pl.BlockSpec((1, d), const),
                 pl.BlockSpec((ROUTER_ROWS, d), const), pl.BlockSpec((ROUTER_ROWS, 1), const)]
    args = list(os) + (list(sts) if n_groups > 1 else []) + [h, wo, gffn, wr, br]
    return pl.pallas_call(
        functools.partial(_attn_out_kernel, n_groups=n_groups),
        grid=(t // tm,),
        in_specs=in_specs,
        out_specs=[pl.BlockSpec((tm, d), row), pl.BlockSpec((tm, d), row),
                   pl.BlockSpec((2, tm), lambda i: (0, i)), pl.BlockSpec((2, tm), lambda i: (0, i))],
        out_shape=[jax.ShapeDtypeStruct((t, d), _F32), jax.ShapeDtypeStruct((t, d), _F32),
                   jax.ShapeDtypeStruct((2, t), jnp.int32), jax.ShapeDtypeStruct((2, t), _F32)],
        scratch_shapes=[pltpu.VMEM((tm, d), _BF16)],
        compiler_params=pltpu.CompilerParams(
            dimension_semantics=("arbitrary",), vmem_limit_bytes=VMEM_LIMIT),
        name=f"attn_out_g{n_groups}",
    )(*args)


def _final_kernel(h_ref, y0_ref, y1_ref, g_ref, out_ref):
    out_ref[...] = _rms(h_ref[...] + (y0_ref[...] + y1_ref[...]), g_ref[...])


def _final(h, ypair, g, *, tm):
    t, d = h.shape
    nt = t // tm
    row = lambda i: (i, 0)
    return pl.pallas_call(
        _final_kernel,
        grid=(nt,),
        in_specs=[pl.BlockSpec((tm, d), row), pl.BlockSpec((tm, d), row),
                  pl.BlockSpec((tm, d), lambda i: (nt + i, 0)), pl.BlockSpec((1, d), lambda i: (0, 0))],
        out_specs=pl.BlockSpec((tm, d), row),
        out_shape=jax.ShapeDtypeStruct((t, d), _F32),
        compiler_params=pltpu.CompilerParams(
            dimension_semantics=("arbitrary",), vmem_limit_bytes=VMEM_LIMIT),
        name="final_norm",
    )(h, ypair, ypair, g)


def _rope_tables(pos):
    half = HEAD_DIM // 2
    inv_freq = jnp.power(jnp.float32(ROPE_THETA), -jnp.arange(half, dtype=jnp.float32) / half)
    ang = pos.astype(jnp.float32)[:, None] * inv_freq[None, :]
    cos = jnp.cos(ang)
    sin = jnp.sin(ang)
    return jnp.concatenate([cos, cos], axis=-1), jnp.concatenate([-sin, sin], axis=-1)


def _router_params(wg, bg, we, be):
    wr = jnp.zeros((ROUTER_ROWS, D_MODEL), _F32)
    wr = wr.at[0:MOE_GROUPS].set(wg.T).at[EXPERT_ROW0:EXPERT_ROW0 + N_EXPERTS].set(we.T)
    br = jnp.zeros((ROUTER_ROWS, 1), _F32)
    br = br.at[0:MOE_GROUPS, 0].set(bg).at[EXPERT_ROW0:EXPERT_ROW0 + N_EXPERTS, 0].set(be)
    return wr, br


def _moe(xn2, eid, gate, w1, w3, w2):
    block_expert, n_valid, table, gates = _moe_plan(eid, gate, xn2.shape[0])
    return _moe_ffn(block_expert, n_valid, table, gates, xn2, w1, w3, w2)


def _forward(x3, prev3, pos_rows, attend, p, *, shift, tm):
    h1, xn2, eid, gate, state = _conv_layer(
        x3, prev3, p['gmix'][0], p['win'], p['ck'], p['wout'], p['gffn'][0], *p['router'][0],
        shift=shift, tm=tm)
    t = x3.shape[0] * x3.shape[1]
    ypair = _moe(xn2, eid, gate, *p['experts'][0])
    cos, sin = _rope_tables(pos_rows)
    h2, q0, q1, q2, kb, vb, kf, vf = _qkv(h1.reshape(t, D_MODEL), ypair, p['gmix'][1], p['gkv'],
                                           p['wq'], p['wkv'], cos, sin, tm=256)
    os, sts = attend((q0, q1, q2), kb, vb, kf, vf)
    h3, xn2, eid, gate = _attn_out(os, sts, h2, p['wo'], p['gffn'][1], *p['router'][1], tm=256)
    ypair = _moe(xn2, eid, gate, *p['experts'][1])
    y = _final(h3, ypair, p['gfinal'], tm=512)
    return y, state, kf, vf


def kernel(x_prompt, x_sample, cache_k, cache_v, state_conv, norm_mix, norm_ffn, norm_kv, norm_final,
           conv_w_in, conv_kernel, conv_w_out, attn_w_q, attn_w_kv, attn_w_o, router_group_w,
           router_group_b, router_expert_w, router_expert_b, expert_w1, expert_w3, expert_w2):
    b_p, s_p, d = x_prompt.shape
    b_s, n_new, _ = x_sample.shape
    assert d == D_MODEL and n_new == N_NEW and s_p % (DILATIONS[2] * SPAN) == 0
    assert norm_mix.shape[0] == 2 and conv_w_in.shape[0] == 1 and attn_w_q.shape[0] == 1
    assert cache_k.shape[1] == PAST_LEN

    p = {
        'gmix': [norm_mix[l].reshape(1, d) for l in range(2)],
        'gffn': [norm_ffn[l].reshape(1, d) for l in range(2)],
        'gkv': norm_kv.reshape(1, d),
        'gfinal': norm_final.reshape(1, d),
        'win': conv_w_in[0].astype(_BF16),
        'ck': conv_kernel[0],
        'wout': conv_w_out[0].astype(_BF16),
        'wq': attn_w_q[0].astype(_BF16),
        'wkv': attn_w_kv.astype(_BF16),
        'wo': attn_w_o[0].astype(_BF16),
        'router': [_router_params(router_group_w[l], router_group_b[l], router_expert_w[l], router_expert_b[l])
                   for l in range(2)],
        'experts': [(expert_w1[l].astype(_BF16), expert_w3[l].astype(_BF16), expert_w2[l].astype(_BF16))
                    for l in range(2)],
    }

    def attend_prompt(qs, kb, vb, kf, vf):
        os, sts = [], []
        for g in range(N_GROUPS):
            o, st = _attn_prompt(qs[g].reshape(b_p, s_p, d), kb.reshape(b_p, s_p, d),
                                 vb.reshape(b_p, s_p, d), DILATIONS[g])
            os.append(o)
            sts.append(st)
        return os, sts

    pos_p = jnp.tile(jnp.arange(s_p, dtype=jnp.int32), b_p)
    zero_state = jnp.zeros((b_p, CONV_WIDTH - 1, d), x_prompt.dtype)
    y_p, st_p, kf_p, vf_p = _forward(x_prompt, zero_state, pos_p, attend_prompt, p, shift=1, tm=512)
    keep = min(max(WINDOWS), s_p)
    y_prompt = y_p.reshape(b_p, s_p, d)
    k_p = kf_p.reshape(b_p, s_p, N_HEADS, HEAD_DIM)[:, s_p - keep:]
    v_p = vf_p.reshape(b_p, s_p, N_HEADS, HEAD_DIM)[:, s_p - keep:]
    conv_p = st_p[None]

    halves = 2
    bh = b_s // halves

    def to_rows(a):
        w = a.shape[-1]
        return a.reshape(halves, bh, n_new, w).transpose(0, 2, 1, 3).reshape(halves * n_new * bh, w)

    def to_batch(a):
        w = a.shape[-1]
        return a.reshape(halves, n_new, bh, w).transpose(0, 2, 1, 3).reshape(b_s, n_new, w)

    def attend_sample(qs, kb, vb, kf, vf):
        o = _attn_sample(to_batch(qs[0]), to_batch(qs[1]), to_batch(qs[2]),
                         cache_k.reshape(b_s, PAST_LEN, d), cache_v.reshape(b_s, PAST_LEN, d),
                         to_batch(kf), to_batch(vf))
        return [to_rows(o)], None

    x_s = to_rows(x_sample).reshape(halves, n_new * bh, d)
    prev_s = state_conv[0].reshape(halves, bh, CONV_WIDTH - 1, d).transpose(0, 2, 1, 3).reshape(
        halves, (CONV_WIDTH - 1) * bh, d)
    pos_s = jnp.tile(jnp.repeat(PAST_LEN + jnp.arange(n_new, dtype=jnp.int32), bh), halves)
    y_s, st_s, kf_s, vf_s = _forward(x_s, prev_s, pos_s, attend_sample, p, shift=bh, tm=n_new * bh)
    y_sample = to_batch(y_s)
    k_s = to_batch(kf_s).reshape(b_s, n_new, N_HEADS, HEAD_DIM)
    v_s = to_batch(vf_s).reshape(b_s, n_new, N_HEADS, HEAD_DIM)
    conv_s = st_s.reshape(halves, CONV_WIDTH - 1, bh, d).transpose(0, 2, 1, 3).reshape(
        b_s, CONV_WIDTH - 1, d)[None]

    return (y_prompt, y_sample, k_p, v_p, conv_p, k_s, v_s, conv_s)
```

```python
import functools

import numpy as np
import jax
import jax.numpy as jnp
from jax import lax
from jax.experimental import pallas as pl
from jax.experimental.pallas import tpu as pltpu

D_MODEL = 1024
CONV_WIDTH = 3
WINDOWS = (128, 512, 2048)
DILATIONS = (1, 4, 16)
N_GROUPS = 3
N_HEADS = 8
HEAD_DIM = 128
SPAN = 128
ROPE_THETA = 10000.0
MOE_GROUPS = 4
EXPERTS_PER_GROUP = 4
N_EXPERTS = 16
D_EXPERT = 512
EXPERT_BLOCK = 128
RMS_EPS = 1e-6
NEG_BIG = -1e30
PAST_LEN = 2048

ROUTER_ROWS = 32
EXPERT_ROW0 = 8
VMEM_LIMIT = 56 * 1024 * 1024

_F32 = jnp.float32
_BF16 = jnp.bfloat16


def _rms(x, g):
    ms = jnp.mean(x * x, axis=-1, keepdims=True)
    return (x * lax.rsqrt(ms + RMS_EPS)) * g


def _dot(a, b):
    return jnp.dot(a, b, preferred_element_type=_F32)


def _dot_nt(a, b, precision=None):
    return lax.dot_general(a, b, (((1,), (1,)), ((), ())), precision=precision,
                           preferred_element_type=_F32)


def _route(xn, wr_ref, br_ref, eid_ref, gate_ref):
    logits = _dot_nt(wr_ref[...], xn, precision=lax.Precision.HIGHEST) + br_ref[...]
    lg = logits[0:MOE_GROUPS]
    row = lax.broadcasted_iota(jnp.int32, lg.shape, 0).astype(_F32)
    eg = jnp.exp(lg - jnp.max(lg, axis=0, keepdims=True))
    pg = eg / jnp.sum(eg, axis=0, keepdims=True)
    gp = jnp.max(pg, axis=0, keepdims=True)
    gi = jnp.min(jnp.where(pg == gp, row, float(MOE_GROUPS)), axis=0, keepdims=True)
    le = jnp.zeros_like(lg)
    for g in range(MOE_GROUPS):
        r0 = EXPERT_ROW0 + g * EXPERTS_PER_GROUP
        le = le + jnp.where(gi == float(g), logits[r0:r0 + EXPERTS_PER_GROUP], 0.0)
    ee = jnp.exp(le - jnp.max(le, axis=0, keepdims=True))
    ev = ee / jnp.sum(ee, axis=0, keepdims=True)
    v1 = jnp.max(ev, axis=0, keepdims=True)
    i1 = jnp.min(jnp.where(ev == v1, row, float(EXPERTS_PER_GROUP)), axis=0, keepdims=True)
    ev2 = jnp.where(row == i1, -1.0, ev)
    v2 = jnp.max(ev2, axis=0, keepdims=True)
    i2 = jnp.min(jnp.where(ev2 == v2, row, float(EXPERTS_PER_GROUP)), axis=0, keepdims=True)
    den = v1 + v2
    ids = jnp.concatenate([gi * EXPERTS_PER_GROUP + i1, gi * EXPERTS_PER_GROUP + i2], axis=0)
    eid_ref[...] = ids.astype(jnp.int32)
    gate_ref[...] = jnp.concatenate([gp * v1 / den, gp * v2 / den], axis=0)


def _conv_layer_kernel(x_ref, prev_ref, gmix_ref, win_ref, ck_ref, wout_ref, gffn_ref, wr_ref, br_ref,
                       h_ref, xn2_ref, eid_ref, gate_ref, st_ref, ubuf, *, shift, tm):
    i = pl.program_id(1)
    halo = ubuf.shape[0] - tm
    keep = (CONV_WIDTH - 1) * shift

    @pl.when(i == 0)
    def _():
        ubuf[halo - keep:halo, :] = prev_ref[0]

    @pl.when(i > 0)
    def _():
        ubuf[halo - keep:halo, :] = ubuf[halo + tm - keep:halo + tm, :]

    x = x_ref[0]
    xn = _rms(x, gmix_ref[...]).astype(_BF16)
    c_gate = _dot(xn, win_ref[:, 0:D_MODEL])
    hid = _dot(xn, win_ref[:, 2 * D_MODEL:3 * D_MODEL])
    u = c_gate * hid
    ubuf[halo:halo + tm, :] = u
    u1 = ubuf[halo - shift:halo - shift + tm, :]
    u2 = ubuf[halo - 2 * shift:halo - 2 * shift + tm, :]
    conv = ck_ref[0:1, :] * u2 + ck_ref[1:2, :] * u1 + ck_ref[2:3, :] * u
    b_gate = _dot(xn, win_ref[:, D_MODEL:2 * D_MODEL])
    y = _dot((b_gate * conv).astype(_BF16), wout_ref[...])
    h = x + y
    h_ref[0] = h
    st_ref[0] = ubuf[halo + tm - keep:halo + tm, :]
    xn2 = _rms(h, gffn_ref[...])
    xn2_ref[...] = xn2
    _route(xn2, wr_ref, br_ref, eid_ref, gate_ref)


def _conv_layer(x3, prev3, gmix, win, ck, wout, gffn, wr, br, *, shift, tm):
    nb, s, d = x3.shape
    nt = s // tm
    keep = (CONV_WIDTH - 1) * shift
    halo = -(-keep // 8) * 8
    t_all = nb * s
    const = lambda b, i: (0, 0)
    return pl.pallas_call(
        functools.partial(_conv_layer_kernel, shift=shift, tm=tm),
        grid=(nb, nt),
        in_specs=[
            pl.BlockSpec((1, tm, d), lambda b, i: (b, i, 0)),
            pl.BlockSpec((1, keep, d), lambda b, i: (b, 0, 0)),
            pl.BlockSpec((1, d), const),
            pl.BlockSpec((d, 3 * d), const),
            pl.BlockSpec((CONV_WIDTH, d), const),
            pl.BlockSpec((d, d), const),
            pl.BlockSpec((1, d), const),
            pl.BlockSpec((ROUTER_ROWS, d), const),
            pl.BlockSpec((ROUTER_ROWS, 1), const),
        ],
        out_specs=[
            pl.BlockSpec((1, tm, d), lambda b, i: (b, i, 0)),
            pl.BlockSpec((tm, d), lambda b, i: (b * nt + i, 0)),
            pl.BlockSpec((2, tm), lambda b, i: (0, b * nt + i)),
            pl.BlockSpec((2, tm), lambda b, i: (0, b * nt + i)),
            pl.BlockSpec((1, keep, d), lambda b, i: (b, 0, 0)),
        ],
        out_shape=[
            jax.ShapeDtypeStruct((nb, s, d), _F32),
            jax.ShapeDtypeStruct((t_all, d), _F32),
            jax.ShapeDtypeStruct((2, t_all), jnp.int32),
            jax.ShapeDtypeStruct((2, t_all), _F32),
            jax.ShapeDtypeStruct((nb, keep, d), _F32),
        ],
        scratch_shapes=[pltpu.VMEM((halo + tm, d), _F32)],
        compiler_params=pltpu.CompilerParams(
            dimension_semantics=("arbitrary", "arbitrary"), vmem_limit_bytes=VMEM_LIMIT),
        name="conv_layer",
    )(x3, prev3, gmix, win, ck, wout, gffn, wr, br)


def _moe_ffn_kernel(be_ref, nv_ref, tab_ref, x_hbm, w1_ref, w3_ref, w2_ref, y_hbm,
                    xbuf, obuf, gsem, ssem, *, n_tok, n_blocks):
    j = pl.program_id(0)
    slot = j % 2
    other = 1 - slot
    half = EXPERT_BLOCK // 2

    def token_of(v):
        if n_tok & (n_tok - 1) == 0:
            return v & (n_tok - 1)
        return v - jnp.where(v >= n_tok, n_tok, 0)

    def gather_rows(blk, s, k0, k1):
        for k in range(k0, k1):
            tok = token_of(tab_ref[blk * EXPERT_BLOCK + k])
            pltpu.make_async_copy(x_hbm.at[pl.ds(tok, 1)], xbuf.at[s, pl.ds(k, 1)], gsem.at[s]).start()

    def scatter_rows(blk, s, n, k0, k1):
        for k in range(k0, k1):
            @pl.when(k < n)
            def _():
                v = tab_ref[blk * EXPERT_BLOCK + k]
                pltpu.make_async_copy(obuf.at[s, pl.ds(k, 1)], y_hbm.at[pl.ds(v, 1)], ssem.at[s]).start()

    def wait_gather(s):
        pltpu.make_async_copy(x_hbm.at[pl.ds(0, EXPERT_BLOCK)], xbuf.at[s], gsem.at[s]).wait()

    def wait_scatter(n, s):
        n8 = pl.multiple_of((n >> 3) << 3, 8)

        @pl.when(n8 > 0)
        def _():
            pltpu.make_async_copy(obuf.at[s, pl.ds(0, n8)], y_hbm.at[pl.ds(0, n8)], ssem.at[s]).wait()

        for i in range(7):
            @pl.when((n & 7) > i)
            def _():
                pltpu.make_async_copy(obuf.at[s, pl.ds(0, 1)], y_hbm.at[pl.ds(0, 1)], ssem.at[s]).wait()

    @pl.when(j == 0)
    def _():
        gather_rows(0, 0, 0, EXPERT_BLOCK)

    nxt = jnp.minimum(j + 1, n_blocks - 1)
    prv = jnp.maximum(j - 1, 0)
    n_prev = jnp.where(j > 0, nv_ref[prv], 0)

    wait_gather(slot)
    x = xbuf[slot].astype(_BF16)
    gather_rows(nxt, other, 0, half)
    a = _dot(x, w1_ref[0])
    gather_rows(nxt, other, half, EXPERT_BLOCK)
    b = _dot(x, w3_ref[0])
    scatter_rows(prv, other, n_prev, 0, half)
    hmid = (jax.nn.silu(a) * b).astype(_BF16)
    y = _dot(hmid, w2_ref[0])
    scatter_rows(prv, other, n_prev, half, EXPERT_BLOCK)

    @pl.when(j >= 2)
    def _():
        wait_scatter(nv_ref[jnp.maximum(j - 2, 0)], slot)

    obuf[slot] = y

    @pl.when(j == n_blocks - 1)
    def _():
        wait_gather(other)
        wait_scatter(n_prev, other)
        n_last = nv_ref[j]
        scatter_rows(j, slot, n_last, 0, EXPERT_BLOCK)
        wait_scatter(n_last, slot)


def _moe_ffn(block_expert, n_valid, table, xn, w1, w3, w2):
    n_tok, d = xn.shape
    n_blocks = block_expert.shape[0]
    assert n_blocks >= 2
    wmap = lambda j, be, nv, tab: (be[j], 0, 0)
    return pl.pallas_call(
        functools.partial(_moe_ffn_kernel, n_tok=n_tok, n_blocks=n_blocks),
        grid_spec=pltpu.PrefetchScalarGridSpec(
            num_scalar_prefetch=3,
            grid=(n_blocks,),
            in_specs=[
                pl.BlockSpec(memory_space=pl.ANY),
                pl.BlockSpec((1, d, D_EXPERT), wmap),
                pl.BlockSpec((1, d, D_EXPERT), wmap),
                pl.BlockSpec((1, D_EXPERT, d), wmap),
            ],
            out_specs=pl.BlockSpec(memory_space=pl.ANY),
            scratch_shapes=[
                pltpu.VMEM((2, EXPERT_BLOCK, d), _F32),
                pltpu.VMEM((2, EXPERT_BLOCK, d), _F32),
                pltpu.SemaphoreType.DMA((2,)),
                pltpu.SemaphoreType.DMA((2,)),
            ],
        ),
        out_shape=jax.ShapeDtypeStruct((2 * n_tok, d), _F32),
        compiler_params=pltpu.CompilerParams(
            dimension_semantics=("arbitrary",), vmem_limit_bytes=VMEM_LIMIT),
        name="moe_ffn",
    )(block_expert, n_valid, table, xn, w1, w3, w2)


def _moe_plan(eid, n_tok):
    n_assign = 2 * n_tok
    flat_e = eid.T.reshape(-1)
    onehot = (flat_e[:, None] == jnp.arange(N_EXPERTS, dtype=jnp.int32)[None, :]).astype(jnp.int32)
    csum = jnp.cumsum(onehot, axis=0)
    counts = csum[-1]
    rank = jnp.sum(csum * onehot, axis=1) - 1
    padded = (counts + EXPERT_BLOCK - 1) // EXPERT_BLOCK * EXPERT_BLOCK
    pad_end = jnp.cumsum(padded)
    pad_start = pad_end - padded
    slot = pad_start[flat_e] + rank
    n_blocks = -(-(n_assign + N_EXPERTS * (EXPERT_BLOCK - 1)) // EXPERT_BLOCK)
    n_slots = n_blocks * EXPERT_BLOCK
    a = jnp.arange(n_assign, dtype=jnp.int32)
    value = (a % 2) * n_tok + a // 2
    table = jnp.zeros((n_slots,), jnp.int32).at[slot].set(value)
    block_start = jnp.arange(n_blocks, dtype=jnp.int32) * EXPERT_BLOCK
    block_expert = jnp.minimum(jnp.searchsorted(pad_end, block_start, side='right'),
                               N_EXPERTS - 1).astype(jnp.int32)
    n_valid = jnp.clip((pad_start + counts)[block_expert] - block_start, 0, EXPERT_BLOCK).astype(jnp.int32)
    return block_expert, n_valid, table


def _rope_heads(x, cos, sin_signed, out_ref, scale):
    for h in range(N_HEADS):
        hs = slice(h * HEAD_DIM, (h + 1) * HEAD_DIM)
        xh = x[:, hs]
        r = xh * cos + pltpu.roll(xh, HEAD_DIM // 2, axis=1) * sin_signed
        if scale is not None:
            r = r * scale
        out_ref[:, hs] = r.astype(out_ref.dtype)


def _combine(h_ref, y0_ref, y1_ref, gate_ref):
    gate = gate_ref[...]
    return h_ref[...] + (gate[:, 0:1] * y0_ref[...] + gate[:, 1:2] * y1_ref[...])


def _qkv_kernel(h_ref, y0_ref, y1_ref, gate_ref, gmix_ref, gkv_ref, wq_ref, wkv_ref, cos_ref, sin_ref,
                h2_ref, q0_ref, q1_ref, q2_ref, kb_ref, vb_ref, kf_ref, vf_ref, ktmp):
    h2 = _combine(h_ref, y0_ref, y1_ref, gate_ref)
    h2_ref[...] = h2
    cos = cos_ref[...]
    sin = sin_ref[...]
    xn = _rms(h2, gmix_ref[...]).astype(_BF16)
    scale = HEAD_DIM ** -0.5
    for g, q_ref in enumerate((q0_ref, q1_ref, q2_ref)):
        q = _dot(xn, wq_ref[:, g * D_MODEL:(g + 1) * D_MODEL])
        _rope_heads(q, cos, sin, q_ref, scale)
    xkv = _rms(h2, gkv_ref[...]).astype(_BF16)
    k = _dot(xkv, wkv_ref[:, 0:D_MODEL])
    _rope_heads(k, cos, sin, ktmp, None)
    kr = ktmp[...]
    kf_ref[...] = kr
    kb_ref[...] = kr.astype(_BF16)
    v = _dot(xkv, wkv_ref[:, D_MODEL:2 * D_MODEL])
    vf_ref[...] = v
    vb_ref[...] = v.astype(_BF16)


def _qkv(h, ypair, gate, gmix, gkv, wq, wkv, cos, sin, *, tm):
    t, d = h.shape
    nt = t // tm
    row = lambda i: (i, 0)
    const = lambda i: (0, 0)
    return pl.pallas_call(
        _qkv_kernel,
        grid=(nt,),
        in_specs=[
            pl.BlockSpec((tm, d), row),
            pl.BlockSpec((tm, d), row),
            pl.BlockSpec((tm, d), lambda i: (nt + i, 0)),
            pl.BlockSpec((tm, 2), row),
            pl.BlockSpec((1, d), const),
            pl.BlockSpec((1, d), const),
            pl.BlockSpec((d, 3 * d), const),
            pl.BlockSpec((d, 2 * d), const),
            pl.BlockSpec((tm, HEAD_DIM), row),
            pl.BlockSpec((tm, HEAD_DIM), row),
        ],
        out_specs=[pl.BlockSpec((tm, d), row)] * 8,
        out_shape=[jax.ShapeDtypeStruct((t, d), _F32)] + [jax.ShapeDtypeStruct((t, d), _BF16)] * 5
                  + [jax.ShapeDtypeStruct((t, d), _F32)] * 2,
        scratch_shapes=[pltpu.VMEM((tm, d), _F32)],
        compiler_params=pltpu.CompilerParams(
            dimension_semantics=("arbitrary",), vmem_limit_bytes=VMEM_LIMIT),
        name="qkv_proj",
    )(h, ypair, ypair, gate, gmix, gkv, wq, wkv, cos, sin)


def _attn_prompt_kernel(q_ref, kp_ref, kc_ref, vp_ref, vc_ref, o_ref, st_ref):
    i = pl.program_id(1)
    qi = lax.broadcasted_iota(jnp.int32, (SPAN, SPAN), 0)
    kj = lax.broadcasted_iota(jnp.int32, (SPAN, SPAN), 1)
    mask_p = kj >= qi + jnp.where(i > 0, 0, SPAN)
    mask_c = kj <= qi
    lane = lax.broadcasted_iota(jnp.int32, (SPAN, HEAD_DIM), 1)
    st = jnp.zeros((SPAN, HEAD_DIM), _F32)
    for h in range(N_HEADS):
        hs = slice(h * HEAD_DIM, (h + 1) * HEAD_DIM)
        qh = q_ref[0, :, hs]
        sp = jnp.where(mask_p, _dot_nt(qh, kp_ref[0, :, hs]), NEG_BIG)
        sc = jnp.where(mask_c, _dot_nt(qh, kc_ref[0, :, hs]), NEG_BIG)
        m = jnp.maximum(jnp.max(sp, axis=1, keepdims=True), jnp.max(sc, axis=1, keepdims=True))
        pp = jnp.exp(sp - m)
        pc = jnp.exp(sc - m)
        l = jnp.sum(pp, axis=1, keepdims=True) + jnp.sum(pc, axis=1, keepdims=True)
        o = _dot(pp.astype(_BF16), vp_ref[0, :, hs]) + _dot(pc.astype(_BF16), vc_ref[0, :, hs])
        o_ref[0, :, hs] = o / l
        st = jnp.where(lane == h, m + jnp.log(l), st)
    st_ref[0] = st


def _attn_prompt(q, k, v, dilation):
    b, s, d = q.shape
    su = s // dilation
    nblk = su // SPAN
    qv = q.reshape(b, su, dilation * d)
    kv = k.reshape(b, su, dilation * d)
    vv = v.reshape(b, su, dilation * d)
    cur = lambda bb, i, r: (bb, i, r)
    prev = lambda bb, i, r: (bb, jnp.maximum(i - 1, 0), r)
    blk = (1, SPAN, d)
    o, st = pl.pallas_call(
        _attn_prompt_kernel,
        grid=(b, nblk, dilation),
        in_specs=[pl.BlockSpec(blk, cur), pl.BlockSpec(blk, prev), pl.BlockSpec(blk, cur),
                  pl.BlockSpec(blk, prev), pl.BlockSpec(blk, cur)],
        out_specs=[pl.BlockSpec(blk, cur), pl.BlockSpec((1, SPAN, HEAD_DIM), cur)],
        out_shape=[jax.ShapeDtypeStruct((b, su, dilation * d), _F32),
                   jax.ShapeDtypeStruct((b, su, dilation * HEAD_DIM), _F32)],
        compiler_params=pltpu.CompilerParams(
            dimension_semantics=("arbitrary", "arbitrary", "arbitrary"), vmem_limit_bytes=VMEM_LIMIT),
        name=f"attn_prompt_d{dilation}",
    )(qv, kv, kv, vv, vv)
    return o.reshape(b * s, d), st.reshape(b * s, HEAD_DIM)


N_NEW = 8
CACHE_A_GROUPS = (PAST_LEN - WINDOWS[1]) // DILATIONS[2]
CACHE_A_ROWS = CACHE_A_GROUPS * N_NEW
CACHE_B_ROWS = WINDOWS[1]
KEYS_REAL = CACHE_A_ROWS + CACHE_B_ROWS + N_NEW
KEYS_PAD = -(-KEYS_REAL // 128) * 128


def _sample_key_positions():
    pos = np.full((KEYS_PAD,), -1, np.int64)
    a = np.arange(CACHE_A_ROWS)
    pos[:CACHE_A_ROWS] = (a // N_NEW) * DILATIONS[2] + a % N_NEW
    pos[CACHE_A_ROWS:CACHE_A_ROWS + CACHE_B_ROWS] = PAST_LEN - CACHE_B_ROWS + np.arange(CACHE_B_ROWS)
    pos[CACHE_A_ROWS + CACHE_B_ROWS:KEYS_REAL] = PAST_LEN + np.arange(N_NEW)
    return pos


def _sample_bias():
    pos = _sample_key_positions()
    bias = np.full((N_GROUPS * N_NEW, KEYS_PAD), NEG_BIG, np.float32)
    for g in range(N_GROUPS):
        for n in range(N_NEW):
            delta = PAST_LEN + n - pos
            ok = (pos >= 0) & (delta >= 0) & (delta <= WINDOWS[g]) & (delta % DILATIONS[g] == 0)
            assert int(ok.sum()) == SPAN + 1
            bias[g * N_NEW + n, ok] = 0.0
    return bias


CACHE_B_GROUPS = CACHE_B_ROWS // DILATIONS[2]


def _attn_sample_kernel(q0_ref, q1_ref, q2_ref, kn_ref, vn_ref, bias_ref, ck_hbm, cv_hbm,
                        o_ref, ka, kb, va, vb, kall, vall, sem, *, n_seq):
    b = pl.program_id(0)
    slot = b % 2

    def cache_copies(seq, s):
        cps = []
        for h in range(N_HEADS):
            for src, dst_a, dst_b in ((ck_hbm, ka, kb), (cv_hbm, va, vb)):
                cps.append(pltpu.make_async_copy(
                    src.at[seq, pl.ds(0, CACHE_A_GROUPS), pl.ds(0, N_NEW), h], dst_a.at[s, h], sem.at[s]))
                cps.append(pltpu.make_async_copy(
                    src.at[seq, pl.ds(CACHE_A_GROUPS, CACHE_B_GROUPS), :, h], dst_b.at[s, h], sem.at[s]))
        return cps

    @pl.when(b == 0)
    def _():
        kall[KEYS_REAL:KEYS_PAD, :] = jnp.zeros((KEYS_PAD - KEYS_REAL, D_MODEL), _BF16)
        vall[KEYS_REAL:KEYS_PAD, :] = jnp.zeros((KEYS_PAD - KEYS_REAL, D_MODEL), _BF16)
        for cp in cache_copies(0, 0):
            cp.start()

    @pl.when(b + 1 < n_seq)
    def _():
        for cp in cache_copies(b + 1, 1 - slot):
            cp.start()

    for cp in cache_copies(b, slot):
        cp.wait()

    nb0 = CACHE_A_ROWS + CACHE_B_ROWS
    for src_a, src_b, src_n, dst in ((ka, kb, kn_ref, kall), (va, vb, vn_ref, vall)):
        for h in range(N_HEADS):
            hs = slice(h * HEAD_DIM, (h + 1) * HEAD_DIM)
            dst[0:CACHE_A_ROWS, hs] = src_a[slot, h].reshape(CACHE_A_ROWS, HEAD_DIM).astype(_BF16)
            dst[CACHE_A_ROWS:nb0, hs] = src_b[slot, h].reshape(CACHE_B_ROWS, HEAD_DIM).astype(_BF16)
        dst[nb0:KEYS_REAL, :] = src_n[0].astype(_BF16)

    bias = bias_ref[...]
    for h in range(N_HEADS):
        hs = slice(h * HEAD_DIM, (h + 1) * HEAD_DIM)
        qh = jnp.concatenate([q0_ref[0, :, hs].astype(_F32), q1_ref[0, :, hs].astype(_F32),
                              q2_ref[0, :, hs].astype(_F32)], axis=0).astype(_BF16)
        s = _dot_nt(qh, kall[:, hs]) + bias
        m = jnp.max(s, axis=1, keepdims=True)
        m8 = jnp.maximum(jnp.maximum(m[0:N_NEW], m[N_NEW:2 * N_NEW]), m[2 * N_NEW:3 * N_NEW])
        p = jnp.exp(s - jnp.concatenate([m8, m8, m8], axis=0))
        p8 = p[0:N_NEW] + p[N_NEW:2 * N_NEW] + p[2 * N_NEW:3 * N_NEW]
        l8 = jnp.sum(p8, axis=1, keepdims=True)
        o_ref[0, :, hs] = _dot(p8.astype(_BF16), vall[:, hs]) / l8


def _attn_sample(q0, q1, q2, cache_k, cache_v, k_new, v_new):
    nb, n_new, d = q0.shape
    past = cache_k.shape[1]
    assert past == PAST_LEN and n_new == N_NEW and d == D_MODEL
    assert cache_k.shape[2:] == (N_HEADS, HEAD_DIM)
    ck = cache_k.reshape(nb, past // DILATIONS[2], DILATIONS[2], N_HEADS, HEAD_DIM)
    cv = cache_v.reshape(nb, past // DILATIONS[2], DILATIONS[2], N_HEADS, HEAD_DIM)
    bias = jnp.asarray(_sample_bias())
    new = pl.BlockSpec((1, n_new, d), lambda b: (b, 0, 0))
    part_a = (2, N_HEADS, CACHE_A_GROUPS, N_NEW, HEAD_DIM)
    part_b = (2, N_HEADS, CACHE_B_GROUPS, DILATIONS[2], HEAD_DIM)
    return pl.pallas_call(
        functools.partial(_attn_sample_kernel, n_seq=nb),
        grid=(nb,),
        in_specs=[new, new, new, new, new,
                  pl.BlockSpec((N_GROUPS * N_NEW, KEYS_PAD), lambda b: (0, 0)),
                  pl.BlockSpec(memory_space=pl.ANY), pl.BlockSpec(memory_space=pl.ANY)],
        out_specs=new,
        out_shape=jax.ShapeDtypeStruct((nb, n_new, d), _F32),
        scratch_shapes=[pltpu.VMEM(part_a, _F32), pltpu.VMEM(part_b, _F32),
                        pltpu.VMEM(part_a, _F32), pltpu.VMEM(part_b, _F32),
                        pltpu.VMEM((KEYS_PAD, d), _BF16), pltpu.VMEM((KEYS_PAD, d), _BF16),
                        pltpu.SemaphoreType.DMA((2,))],
        compiler_params=pltpu.CompilerParams(
            dimension_semantics=("arbitrary",), vmem_limit_bytes=VMEM_LIMIT),
        name="attn_sample",
    )(q0, q1, q2, k_new, v_new, bias, ck, cv)


def _attn_out_kernel(*refs, n_groups):
    o_refs = refs[:n_groups]
    st_refs = refs[n_groups:2 * n_groups] if n_groups > 1 else ()
    rest = refs[2 * n_groups:] if n_groups > 1 else refs[n_groups:]
    h_ref, wo_ref, gffn_ref, wr_ref, br_ref, h3_ref, xn2_ref, eid_ref, gate_ref, obuf = rest
    if n_groups == 1:
        o = o_refs[0][...].astype(_BF16)
    else:
        sts = [r[...] for r in st_refs]
        mx = sts[0]
        for s in sts[1:]:
            mx = jnp.maximum(mx, s)
        es = [jnp.exp(s - mx) for s in sts]
        den = es[0]
        for e in es[1:]:
            den = den + e
        ws = [e / den for e in es]
        for h in range(N_HEADS):
            hs = slice(h * HEAD_DIM, (h + 1) * HEAD_DIM)
            acc = ws[0][:, h:h + 1] * o_refs[0][:, hs]
            for g in range(1, n_groups):
                acc = acc + ws[g][:, h:h + 1] * o_refs[g][:, hs]
            obuf[:, hs] = acc.astype(_BF16)
        o = obuf[...]
    h3 = h_ref[...] + _dot(o, wo_ref[...])
    h3_ref[...] = h3
    xn2 = _rms(h3, gffn_ref[...])
    xn2_ref[...] = xn2
    _route(xn2, wr_ref, br_ref, eid_ref, gate_ref)


def _attn_out(os, sts, h, wo, gffn, wr, br, *, tm):
    t, d = h.shape
    n_groups = len(os)
    row = lambda i: (i, 0)
    const = lambda i: (0, 0)
    in_specs = [pl.BlockSpec((tm, d), row)] * n_groups
    if n_groups > 1:
        in_specs += [pl.BlockSpec((tm, HEAD_DIM), row)] * n_groups
    in_specs += [pl.BlockSpec((tm, d), row), pl.BlockSpec((d, d), const), pl.BlockSpec((1, d), const),
                 pl.BlockSpec((ROUTER_ROWS, d), const), pl.BlockSpec((ROUTER_ROWS, 1), const)]
    args = list(os) + (list(sts) if n_groups > 1 else []) + [h, wo, gffn, wr, br]
    return pl.pallas_call(
        functools.partial(_attn_out_kernel, n_groups=n_groups),
        grid=(t // tm,),
        in_specs=in_specs,
        out_specs=[pl.BlockSpec((tm, d), row), pl.BlockSpec((tm, d), row),
                   pl.BlockSpec((2, tm), lambda i: (0, i)), pl.BlockSpec((2, tm), lambda i: (0, i))],
        out_shape=[jax.ShapeDtypeStruct((t, d), _F32), jax.ShapeDtypeStruct((t, d), _F32),
                   jax.ShapeDtypeStruct((2, t), jnp.int32), jax.ShapeDtypeStruct((2, t), _F32)],
        scratch_shapes=[pltpu.VMEM((tm, d), _BF16)],
        compiler_params=pltpu.CompilerParams(
            dimension_semantics=("arbitrary",), vmem_limit_bytes=VMEM_LIMIT),
        name=f"attn_out_g{n_groups}",
    )(*args)


def _final_kernel(h_ref, y0_ref, y1_ref, gate_ref, g_ref, out_ref):
    out_ref[...] = _rms(_combine(h_ref, y0_ref, y1_ref, gate_ref), g_ref[...])


def _final(h, ypair, gate, g, *, tm):
    t, d = h.shape
    nt = t // tm
    row = lambda i: (i, 0)
    return pl.pallas_call(
        _final_kernel,
        grid=(nt,),
        in_specs=[pl.BlockSpec((tm, d), row), pl.BlockSpec((tm, d), row),
                  pl.BlockSpec((tm, d), lambda i: (nt + i, 0)), pl.BlockSpec((tm, 2), row),
                  pl.BlockSpec((1, d), lambda i: (0, 0))],
        out_specs=pl.BlockSpec((tm, d), row),
        out_shape=jax.ShapeDtypeStruct((t, d), _F32),
        compiler_params=pltpu.CompilerParams(
            dimension_semantics=("arbitrary",), vmem_limit_bytes=VMEM_LIMIT),
        name="final_norm",
    )(h, ypair, ypair, gate, g)


def _rope_tables(pos):
    half = HEAD_DIM // 2
    inv_freq = jnp.power(jnp.float32(ROPE_THETA), -jnp.arange(half, dtype=jnp.float32) / half)
    ang = pos.astype(jnp.float32)[:, None] * inv_freq[None, :]
    cos = jnp.cos(ang)
    sin = jnp.sin(ang)
    return jnp.concatenate([cos, cos], axis=-1), jnp.concatenate([-sin, sin], axis=-1)


def _router_params(wg, bg, we, be):
    wr = jnp.zeros((ROUTER_ROWS, D_MODEL), _F32)
    wr = wr.at[0:MOE_GROUPS].set(wg.T).at[EXPERT_ROW0:EXPERT_ROW0 + N_EXPERTS].set(we.T)
    br = jnp.zeros((ROUTER_ROWS, 1), _F32)
    br = br.at[0:MOE_GROUPS, 0].set(bg).at[EXPERT_ROW0:EXPERT_ROW0 + N_EXPERTS, 0].set(be)
    return wr, br


def _moe(xn2, eid, w1, w3, w2):
    block_expert, n_valid, table = _moe_plan(eid, xn2.shape[0])
    return _moe_ffn(block_expert, n_valid, table, xn2, w1, w3, w2)


def _forward(x3, prev3, pos_rows, attend, p, *, shift, tm):
    h1, xn2, eid, gate, state = _conv_layer(
        x3, prev3, p['gmix'][0], p['win'], p['ck'], p['wout'], p['gffn'][0], *p['router'][0],
        shift=shift, tm=tm)
    t = x3.shape[0] * x3.shape[1]
    ypair = _moe(xn2, eid, *p['experts'][0])
    cos, sin = _rope_tables(pos_rows)
    h2, q0, q1, q2, kb, vb, kf, vf = _qkv(h1.reshape(t, D_MODEL), ypair, gate.T, p['gmix'][1], p['gkv'],
                                           p['wq'], p['wkv'], cos, sin, tm=256)
    os, sts = attend((q0, q1, q2), kb, vb, kf, vf)
    h3, xn2, eid, gate = _attn_out(os, sts, h2, p['wo'], p['gffn'][1], *p['router'][1], tm=256)
    ypair = _moe(xn2, eid, *p['experts'][1])
    y = _final(h3, ypair, gate.T, p['gfinal'], tm=512)
    return y, state, kf, vf


def kernel(x_prompt, x_sample, cache_k, cache_v, state_conv, norm_mix, norm_ffn, norm_kv, norm_final,
           conv_w_in, conv_kernel, conv_w_out, attn_w_q, attn_w_kv, attn_w_o, router_group_w,
           router_group_b, router_expert_w, router_expert_b, expert_w1, expert_w3, expert_w2):
    b_p, s_p, d = x_prompt.shape
    b_s, n_new, _ = x_sample.shape
    assert d == D_MODEL and n_new == N_NEW and s_p % (DILATIONS[2] * SPAN) == 0
    assert norm_mix.shape[0] == 2 and conv_w_in.shape[0] == 1 and attn_w_q.shape[0] == 1
    assert cache_k.shape[1] == PAST_LEN

    p = {
        'gmix': [norm_mix[l].reshape(1, d) for l in range(2)],
        'gffn': [norm_ffn[l].reshape(1, d) for l in range(2)],
        'gkv': norm_kv.reshape(1, d),
        'gfinal': norm_final.reshape(1, d),
        'win': conv_w_in[0].astype(_BF16),
        'ck': conv_kernel[0],
        'wout': conv_w_out[0].astype(_BF16),
        'wq': attn_w_q[0].astype(_BF16),
        'wkv': attn_w_kv.astype(_BF16),
        'wo': attn_w_o[0].astype(_BF16),
        'router': [_router_params(router_group_w[l], router_group_b[l], router_expert_w[l], router_expert_b[l])
                   for l in range(2)],
        'experts': [(expert_w1[l].astype(_BF16), expert_w3[l].astype(_BF16), expert_w2[l].astype(_BF16))
                    for l in range(2)],
    }

    def attend_prompt(qs, kb, vb, kf, vf):
        os, sts = [], []
        for g in range(N_GROUPS):
            o, st = _attn_prompt(qs[g].reshape(b_p, s_p, d), kb.reshape(b_p, s_p, d),
                                 vb.reshape(b_p, s_p, d), DILATIONS[g])
            os.append(o)
            sts.append(st)
        return os, sts

    pos_p = jnp.tile(jnp.arange(s_p, dtype=jnp.int32), b_p)
    zero_state = jnp.zeros((b_p, CONV_WIDTH - 1, d), x_prompt.dtype)
    y_p, st_p, kf_p, vf_p = _forward(x_prompt, zero_state, pos_p, attend_prompt, p, shift=1, tm=512)
    keep = min(max(WINDOWS), s_p)
    y_prompt = y_p.reshape(b_p, s_p, d)
    k_p = kf_p.reshape(b_p, s_p, N_HEADS, HEAD_DIM)[:, s_p - keep:]
    v_p = vf_p.reshape(b_p, s_p, N_HEADS, HEAD_DIM)[:, s_p - keep:]
    conv_p = st_p[None]

    halves = 2
    bh = b_s // halves

    def to_rows(a):
        w = a.shape[-1]
        return a.reshape(halves, bh, n_new, w).transpose(0, 2, 1, 3).reshape(halves * n_new * bh, w)

    def to_batch(a):
        w = a.shape[-1]
        return a.reshape(halves, n_new, bh, w).transpose(0, 2, 1, 3).reshape(b_s, n_new, w)

    def attend_sample(qs, kb, vb, kf, vf):
        o = _attn_sample(to_batch(qs[0]), to_batch(qs[1]), to_batch(qs[2]), cache_k, cache_v,
                         to_batch(kf), to_batch(vf))
        return [to_rows(o)], None

    x_s = to_rows(x_sample).reshape(halves, n_new * bh, d)
    prev_s = state_conv[0].reshape(halves, bh, CONV_WIDTH - 1, d).transpose(0, 2, 1, 3).reshape(
        halves, (CONV_WIDTH - 1) * bh, d)
    pos_s = jnp.tile(jnp.repeat(PAST_LEN + jnp.arange(n_new, dtype=jnp.int32), bh), halves)
    y_s, st_s, kf_s, vf_s = _forward(x_s, prev_s, pos_s, attend_sample, p, shift=bh, tm=n_new * bh)
    y_sample = to_batch(y_s)
    k_s = to_batch(kf_s).reshape(b_s, n_new, N_HEADS, HEAD_DIM)
    v_s = to_batch(vf_s).reshape(b_s, n_new, N_HEADS, HEAD_DIM)
    conv_s = st_s.reshape(halves, CONV_WIDTH - 1, bh, d).transpose(0, 2, 1, 3).reshape(
        b_s, CONV_WIDTH - 1, d)[None]

    return (y_prompt, y_sample, k_p, v_p, conv_p, k_s, v_s, conv_s)
```

```python
import functools

import numpy as np
import jax
import jax.numpy as jnp
from jax import lax
from jax.experimental import pallas as pl
from jax.experimental.pallas import tpu as pltpu

D_MODEL = 1024
CONV_WIDTH = 3
WINDOWS = (128, 512, 2048)
DILATIONS = (1, 4, 16)
N_GROUPS = 3
N_HEADS = 8
HEAD_DIM = 128
SPAN = 128
ROPE_THETA = 10000.0
MOE_GROUPS = 4
EXPERTS_PER_GROUP = 4
N_EXPERTS = 16
D_EXPERT = 512
EXPERT_BLOCK = 128
RMS_EPS = 1e-6
NEG_BIG = -1e30
PAST_LEN = 2048

LANES = 128
N_CHUNKS = D_MODEL // LANES
ROUTER_ROWS = 32
EXPERT_ROW0 = 8
VMEM_LIMIT = 56 * 1024 * 1024

_F32 = jnp.float32
_BF16 = jnp.bfloat16


def _rms(x, g):
    ms = jnp.mean(x * x, axis=-1, keepdims=True)
    return (x * lax.rsqrt(ms + RMS_EPS)) * g


def _dot(a, b):
    return jnp.dot(a, b, preferred_element_type=_F32)


def _dot_nt(a, b, precision=None):
    return lax.dot_general(a, b, (((1,), (1,)), ((), ())), precision=precision,
                           preferred_element_type=_F32)


def _chunk(c):
    return slice(c * LANES, (c + 1) * LANES)


def _store_row_tiles(ref, x, lead=()):
    for c in range(N_CHUNKS):
        ref[lead + (slice(None), c, slice(None))] = x[:, _chunk(c)]


def _load_row_tiles(ref, lead=()):
    return jnp.concatenate([ref[lead + (slice(None), c, slice(None))] for c in range(N_CHUNKS)], axis=1)


def _route(xn, wr_ref, br_ref, eid_ref, gate_ref):
    logits = _dot_nt(wr_ref[...], xn, precision=lax.Precision.HIGHEST) + br_ref[...]
    lg = logits[0:MOE_GROUPS]
    row = lax.broadcasted_iota(jnp.int32, lg.shape, 0).astype(_F32)
    eg = jnp.exp(lg - jnp.max(lg, axis=0, keepdims=True))
    pg = eg / jnp.sum(eg, axis=0, keepdims=True)
    gp = jnp.max(pg, axis=0, keepdims=True)
    gi = jnp.min(jnp.where(pg == gp, row, float(MOE_GROUPS)), axis=0, keepdims=True)
    le = jnp.zeros_like(lg)
    for g in range(MOE_GROUPS):
        r0 = EXPERT_ROW0 + g * EXPERTS_PER_GROUP
        le = le + jnp.where(gi == float(g), logits[r0:r0 + EXPERTS_PER_GROUP], 0.0)
    ee = jnp.exp(le - jnp.max(le, axis=0, keepdims=True))
    ev = ee / jnp.sum(ee, axis=0, keepdims=True)
    v1 = jnp.max(ev, axis=0, keepdims=True)
    i1 = jnp.min(jnp.where(ev == v1, row, float(EXPERTS_PER_GROUP)), axis=0, keepdims=True)
    ev2 = jnp.where(row == i1, -1.0, ev)
    v2 = jnp.max(ev2, axis=0, keepdims=True)
    i2 = jnp.min(jnp.where(ev2 == v2, row, float(EXPERTS_PER_GROUP)), axis=0, keepdims=True)
    den = v1 + v2
    ids = jnp.concatenate([gi * EXPERTS_PER_GROUP + i1, gi * EXPERTS_PER_GROUP + i2], axis=0)
    eid_ref[...] = ids.astype(jnp.int32)
    gate_ref[...] = jnp.concatenate([gp * v1 / den, gp * v2 / den], axis=0)


def _conv_layer_kernel(x_ref, prev_ref, gmix_ref, win_ref, ck_ref, wout_ref, gffn_ref, wr_ref, br_ref,
                       h_ref, xn2_ref, eid_ref, gate_ref, st_ref, ubuf, *, shift, tm):
    i = pl.program_id(1)
    halo = ubuf.shape[0] - tm
    keep = (CONV_WIDTH - 1) * shift

    @pl.when(i == 0)
    def _():
        ubuf[halo - keep:halo, :] = prev_ref[0]

    @pl.when(i > 0)
    def _():
        ubuf[halo - keep:halo, :] = ubuf[halo + tm - keep:halo + tm, :]

    x = x_ref[0]
    xn = _rms(x, gmix_ref[...]).astype(_BF16)
    c_gate = _dot(xn, win_ref[:, 0:D_MODEL])
    hid = _dot(xn, win_ref[:, 2 * D_MODEL:3 * D_MODEL])
    u = c_gate * hid
    ubuf[halo:halo + tm, :] = u
    u1 = ubuf[halo - shift:halo - shift + tm, :]
    u2 = ubuf[halo - 2 * shift:halo - 2 * shift + tm, :]
    conv = ck_ref[0:1, :] * u2 + ck_ref[1:2, :] * u1 + ck_ref[2:3, :] * u
    b_gate = _dot(xn, win_ref[:, D_MODEL:2 * D_MODEL])
    y = _dot((b_gate * conv).astype(_BF16), wout_ref[...])
    h = x + y
    h_ref[0] = h
    st_ref[0] = ubuf[halo + tm - keep:halo + tm, :]
    xn2 = _rms(h, gffn_ref[...])
    _store_row_tiles(xn2_ref, xn2)
    _route(xn2, wr_ref, br_ref, eid_ref, gate_ref)


def _conv_layer(x3, prev3, gmix, win, ck, wout, gffn, wr, br, *, shift, tm):
    nb, s, d = x3.shape
    nt = s // tm
    keep = (CONV_WIDTH - 1) * shift
    halo = -(-keep // 8) * 8
    t_all = nb * s
    const = lambda b, i: (0, 0)
    return pl.pallas_call(
        functools.partial(_conv_layer_kernel, shift=shift, tm=tm),
        grid=(nb, nt),
        in_specs=[
            pl.BlockSpec((1, tm, d), lambda b, i: (b, i, 0)),
            pl.BlockSpec((1, keep, d), lambda b, i: (b, 0, 0)),
            pl.BlockSpec((1, d), const),
            pl.BlockSpec((d, 3 * d), const),
            pl.BlockSpec((CONV_WIDTH, d), const),
            pl.BlockSpec((d, d), const),
            pl.BlockSpec((1, d), const),
            pl.BlockSpec((ROUTER_ROWS, d), const),
            pl.BlockSpec((ROUTER_ROWS, 1), const),
        ],
        out_specs=[
            pl.BlockSpec((1, tm, d), lambda b, i: (b, i, 0)),
            pl.BlockSpec((tm, N_CHUNKS, LANES), lambda b, i: (b * nt + i, 0, 0)),
            pl.BlockSpec((2, tm), lambda b, i: (0, b * nt + i)),
            pl.BlockSpec((2, tm), lambda b, i: (0, b * nt + i)),
            pl.BlockSpec((1, keep, d), lambda b, i: (b, 0, 0)),
        ],
        out_shape=[
            jax.ShapeDtypeStruct((nb, s, d), _F32),
            jax.ShapeDtypeStruct((t_all, N_CHUNKS, LANES), _F32),
            jax.ShapeDtypeStruct((2, t_all), jnp.int32),
            jax.ShapeDtypeStruct((2, t_all), _F32),
            jax.ShapeDtypeStruct((nb, keep, d), _F32),
        ],
        scratch_shapes=[pltpu.VMEM((halo + tm, d), _F32)],
        compiler_params=pltpu.CompilerParams(
            dimension_semantics=("arbitrary", "arbitrary"), vmem_limit_bytes=VMEM_LIMIT),
        name="conv_layer",
    )(x3, prev3, gmix, win, ck, wout, gffn, wr, br)


def _moe_ffn_kernel(be_ref, nv_ref, tab_ref, x_hbm, w1_ref, w3_ref, w2_ref, y_hbm,
                    xbuf, obuf, gsem, ssem, *, n_tok, n_blocks):
    j = pl.program_id(0)
    slot = j % 2
    other = 1 - slot
    half = EXPERT_BLOCK // 2

    def token_of(v):
        if n_tok & (n_tok - 1) == 0:
            return v & (n_tok - 1)
        return v - jnp.where(v >= n_tok, n_tok, 0)

    def gather_rows(blk, s, k0, k1):
        for k in range(k0, k1):
            tok = token_of(tab_ref[blk * EXPERT_BLOCK + k])
            pltpu.make_async_copy(x_hbm.at[tok], xbuf.at[s, k], gsem.at[s]).start()

    def scatter_rows(blk, s, n, k0, k1):
        for k in range(k0, k1):
            @pl.when(k < n)
            def _():
                v = tab_ref[blk * EXPERT_BLOCK + k]
                pltpu.make_async_copy(obuf.at[s, k], y_hbm.at[v], ssem.at[s]).start()

    def wait_gather(s):
        pltpu.make_async_copy(x_hbm.at[pl.ds(0, EXPERT_BLOCK)], xbuf.at[s], gsem.at[s]).wait()

    def wait_scatter(n, s):
        @pl.when(n > 0)
        def _():
            pltpu.make_async_copy(obuf.at[s, pl.ds(0, n)], y_hbm.at[pl.ds(0, n)], ssem.at[s]).wait()

    @pl.when(j == 0)
    def _():
        gather_rows(0, 0, 0, EXPERT_BLOCK)

    nxt = jnp.minimum(j + 1, n_blocks - 1)
    prv = jnp.maximum(j - 1, 0)
    n_prev = jnp.where(j > 0, nv_ref[prv], 0)

    wait_gather(slot)
    x = _load_row_tiles(xbuf, (slot,)).astype(_BF16)
    gather_rows(nxt, other, 0, half)
    a = _dot(x, w1_ref[0])
    gather_rows(nxt, other, half, EXPERT_BLOCK)
    b = _dot(x, w3_ref[0])
    scatter_rows(prv, other, n_prev, 0, half)
    hmid = (jax.nn.silu(a) * b).astype(_BF16)
    y = _dot(hmid, w2_ref[0])
    scatter_rows(prv, other, n_prev, half, EXPERT_BLOCK)

    @pl.when(j >= 2)
    def _():
        wait_scatter(nv_ref[jnp.maximum(j - 2, 0)], slot)

    _store_row_tiles(obuf, y, (slot,))

    @pl.when(j == n_blocks - 1)
    def _():
        wait_gather(other)
        wait_scatter(n_prev, other)
        n_last = nv_ref[j]
        scatter_rows(j, slot, n_last, 0, EXPERT_BLOCK)
        wait_scatter(n_last, slot)


def _moe_ffn(block_expert, n_valid, table, xn, w1, w3, w2):
    n_tok = xn.shape[0]
    d = D_MODEL
    n_blocks = block_expert.shape[0]
    assert n_blocks >= 2 and xn.shape[1:] == (N_CHUNKS, LANES)
    wmap = lambda j, be, nv, tab: (be[j], 0, 0)
    return pl.pallas_call(
        functools.partial(_moe_ffn_kernel, n_tok=n_tok, n_blocks=n_blocks),
        grid_spec=pltpu.PrefetchScalarGridSpec(
            num_scalar_prefetch=3,
            grid=(n_blocks,),
            in_specs=[
                pl.BlockSpec(memory_space=pl.ANY),
                pl.BlockSpec((1, d, D_EXPERT), wmap),
                pl.BlockSpec((1, d, D_EXPERT), wmap),
                pl.BlockSpec((1, D_EXPERT, d), wmap),
            ],
            out_specs=pl.BlockSpec(memory_space=pl.ANY),
            scratch_shapes=[
                pltpu.VMEM((2, EXPERT_BLOCK, N_CHUNKS, LANES), _F32),
                pltpu.VMEM((2, EXPERT_BLOCK, N_CHUNKS, LANES), _F32),
                pltpu.SemaphoreType.DMA((2,)),
                pltpu.SemaphoreType.DMA((2,)),
            ],
        ),
        out_shape=jax.ShapeDtypeStruct((2 * n_tok, N_CHUNKS, LANES), _F32),
        compiler_params=pltpu.CompilerParams(
            dimension_semantics=("arbitrary",), vmem_limit_bytes=VMEM_LIMIT),
        name="moe_ffn",
    )(block_expert, n_valid, table, xn, w1, w3, w2)


def _moe_plan(eid, n_tok):
    n_assign = 2 * n_tok
    flat_e = eid.T.reshape(-1)
    onehot = (flat_e[:, None] == jnp.arange(N_EXPERTS, dtype=jnp.int32)[None, :]).astype(jnp.int32)
    csum = jnp.cumsum(onehot, axis=0)
    counts = csum[-1]
    rank = jnp.sum(csum * onehot, axis=1) - 1
    padded = (counts + EXPERT_BLOCK - 1) // EXPERT_BLOCK * EXPERT_BLOCK
    pad_end = jnp.cumsum(padded)
    pad_start = pad_end - padded
    slot = pad_start[flat_e] + rank
    n_blocks = -(-(n_assign + N_EXPERTS * (EXPERT_BLOCK - 1)) // EXPERT_BLOCK)
    n_slots = n_blocks * EXPERT_BLOCK
    a = jnp.arange(n_assign, dtype=jnp.int32)
    value = (a % 2) * n_tok + a // 2
    table = jnp.zeros((n_slots,), jnp.int32).at[slot].set(value)
    block_start = jnp.arange(n_blocks, dtype=jnp.int32) * EXPERT_BLOCK
    block_expert = jnp.minimum(jnp.searchsorted(pad_end, block_start, side='right'),
                               N_EXPERTS - 1).astype(jnp.int32)
    n_valid = jnp.clip((pad_start + counts)[block_expert] - block_start, 0, EXPERT_BLOCK).astype(jnp.int32)
    return block_expert, n_valid, table


def _combine(h_ref, y0_ref, y1_ref, gate_ref):
    gate = gate_ref[...]
    return h_ref[...] + (gate[:, 0:1] * _load_row_tiles(y0_ref) + gate[:, 1:2] * _load_row_tiles(y1_ref))


def _emit_head(r, h, outs, scr):
    tm = r.shape[0]
    if scr is not None:
        scr[h] = r
    for ref, dil in outs:
        if dil == 1:
            ref[:, _chunk(h)] = r.astype(ref.dtype)
        else:
            for res in range(dil):
                col = res * D_MODEL + h * HEAD_DIM
                ref[:, col:col + HEAD_DIM] = scr[h, pl.ds(res, tm // dil, stride=dil), :].astype(ref.dtype)


def _rope(xh, cos, sin_signed):
    return xh * cos + pltpu.roll(xh, HEAD_DIM // 2, axis=1) * sin_signed


def _qkv_kernel(*refs, streams):
    (h_ref, y0_ref, y1_ref, gate_ref, gmix_ref, gkv_ref, wq_ref, wkv_ref, cos_ref, sin_ref) = refs[:10]
    outs = refs[10:]
    if streams:
        (h2_ref, q0_ref, q1_ref, q2_ref, k0_ref, k1_ref, k2_ref, v0_ref, v1_ref, v2_ref, kf_ref, vf_ref,
         scr_q1, scr_q2, scr_k, scr_v) = outs
        q_outs = [([(q0_ref, 1)], None), ([(q1_ref, DILATIONS[1])], scr_q1), ([(q2_ref, DILATIONS[2])], scr_q2)]
        k_outs = ([(kf_ref, 1), (k0_ref, 1), (k1_ref, DILATIONS[1]), (k2_ref, DILATIONS[2])], scr_k)
        v_outs = ([(vf_ref, 1), (v0_ref, 1), (v1_ref, DILATIONS[1]), (v2_ref, DILATIONS[2])], scr_v)
    else:
        h2_ref, q0_ref, q1_ref, q2_ref, kf_ref, vf_ref = outs
        q_outs = [([(q0_ref, 1)], None), ([(q1_ref, 1)], None), ([(q2_ref, 1)], None)]
        k_outs = ([(kf_ref, 1)], None)
        v_outs = ([(vf_ref, 1)], None)

    h2 = _combine(h_ref, y0_ref, y1_ref, gate_ref)
    h2_ref[...] = h2
    cos = cos_ref[...]
    sin = sin_ref[...]
    xn = _rms(h2, gmix_ref[...]).astype(_BF16)
    scale = HEAD_DIM ** -0.5
    for g in range(N_GROUPS):
        q = _dot(xn, wq_ref[:, g * D_MODEL:(g + 1) * D_MODEL])
        for h in range(N_HEADS):
            _emit_head(_rope(q[:, _chunk(h)], cos, sin) * scale, h, *q_outs[g])
    xkv = _rms(h2, gkv_ref[...]).astype(_BF16)
    k = _dot(xkv, wkv_ref[:, 0:D_MODEL])
    for h in range(N_HEADS):
        _emit_head(_rope(k[:, _chunk(h)], cos, sin), h, *k_outs)
    v = _dot(xkv, wkv_ref[:, D_MODEL:2 * D_MODEL])
    for h in range(N_HEADS):
        _emit_head(v[:, _chunk(h)], h, *v_outs)


def _qkv(h, ypair, gate, gmix, gkv, wq, wkv, cos, sin, *, tm, streams):
    t, d = h.shape
    nt = t // tm
    row = lambda i: (i, 0)
    const = lambda i: (0, 0)
    nat = lambda dt: (pl.BlockSpec((tm, d), row), jax.ShapeDtypeStruct((t, d), dt))

    def stream(dil):
        return (pl.BlockSpec((tm // dil, dil * d), row), jax.ShapeDtypeStruct((t // dil, dil * d), _BF16))

    if streams:
        d1, d2 = DILATIONS[1], DILATIONS[2]
        outs = [nat(_F32), nat(_BF16), stream(d1), stream(d2), nat(_BF16), stream(d1), stream(d2),
                nat(_BF16), stream(d1), stream(d2), nat(_F32), nat(_F32)]
        scratch = [pltpu.VMEM((N_HEADS, tm, HEAD_DIM), _F32)] * 4
    else:
        outs = [nat(_F32), nat(_BF16), nat(_BF16), nat(_BF16), nat(_F32), nat(_F32)]
        scratch = []
    return pl.pallas_call(
        functools.partial(_qkv_kernel, streams=streams),
        grid=(nt,),
        in_specs=[
            pl.BlockSpec((tm, d), row),
            pl.BlockSpec((tm, N_CHUNKS, LANES), lambda i: (i, 0, 0)),
            pl.BlockSpec((tm, N_CHUNKS, LANES), lambda i: (nt + i, 0, 0)),
            pl.BlockSpec((tm, 2), row),
            pl.BlockSpec((1, d), const),
            pl.BlockSpec((1, d), const),
            pl.BlockSpec((d, 3 * d), const),
            pl.BlockSpec((d, 2 * d), const),
            pl.BlockSpec((tm, HEAD_DIM), row),
            pl.BlockSpec((tm, HEAD_DIM), row),
        ],
        out_specs=[o[0] for o in outs],
        out_shape=[o[1] for o in outs],
        scratch_shapes=scratch,
        compiler_params=pltpu.CompilerParams(
            dimension_semantics=("arbitrary",), vmem_limit_bytes=VMEM_LIMIT),
        name="qkv_proj_streams" if streams else "qkv_proj",
    )(h, ypair, ypair, gate, gmix, gkv, wq, wkv, cos, sin)


ATTN_TQ = 512


def _attn_prompt_kernel(q_ref, kp_ref, kc_ref, vp_ref, vc_ref, o_ref, st_ref, *, tq):
    i = pl.program_id(1)
    qi = lax.broadcasted_iota(jnp.int32, (SPAN, SPAN), 0)
    kj = lax.broadcasted_iota(jnp.int32, (SPAN, SPAN), 1)
    mask_c = kj <= qi
    mask_p = kj >= qi
    mask_p0 = kj >= qi + jnp.where(i > 0, 0, SPAN)
    lane = lax.broadcasted_iota(jnp.int32, (SPAN, HEAD_DIM), 1)
    for a in range(tq // SPAN):
        rows = slice(a * SPAN, (a + 1) * SPAN)
        before = slice((a - 1) * SPAN, a * SPAN)
        st = jnp.zeros((SPAN, HEAD_DIM), _F32)
        for h in range(N_HEADS):
            hs = _chunk(h)
            qh = q_ref[rows, hs]
            if a == 0:
                k_prev, v_prev, m_prev = kp_ref[:, hs], vp_ref[:, hs], mask_p0
            else:
                k_prev, v_prev, m_prev = kc_ref[before, hs], vc_ref[before, hs], mask_p
            sp = jnp.where(m_prev, _dot_nt(qh, k_prev), NEG_BIG)
            sc = jnp.where(mask_c, _dot_nt(qh, kc_ref[rows, hs]), NEG_BIG)
            m = jnp.maximum(jnp.max(sp, axis=1, keepdims=True), jnp.max(sc, axis=1, keepdims=True))
            pp = jnp.exp(sp - m)
            pc = jnp.exp(sc - m)
            l = jnp.sum(pp, axis=1, keepdims=True) + jnp.sum(pc, axis=1, keepdims=True)
            o = _dot(pp.astype(_BF16), v_prev) + _dot(pc.astype(_BF16), vc_ref[rows, hs])
            o_ref[rows, hs] = o / l
            st = jnp.where(lane == h, m + jnp.log(l), st)
        st_ref[rows, :] = st


def _attn_prompt(q, k, v, batch, dilation):
    rows, width = q.shape
    d = width // dilation
    per_batch = rows // batch
    tq = min(ATTN_TQ, per_batch)
    assert per_batch % tq == 0 and tq % SPAN == 0
    nq = per_batch // tq
    sub = tq // SPAN
    cur = lambda b, i, r: (b * nq + i, r)
    prev = lambda b, i, r: ((b * nq + i) * sub - jnp.where(i > 0, 1, 0), r)
    big = (tq, d)
    small = (SPAN, d)
    return pl.pallas_call(
        functools.partial(_attn_prompt_kernel, tq=tq),
        grid=(batch, nq, dilation),
        in_specs=[pl.BlockSpec(big, cur), pl.BlockSpec(small, prev), pl.BlockSpec(big, cur),
                  pl.BlockSpec(small, prev), pl.BlockSpec(big, cur)],
        out_specs=[pl.BlockSpec(big, cur), pl.BlockSpec((tq, HEAD_DIM), cur)],
        out_shape=[jax.ShapeDtypeStruct((rows, width), _F32),
                   jax.ShapeDtypeStruct((rows, dilation * HEAD_DIM), _F32)],
        compiler_params=pltpu.CompilerParams(
            dimension_semantics=("arbitrary", "arbitrary", "arbitrary"), vmem_limit_bytes=VMEM_LIMIT),
        name=f"attn_prompt_d{dilation}",
    )(q, k, k, v, v)


N_NEW = 8
CACHE_A_GROUPS = (PAST_LEN - WINDOWS[1]) // DILATIONS[2]
CACHE_A_ROWS = CACHE_A_GROUPS * N_NEW
CACHE_B_ROWS = WINDOWS[1]
CACHE_B_GROUPS = CACHE_B_ROWS // DILATIONS[2]
KEYS_REAL = CACHE_A_ROWS + CACHE_B_ROWS + N_NEW
KEYS_PAD = -(-KEYS_REAL // 128) * 128


def _sample_key_positions():
    pos = np.full((KEYS_PAD,), -1, np.int64)
    a = np.arange(CACHE_A_ROWS)
    pos[:CACHE_A_ROWS] = (a // N_NEW) * DILATIONS[2] + a % N_NEW
    pos[CACHE_A_ROWS:CACHE_A_ROWS + CACHE_B_ROWS] = PAST_LEN - CACHE_B_ROWS + np.arange(CACHE_B_ROWS)
    pos[CACHE_A_ROWS + CACHE_B_ROWS:KEYS_REAL] = PAST_LEN + np.arange(N_NEW)
    return pos


def _sample_bias():
    pos = _sample_key_positions()
    bias = np.full((N_GROUPS * N_NEW, KEYS_PAD), NEG_BIG, np.float32)
    for g in range(N_GROUPS):
        for n in range(N_NEW):
            delta = PAST_LEN + n - pos
            ok = (pos >= 0) & (delta >= 0) & (delta <= WINDOWS[g]) & (delta % DILATIONS[g] == 0)
            assert int(ok.sum()) == SPAN + 1
            bias[g * N_NEW + n, ok] = 0.0
    return bias


def _attn_sample_kernel(q0_ref, q1_ref, q2_ref, kn_ref, vn_ref, bias_ref, ck_hbm, cv_hbm,
                        o_ref, ka, kb, va, vb, kall, vall, sem, *, n_seq):
    b = pl.program_id(0)
    slot = b % 2

    def cache_copies(seq, s):
        cps = []
        for h in range(N_HEADS):
            for src, dst_a, dst_b in ((ck_hbm, ka, kb), (cv_hbm, va, vb)):
                cps.append(pltpu.make_async_copy(
                    src.at[seq, pl.ds(0, CACHE_A_GROUPS), pl.ds(0, N_NEW), h], dst_a.at[s, h], sem.at[s]))
                cps.append(pltpu.make_async_copy(
                    src.at[seq, pl.ds(CACHE_A_GROUPS, CACHE_B_GROUPS), :, h], dst_b.at[s, h], sem.at[s]))
        return cps

    @pl.when(b == 0)
    def _():
        kall[KEYS_REAL:KEYS_PAD, :] = jnp.zeros((KEYS_PAD - KEYS_REAL, D_MODEL), _BF16)
        vall[KEYS_REAL:KEYS_PAD, :] = jnp.zeros((KEYS_PAD - KEYS_REAL, D_MODEL), _BF16)
        for cp in cache_copies(0, 0):
            cp.start()

    @pl.when(b + 1 < n_seq)
    def _():
        for cp in cache_copies(b + 1, 1 - slot):
            cp.start()

    for cp in cache_copies(b, slot):
        cp.wait()

    nb0 = CACHE_A_ROWS + CACHE_B_ROWS
    for src_a, src_b, src_n, dst in ((ka, kb, kn_ref, kall), (va, vb, vn_ref, vall)):
        for h in range(N_HEADS):
            hs = _chunk(h)
            dst[0:CACHE_A_ROWS, hs] = src_a[slot, h].reshape(CACHE_A_ROWS, HEAD_DIM).astype(_BF16)
            dst[CACHE_A_ROWS:nb0, hs] = src_b[slot, h].reshape(CACHE_B_ROWS, HEAD_DIM).astype(_BF16)
        dst[nb0:KEYS_REAL, :] = src_n[0].astype(_BF16)

    bias = bias_ref[...]
    for h in range(N_HEADS):
        hs = _chunk(h)
        qh = jnp.concatenate([q0_ref[0, :, hs].astype(_F32), q1_ref[0, :, hs].astype(_F32),
                              q2_ref[0, :, hs].astype(_F32)], axis=0).astype(_BF16)
        s = _dot_nt(qh, kall[:, hs]) + bias
        m = jnp.max(s, axis=1, keepdims=True)
        m8 = jnp.maximum(jnp.maximum(m[0:N_NEW], m[N_NEW:2 * N_NEW]), m[2 * N_NEW:3 * N_NEW])
        p = jnp.exp(s - jnp.concatenate([m8, m8, m8], axis=0))
        p8 = p[0:N_NEW] + p[N_NEW:2 * N_NEW] + p[2 * N_NEW:3 * N_NEW]
        l8 = jnp.sum(p8, axis=1, keepdims=True)
        o_ref[0, :, hs] = _dot(p8.astype(_BF16), vall[:, hs]) / l8


def _attn_sample(q0, q1, q2, cache_k, cache_v, k_new, v_new):
    nb, n_new, d = q0.shape
    past = cache_k.shape[1]
    assert past == PAST_LEN and n_new == N_NEW and d == D_MODEL
    assert cache_k.shape[2:] == (N_HEADS, HEAD_DIM)
    ck = cache_k.reshape(nb, past // DILATIONS[2], DILATIONS[2], N_HEADS, HEAD_DIM)
    cv = cache_v.reshape(nb, past // DILATIONS[2], DILATIONS[2], N_HEADS, HEAD_DIM)
    bias = jnp.asarray(_sample_bias())
    new = pl.BlockSpec((1, n_new, d), lambda b: (b, 0, 0))
    part_a = (2, N_HEADS, CACHE_A_GROUPS, N_NEW, HEAD_DIM)
    part_b = (2, N_HEADS, CACHE_B_GROUPS, DILATIONS[2], HEAD_DIM)
    return pl.pallas_call(
        functools.partial(_attn_sample_kernel, n_seq=nb),
        grid=(nb,),
        in_specs=[new, new, new, new, new,
                  pl.BlockSpec((N_GROUPS * N_NEW, KEYS_PAD), lambda b: (0, 0)),
                  pl.BlockSpec(memory_space=pl.ANY), pl.BlockSpec(memory_space=pl.ANY)],
        out_specs=new,
        out_shape=jax.ShapeDtypeStruct((nb, n_new, d), _F32),
        scratch_shapes=[pltpu.VMEM(part_a, _F32), pltpu.VMEM(part_b, _F32),
                        pltpu.VMEM(part_a, _F32), pltpu.VMEM(part_b, _F32),
                        pltpu.VMEM((KEYS_PAD, d), _BF16), pltpu.VMEM((KEYS_PAD, d), _BF16),
                        pltpu.SemaphoreType.DMA((2,))],
        compiler_params=pltpu.CompilerParams(
            dimension_semantics=("arbitrary",), vmem_limit_bytes=VMEM_LIMIT),
        name="attn_sample",
    )(q0, q1, q2, k_new, v_new, bias, ck, cv)


def _attn_out_kernel(*refs, n_groups, tm):
    if n_groups == 1:
        (o_ref, h_ref, wo_ref, gffn_ref, wr_ref, br_ref,
         h3_ref, xn2_ref, eid_ref, gate_ref) = refs
        o = o_ref[...].astype(_BF16)
    else:
        (o0_ref, o1_ref, o2_ref, s0_ref, s1_ref, s2_ref, h_ref, wo_ref, gffn_ref, wr_ref, br_ref,
         h3_ref, xn2_ref, eid_ref, gate_ref, obuf, scr_o1, scr_o2, scr_s1, scr_s2) = refs
        for o_ref, s_ref, scr_o, scr_s, dil in ((o1_ref, s1_ref, scr_o1, scr_s1, DILATIONS[1]),
                                                (o2_ref, s2_ref, scr_o2, scr_s2, DILATIONS[2])):
            for res in range(dil):
                dst = pl.ds(res, tm // dil, stride=dil)
                scr_s[dst, :] = s_ref[:, res * HEAD_DIM:(res + 1) * HEAD_DIM]
                for h in range(N_HEADS):
                    col = res * D_MODEL + h * HEAD_DIM
                    scr_o[h, dst, :] = o_ref[:, col:col + HEAD_DIM]
        sts = [s0_ref[...], scr_s1[...], scr_s2[...]]
        mx = jnp.maximum(jnp.maximum(sts[0], sts[1]), sts[2])
        es = [jnp.exp(s - mx) for s in sts]
        den = es[0] + es[1] + es[2]
        ws = [e / den for e in es]
        for h in range(N_HEADS):
            acc = ws[0][:, h:h + 1] * o0_ref[:, _chunk(h)]
            acc = acc + ws[1][:, h:h + 1] * scr_o1[h]
            acc = acc + ws[2][:, h:h + 1] * scr_o2[h]
            obuf[:, _chunk(h)] = acc.astype(_BF16)
        o = obuf[...]
    h3 = h_ref[...] + _dot(o, wo_ref[...])
    h3_ref[...] = h3
    xn2 = _rms(h3, gffn_ref[...])
    _store_row_tiles(xn2_ref, xn2)
    _route(xn2, wr_ref, br_ref, eid_ref, gate_ref)


def _attn_out(os, sts, h, wo, gffn, wr, br, *, tm):
    t, d = h.shape
    n_groups = len(os)
    row = lambda i: (i, 0)
    const = lambda i: (0, 0)
    if n_groups == 1:
        in_specs = [pl.BlockSpec((tm, d), row)]
        scratch = []
        args = [os[0]]
    else:
        in_specs = [pl.BlockSpec((tm // dil, dil * d), row) for dil in DILATIONS]
        in_specs += [pl.BlockSpec((tm // dil, dil * HEAD_DIM), row) for dil in DILATIONS]
        scratch = [pltpu.VMEM((tm, d), _BF16),
                   pltpu.VMEM((N_HEADS, tm, HEAD_DIM), _F32), pltpu.VMEM((N_HEADS, tm, HEAD_DIM), _F32),
                   pltpu.VMEM((tm, HEAD_DIM), _F32), pltpu.VMEM((tm, HEAD_DIM), _F32)]
        args = list(os) + list(sts)
    in_specs += [pl.BlockSpec((tm, d), row), pl.BlockSpec((d, d), const), pl.BlockSpec((1, d), const),
                 pl.BlockSpec((ROUTER_ROWS, d), const), pl.BlockSpec((ROUTER_ROWS, 1), const)]
    return pl.pallas_call(
        functools.partial(_attn_out_kernel, n_groups=n_groups, tm=tm),
        grid=(t // tm,),
        in_specs=in_specs,
        out_specs=[pl.BlockSpec((tm, d), row), pl.BlockSpec((tm, N_CHUNKS, LANES), lambda i: (i, 0, 0)),
                   pl.BlockSpec((2, tm), lambda i: (0, i)), pl.BlockSpec((2, tm), lambda i: (0, i))],
        out_shape=[jax.ShapeDtypeStruct((t, d), _F32), jax.ShapeDtypeStruct((t, N_CHUNKS, LANES), _F32),
                   jax.ShapeDtypeStruct((2, t), jnp.int32), jax.ShapeDtypeStruct((2, t), _F32)],
        scratch_shapes=scratch,
        compiler_params=pltpu.CompilerParams(
            dimension_semantics=("arbitrary",), vmem_limit_bytes=VMEM_LIMIT),
        name=f"attn_out_g{n_groups}",
    )(*args, h, wo, gffn, wr, br)


def _final_kernel(h_ref, y0_ref, y1_ref, gate_ref, g_ref, out_ref):
    out_ref[...] = _rms(_combine(h_ref, y0_ref, y1_ref, gate_ref), g_ref[...])


def _final(h, ypair, gate, g, *, tm):
    t, d = h.shape
    nt = t // tm
    row = lambda i: (i, 0)
    return pl.pallas_call(
        _final_kernel,
        grid=(nt,),
        in_specs=[pl.BlockSpec((tm, d), row),
                  pl.BlockSpec((tm, N_CHUNKS, LANES), lambda i: (i, 0, 0)),
                  pl.BlockSpec((tm, N_CHUNKS, LANES), lambda i: (nt + i, 0, 0)),
                  pl.BlockSpec((tm, 2), row),
                  pl.BlockSpec((1, d), lambda i: (0, 0))],
        out_specs=pl.BlockSpec((tm, d), row),
        out_shape=jax.ShapeDtypeStruct((t, d), _F32),
        compiler_params=pltpu.CompilerParams(
            dimension_semantics=("arbitrary",), vmem_limit_bytes=VMEM_LIMIT),
        name="final_norm",
    )(h, ypair, ypair, gate, g)


def _rope_tables(pos):
    half = HEAD_DIM // 2
    inv_freq = jnp.power(jnp.float32(ROPE_THETA), -jnp.arange(half, dtype=jnp.float32) / half)
    ang = pos.astype(jnp.float32)[:, None] * inv_freq[None, :]
    cos = jnp.cos(ang)
    sin = jnp.sin(ang)
    return jnp.concatenate([cos, cos], axis=-1), jnp.concatenate([-sin, sin], axis=-1)


def _router_params(wg, bg, we, be):
    wr = jnp.zeros((ROUTER_ROWS, D_MODEL), _F32)
    wr = wr.at[0:MOE_GROUPS].set(wg.T).at[EXPERT_ROW0:EXPERT_ROW0 + N_EXPERTS].set(we.T)
    br = jnp.zeros((ROUTER_ROWS, 1), _F32)
    br = br.at[0:MOE_GROUPS, 0].set(bg).at[EXPERT_ROW0:EXPERT_ROW0 + N_EXPERTS, 0].set(be)
    return wr, br


def _moe(xn2, eid, w1, w3, w2):
    block_expert, n_valid, table = _moe_plan(eid, xn2.shape[0])
    return _moe_ffn(block_expert, n_valid, table, xn2, w1, w3, w2)


def _forward(x3, prev3, pos_rows, attend, p, *, shift, tm, streams):
    h1, xn2, eid, gate, state = _conv_layer(
        x3, prev3, p['gmix'][0], p['win'], p['ck'], p['wout'], p['gffn'][0], *p['router'][0],
        shift=shift, tm=tm)
    t = x3.shape[0] * x3.shape[1]
    ypair = _moe(xn2, eid, *p['experts'][0])
    cos, sin = _rope_tables(pos_rows)
    h2, *qkv, kf, vf = _qkv(h1.reshape(t, D_MODEL), ypair, gate.T, p['gmix'][1], p['gkv'],
                            p['wq'], p['wkv'], cos, sin, tm=256, streams=streams)
    os, sts = attend(qkv, kf, vf)
    h3, xn2, eid, gate = _attn_out(os, sts, h2, p['wo'], p['gffn'][1], *p['router'][1], tm=256)
    ypair = _moe(xn2, eid, *p['experts'][1])
    y = _final(h3, ypair, gate.T, p['gfinal'], tm=512)
    return y, state, kf, vf


def kernel(x_prompt, x_sample, cache_k, cache_v, state_conv, norm_mix, norm_ffn, norm_kv, norm_final,
           conv_w_in, conv_kernel, conv_w_out, attn_w_q, attn_w_kv, attn_w_o, router_group_w,
           router_group_b, router_expert_w, router_expert_b, expert_w1, expert_w3, expert_w2):
    b_p, s_p, d = x_prompt.shape
    b_s, n_new, _ = x_sample.shape
    assert d == D_MODEL and n_new == N_NEW and s_p % (DILATIONS[2] * SPAN) == 0
    assert norm_mix.shape[0] == 2 and conv_w_in.shape[0] == 1 and attn_w_q.shape[0] == 1
    assert cache_k.shape[1] == PAST_LEN

    p = {
        'gmix': [norm_mix[l].reshape(1, d) for l in range(2)],
        'gffn': [norm_ffn[l].reshape(1, d) for l in range(2)],
        'gkv': norm_kv.reshape(1, d),
        'gfinal': norm_final.reshape(1, d),
        'win': conv_w_in[0].astype(_BF16),
        'ck': conv_kernel[0],
        'wout': conv_w_out[0].astype(_BF16),
        'wq': attn_w_q[0].astype(_BF16),
        'wkv': attn_w_kv.astype(_BF16),
        'wo': attn_w_o[0].astype(_BF16),
        'router': [_router_params(router_group_w[l], router_group_b[l], router_expert_w[l], router_expert_b[l])
                   for l in range(2)],
        'experts': [(expert_w1[l].astype(_BF16), expert_w3[l].astype(_BF16), expert_w2[l].astype(_BF16))
                    for l in range(2)],
    }

    def attend_prompt(qkv, kf, vf):
        q0, q1, q2, k0, k1, k2, v0, v1, v2 = qkv
        os, sts = [], []
        for q, k, v, dil in ((q0, k0, v0, DILATIONS[0]), (q1, k1, v1, DILATIONS[1]), (q2, k2, v2, DILATIONS[2])):
            o, st = _attn_prompt(q, k, v, b_p, dil)
            os.append(o)
            sts.append(st)
        return os, sts

    pos_p = jnp.tile(jnp.arange(s_p, dtype=jnp.int32), b_p)
    zero_state = jnp.zeros((b_p, CONV_WIDTH - 1, d), x_prompt.dtype)
    y_p, st_p, kf_p, vf_p = _forward(x_prompt, zero_state, pos_p, attend_prompt, p, shift=1, tm=512,
                                     streams=True)
    keep = min(max(WINDOWS), s_p)
    y_prompt = y_p.reshape(b_p, s_p, d)
    k_p = kf_p.reshape(b_p, s_p, N_HEADS, HEAD_DIM)[:, s_p - keep:]
    v_p = vf_p.reshape(b_p, s_p, N_HEADS, HEAD_DIM)[:, s_p - keep:]
    conv_p = st_p[None]

    halves = 2
    bh = b_s // halves

    def to_rows(a):
        w = a.shape[-1]
        return a.reshape(halves, bh, n_new, w).transpose(0, 2, 1, 3).reshape(halves * n_new * bh, w)

    def to_batch(a):
        w = a.shape[-1]
        return a.reshape(halves, n_new, bh, w).transpose(0, 2, 1, 3).reshape(b_s, n_new, w)

    def attend_sample(qkv, kf, vf):
        q0, q1, q2 = qkv
        o = _attn_sample(to_batch(q0), to_batch(q1), to_batch(q2), cache_k, cache_v,
                         to_batch(kf), to_batch(vf))
        return [to_rows(o)], None

    x_s = to_rows(x_sample).reshape(halves, n_new * bh, d)
    prev_s = state_conv[0].reshape(halves, bh, CONV_WIDTH - 1, d).transpose(0, 2, 1, 3).reshape(
        halves, (CONV_WIDTH - 1) * bh, d)
    pos_s = jnp.tile(jnp.repeat(PAST_LEN + jnp.arange(n_new, dtype=jnp.int32), bh), halves)
    y_s, st_s, kf_s, vf_s = _forward(x_s, prev_s, pos_s, attend_sample, p, shift=bh, tm=n_new * bh,
                                     streams=False)
    y_sample = to_batch(y_s)
    k_s = to_batch(kf_s).reshape(b_s, n_new, N_HEADS, HEAD_DIM)
    v_s = to_batch(vf_s).reshape(b_s, n_new, N_HEADS, HEAD_DIM)
    conv_s = st_s.reshape(halves, CONV_WIDTH - 1, bh, d).transpose(0, 2, 1, 3).reshape(
        b_s, CONV_WIDTH - 1, d)[None]

    return (y_prompt, y_sample, k_p, v_p, conv_p, k_s, v_s, conv_s)
```

```python
import functools

import numpy as np
import jax
import jax.numpy as jnp
from jax import lax
from jax.experimental import pallas as pl
from jax.experimental.pallas import tpu as pltpu

D_MODEL = 1024
CONV_WIDTH = 3
WINDOWS = (128, 512, 2048)
DILATIONS = (1, 4, 16)
N_GROUPS = 3
N_HEADS = 8
HEAD_DIM = 128
SPAN = 128
ROPE_THETA = 10000.0
MOE_GROUPS = 4
EXPERTS_PER_GROUP = 4
N_EXPERTS = 16
D_EXPERT = 512
EXPERT_BLOCK = 128
RMS_EPS = 1e-6
NEG_BIG = -1e30
PAST_LEN = 2048

LANES = 128
N_CHUNKS = D_MODEL // LANES
ROUTER_ROWS = 32
EXPERT_ROW0 = 8
VMEM_LIMIT = 56 * 1024 * 1024

_F32 = jnp.float32
_BF16 = jnp.bfloat16


def _rms(x, g):
    ms = jnp.mean(x * x, axis=-1, keepdims=True)
    return (x * lax.rsqrt(ms + RMS_EPS)) * g


def _dot(a, b):
    return jnp.dot(a, b, preferred_element_type=_F32)


def _dot_nt(a, b, precision=None):
    return lax.dot_general(a, b, (((1,), (1,)), ((), ())), precision=precision,
                           preferred_element_type=_F32)


def _chunk(c):
    return slice(c * LANES, (c + 1) * LANES)


def _store_row_tiles(ref, x, lead=()):
    for c in range(N_CHUNKS):
        ref[lead + (slice(None), c, slice(None))] = x[:, _chunk(c)]


def _load_row_tiles(ref, lead=()):
    return jnp.concatenate([ref[lead + (slice(None), c, slice(None))] for c in range(N_CHUNKS)], axis=1)


def _route(xn, wr_ref, br_ref, eid_ref, gate_ref):
    logits = _dot_nt(wr_ref[...], xn, precision=lax.Precision.HIGHEST) + br_ref[...]
    lg = logits[0:MOE_GROUPS]
    row = lax.broadcasted_iota(jnp.int32, lg.shape, 0).astype(_F32)
    eg = jnp.exp(lg - jnp.max(lg, axis=0, keepdims=True))
    pg = eg / jnp.sum(eg, axis=0, keepdims=True)
    gp = jnp.max(pg, axis=0, keepdims=True)
    gi = jnp.min(jnp.where(pg == gp, row, float(MOE_GROUPS)), axis=0, keepdims=True)
    le = jnp.zeros_like(lg)
    for g in range(MOE_GROUPS):
        r0 = EXPERT_ROW0 + g * EXPERTS_PER_GROUP
        le = le + jnp.where(gi == float(g), logits[r0:r0 + EXPERTS_PER_GROUP], 0.0)
    ee = jnp.exp(le - jnp.max(le, axis=0, keepdims=True))
    ev = ee / jnp.sum(ee, axis=0, keepdims=True)
    v1 = jnp.max(ev, axis=0, keepdims=True)
    i1 = jnp.min(jnp.where(ev == v1, row, float(EXPERTS_PER_GROUP)), axis=0, keepdims=True)
    ev2 = jnp.where(row == i1, -1.0, ev)
    v2 = jnp.max(ev2, axis=0, keepdims=True)
    i2 = jnp.min(jnp.where(ev2 == v2, row, float(EXPERTS_PER_GROUP)), axis=0, keepdims=True)
    den = v1 + v2
    ids = jnp.concatenate([gi * EXPERTS_PER_GROUP + i1, gi * EXPERTS_PER_GROUP + i2], axis=0)
    eid_ref[...] = ids.astype(jnp.int32)
    gate_ref[...] = jnp.concatenate([gp * v1 / den, gp * v2 / den], axis=0)


def _conv_layer_kernel(x_ref, prev_ref, gmix_ref, win_ref, ck_ref, wout_ref, gffn_ref, wr_ref, br_ref,
                       h_ref, xn2_ref, eid_ref, gate_ref, st_ref, ubuf, *, shift, tm):
    i = pl.program_id(1)
    halo = ubuf.shape[0] - tm
    keep = (CONV_WIDTH - 1) * shift

    @pl.when(i == 0)
    def _():
        ubuf[halo - keep:halo, :] = prev_ref[0]

    @pl.when(i > 0)
    def _():
        ubuf[halo - keep:halo, :] = ubuf[halo + tm - keep:halo + tm, :]

    x = x_ref[0]
    xn = _rms(x, gmix_ref[...]).astype(_BF16)
    c_gate = _dot(xn, win_ref[:, 0:D_MODEL])
    hid = _dot(xn, win_ref[:, 2 * D_MODEL:3 * D_MODEL])
    u = c_gate * hid
    ubuf[halo:halo + tm, :] = u
    u1 = ubuf[halo - shift:halo - shift + tm, :]
    u2 = ubuf[halo - 2 * shift:halo - 2 * shift + tm, :]
    conv = ck_ref[0:1, :] * u2 + ck_ref[1:2, :] * u1 + ck_ref[2:3, :] * u
    b_gate = _dot(xn, win_ref[:, D_MODEL:2 * D_MODEL])
    y = _dot((b_gate * conv).astype(_BF16), wout_ref[...])
    h = x + y
    h_ref[0] = h
    st_ref[0] = ubuf[halo + tm - keep:halo + tm, :]
    xn2 = _rms(h, gffn_ref[...])
    _store_row_tiles(xn2_ref, xn2)
    _route(xn2, wr_ref, br_ref, eid_ref, gate_ref)


def _conv_layer(x3, prev3, gmix, win, ck, wout, gffn, wr, br, *, shift, tm):
    nb, s, d = x3.shape
    nt = s // tm
    keep = (CONV_WIDTH - 1) * shift
    halo = -(-keep // 8) * 8
    t_all = nb * s
    const = lambda b, i: (0, 0)
    return pl.pallas_call(
        functools.partial(_conv_layer_kernel, shift=shift, tm=tm),
        grid=(nb, nt),
        in_specs=[
            pl.BlockSpec((1, tm, d), lambda b, i: (b, i, 0)),
            pl.BlockSpec((1, keep, d), lambda b, i: (b, 0, 0)),
            pl.BlockSpec((1, d), const),
            pl.BlockSpec((d, 3 * d), const),
            pl.BlockSpec((CONV_WIDTH, d), const),
            pl.BlockSpec((d, d), const),
            pl.BlockSpec((1, d), const),
            pl.BlockSpec((ROUTER_ROWS, d), const),
            pl.BlockSpec((ROUTER_ROWS, 1), const),
        ],
        out_specs=[
            pl.BlockSpec((1, tm, d), lambda b, i: (b, i, 0)),
            pl.BlockSpec((tm, N_CHUNKS, LANES), lambda b, i: (b * nt + i, 0, 0)),
            pl.BlockSpec((2, tm), lambda b, i: (0, b * nt + i)),
            pl.BlockSpec((2, tm), lambda b, i: (0, b * nt + i)),
            pl.BlockSpec((1, keep, d), lambda b, i: (b, 0, 0)),
        ],
        out_shape=[
            jax.ShapeDtypeStruct((nb, s, d), _F32),
            jax.ShapeDtypeStruct((t_all, N_CHUNKS, LANES), _F32),
            jax.ShapeDtypeStruct((2, t_all), jnp.int32),
            jax.ShapeDtypeStruct((2, t_all), _F32),
            jax.ShapeDtypeStruct((nb, keep, d), _F32),
        ],
        scratch_shapes=[pltpu.VMEM((halo + tm, d), _F32)],
        compiler_params=pltpu.CompilerParams(
            dimension_semantics=("arbitrary", "arbitrary"), vmem_limit_bytes=VMEM_LIMIT),
        name="conv_layer",
    )(x3, prev3, gmix, win, ck, wout, gffn, wr, br)


def _moe_ffn_kernel(be_ref, nv_ref, tab_ref, x_hbm, w1_ref, w3_ref, w2_ref, y_hbm,
                    xbuf, obuf, gsem, ssem, *, n_tok, n_blocks):
    j = pl.program_id(0)
    slot = j % 2
    other = 1 - slot
    half = EXPERT_BLOCK // 2

    def token_of(v):
        if n_tok & (n_tok - 1) == 0:
            return v & (n_tok - 1)
        return v - jnp.where(v >= n_tok, n_tok, 0)

    def gather_rows(blk, s, k0, k1):
        for k in range(k0, k1):
            tok = token_of(tab_ref[blk * EXPERT_BLOCK + k])
            pltpu.make_async_copy(x_hbm.at[tok], xbuf.at[s, k], gsem.at[s]).start(priority=k % 2)

    def scatter_rows(blk, s, n, k0, k1):
        for k in range(k0, k1):
            @pl.when(k < n)
            def _():
                v = tab_ref[blk * EXPERT_BLOCK + k]
                pltpu.make_async_copy(obuf.at[s, k], y_hbm.at[v], ssem.at[s]).start(priority=k % 2)

    def wait_gather(s):
        pltpu.make_async_copy(x_hbm.at[pl.ds(0, EXPERT_BLOCK)], xbuf.at[s], gsem.at[s]).wait()

    def wait_scatter(n, s):
        @pl.when(n > 0)
        def _():
            pltpu.make_async_copy(obuf.at[s, pl.ds(0, n)], y_hbm.at[pl.ds(0, n)], ssem.at[s]).wait()

    @pl.when(j == 0)
    def _():
        gather_rows(0, 0, 0, EXPERT_BLOCK)

    nxt = jnp.minimum(j + 1, n_blocks - 1)
    prv = jnp.maximum(j - 1, 0)
    n_prev = jnp.where(j > 0, nv_ref[prv], 0)

    wait_gather(slot)
    x = _load_row_tiles(xbuf, (slot,)).astype(_BF16)
    gather_rows(nxt, other, 0, half)
    a = _dot(x, w1_ref[0])
    gather_rows(nxt, other, half, EXPERT_BLOCK)
    b = _dot(x, w3_ref[0])
    scatter_rows(prv, other, n_prev, 0, half)
    hmid = (jax.nn.silu(a) * b).astype(_BF16)
    y = _dot(hmid, w2_ref[0])
    scatter_rows(prv, other, n_prev, half, EXPERT_BLOCK)

    @pl.when(j >= 2)
    def _():
        wait_scatter(nv_ref[jnp.maximum(j - 2, 0)], slot)

    _store_row_tiles(obuf, y, (slot,))

    @pl.when(j == n_blocks - 1)
    def _():
        wait_gather(other)
        wait_scatter(n_prev, other)
        n_last = nv_ref[j]
        scatter_rows(j, slot, n_last, 0, EXPERT_BLOCK)
        wait_scatter(n_last, slot)


def _moe_ffn(block_expert, n_valid, table, xn, w1, w3, w2):
    n_tok = xn.shape[0]
    d = D_MODEL
    n_blocks = block_expert.shape[0]
    assert n_blocks >= 2 and xn.shape[1:] == (N_CHUNKS, LANES)
    wmap = lambda j, be, nv, tab: (be[j], 0, 0)
    return pl.pallas_call(
        functools.partial(_moe_ffn_kernel, n_tok=n_tok, n_blocks=n_blocks),
        grid_spec=pltpu.PrefetchScalarGridSpec(
            num_scalar_prefetch=3,
            grid=(n_blocks,),
            in_specs=[
                pl.BlockSpec(memory_space=pl.ANY),
                pl.BlockSpec((1, d, D_EXPERT), wmap),
                pl.BlockSpec((1, d, D_EXPERT), wmap),
                pl.BlockSpec((1, D_EXPERT, d), wmap),
            ],
            out_specs=pl.BlockSpec(memory_space=pl.ANY),
            scratch_shapes=[
                pltpu.VMEM((2, EXPERT_BLOCK, N_CHUNKS, LANES), _F32),
                pltpu.VMEM((2, EXPERT_BLOCK, N_CHUNKS, LANES), _F32),
                pltpu.SemaphoreType.DMA((2,)),
                pltpu.SemaphoreType.DMA((2,)),
            ],
        ),
        out_shape=jax.ShapeDtypeStruct((2 * n_tok, N_CHUNKS, LANES), _F32),
        compiler_params=pltpu.CompilerParams(
            dimension_semantics=("arbitrary",), vmem_limit_bytes=VMEM_LIMIT),
        name="moe_ffn",
    )(block_expert, n_valid, table, xn, w1, w3, w2)


def _moe_plan(eid, n_tok):
    n_assign = 2 * n_tok
    flat_e = eid.T.reshape(-1)
    onehot = (flat_e[:, None] == jnp.arange(N_EXPERTS, dtype=jnp.int32)[None, :]).astype(jnp.int32)
    csum = jnp.cumsum(onehot, axis=0)
    counts = csum[-1]
    rank = jnp.sum(csum * onehot, axis=1) - 1
    padded = (counts + EXPERT_BLOCK - 1) // EXPERT_BLOCK * EXPERT_BLOCK
    pad_end = jnp.cumsum(padded)
    pad_start = pad_end - padded
    slot = pad_start[flat_e] + rank
    n_blocks = -(-(n_assign + N_EXPERTS * (EXPERT_BLOCK - 1)) // EXPERT_BLOCK)
    n_slots = n_blocks * EXPERT_BLOCK
    a = jnp.arange(n_assign, dtype=jnp.int32)
    value = (a % 2) * n_tok + a // 2
    table = jnp.zeros((n_slots,), jnp.int32).at[slot].set(value)
    block_start = jnp.arange(n_blocks, dtype=jnp.int32) * EXPERT_BLOCK
    block_expert = jnp.minimum(jnp.sum((pad_end[None, :] <= block_start[:, None]).astype(jnp.int32), axis=1),
                               N_EXPERTS - 1)
    n_valid = jnp.clip((pad_start + counts)[block_expert] - block_start, 0, EXPERT_BLOCK).astype(jnp.int32)
    return block_expert, n_valid, table


def _combine(h_ref, y0_ref, y1_ref, gate_ref):
    gate = gate_ref[...]
    return h_ref[...] + (gate[:, 0:1] * _load_row_tiles(y0_ref) + gate[:, 1:2] * _load_row_tiles(y1_ref))


def _emit_head(r, h, outs, scr):
    tm = r.shape[0]
    if scr is not None:
        scr[h] = r
    for ref, dil in outs:
        if dil == 0:
            ref[:, h, :] = r
        elif dil == 1:
            ref[:, _chunk(h)] = r.astype(ref.dtype)
        else:
            for res in range(dil):
                col = res * D_MODEL + h * HEAD_DIM
                ref[:, col:col + HEAD_DIM] = scr[h, pl.ds(res, tm // dil, stride=dil), :].astype(ref.dtype)


def _rope(xh, cos, sin_signed):
    return xh * cos + pltpu.roll(xh, HEAD_DIM // 2, axis=1) * sin_signed


def _qkv_kernel(*refs, streams, per_seq, kept):
    (h_ref, y0_ref, y1_ref, gate_ref, gmix_ref, gkv_ref, wq_ref, wkv_ref, cos_ref, sin_ref) = refs[:10]
    outs = refs[10:]
    if streams:
        (h2_ref, q0_ref, q1_ref, q2_ref, k0_ref, k1_ref, k2_ref, v0_ref, v1_ref, v2_ref, kf_ref, vf_ref,
         scr_q1, scr_q2, scr_k, scr_v) = outs
        q_outs = [([(q0_ref, 1)], None), ([(q1_ref, DILATIONS[1])], scr_q1), ([(q2_ref, DILATIONS[2])], scr_q2)]
        k_outs = ([(k0_ref, 1), (k1_ref, DILATIONS[1]), (k2_ref, DILATIONS[2])], scr_k)
        v_outs = ([(v0_ref, 1), (v1_ref, DILATIONS[1]), (v2_ref, DILATIONS[2])], scr_v)
    else:
        h2_ref, q0_ref, q1_ref, q2_ref, kf_ref, vf_ref = outs
        q_outs = [([(q0_ref, 1)], None), ([(q1_ref, 1)], None), ([(q2_ref, 1)], None)]
        k_outs = ([(kf_ref, 0)], None)
        v_outs = ([(vf_ref, 0)], None)

    h2 = _combine(h_ref, y0_ref, y1_ref, gate_ref)
    h2_ref[...] = h2
    cos = cos_ref[...]
    sin = sin_ref[...]
    xn = _rms(h2, gmix_ref[...]).astype(_BF16)
    scale = HEAD_DIM ** -0.5
    for g in range(N_GROUPS):
        q = _dot(xn, wq_ref[:, g * D_MODEL:(g + 1) * D_MODEL])
        for h in range(N_HEADS):
            _emit_head(_rope(q[:, _chunk(h)], cos, sin) * scale, h, *q_outs[g])
    xkv = _rms(h2, gkv_ref[...]).astype(_BF16)
    k = _dot(xkv, wkv_ref[:, 0:D_MODEL])
    for h in range(N_HEADS):
        _emit_head(_rope(k[:, _chunk(h)], cos, sin), h, *k_outs)
    v = _dot(xkv, wkv_ref[:, D_MODEL:2 * D_MODEL])
    for h in range(N_HEADS):
        _emit_head(v[:, _chunk(h)], h, *v_outs)
    if streams:
        @pl.when(pl.program_id(0) % per_seq >= per_seq - kept)
        def _():
            for h in range(N_HEADS):
                kf_ref[:, h, :] = scr_k[h]
                vf_ref[:, h, :] = scr_v[h]


def _qkv(h, ypair, gate, gmix, gkv, wq, wkv, cos, sin, *, tm, streams, seq, keep):
    t, d = h.shape
    nt = t // tm
    per_seq = seq // tm
    kept = keep // tm
    assert seq % tm == 0 and keep % tm == 0

    def tail(i):
        return ((i // per_seq) * kept + jnp.maximum(i % per_seq - (per_seq - kept), 0), 0, 0)

    tail_f32 = (pl.BlockSpec((tm, N_HEADS, HEAD_DIM), tail),
                jax.ShapeDtypeStruct((t // seq * keep, N_HEADS, HEAD_DIM), _F32))
    row = lambda i: (i, 0)
    const = lambda i: (0, 0)
    nat = lambda dt: (pl.BlockSpec((tm, d), row), jax.ShapeDtypeStruct((t, d), dt))

    def stream(dil):
        return (pl.BlockSpec((tm // dil, dil * d), row), jax.ShapeDtypeStruct((t // dil, dil * d), _BF16))

    if streams:
        d1, d2 = DILATIONS[1], DILATIONS[2]
        outs = [nat(_F32), nat(_BF16), stream(d1), stream(d2), nat(_BF16), stream(d1), stream(d2),
                nat(_BF16), stream(d1), stream(d2), tail_f32, tail_f32]
        scratch = [pltpu.VMEM((N_HEADS, tm, HEAD_DIM), _F32)] * 4
    else:
        outs = [nat(_F32), nat(_BF16), nat(_BF16), nat(_BF16), tail_f32, tail_f32]
        scratch = []
    return pl.pallas_call(
        functools.partial(_qkv_kernel, streams=streams, per_seq=per_seq, kept=kept),
        grid=(nt,),
        in_specs=[
            pl.BlockSpec((tm, d), row),
            pl.BlockSpec((tm, N_CHUNKS, LANES), lambda i: (i, 0, 0)),
            pl.BlockSpec((tm, N_CHUNKS, LANES), lambda i: (nt + i, 0, 0)),
            pl.BlockSpec((tm, 2), row),
            pl.BlockSpec((1, d), const),
            pl.BlockSpec((1, d), const),
            pl.BlockSpec((d, 3 * d), const),
            pl.BlockSpec((d, 2 * d), const),
            pl.BlockSpec((tm, HEAD_DIM), row),
            pl.BlockSpec((tm, HEAD_DIM), row),
        ],
        out_specs=[o[0] for o in outs],
        out_shape=[o[1] for o in outs],
        scratch_shapes=scratch,
        compiler_params=pltpu.CompilerParams(
            dimension_semantics=("arbitrary",), vmem_limit_bytes=VMEM_LIMIT),
        name="qkv_proj_streams" if streams else "qkv_proj",
    )(h, ypair, ypair, gate, gmix, gkv, wq, wkv, cos, sin)


ATTN_TQ = 512


def _attn_prompt_kernel(q_ref, kp_ref, kc_ref, vp_ref, vc_ref, o_ref, st_ref, *, tq):
    i = pl.program_id(1)
    qi = lax.broadcasted_iota(jnp.int32, (SPAN, SPAN), 0)
    kj = lax.broadcasted_iota(jnp.int32, (SPAN, SPAN), 1)
    mask_c = kj <= qi
    mask_p = kj >= qi
    mask_p0 = kj >= qi + jnp.where(i > 0, 0, SPAN)
    lane = lax.broadcasted_iota(jnp.int32, (SPAN, HEAD_DIM), 1)
    for a in range(tq // SPAN):
        rows = slice(a * SPAN, (a + 1) * SPAN)
        before = slice((a - 1) * SPAN, a * SPAN)
        st = jnp.zeros((SPAN, HEAD_DIM), _F32)
        for h in range(N_HEADS):
            hs = _chunk(h)
            qh = q_ref[rows, hs]
            if a == 0:
                k_prev, v_prev, m_prev = kp_ref[:, hs], vp_ref[:, hs], mask_p0
            else:
                k_prev, v_prev, m_prev = kc_ref[before, hs], vc_ref[before, hs], mask_p
            sp = jnp.where(m_prev, _dot_nt(qh, k_prev), NEG_BIG)
            sc = jnp.where(mask_c, _dot_nt(qh, kc_ref[rows, hs]), NEG_BIG)
            m = jnp.maximum(jnp.max(sp, axis=1, keepdims=True), jnp.max(sc, axis=1, keepdims=True))
            pp = jnp.exp(sp - m)
            pc = jnp.exp(sc - m)
            l = jnp.sum(pp, axis=1, keepdims=True) + jnp.sum(pc, axis=1, keepdims=True)
            o = _dot(pp.astype(_BF16), v_prev) + _dot(pc.astype(_BF16), vc_ref[rows, hs])
            o_ref[rows, hs] = o / l
            st = jnp.where(lane == h, m + jnp.log(l), st)
        st_ref[rows, :] = st


def _attn_prompt(q, k, v, batch, dilation):
    rows, width = q.shape
    d = width // dilation
    per_batch = rows // batch
    tq = min(ATTN_TQ, per_batch)
    assert per_batch % tq == 0 and tq % SPAN == 0
    nq = per_batch // tq
    sub = tq // SPAN
    cur = lambda b, i, r: (b * nq + i, r)
    prev = lambda b, i, r: ((b * nq + i) * sub - jnp.where(i > 0, 1, 0), r)
    big = (tq, d)
    small = (SPAN, d)
    return pl.pallas_call(
        functools.partial(_attn_prompt_kernel, tq=tq),
        grid=(batch, nq, dilation),
        in_specs=[pl.BlockSpec(big, cur), pl.BlockSpec(small, prev), pl.BlockSpec(big, cur),
                  pl.BlockSpec(small, prev), pl.BlockSpec(big, cur)],
        out_specs=[pl.BlockSpec(big, cur), pl.BlockSpec((tq, HEAD_DIM), cur)],
        out_shape=[jax.ShapeDtypeStruct((rows, width), _F32),
                   jax.ShapeDtypeStruct((rows, dilation * HEAD_DIM), _F32)],
        compiler_params=pltpu.CompilerParams(
            dimension_semantics=("arbitrary", "arbitrary", "arbitrary"), vmem_limit_bytes=VMEM_LIMIT),
        name=f"attn_prompt_d{dilation}",
    )(q, k, k, v, v)


N_NEW = 8
CACHE_A_GROUPS = (PAST_LEN - WINDOWS[1]) // DILATIONS[2]
CACHE_A_ROWS = CACHE_A_GROUPS * N_NEW
CACHE_B_ROWS = WINDOWS[1]
CACHE_B_GROUPS = CACHE_B_ROWS // DILATIONS[2]
KEYS_REAL = CACHE_A_ROWS + CACHE_B_ROWS + N_NEW
KEYS_PAD = -(-KEYS_REAL // 128) * 128


def _sample_key_positions():
    pos = np.full((KEYS_PAD,), -1, np.int64)
    a = np.arange(CACHE_A_ROWS)
    pos[:CACHE_A_ROWS] = (a // N_NEW) * DILATIONS[2] + a % N_NEW
    pos[CACHE_A_ROWS:CACHE_A_ROWS + CACHE_B_ROWS] = PAST_LEN - CACHE_B_ROWS + np.arange(CACHE_B_ROWS)
    pos[CACHE_A_ROWS + CACHE_B_ROWS:KEYS_REAL] = PAST_LEN + np.arange(N_NEW)
    return pos


def _sample_bias():
    pos = _sample_key_positions()
    bias = np.full((N_GROUPS * N_NEW, KEYS_PAD), NEG_BIG, np.float32)
    for g in range(N_GROUPS):
        for n in range(N_NEW):
            delta = PAST_LEN + n - pos
            ok = (pos >= 0) & (delta >= 0) & (delta <= WINDOWS[g]) & (delta % DILATIONS[g] == 0)
            assert int(ok.sum()) == SPAN + 1
            bias[g * N_NEW + n, ok] = 0.0
    return bias


def _attn_sample_kernel(q0_ref, q1_ref, q2_ref, kn_ref, vn_ref, bias_ref, ck_hbm, cv_hbm,
                        o_ref, ka, kb, va, vb, kall, vall, sem, *, n_seq):
    b = pl.program_id(0)
    slot = b % 2

    def cache_copies(seq, s):
        cps = []
        for h in range(N_HEADS):
            for src, dst_a, dst_b in ((ck_hbm, ka, kb), (cv_hbm, va, vb)):
                cps.append(pltpu.make_async_copy(
                    src.at[seq, pl.ds(0, CACHE_A_GROUPS), pl.ds(0, N_NEW), h], dst_a.at[s, h], sem.at[s]))
                cps.append(pltpu.make_async_copy(
                    src.at[seq, pl.ds(CACHE_A_GROUPS, CACHE_B_GROUPS), :, h], dst_b.at[s, h], sem.at[s]))
        return cps

    @pl.when(b == 0)
    def _():
        kall[KEYS_REAL:KEYS_PAD, :] = jnp.zeros((KEYS_PAD - KEYS_REAL, D_MODEL), _BF16)
        vall[KEYS_REAL:KEYS_PAD, :] = jnp.zeros((KEYS_PAD - KEYS_REAL, D_MODEL), _BF16)
        for cp in cache_copies(0, 0):
            cp.start()

    @pl.when(b + 1 < n_seq)
    def _():
        for cp in cache_copies(b + 1, 1 - slot):
            cp.start()

    for cp in cache_copies(b, slot):
        cp.wait()

    nb0 = CACHE_A_ROWS + CACHE_B_ROWS
    for src_a, src_b, src_n, dst in ((ka, kb, kn_ref, kall), (va, vb, vn_ref, vall)):
        for h in range(N_HEADS):
            hs = _chunk(h)
            dst[0:CACHE_A_ROWS, hs] = src_a[slot, h].reshape(CACHE_A_ROWS, HEAD_DIM).astype(_BF16)
            dst[CACHE_A_ROWS:nb0, hs] = src_b[slot, h].reshape(CACHE_B_ROWS, HEAD_DIM).astype(_BF16)
            dst[nb0:KEYS_REAL, hs] = src_n[0, :, h, :].astype(_BF16)

    bias = bias_ref[...]
    for h in range(N_HEADS):
        hs = _chunk(h)
        qh = jnp.concatenate([q0_ref[0, :, hs].astype(_F32), q1_ref[0, :, hs].astype(_F32),
                              q2_ref[0, :, hs].astype(_F32)], axis=0).astype(_BF16)
        s = _dot_nt(qh, kall[:, hs]) + bias
        m = jnp.max(s, axis=1, keepdims=True)
        m8 = jnp.maximum(jnp.maximum(m[0:N_NEW], m[N_NEW:2 * N_NEW]), m[2 * N_NEW:3 * N_NEW])
        p = jnp.exp(s - jnp.concatenate([m8, m8, m8], axis=0))
        p8 = p[0:N_NEW] + p[N_NEW:2 * N_NEW] + p[2 * N_NEW:3 * N_NEW]
        l8 = jnp.sum(p8, axis=1, keepdims=True)
        o_ref[0, :, hs] = _dot(p8.astype(_BF16), vall[:, hs]) / l8


def _attn_sample(q0, q1, q2, cache_k, cache_v, k_new, v_new):
    nb, n_new, d = q0.shape
    past = cache_k.shape[1]
    assert past == PAST_LEN and n_new == N_NEW and d == D_MODEL
    assert cache_k.shape[2:] == (N_HEADS, HEAD_DIM)
    ck = cache_k.reshape(nb, past // DILATIONS[2], DILATIONS[2], N_HEADS, HEAD_DIM)
    cv = cache_v.reshape(nb, past // DILATIONS[2], DILATIONS[2], N_HEADS, HEAD_DIM)
    bias = jnp.asarray(_sample_bias())
    new = pl.BlockSpec((1, n_new, d), lambda b: (b, 0, 0))
    new_kv = pl.BlockSpec((1, n_new, N_HEADS, HEAD_DIM), lambda b: (b, 0, 0, 0))
    part_a = (2, N_HEADS, CACHE_A_GROUPS, N_NEW, HEAD_DIM)
    part_b = (2, N_HEADS, CACHE_B_GROUPS, DILATIONS[2], HEAD_DIM)
    return pl.pallas_call(
        functools.partial(_attn_sample_kernel, n_seq=nb),
        grid=(nb,),
        in_specs=[new, new, new, new_kv, new_kv,
                  pl.BlockSpec((N_GROUPS * N_NEW, KEYS_PAD), lambda b: (0, 0)),
                  pl.BlockSpec(memory_space=pl.ANY), pl.BlockSpec(memory_space=pl.ANY)],
        out_specs=new,
        out_shape=jax.ShapeDtypeStruct((nb, n_new, d), _F32),
        scratch_shapes=[pltpu.VMEM(part_a, _F32), pltpu.VMEM(part_b, _F32),
                        pltpu.VMEM(part_a, _F32), pltpu.VMEM(part_b, _F32),
                        pltpu.VMEM((KEYS_PAD, d), _BF16), pltpu.VMEM((KEYS_PAD, d), _BF16),
                        pltpu.SemaphoreType.DMA((2,))],
        compiler_params=pltpu.CompilerParams(
            dimension_semantics=("arbitrary",), vmem_limit_bytes=VMEM_LIMIT),
        name="attn_sample",
    )(q0, q1, q2, k_new, v_new, bias, ck, cv)


def _attn_out_kernel(*refs, n_groups, tm):
    if n_groups == 1:
        (o_ref, h_ref, wo_ref, gffn_ref, wr_ref, br_ref,
         h3_ref, xn2_ref, eid_ref, gate_ref) = refs
        o = o_ref[...].astype(_BF16)
    else:
        (o0_ref, o1_ref, o2_ref, s0_ref, s1_ref, s2_ref, h_ref, wo_ref, gffn_ref, wr_ref, br_ref,
         h3_ref, xn2_ref, eid_ref, gate_ref, obuf, scr_o1, scr_o2, scr_s1, scr_s2) = refs
        for o_ref, s_ref, scr_o, scr_s, dil in ((o1_ref, s1_ref, scr_o1, scr_s1, DILATIONS[1]),
                                                (o2_ref, s2_ref, scr_o2, scr_s2, DILATIONS[2])):
            for res in range(dil):
                dst = pl.ds(res, tm // dil, stride=dil)
                scr_s[dst, :] = s_ref[:, res * HEAD_DIM:(res + 1) * HEAD_DIM]
                for h in range(N_HEADS):
                    col = res * D_MODEL + h * HEAD_DIM
                    scr_o[h, dst, :] = o_ref[:, col:col + HEAD_DIM]
        sts = [s0_ref[...], scr_s1[...], scr_s2[...]]
        mx = jnp.maximum(jnp.maximum(sts[0], sts[1]), sts[2])
        es = [jnp.exp(s - mx) for s in sts]
        den = es[0] + es[1] + es[2]
        ws = [e / den for e in es]
        for h in range(N_HEADS):
            acc = ws[0][:, h:h + 1] * o0_ref[:, _chunk(h)]
            acc = acc + ws[1][:, h:h + 1] * scr_o1[h]
            acc = acc + ws[2][:, h:h + 1] * scr_o2[h]
            obuf[:, _chunk(h)] = acc.astype(_BF16)
        o = obuf[...]
    h3 = h_ref[...] + _dot(o, wo_ref[...])
    h3_ref[...] = h3
    xn2 = _rms(h3, gffn_ref[...])
    _store_row_tiles(xn2_ref, xn2)
    _route(xn2, wr_ref, br_ref, eid_ref, gate_ref)


def _attn_out(os, sts, h, wo, gffn, wr, br, *, tm):
    t, d = h.shape
    n_groups = len(os)
    row = lambda i: (i, 0)
    const = lambda i: (0, 0)
    if n_groups == 1:
        in_specs = [pl.BlockSpec((tm, d), row)]
        scratch = []
        args = [os[0]]
    else:
        in_specs = [pl.BlockSpec((tm // dil, dil * d), row) for dil in DILATIONS]
        in_specs += [pl.BlockSpec((tm // dil, dil * HEAD_DIM), row) for dil in DILATIONS]
        scratch = [pltpu.VMEM((tm, d), _BF16),
                   pltpu.VMEM((N_HEADS, tm, HEAD_DIM), _F32), pltpu.VMEM((N_HEADS, tm, HEAD_DIM), _F32),
                   pltpu.VMEM((tm, HEAD_DIM), _F32), pltpu.VMEM((tm, HEAD_DIM), _F32)]
        args = list(os) + list(sts)
    in_specs += [pl.BlockSpec((tm, d), row), pl.BlockSpec((d, d), const), pl.BlockSpec((1, d), const),
                 pl.BlockSpec((ROUTER_ROWS, d), const), pl.BlockSpec((ROUTER_ROWS, 1), const)]
    return pl.pallas_call(
        functools.partial(_attn_out_kernel, n_groups=n_groups, tm=tm),
        grid=(t // tm,),
        in_specs=in_specs,
        out_specs=[pl.BlockSpec((tm, d), row), pl.BlockSpec((tm, N_CHUNKS, LANES), lambda i: (i, 0, 0)),
                   pl.BlockSpec((2, tm), lambda i: (0, i)), pl.BlockSpec((2, tm), lambda i: (0, i))],
        out_shape=[jax.ShapeDtypeStruct((t, d), _F32), jax.ShapeDtypeStruct((t, N_CHUNKS, LANES), _F32),
                   jax.ShapeDtypeStruct((2, t), jnp.int32), jax.ShapeDtypeStruct((2, t), _F32)],
        scratch_shapes=scratch,
        compiler_params=pltpu.CompilerParams(
            dimension_semantics=("arbitrary",), vmem_limit_bytes=VMEM_LIMIT),
        name=f"attn_out_g{n_groups}",
    )(*args, h, wo, gffn, wr, br)


def _final_kernel(h_ref, y0_ref, y1_ref, gate_ref, g_ref, out_ref):
    out_ref[...] = _rms(_combine(h_ref, y0_ref, y1_ref, gate_ref), g_ref[...])


def _final(h, ypair, gate, g, *, tm):
    t, d = h.shape
    nt = t // tm
    row = lambda i: (i, 0)
    return pl.pallas_call(
        _final_kernel,
        grid=(nt,),
        in_specs=[pl.BlockSpec((tm, d), row),
                  pl.BlockSpec((tm, N_CHUNKS, LANES), lambda i: (i, 0, 0)),
                  pl.BlockSpec((tm, N_CHUNKS, LANES), lambda i: (nt + i, 0, 0)),
                  pl.BlockSpec((tm, 2), row),
                  pl.BlockSpec((1, d), lambda i: (0, 0))],
        out_specs=pl.BlockSpec((tm, d), row),
        out_shape=jax.ShapeDtypeStruct((t, d), _F32),
        compiler_params=pltpu.CompilerParams(
            dimension_semantics=("arbitrary",), vmem_limit_bytes=VMEM_LIMIT),
        name="final_norm",
    )(h, ypair, ypair, gate, g)


def _rope_tables(pos):
    half = HEAD_DIM // 2
    inv_freq = jnp.power(jnp.float32(ROPE_THETA), -jnp.arange(half, dtype=jnp.float32) / half)
    ang = pos.astype(jnp.float32)[:, None] * inv_freq[None, :]
    cos = jnp.cos(ang)
    sin = jnp.sin(ang)
    return jnp.concatenate([cos, cos], axis=-1), jnp.concatenate([-sin, sin], axis=-1)


def _router_params(wg, bg, we, be):
    wr = jnp.zeros((ROUTER_ROWS, D_MODEL), _F32)
    wr = wr.at[0:MOE_GROUPS].set(wg.T).at[EXPERT_ROW0:EXPERT_ROW0 + N_EXPERTS].set(we.T)
    br = jnp.zeros((ROUTER_ROWS, 1), _F32)
    br = br.at[0:MOE_GROUPS, 0].set(bg).at[EXPERT_ROW0:EXPERT_ROW0 + N_EXPERTS, 0].set(be)
    return wr, br


def _moe(xn2, eid, w1, w3, w2):
    block_expert, n_valid, table = _moe_plan(eid, xn2.shape[0])
    return _moe_ffn(block_expert, n_valid, table, xn2, w1, w3, w2)


def _forward(x3, prev3, pos_rows, attend, p, *, shift, tm, streams, keep):
    h1, xn2, eid, gate, state = _conv_layer(
        x3, prev3, p['gmix'][0], p['win'], p['ck'], p['wout'], p['gffn'][0], *p['router'][0],
        shift=shift, tm=tm)
    t = x3.shape[0] * x3.shape[1]
    ypair = _moe(xn2, eid, *p['experts'][0])
    cos, sin = _rope_tables(pos_rows)
    h2, *qkv, kf, vf = _qkv(h1.reshape(t, D_MODEL), ypair, gate.T, p['gmix'][1], p['gkv'],
                            p['wq'], p['wkv'], cos, sin, tm=256, streams=streams,
                            seq=x3.shape[1], keep=keep)
    os, sts = attend(qkv, kf, vf)
    h3, xn2, eid, gate = _attn_out(os, sts, h2, p['wo'], p['gffn'][1], *p['router'][1], tm=256)
    ypair = _moe(xn2, eid, *p['experts'][1])
    y = _final(h3, ypair, gate.T, p['gfinal'], tm=512)
    return y, state, kf, vf


def kernel(x_prompt, x_sample, cache_k, cache_v, state_conv, norm_mix, norm_ffn, norm_kv, norm_final,
           conv_w_in, conv_kernel, conv_w_out, attn_w_q, attn_w_kv, attn_w_o, router_group_w,
           router_group_b, router_expert_w, router_expert_b, expert_w1, expert_w3, expert_w2):
    b_p, s_p, d = x_prompt.shape
    b_s, n_new, _ = x_sample.shape
    assert d == D_MODEL and n_new == N_NEW and s_p % (DILATIONS[2] * SPAN) == 0
    assert norm_mix.shape[0] == 2 and conv_w_in.shape[0] == 1 and attn_w_q.shape[0] == 1
    assert cache_k.shape[1] == PAST_LEN

    p = {
        'gmix': [norm_mix[l].reshape(1, d) for l in range(2)],
        'gffn': [norm_ffn[l].reshape(1, d) for l in range(2)],
        'gkv': norm_kv.reshape(1, d),
        'gfinal': norm_final.reshape(1, d),
        'win': conv_w_in[0].astype(_BF16),
        'ck': conv_kernel[0],
        'wout': conv_w_out[0].astype(_BF16),
        'wq': attn_w_q[0].astype(_BF16),
        'wkv': attn_w_kv.astype(_BF16),
        'wo': attn_w_o[0].astype(_BF16),
        'router': [_router_params(router_group_w[l], router_group_b[l], router_expert_w[l], router_expert_b[l])
                   for l in range(2)],
        'experts': [(expert_w1[l].astype(_BF16), expert_w3[l].astype(_BF16), expert_w2[l].astype(_BF16))
                    for l in range(2)],
    }

    def attend_prompt(qkv, kf, vf):
        q0, q1, q2, k0, k1, k2, v0, v1, v2 = qkv
        os, sts = [], []
        for q, k, v, dil in ((q0, k0, v0, DILATIONS[0]), (q1, k1, v1, DILATIONS[1]), (q2, k2, v2, DILATIONS[2])):
            o, st = _attn_prompt(q, k, v, b_p, dil)
            os.append(o)
            sts.append(st)
        return os, sts

    pos_p = jnp.tile(jnp.arange(s_p, dtype=jnp.int32), b_p)
    zero_state = jnp.zeros((b_p, CONV_WIDTH - 1, d), x_prompt.dtype)
    keep = min(max(WINDOWS), s_p)
    y_p, st_p, kf_p, vf_p = _forward(x_prompt, zero_state, pos_p, attend_prompt, p, shift=1, tm=512,
                                     streams=True, keep=keep)
    y_prompt = y_p.reshape(b_p, s_p, d)
    k_p = kf_p.reshape(b_p, keep, N_HEADS, HEAD_DIM)
    v_p = vf_p.reshape(b_p, keep, N_HEADS, HEAD_DIM)
    conv_p = st_p[None]

    halves = 2
    bh = b_s // halves

    def to_rows(a):
        w = a.shape[-1]
        return a.reshape(halves, bh, n_new, w).transpose(0, 2, 1, 3).reshape(halves * n_new * bh, w)

    def to_batch(a):
        w = a.shape[1:]
        return jnp.swapaxes(a.reshape(halves, n_new, bh, *w), 1, 2).reshape(b_s, n_new, *w)

    def attend_sample(qkv, kf, vf):
        q0, q1, q2 = qkv
        o = _attn_sample(to_batch(q0), to_batch(q1), to_batch(q2), cache_k, cache_v,
                         to_batch(kf), to_batch(vf))
        return [to_rows(o)], None

    x_s = to_rows(x_sample).reshape(halves, n_new * bh, d)
    prev_s = state_conv[0].reshape(halves, bh, CONV_WIDTH - 1, d).transpose(0, 2, 1, 3).reshape(
        halves, (CONV_WIDTH - 1) * bh, d)
    pos_s = jnp.tile(jnp.repeat(PAST_LEN + jnp.arange(n_new, dtype=jnp.int32), bh), halves)
    y_s, st_s, kf_s, vf_s = _forward(x_s, prev_s, pos_s, attend_sample, p, shift=bh, tm=n_new * bh,
                                     streams=False, keep=n_new * bh)
    y_sample = to_batch(y_s)
    k_s = to_batch(kf_s)
    v_s = to_batch(vf_s)
    conv_s = st_s.reshape(halves, CONV_WIDTH - 1, bh, d).transpose(0, 2, 1, 3).reshape(
        b_s, CONV_WIDTH - 1, d)[None]

    return (y_prompt, y_sample, k_p, v_p, conv_p, k_s, v_s, conv_s)
```

```python
import functools

import numpy as np
import jax
import jax.numpy as jnp
from jax import lax
from jax.experimental import pallas as pl
from jax.experimental.pallas import tpu as pltpu

D_MODEL = 1024
CONV_WIDTH = 3
WINDOWS = (128, 512, 2048)
DILATIONS = (1, 4, 16)
N_GROUPS = 3
N_HEADS = 8
HEAD_DIM = 128
SPAN = 128
ROPE_THETA = 10000.0
MOE_GROUPS = 4
EXPERTS_PER_GROUP = 4
N_EXPERTS = 16
D_EXPERT = 512
EXPERT_BLOCK = 128
RMS_EPS = 1e-6
NEG_BIG = -1e30
PAST_LEN = 2048

LANES = 128
N_CHUNKS = D_MODEL // LANES
ROUTER_ROWS = 32
EXPERT_ROW0 = 8
VMEM_LIMIT = 56 * 1024 * 1024

_F32 = jnp.float32
_BF16 = jnp.bfloat16


def _rms(x, g):
    ms = jnp.mean(x * x, axis=-1, keepdims=True)
    return (x * lax.rsqrt(ms + RMS_EPS)) * g


def _dot(a, b):
    return jnp.dot(a, b, preferred_element_type=_F32)


def _dot_nt(a, b, precision=None):
    return lax.dot_general(a, b, (((1,), (1,)), ((), ())), precision=precision,
                           preferred_element_type=_F32)


def _chunk(c):
    return slice(c * LANES, (c + 1) * LANES)


def _store_row_tiles(ref, x, lead=()):
    for c in range(N_CHUNKS):
        ref[lead + (slice(None), c, slice(None))] = x[:, _chunk(c)]


def _load_row_tiles(ref, lead=()):
    return jnp.concatenate([ref[lead + (slice(None), c, slice(None))] for c in range(N_CHUNKS)], axis=1)


def _route(xn, wr_ref, br_ref, eid_ref, gate_ref):
    logits = _dot_nt(wr_ref[...], xn, precision=lax.Precision.HIGHEST) + br_ref[...]
    lg = logits[0:MOE_GROUPS]
    row = lax.broadcasted_iota(jnp.int32, lg.shape, 0).astype(_F32)
    eg = jnp.exp(lg - jnp.max(lg, axis=0, keepdims=True))
    pg = eg / jnp.sum(eg, axis=0, keepdims=True)
    gp = jnp.max(pg, axis=0, keepdims=True)
    gi = jnp.min(jnp.where(pg == gp, row, float(MOE_GROUPS)), axis=0, keepdims=True)
    le = jnp.zeros_like(lg)
    for g in range(MOE_GROUPS):
        r0 = EXPERT_ROW0 + g * EXPERTS_PER_GROUP
        le = le + jnp.where(gi == float(g), logits[r0:r0 + EXPERTS_PER_GROUP], 0.0)
    ee = jnp.exp(le - jnp.max(le, axis=0, keepdims=True))
    ev = ee / jnp.sum(ee, axis=0, keepdims=True)
    v1 = jnp.max(ev, axis=0, keepdims=True)
    i1 = jnp.min(jnp.where(ev == v1, row, float(EXPERTS_PER_GROUP)), axis=0, keepdims=True)
    ev2 = jnp.where(row == i1, -1.0, ev)
    v2 = jnp.max(ev2, axis=0, keepdims=True)
    i2 = jnp.min(jnp.where(ev2 == v2, row, float(EXPERTS_PER_GROUP)), axis=0, keepdims=True)
    den = v1 + v2
    ids = jnp.concatenate([gi * EXPERTS_PER_GROUP + i1, gi * EXPERTS_PER_GROUP + i2], axis=0)
    eid_ref[...] = ids.astype(jnp.int32)
    gate_ref[...] = jnp.concatenate([gp * v1 / den, gp * v2 / den], axis=0)


def _conv_layer_kernel(x_ref, prev_ref, gmix_ref, win_ref, ck_ref, wout_ref, gffn_ref, wr_ref, br_ref,
                       h_ref, xn2_ref, eid_ref, gate_ref, st_ref, ubuf, *, shift, tm):
    i = pl.program_id(1)
    halo = ubuf.shape[0] - tm
    keep = (CONV_WIDTH - 1) * shift

    @pl.when(i == 0)
    def _():
        ubuf[halo - keep:halo, :] = prev_ref[0]

    @pl.when(i > 0)
    def _():
        ubuf[halo - keep:halo, :] = ubuf[halo + tm - keep:halo + tm, :]

    x = x_ref[0]
    xn = _rms(x, gmix_ref[...]).astype(_BF16)
    c_gate = _dot(xn, win_ref[:, 0:D_MODEL])
    hid = _dot(xn, win_ref[:, 2 * D_MODEL:3 * D_MODEL])
    u = c_gate * hid
    ubuf[halo:halo + tm, :] = u
    u1 = ubuf[halo - shift:halo - shift + tm, :]
    u2 = ubuf[halo - 2 * shift:halo - 2 * shift + tm, :]
    conv = ck_ref[0:1, :] * u2 + ck_ref[1:2, :] * u1 + ck_ref[2:3, :] * u
    b_gate = _dot(xn, win_ref[:, D_MODEL:2 * D_MODEL])
    y = _dot((b_gate * conv).astype(_BF16), wout_ref[...])
    h = x + y
    h_ref[0] = h
    st_ref[0] = ubuf[halo + tm - keep:halo + tm, :]
    xn2 = _rms(h, gffn_ref[...])
    _store_row_tiles(xn2_ref, xn2)
    _route(xn2, wr_ref, br_ref, eid_ref, gate_ref)


def _conv_layer(x3, prev3, gmix, win, ck, wout, gffn, wr, br, *, shift, tm):
    nb, s, d = x3.shape
    nt = s // tm
    keep = (CONV_WIDTH - 1) * shift
    halo = -(-keep // 8) * 8
    t_all = nb * s
    const = lambda b, i: (0, 0)
    return pl.pallas_call(
        functools.partial(_conv_layer_kernel, shift=shift, tm=tm),
        grid=(nb, nt),
        in_specs=[
            pl.BlockSpec((1, tm, d), lambda b, i: (b, i, 0)),
            pl.BlockSpec((1, keep, d), lambda b, i: (b, 0, 0)),
            pl.BlockSpec((1, d), const),
            pl.BlockSpec((d, 3 * d), const),
            pl.BlockSpec((CONV_WIDTH, d), const),
            pl.BlockSpec((d, d), const),
            pl.BlockSpec((1, d), const),
            pl.BlockSpec((ROUTER_ROWS, d), const),
            pl.BlockSpec((ROUTER_ROWS, 1), const),
        ],
        out_specs=[
            pl.BlockSpec((1, tm, d), lambda b, i: (b, i, 0)),
            pl.BlockSpec((tm, N_CHUNKS, LANES), lambda b, i: (b * nt + i, 0, 0)),
            pl.BlockSpec((2, tm), lambda b, i: (0, b * nt + i)),
            pl.BlockSpec((2, tm), lambda b, i: (0, b * nt + i)),
            pl.BlockSpec((1, keep, d), lambda b, i: (b, 0, 0)),
        ],
        out_shape=[
            jax.ShapeDtypeStruct((nb, s, d), _F32),
            jax.ShapeDtypeStruct((t_all, N_CHUNKS, LANES), _F32),
            jax.ShapeDtypeStruct((2, t_all), jnp.int32),
            jax.ShapeDtypeStruct((2, t_all), _F32),
            jax.ShapeDtypeStruct((nb, keep, d), _F32),
        ],
        scratch_shapes=[pltpu.VMEM((halo + tm, d), _F32)],
        compiler_params=pltpu.CompilerParams(
            dimension_semantics=("arbitrary", "arbitrary"), vmem_limit_bytes=VMEM_LIMIT),
        name="conv_layer",
    )(x3, prev3, gmix, win, ck, wout, gffn, wr, br)


def _moe_ffn_kernel(be_ref, nv_ref, tab_ref, x_hbm, w1_ref, w3_ref, w2_ref, y_hbm,
                    xbuf, obuf, gsem, ssem, *, n_tok, n_blocks):
    j = pl.program_id(0)
    slot = j % 2
    other = 1 - slot
    half = EXPERT_BLOCK // 2

    def token_of(v):
        if n_tok & (n_tok - 1) == 0:
            return v & (n_tok - 1)
        return v - jnp.where(v >= n_tok, n_tok, 0)

    def gather_rows(blk, s, n, k0, k1):
        for k in range(k0, k1):
            @pl.when(k < n)
            def _():
                tok = token_of(tab_ref[blk * EXPERT_BLOCK + k])
                pltpu.make_async_copy(x_hbm.at[tok], xbuf.at[s, k], gsem.at[s]).start()

    def scatter_rows(blk, s, n, k0, k1):
        for k in range(k0, k1):
            @pl.when(k < n)
            def _():
                v = tab_ref[blk * EXPERT_BLOCK + k]
                pltpu.make_async_copy(obuf.at[s, k], y_hbm.at[v], ssem.at[s]).start()

    def wait_gather(n, s):
        @pl.when(n > 0)
        def _():
            pltpu.make_async_copy(x_hbm.at[pl.ds(0, n)], xbuf.at[s, pl.ds(0, n)], gsem.at[s]).wait()

    def wait_scatter(n, s):
        @pl.when(n > 0)
        def _():
            pltpu.make_async_copy(obuf.at[s, pl.ds(0, n)], y_hbm.at[pl.ds(0, n)], ssem.at[s]).wait()

    @pl.when(j == 0)
    def _():
        xbuf[...] = jnp.zeros(xbuf.shape, _F32)
        gather_rows(0, 0, nv_ref[0], 0, EXPERT_BLOCK)

    nxt = jnp.minimum(j + 1, n_blocks - 1)
    prv = jnp.maximum(j - 1, 0)
    n_next = jnp.where(j + 1 < n_blocks, nv_ref[nxt], 0)
    n_prev = jnp.where(j > 0, nv_ref[prv], 0)

    wait_gather(nv_ref[j], slot)
    x = _load_row_tiles(xbuf, (slot,)).astype(_BF16)
    gather_rows(nxt, other, n_next, 0, half)
    a = _dot(x, w1_ref[0])
    gather_rows(nxt, other, n_next, half, EXPERT_BLOCK)
    b = _dot(x, w3_ref[0])
    scatter_rows(prv, other, n_prev, 0, half)
    hmid = (jax.nn.silu(a) * b).astype(_BF16)
    y = _dot(hmid, w2_ref[0])
    scatter_rows(prv, other, n_prev, half, EXPERT_BLOCK)

    @pl.when(j >= 2)
    def _():
        wait_scatter(nv_ref[jnp.maximum(j - 2, 0)], slot)

    _store_row_tiles(obuf, y, (slot,))

    @pl.when(j == n_blocks - 1)
    def _():
        wait_scatter(n_prev, other)
        n_last = nv_ref[j]
        scatter_rows(j, slot, n_last, 0, EXPERT_BLOCK)
        wait_scatter(n_last, slot)


def _moe_ffn(block_expert, n_valid, table, xn, w1, w3, w2):
    n_tok = xn.shape[0]
    d = D_MODEL
    n_blocks = block_expert.shape[0]
    assert n_blocks >= 2 and xn.shape[1:] == (N_CHUNKS, LANES)
    wmap = lambda j, be, nv, tab: (be[j], 0, 0)
    return pl.pallas_call(
        functools.partial(_moe_ffn_kernel, n_tok=n_tok, n_blocks=n_blocks),
        grid_spec=pltpu.PrefetchScalarGridSpec(
            num_scalar_prefetch=3,
            grid=(n_blocks,),
            in_specs=[
                pl.BlockSpec(memory_space=pl.ANY),
                pl.BlockSpec((1, d, D_EXPERT), wmap),
                pl.BlockSpec((1, d, D_EXPERT), wmap),
                pl.BlockSpec((1, D_EXPERT, d), wmap),
            ],
            out_specs=pl.BlockSpec(memory_space=pl.ANY),
            scratch_shapes=[
                pltpu.VMEM((2, EXPERT_BLOCK, N_CHUNKS, LANES), _F32),
                pltpu.VMEM((2, EXPERT_BLOCK, N_CHUNKS, LANES), _F32),
                pltpu.SemaphoreType.DMA((2,)),
                pltpu.SemaphoreType.DMA((2,)),
            ],
        ),
        out_shape=jax.ShapeDtypeStruct((2 * n_tok, N_CHUNKS, LANES), _F32),
        compiler_params=pltpu.CompilerParams(
            dimension_semantics=("arbitrary",), vmem_limit_bytes=VMEM_LIMIT),
        name="moe_ffn",
    )(block_expert, n_valid, table, xn, w1, w3, w2)


def _moe_plan(eid, n_tok):
    n_assign = 2 * n_tok
    flat_e = eid.T.reshape(-1)
    onehot = (flat_e[:, None] == jnp.arange(N_EXPERTS, dtype=jnp.int32)[None, :]).astype(jnp.int32)
    csum = jnp.cumsum(onehot, axis=0)
    counts = csum[-1]
    rank = jnp.sum(csum * onehot, axis=1) - 1
    padded = (counts + EXPERT_BLOCK - 1) // EXPERT_BLOCK * EXPERT_BLOCK
    pad_end = jnp.cumsum(padded)
    pad_start = pad_end - padded
    slot = pad_start[flat_e] + rank
    n_blocks = -(-(n_assign + N_EXPERTS * (EXPERT_BLOCK - 1)) // EXPERT_BLOCK)
    n_slots = n_blocks * EXPERT_BLOCK
    a = jnp.arange(n_assign, dtype=jnp.int32)
    value = (a % 2) * n_tok + a // 2
    table = jnp.zeros((n_slots,), jnp.int32).at[slot].set(value)
    block_start = jnp.arange(n_blocks, dtype=jnp.int32) * EXPERT_BLOCK
    block_expert = jnp.minimum(jnp.sum((pad_end[None, :] <= block_start[:, None]).astype(jnp.int32), axis=1),
                               N_EXPERTS - 1)
    n_valid = jnp.clip((pad_start + counts)[block_expert] - block_start, 0, EXPERT_BLOCK).astype(jnp.int32)
    return block_expert, n_valid, table


def _combine(h_ref, y0_ref, y1_ref, gate_ref):
    gate = gate_ref[...]
    return h_ref[...] + (gate[:, 0:1] * _load_row_tiles(y0_ref) + gate[:, 1:2] * _load_row_tiles(y1_ref))


def _emit_head(r, h, outs, scr):
    tm = r.shape[0]
    if scr is not None:
        scr[h] = r
    for ref, dil in outs:
        if dil == 0:
            ref[:, h, :] = r
        elif dil == 1:
            ref[:, _chunk(h)] = r.astype(ref.dtype)
        else:
            for res in range(dil):
                col = res * D_MODEL + h * HEAD_DIM
                ref[:, col:col + HEAD_DIM] = scr[h, pl.ds(res, tm // dil, stride=dil), :].astype(ref.dtype)


def _rope(xh, cos, sin_signed):
    return xh * cos + pltpu.roll(xh, HEAD_DIM // 2, axis=1) * sin_signed


def _qkv_kernel(*refs, streams, per_seq, kept):
    (h_ref, y0_ref, y1_ref, gate_ref, gmix_ref, gkv_ref, wq_ref, wkv_ref, cos_ref, sin_ref) = refs[:10]
    outs = refs[10:]
    if streams:
        (h2_ref, q0_ref, q1_ref, q2_ref, k0_ref, k1_ref, k2_ref, v0_ref, v1_ref, v2_ref, kf_ref, vf_ref,
         scr_q1, scr_q2, scr_k, scr_v) = outs
        q_outs = [([(q0_ref, 1)], None), ([(q1_ref, DILATIONS[1])], scr_q1), ([(q2_ref, DILATIONS[2])], scr_q2)]
        k_outs = ([(k0_ref, 1), (k1_ref, DILATIONS[1]), (k2_ref, DILATIONS[2])], scr_k)
        v_outs = ([(v0_ref, 1), (v1_ref, DILATIONS[1]), (v2_ref, DILATIONS[2])], scr_v)
    else:
        h2_ref, q0_ref, q1_ref, q2_ref, kf_ref, vf_ref = outs
        q_outs = [([(q0_ref, 1)], None), ([(q1_ref, 1)], None), ([(q2_ref, 1)], None)]
        k_outs = ([(kf_ref, 0)], None)
        v_outs = ([(vf_ref, 0)], None)

    h2 = _combine(h_ref, y0_ref, y1_ref, gate_ref)
    h2_ref[...] = h2
    cos = cos_ref[...]
    sin = sin_ref[...]
    xn = _rms(h2, gmix_ref[...]).astype(_BF16)
    scale = HEAD_DIM ** -0.5
    for g in range(N_GROUPS):
        q = _dot(xn, wq_ref[:, g * D_MODEL:(g + 1) * D_MODEL])
        for h in range(N_HEADS):
            _emit_head(_rope(q[:, _chunk(h)], cos, sin) * scale, h, *q_outs[g])
    xkv = _rms(h2, gkv_ref[...]).astype(_BF16)
    k = _dot(xkv, wkv_ref[:, 0:D_MODEL])
    for h in range(N_HEADS):
        _emit_head(_rope(k[:, _chunk(h)], cos, sin), h, *k_outs)
    v = _dot(xkv, wkv_ref[:, D_MODEL:2 * D_MODEL])
    for h in range(N_HEADS):
        _emit_head(v[:, _chunk(h)], h, *v_outs)
    if streams:
        @pl.when(pl.program_id(0) % per_seq >= per_seq - kept)
        def _():
            for h in range(N_HEADS):
                kf_ref[:, h, :] = scr_k[h]
                vf_ref[:, h, :] = scr_v[h]


def _qkv(h, ypair, gate, gmix, gkv, wq, wkv, cos, sin, *, tm, streams, seq, keep):
    t, d = h.shape
    nt = t // tm
    per_seq = seq // tm
    kept = keep // tm
    assert seq % tm == 0 and keep % tm == 0

    def tail(i):
        return ((i // per_seq) * kept + jnp.maximum(i % per_seq - (per_seq - kept), 0), 0, 0)

    tail_f32 = (pl.BlockSpec((tm, N_HEADS, HEAD_DIM), tail),
                jax.ShapeDtypeStruct((t // seq * keep, N_HEADS, HEAD_DIM), _F32))
    row = lambda i: (i, 0)
    const = lambda i: (0, 0)
    nat = lambda dt: (pl.BlockSpec((tm, d), row), jax.ShapeDtypeStruct((t, d), dt))

    def stream(dil):
        return (pl.BlockSpec((tm // dil, dil * d), row), jax.ShapeDtypeStruct((t // dil, dil * d), _BF16))

    if streams:
        d1, d2 = DILATIONS[1], DILATIONS[2]
        outs = [nat(_F32), nat(_BF16), stream(d1), stream(d2), nat(_BF16), stream(d1), stream(d2),
                nat(_BF16), stream(d1), stream(d2), tail_f32, tail_f32]
        scratch = [pltpu.VMEM((N_HEADS, tm, HEAD_DIM), _F32)] * 4
    else:
        outs = [nat(_F32), nat(_BF16), nat(_BF16), nat(_BF16), tail_f32, tail_f32]
        scratch = []
    return pl.pallas_call(
        functools.partial(_qkv_kernel, streams=streams, per_seq=per_seq, kept=kept),
        grid=(nt,),
        in_specs=[
            pl.BlockSpec((tm, d), row),
            pl.BlockSpec((tm, N_CHUNKS, LANES), lambda i: (i, 0, 0)),
            pl.BlockSpec((tm, N_CHUNKS, LANES), lambda i: (nt + i, 0, 0)),
            pl.BlockSpec((tm, 2), row),
            pl.BlockSpec((1, d), const),
            pl.BlockSpec((1, d), const),
            pl.BlockSpec((d, 3 * d), const),
            pl.BlockSpec((d, 2 * d), const),
            pl.BlockSpec((tm, HEAD_DIM), row),
            pl.BlockSpec((tm, HEAD_DIM), row),
        ],
        out_specs=[o[0] for o in outs],
        out_shape=[o[1] for o in outs],
        scratch_shapes=scratch,
        compiler_params=pltpu.CompilerParams(
            dimension_semantics=("arbitrary",), vmem_limit_bytes=VMEM_LIMIT),
        name="qkv_proj_streams" if streams else "qkv_proj",
    )(h, ypair, ypair, gate, gmix, gkv, wq, wkv, cos, sin)


ATTN_TQ = 512


def _attn_prompt_kernel(q_ref, kp_ref, kc_ref, vp_ref, vc_ref, o_ref, st_ref, *, tq):
    i = pl.program_id(1)
    qi = lax.broadcasted_iota(jnp.int32, (SPAN, 2 * SPAN), 0)
    kj = lax.broadcasted_iota(jnp.int32, (SPAN, 2 * SPAN), 1)
    band = (kj >= qi) & (kj <= qi + SPAN)
    band0 = band & (kj >= jnp.where(i > 0, 0, SPAN))
    lane = lax.broadcasted_iota(jnp.int32, (SPAN, HEAD_DIM), 1)
    ones = jnp.ones((2 * SPAN, HEAD_DIM), _BF16)
    for a in range(tq // SPAN):
        rows = slice(a * SPAN, (a + 1) * SPAN)
        st = jnp.zeros((SPAN, HEAD_DIM), _F32)
        for h in range(N_HEADS):
            hs = _chunk(h)
            if a == 0:
                keys = jnp.concatenate([kp_ref[:, hs], kc_ref[rows, hs]], axis=0)
                vals = jnp.concatenate([vp_ref[:, hs], vc_ref[rows, hs]], axis=0)
                mask = band0
            else:
                both = slice((a - 1) * SPAN, (a + 1) * SPAN)
                keys, vals, mask = kc_ref[both, hs], vc_ref[both, hs], band
            s = jnp.where(mask, _dot_nt(q_ref[rows, hs], keys), NEG_BIG)
            m = jnp.max(s, axis=1, keepdims=True)
            p = jnp.exp(s - m).astype(_BF16)
            ol = _dot(p, jnp.concatenate([vals, ones], axis=1))
            l = ol[:, HEAD_DIM:]
            o_ref[rows, hs] = ol[:, :HEAD_DIM] / l
            st = jnp.where(lane == h, m + jnp.log(l), st)
        st_ref[rows, :] = st


def _attn_prompt(q, k, v, batch, dilation):
    rows, width = q.shape
    d = width // dilation
    per_batch = rows // batch
    tq = min(ATTN_TQ, per_batch)
    assert per_batch % tq == 0 and tq % SPAN == 0
    nq = per_batch // tq
    sub = tq // SPAN
    cur = lambda b, i, r: (b * nq + i, r)
    prev = lambda b, i, r: ((b * nq + i) * sub - jnp.where(i > 0, 1, 0), r)
    big = (tq, d)
    small = (SPAN, d)
    return pl.pallas_call(
        functools.partial(_attn_prompt_kernel, tq=tq),
        grid=(batch, nq, dilation),
        in_specs=[pl.BlockSpec(big, cur), pl.BlockSpec(small, prev), pl.BlockSpec(big, cur),
                  pl.BlockSpec(small, prev), pl.BlockSpec(big, cur)],
        out_specs=[pl.BlockSpec(big, cur), pl.BlockSpec((tq, HEAD_DIM), cur)],
        out_shape=[jax.ShapeDtypeStruct((rows, width), _F32),
                   jax.ShapeDtypeStruct((rows, dilation * HEAD_DIM), _F32)],
        compiler_params=pltpu.CompilerParams(
            dimension_semantics=("arbitrary", "arbitrary", "arbitrary"), vmem_limit_bytes=VMEM_LIMIT),
        name=f"attn_prompt_d{dilation}",
    )(q, k, k, v, v)


N_NEW = 8
CACHE_A_GROUPS = (PAST_LEN - WINDOWS[1]) // DILATIONS[2]
CACHE_A_ROWS = CACHE_A_GROUPS * N_NEW
CACHE_B_ROWS = WINDOWS[1]
CACHE_B_GROUPS = CACHE_B_ROWS // DILATIONS[2]
KEYS_REAL = CACHE_A_ROWS + CACHE_B_ROWS + N_NEW
KEYS_PAD = -(-KEYS_REAL // 128) * 128


def _sample_key_positions():
    pos = np.full((KEYS_PAD,), -1, np.int64)
    a = np.arange(CACHE_A_ROWS)
    pos[:CACHE_A_ROWS] = (a // N_NEW) * DILATIONS[2] + a % N_NEW
    pos[CACHE_A_ROWS:CACHE_A_ROWS + CACHE_B_ROWS] = PAST_LEN - CACHE_B_ROWS + np.arange(CACHE_B_ROWS)
    pos[CACHE_A_ROWS + CACHE_B_ROWS:KEYS_REAL] = PAST_LEN + np.arange(N_NEW)
    return pos


def _sample_bias():
    pos = _sample_key_positions()
    bias = np.full((N_GROUPS * N_NEW, KEYS_PAD), NEG_BIG, np.float32)
    for g in range(N_GROUPS):
        for n in range(N_NEW):
            delta = PAST_LEN + n - pos
            ok = (pos >= 0) & (delta >= 0) & (delta <= WINDOWS[g]) & (delta % DILATIONS[g] == 0)
            assert int(ok.sum()) == SPAN + 1
            bias[g * N_NEW + n, ok] = 0.0
    return bias


def _attn_sample_kernel(q0_ref, q1_ref, q2_ref, kn_ref, vn_ref, bias_ref, ck_hbm, cv_hbm,
                        o_ref, ka, kb, va, vb, kall, vall, sem, *, n_seq):
    b = pl.program_id(0)
    slot = b % 2

    def cache_copies(seq, s):
        cps = []
        for h in range(N_HEADS):
            for src, dst_a, dst_b in ((ck_hbm, ka, kb), (cv_hbm, va, vb)):
                cps.append(pltpu.make_async_copy(
                    src.at[seq, pl.ds(0, CACHE_A_GROUPS), pl.ds(0, N_NEW), h], dst_a.at[s, h], sem.at[s]))
                cps.append(pltpu.make_async_copy(
                    src.at[seq, pl.ds(CACHE_A_GROUPS, CACHE_B_GROUPS), :, h], dst_b.at[s, h], sem.at[s]))
        return cps

    @pl.when(b == 0)
    def _():
        kall[KEYS_REAL:KEYS_PAD, :] = jnp.zeros((KEYS_PAD - KEYS_REAL, D_MODEL), _BF16)
        vall[KEYS_REAL:KEYS_PAD, :] = jnp.zeros((KEYS_PAD - KEYS_REAL, D_MODEL), _BF16)
        for cp in cache_copies(0, 0):
            cp.start()

    @pl.when(b + 1 < n_seq)
    def _():
        for cp in cache_copies(b + 1, 1 - slot):
            cp.start()

    for cp in cache_copies(b, slot):
        cp.wait()

    nb0 = CACHE_A_ROWS + CACHE_B_ROWS
    for src_a, src_b, src_n, dst in ((ka, kb, kn_ref, kall), (va, vb, vn_ref, vall)):
        for h in range(N_HEADS):
            hs = _chunk(h)
            dst[0:CACHE_A_ROWS, hs] = src_a[slot, h].reshape(CACHE_A_ROWS, HEAD_DIM).astype(_BF16)
            dst[CACHE_A_ROWS:nb0, hs] = src_b[slot, h].reshape(CACHE_B_ROWS, HEAD_DIM).astype(_BF16)
            dst[nb0:KEYS_REAL, hs] = src_n[0, :, h, :].astype(_BF16)

    bias = bias_ref[...]
    for h in range(N_HEADS):
        hs = _chunk(h)
        qh = jnp.concatenate([q0_ref[0, :, hs].astype(_F32), q1_ref[0, :, hs].astype(_F32),
                              q2_ref[0, :, hs].astype(_F32)], axis=0).astype(_BF16)
        s = _dot_nt(qh, kall[:, hs]) + bias
        m = jnp.max(s, axis=1, keepdims=True)
        m8 = jnp.maximum(jnp.maximum(m[0:N_NEW], m[N_NEW:2 * N_NEW]), m[2 * N_NEW:3 * N_NEW])
        p = jnp.exp(s - jnp.concatenate([m8, m8, m8], axis=0))
        p8 = p[0:N_NEW] + p[N_NEW:2 * N_NEW] + p[2 * N_NEW:3 * N_NEW]
        l8 = jnp.sum(p8, axis=1, keepdims=True)
        o_ref[0, :, hs] = _dot(p8.astype(_BF16), vall[:, hs]) / l8


def _attn_sample(q0, q1, q2, cache_k, cache_v, k_new, v_new):
    nb, n_new, d = q0.shape
    past = cache_k.shape[1]
    assert past == PAST_LEN and n_new == N_NEW and d == D_MODEL
    assert cache_k.shape[2:] == (N_HEADS, HEAD_DIM)
    ck = cache_k.reshape(nb, past // DILATIONS[2], DILATIONS[2], N_HEADS, HEAD_DIM)
    cv = cache_v.reshape(nb, past // DILATIONS[2], DILATIONS[2], N_HEADS, HEAD_DIM)
    bias = jnp.asarray(_sample_bias())
    new = pl.BlockSpec((1, n_new, d), lambda b: (b, 0, 0))
    new_kv = pl.BlockSpec((1, n_new, N_HEADS, HEAD_DIM), lambda b: (b, 0, 0, 0))
    part_a = (2, N_HEADS, CACHE_A_GROUPS, N_NEW, HEAD_DIM)
    part_b = (2, N_HEADS, CACHE_B_GROUPS, DILATIONS[2], HEAD_DIM)
    return pl.pallas_call(
        functools.partial(_attn_sample_kernel, n_seq=nb),
        grid=(nb,),
        in_specs=[new, new, new, new_kv, new_kv,
                  pl.BlockSpec((N_GROUPS * N_NEW, KEYS_PAD), lambda b: (0, 0)),
                  pl.BlockSpec(memory_space=pl.ANY), pl.BlockSpec(memory_space=pl.ANY)],
        out_specs=new,
        out_shape=jax.ShapeDtypeStruct((nb, n_new, d), _F32),
        scratch_shapes=[pltpu.VMEM(part_a, _F32), pltpu.VMEM(part_b, _F32),
                        pltpu.VMEM(part_a, _F32), pltpu.VMEM(part_b, _F32),
                        pltpu.VMEM((KEYS_PAD, d), _BF16), pltpu.VMEM((KEYS_PAD, d), _BF16),
                        pltpu.SemaphoreType.DMA((2,))],
        compiler_params=pltpu.CompilerParams(
            dimension_semantics=("arbitrary",), vmem_limit_bytes=VMEM_LIMIT),
        name="attn_sample",
    )(q0, q1, q2, k_new, v_new, bias, ck, cv)


def _attn_out_kernel(*refs, n_groups, tm):
    if n_groups == 1:
        (o_ref, h_ref, wo_ref, gffn_ref, wr_ref, br_ref,
         h3_ref, xn2_ref, eid_ref, gate_ref) = refs
        o = o_ref[...].astype(_BF16)
    else:
        (o0_ref, o1_ref, o2_ref, s0_ref, s1_ref, s2_ref, h_ref, wo_ref, gffn_ref, wr_ref, br_ref,
         h3_ref, xn2_ref, eid_ref, gate_ref, obuf, scr_o1, scr_o2, scr_s1, scr_s2) = refs
        for o_ref, s_ref, scr_o, scr_s, dil in ((o1_ref, s1_ref, scr_o1, scr_s1, DILATIONS[1]),
                                                (o2_ref, s2_ref, scr_o2, scr_s2, DILATIONS[2])):
            for res in range(dil):
                dst = pl.ds(res, tm // dil, stride=dil)
                scr_s[dst, :] = s_ref[:, res * HEAD_DIM:(res + 1) * HEAD_DIM]
                for h in range(N_HEADS):
                    col = res * D_MODEL + h * HEAD_DIM
                    scr_o[h, dst, :] = o_ref[:, col:col + HEAD_DIM]
        sts = [s0_ref[...], scr_s1[...], scr_s2[...]]
        mx = jnp.maximum(jnp.maximum(sts[0], sts[1]), sts[2])
        es = [jnp.exp(s - mx) for s in sts]
        den = es[0] + es[1] + es[2]
        ws = [e / den for e in es]
        for h in range(N_HEADS):
            acc = ws[0][:, h:h + 1] * o0_ref[:, _chunk(h)]
            acc = acc + ws[1][:, h:h + 1] * scr_o1[h]
            acc = acc + ws[2][:, h:h + 1] * scr_o2[h]
            obuf[:, _chunk(h)] = acc.astype(_BF16)
        o = obuf[...]
    h3 = h_ref[...] + _dot(o, wo_ref[...])
    h3_ref[...] = h3
    xn2 = _rms(h3, gffn_ref[...])
    _store_row_tiles(xn2_ref, xn2)
    _route(xn2, wr_ref, br_ref, eid_ref, gate_ref)


def _attn_out(os, sts, h, wo, gffn, wr, br, *, tm):
    t, d = h.shape
    n_groups = len(os)
    row = lambda i: (i, 0)
    const = lambda i: (0, 0)
    if n_groups == 1:
        in_specs = [pl.BlockSpec((tm, d), row)]
        scratch = []
        args = [os[0]]
    else:
        in_specs = [pl.BlockSpec((tm // dil, dil * d), row) for dil in DILATIONS]
        in_specs += [pl.BlockSpec((tm // dil, dil * HEAD_DIM), row) for dil in DILATIONS]
        scratch = [pltpu.VMEM((tm, d), _BF16),
                   pltpu.VMEM((N_HEADS, tm, HEAD_DIM), _F32), pltpu.VMEM((N_HEADS, tm, HEAD_DIM), _F32),
                   pltpu.VMEM((tm, HEAD_DIM), _F32), pltpu.VMEM((tm, HEAD_DIM), _F32)]
        args = list(os) + list(sts)
    in_specs += [pl.BlockSpec((tm, d), row), pl.BlockSpec((d, d), const), pl.BlockSpec((1, d), const),
                 pl.BlockSpec((ROUTER_ROWS, d), const), pl.BlockSpec((ROUTER_ROWS, 1), const)]
    return pl.pallas_call(
        functools.partial(_attn_out_kernel, n_groups=n_groups, tm=tm),
        grid=(t // tm,),
        in_specs=in_specs,
        out_specs=[pl.BlockSpec((tm, d), row), pl.BlockSpec((tm, N_CHUNKS, LANES), lambda i: (i, 0, 0)),
                   pl.BlockSpec((2, tm), lambda i: (0, i)), pl.BlockSpec((2, tm), lambda i: (0, i))],
        out_shape=[jax.ShapeDtypeStruct((t, d), _F32), jax.ShapeDtypeStruct((t, N_CHUNKS, LANES), _F32),
                   jax.ShapeDtypeStruct((2, t), jnp.int32), jax.ShapeDtypeStruct((2, t), _F32)],
        scratch_shapes=scratch,
        compiler_params=pltpu.CompilerParams(
            dimension_semantics=("arbitrary",), vmem_limit_bytes=VMEM_LIMIT),
        name=f"attn_out_g{n_groups}",
    )(*args, h, wo, gffn, wr, br)


def _final_kernel(h_ref, y0_ref, y1_ref, gate_ref, g_ref, out_ref):
    out_ref[...] = _rms(_combine(h_ref, y0_ref, y1_ref, gate_ref), g_ref[...])


def _final(h, ypair, gate, g, *, tm):
    t, d = h.shape
    nt = t // tm
    row = lambda i: (i, 0)
    return pl.pallas_call(
        _final_kernel,
        grid=(nt,),
        in_specs=[pl.BlockSpec((tm, d), row),
                  pl.BlockSpec((tm, N_CHUNKS, LANES), lambda i: (i, 0, 0)),
                  pl.BlockSpec((tm, N_CHUNKS, LANES), lambda i: (nt + i, 0, 0)),
                  pl.BlockSpec((tm, 2), row),
                  pl.BlockSpec((1, d), lambda i: (0, 0))],
        out_specs=pl.BlockSpec((tm, d), row),
        out_shape=jax.ShapeDtypeStruct((t, d), _F32),
        compiler_params=pltpu.CompilerParams(
            dimension_semantics=("arbitrary",), vmem_limit_bytes=VMEM_LIMIT),
        name="final_norm",
    )(h, ypair, ypair, gate, g)


def _rope_tables(pos):
    half = HEAD_DIM // 2
    inv_freq = jnp.power(jnp.float32(ROPE_THETA), -jnp.arange(half, dtype=jnp.float32) / half)
    ang = pos.astype(jnp.float32)[:, None] * inv_freq[None, :]
    cos = jnp.cos(ang)
    sin = jnp.sin(ang)
    return jnp.concatenate([cos, cos], axis=-1), jnp.concatenate([-sin, sin], axis=-1)


def _router_params(wg, bg, we, be):
    wr = jnp.zeros((ROUTER_ROWS, D_MODEL), _F32)
    wr = wr.at[0:MOE_GROUPS].set(wg.T).at[EXPERT_ROW0:EXPERT_ROW0 + N_EXPERTS].set(we.T)
    br = jnp.zeros((ROUTER_ROWS, 1), _F32)
    br = br.at[0:MOE_GROUPS, 0].set(bg).at[EXPERT_ROW0:EXPERT_ROW0 + N_EXPERTS, 0].set(be)
    return wr, br


def _moe(xn2, eid, w1, w3, w2):
    block_expert, n_valid, table = _moe_plan(eid, xn2.shape[0])
    return _moe_ffn(block_expert, n_valid, table, xn2, w1, w3, w2)


def _forward(x3, prev3, pos_rows, attend, p, *, shift, tm, streams, keep):
    h1, xn2, eid, gate, state = _conv_layer(
        x3, prev3, p['gmix'][0], p['win'], p['ck'], p['wout'], p['gffn'][0], *p['router'][0],
        shift=shift, tm=tm)
    t = x3.shape[0] * x3.shape[1]
    ypair = _moe(xn2, eid, *p['experts'][0])
    cos, sin = _rope_tables(pos_rows)
    h2, *qkv, kf, vf = _qkv(h1.reshape(t, D_MODEL), ypair, gate.T, p['gmix'][1], p['gkv'],
                            p['wq'], p['wkv'], cos, sin, tm=256, streams=streams,
                            seq=x3.shape[1], keep=keep)
    os, sts = attend(qkv, kf, vf)
    h3, xn2, eid, gate = _attn_out(os, sts, h2, p['wo'], p['gffn'][1], *p['router'][1], tm=256)
    ypair = _moe(xn2, eid, *p['experts'][1])
    y = _final(h3, ypair, gate.T, p['gfinal'], tm=512)
    return y, state, kf, vf


def kernel(x_prompt, x_sample, cache_k, cache_v, state_conv, norm_mix, norm_ffn, norm_kv, norm_final,
           conv_w_in, conv_kernel, conv_w_out, attn_w_q, attn_w_kv, attn_w_o, router_group_w,
           router_group_b, router_expert_w, router_expert_b, expert_w1, expert_w3, expert_w2):
    b_p, s_p, d = x_prompt.shape
    b_s, n_new, _ = x_sample.shape
    assert d == D_MODEL and n_new == N_NEW and s_p % (DILATIONS[2] * SPAN) == 0
    assert norm_mix.shape[0] == 2 and conv_w_in.shape[0] == 1 and attn_w_q.shape[0] == 1
    assert cache_k.shape[1] == PAST_LEN

    p = {
        'gmix': [norm_mix[l].reshape(1, d) for l in range(2)],
        'gffn': [norm_ffn[l].reshape(1, d) for l in range(2)],
        'gkv': norm_kv.reshape(1, d),
        'gfinal': norm_final.reshape(1, d),
        'win': conv_w_in[0].astype(_BF16),
        'ck': conv_kernel[0],
        'wout': conv_w_out[0].astype(_BF16),
        'wq': attn_w_q[0].astype(_BF16),
        'wkv': attn_w_kv.astype(_BF16),
        'wo': attn_w_o[0].astype(_BF16),
        'router': [_router_params(router_group_w[l], router_group_b[l], router_expert_w[l], router_expert_b[l])
                   for l in range(2)],
        'experts': [(expert_w1[l].astype(_BF16), expert_w3[l].astype(_BF16), expert_w2[l].astype(_BF16))
                    for l in range(2)],
    }

    def attend_prompt(qkv, kf, vf):
        q0, q1, q2, k0, k1, k2, v0, v1, v2 = qkv
        os, sts = [], []
        for q, k, v, dil in ((q0, k0, v0, DILATIONS[0]), (q1, k1, v1, DILATIONS[1]), (q2, k2, v2, DILATIONS[2])):
            o, st = _attn_prompt(q, k, v, b_p, dil)
            os.append(o)
            sts.append(st)
        return os, sts

    pos_p = jnp.tile(jnp.arange(s_p, dtype=jnp.int32), b_p)
    zero_state = jnp.zeros((b_p, CONV_WIDTH - 1, d), x_prompt.dtype)
    keep = min(max(WINDOWS), s_p)
    y_p, st_p, kf_p, vf_p = _forward(x_prompt, zero_state, pos_p, attend_prompt, p, shift=1, tm=512,
                                     streams=True, keep=keep)
    y_prompt = y_p.reshape(b_p, s_p, d)
    k_p = kf_p.reshape(b_p, keep, N_HEADS, HEAD_DIM)
    v_p = vf_p.reshape(b_p, keep, N_HEADS, HEAD_DIM)
    conv_p = st_p[None]

    halves = 2
    bh = b_s // halves

    def to_rows(a):
        w = a.shape[-1]
        return a.reshape(halves, bh, n_new, w).transpose(0, 2, 1, 3).reshape(halves * n_new * bh, w)

    def to_batch(a):
        w = a.shape[1:]
        return jnp.swapaxes(a.reshape(halves, n_new, bh, *w), 1, 2).reshape(b_s, n_new, *w)

    def attend_sample(qkv, kf, vf):
        q0, q1, q2 = qkv
        o = _attn_sample(to_batch(q0), to_batch(q1), to_batch(q2), cache_k, cache_v,
                         to_batch(kf), to_batch(vf))
        return [to_rows(o)], None

    x_s = to_rows(x_sample).reshape(halves, n_new * bh, d)
    prev_s = state_conv[0].reshape(halves, bh, CONV_WIDTH - 1, d).transpose(0, 2, 1, 3).reshape(
        halves, (CONV_WIDTH - 1) * bh, d)
    pos_s = jnp.tile(jnp.repeat(PAST_LEN + jnp.arange(n_new, dtype=jnp.int32), bh), halves)
    y_s, st_s, kf_s, vf_s = _forward(x_s, prev_s, pos_s, attend_sample, p, shift=bh, tm=n_new * bh,
                                     streams=False, keep=n_new * bh)
    y_sample = to_batch(y_s)
    k_s = to_batch(kf_s)
    v_s = to_batch(vf_s)
    conv_s = st_s.reshape(halves, CONV_WIDTH - 1, bh, d).transpose(0, 2, 1, 3).reshape(
        b_s, CONV_WIDTH - 1, d)[None]

    return (y_prompt, y_sample, k_p, v_p, conv_p, k_s, v_s, conv_s)
```

```python
import functools

import numpy as np
import jax
import jax.numpy as jnp
from jax import lax
from jax.experimental import pallas as pl
from jax.experimental.pallas import tpu as pltpu

D_MODEL = 1024
CONV_WIDTH = 3
WINDOWS = (128, 512, 2048)
DILATIONS = (1, 4, 16)
N_GROUPS = 3
N_HEADS = 8
HEAD_DIM = 128
SPAN = 128
ROPE_THETA = 10000.0
MOE_GROUPS = 4
EXPERTS_PER_GROUP = 4
N_EXPERTS = 16
D_EXPERT = 512
EXPERT_BLOCK = 128
RMS_EPS = 1e-6
NEG_BIG = -1e30
PAST_LEN = 2048

LANES = 128
N_CHUNKS = D_MODEL // LANES
ROUTER_ROWS = 32
EXPERT_ROW0 = 8
VMEM_LIMIT = 56 * 1024 * 1024

_F32 = jnp.float32
_BF16 = jnp.bfloat16


def _rms(x, g):
    ms = jnp.mean(x * x, axis=-1, keepdims=True)
    return (x * lax.rsqrt(ms + RMS_EPS)) * g


def _dot(a, b):
    return jnp.dot(a, b, preferred_element_type=_F32)


def _dot_nt(a, b, precision=None):
    return lax.dot_general(a, b, (((1,), (1,)), ((), ())), precision=precision,
                           preferred_element_type=_F32)


def _chunk(c):
    return slice(c * LANES, (c + 1) * LANES)


def _store_row_tiles(ref, x, lead=()):
    for c in range(N_CHUNKS):
        ref[lead + (slice(None), c, slice(None))] = x[:, _chunk(c)]


def _load_row_tiles(ref, lead=()):
    return jnp.concatenate([ref[lead + (slice(None), c, slice(None))] for c in range(N_CHUNKS)], axis=1)


def _route(xn, wr_ref, br_ref, eid_ref, gate_ref):
    logits = _dot_nt(wr_ref[...], xn, precision=lax.Precision.HIGHEST) + br_ref[...]
    lg = logits[0:MOE_GROUPS]
    row = lax.broadcasted_iota(jnp.int32, lg.shape, 0).astype(_F32)
    eg = jnp.exp(lg - jnp.max(lg, axis=0, keepdims=True))
    pg = eg / jnp.sum(eg, axis=0, keepdims=True)
    gp = jnp.max(pg, axis=0, keepdims=True)
    gi = jnp.min(jnp.where(pg == gp, row, float(MOE_GROUPS)), axis=0, keepdims=True)
    le = jnp.zeros_like(lg)
    for g in range(MOE_GROUPS):
        r0 = EXPERT_ROW0 + g * EXPERTS_PER_GROUP
        le = le + jnp.where(gi == float(g), logits[r0:r0 + EXPERTS_PER_GROUP], 0.0)
    ee = jnp.exp(le - jnp.max(le, axis=0, keepdims=True))
    ev = ee / jnp.sum(ee, axis=0, keepdims=True)
    v1 = jnp.max(ev, axis=0, keepdims=True)
    i1 = jnp.min(jnp.where(ev == v1, row, float(EXPERTS_PER_GROUP)), axis=0, keepdims=True)
    ev2 = jnp.where(row == i1, -1.0, ev)
    v2 = jnp.max(ev2, axis=0, keepdims=True)
    i2 = jnp.min(jnp.where(ev2 == v2, row, float(EXPERTS_PER_GROUP)), axis=0, keepdims=True)
    den = v1 + v2
    ids = jnp.concatenate([gi * EXPERTS_PER_GROUP + i1, gi * EXPERTS_PER_GROUP + i2], axis=0)
    eid_ref[...] = ids.astype(jnp.int32)
    gate_ref[...] = jnp.concatenate([gp * v1 / den, gp * v2 / den], axis=0)


def _conv_layer_kernel(x_ref, prev_ref, gmix_ref, win_ref, ck_ref, wout_ref, gffn_ref, wr_ref, br_ref,
                       h_ref, xn2_ref, eid_ref, gate_ref, st_ref, ubuf, *, shift, tm):
    i = pl.program_id(1)
    halo = ubuf.shape[0] - tm
    keep = (CONV_WIDTH - 1) * shift

    @pl.when(i == 0)
    def _():
        ubuf[halo - keep:halo, :] = prev_ref[0]

    @pl.when(i > 0)
    def _():
        ubuf[halo - keep:halo, :] = ubuf[halo + tm - keep:halo + tm, :]

    x = x_ref[0]
    xn = _rms(x, gmix_ref[...]).astype(_BF16)
    c_gate = _dot(xn, win_ref[:, 0:D_MODEL])
    hid = _dot(xn, win_ref[:, 2 * D_MODEL:3 * D_MODEL])
    u = c_gate * hid
    ubuf[halo:halo + tm, :] = u
    u1 = ubuf[halo - shift:halo - shift + tm, :]
    u2 = ubuf[halo - 2 * shift:halo - 2 * shift + tm, :]
    conv = ck_ref[0:1, :] * u2 + ck_ref[1:2, :] * u1 + ck_ref[2:3, :] * u
    b_gate = _dot(xn, win_ref[:, D_MODEL:2 * D_MODEL])
    y = _dot((b_gate * conv).astype(_BF16), wout_ref[...])
    h = x + y
    h_ref[0] = h
    st_ref[0] = ubuf[halo + tm - keep:halo + tm, :]
    xn2 = _rms(h, gffn_ref[...])
    xn2_ref[...] = xn2.astype(_BF16)
    _route(xn2, wr_ref, br_ref, eid_ref, gate_ref)


def _conv_layer(x3, prev3, gmix, win, ck, wout, gffn, wr, br, *, shift, tm):
    nb, s, d = x3.shape
    nt = s // tm
    keep = (CONV_WIDTH - 1) * shift
    halo = -(-keep // 8) * 8
    t_all = nb * s
    const = lambda b, i: (0, 0)
    return pl.pallas_call(
        functools.partial(_conv_layer_kernel, shift=shift, tm=tm),
        grid=(nb, nt),
        in_specs=[
            pl.BlockSpec((1, tm, d), lambda b, i: (b, i, 0)),
            pl.BlockSpec((1, keep, d), lambda b, i: (b, 0, 0)),
            pl.BlockSpec((1, d), const),
            pl.BlockSpec((d, 3 * d), const),
            pl.BlockSpec((CONV_WIDTH, d), const),
            pl.BlockSpec((d, d), const),
            pl.BlockSpec((1, d), const),
            pl.BlockSpec((ROUTER_ROWS, d), const),
            pl.BlockSpec((ROUTER_ROWS, 1), const),
        ],
        out_specs=[
            pl.BlockSpec((1, tm, d), lambda b, i: (b, i, 0)),
            pl.BlockSpec((tm, d), lambda b, i: (b * nt + i, 0)),
            pl.BlockSpec((2, tm), lambda b, i: (0, b * nt + i)),
            pl.BlockSpec((2, tm), lambda b, i: (0, b * nt + i)),
            pl.BlockSpec((1, keep, d), lambda b, i: (b, 0, 0)),
        ],
        out_shape=[
            jax.ShapeDtypeStruct((nb, s, d), _F32),
            jax.ShapeDtypeStruct((t_all, d), _BF16),
            jax.ShapeDtypeStruct((2, t_all), jnp.int32),
            jax.ShapeDtypeStruct((2, t_all), _F32),
            jax.ShapeDtypeStruct((nb, keep, d), _F32),
        ],
        scratch_shapes=[pltpu.VMEM((halo + tm, d), _F32)],
        compiler_params=pltpu.CompilerParams(
            dimension_semantics=("arbitrary", "arbitrary"), vmem_limit_bytes=VMEM_LIMIT),
        name="conv_layer",
    )(x3, prev3, gmix, win, ck, wout, gffn, wr, br)


GATHER_CHUNK = 256


def _moe_ffn_kernel(be_ref, nv_ref, tab_ref, x_hbm, tok_ref, w1_ref, w3_ref, w2_ref, y_hbm,
                    xres, xacc, obuf, xsem, ssem, *, n_tok, n_blocks):
    j = pl.program_id(0)
    slot = j % 2
    other = 1 - slot
    half = EXPERT_BLOCK // 2

    def token_of(v):
        if n_tok & (n_tok - 1) == 0:
            return v & (n_tok - 1)
        return v - jnp.where(v >= n_tok, n_tok, 0)

    def scatter_rows(blk, s, n, k0, k1):
        for k in range(k0, k1):
            @pl.when(k < n)
            def _():
                v = tab_ref[blk * EXPERT_BLOCK + k]
                pltpu.make_async_copy(obuf.at[s, k], y_hbm.at[v], ssem.at[s]).start()

    def wait_scatter(n, s):
        @pl.when(n > 0)
        def _():
            pltpu.make_async_copy(obuf.at[s, pl.ds(0, n)], y_hbm.at[pl.ds(0, n)], ssem.at[s]).wait()

    @pl.when(j == 0)
    def _():
        load = pltpu.make_async_copy(x_hbm, xres, xsem.at[0])
        load.start()
        load.wait()

    n_here = nv_ref[j]
    prv = jnp.maximum(j - 1, 0)
    n_prev = jnp.where(j > 0, nv_ref[prv], 0)

    first = token_of(tab_ref[j * EXPERT_BLOCK])
    last = token_of(tab_ref[j * EXPERT_BLOCK + jnp.maximum(n_here - 1, 0)])
    c_lo = first // GATHER_CHUNK
    c_hi = jnp.where(n_here > 0, last // GATHER_CHUNK + 1, c_lo)
    tok = tok_ref[...]
    col = lax.broadcasted_iota(jnp.int32, (EXPERT_BLOCK, GATHER_CHUNK), 1)
    xacc[...] = jnp.zeros(xacc.shape, _F32)

    def pick(c, carry):
        base = pl.multiple_of(c * GATHER_CHUNK, GATHER_CHUNK)
        onehot = jnp.where(tok - base == col, 1.0, 0.0).astype(_BF16)
        xacc[...] += _dot(onehot, xres[pl.ds(base, GATHER_CHUNK), :])
        return carry

    lax.fori_loop(c_lo, c_hi, pick, 0)
    x = xacc[...].astype(_BF16)

    scatter_rows(prv, other, n_prev, 0, half)
    a = _dot(x, w1_ref[0])
    b = _dot(x, w3_ref[0])
    scatter_rows(prv, other, n_prev, half, EXPERT_BLOCK)
    hmid = (jax.nn.silu(a) * b).astype(_BF16)
    y = _dot(hmid, w2_ref[0])

    @pl.when(j >= 2)
    def _():
        wait_scatter(nv_ref[jnp.maximum(j - 2, 0)], slot)

    _store_row_tiles(obuf, y, (slot,))

    @pl.when(j == n_blocks - 1)
    def _():
        wait_scatter(n_prev, other)
        scatter_rows(j, slot, n_here, 0, EXPERT_BLOCK)
        wait_scatter(n_here, slot)


def _moe_ffn(block_expert, n_valid, table, xn, w1, w3, w2):
    n_tok, d = xn.shape
    n_blocks = block_expert.shape[0]
    assert n_blocks >= 2 and n_tok % GATHER_CHUNK == 0
    tok_col = jnp.where(table < 0, -1, table % n_tok).reshape(-1, 1)
    wmap = lambda j, be, nv, tab: (be[j], 0, 0)
    return pl.pallas_call(
        functools.partial(_moe_ffn_kernel, n_tok=n_tok, n_blocks=n_blocks),
        grid_spec=pltpu.PrefetchScalarGridSpec(
            num_scalar_prefetch=3,
            grid=(n_blocks,),
            in_specs=[
                pl.BlockSpec(memory_space=pl.ANY),
                pl.BlockSpec((EXPERT_BLOCK, 1), lambda j, be, nv, tab: (j, 0)),
                pl.BlockSpec((1, d, D_EXPERT), wmap),
                pl.BlockSpec((1, d, D_EXPERT), wmap),
                pl.BlockSpec((1, D_EXPERT, d), wmap),
            ],
            out_specs=pl.BlockSpec(memory_space=pl.ANY),
            scratch_shapes=[
                pltpu.VMEM((n_tok, d), _BF16),
                pltpu.VMEM((EXPERT_BLOCK, d), _F32),
                pltpu.VMEM((2, EXPERT_BLOCK, N_CHUNKS, LANES), _F32),
                pltpu.SemaphoreType.DMA((1,)),
                pltpu.SemaphoreType.DMA((2,)),
            ],
        ),
        out_shape=jax.ShapeDtypeStruct((2 * n_tok, N_CHUNKS, LANES), _F32),
        compiler_params=pltpu.CompilerParams(
            dimension_semantics=("arbitrary",), vmem_limit_bytes=VMEM_LIMIT),
        name="moe_ffn",
    )(block_expert, n_valid, table, xn, tok_col, w1, w3, w2)


def _moe_plan(eid, n_tok):
    n_assign = 2 * n_tok
    flat_e = eid.T.reshape(-1)
    onehot = (flat_e[:, None] == jnp.arange(N_EXPERTS, dtype=jnp.int32)[None, :]).astype(jnp.int32)
    csum = jnp.cumsum(onehot, axis=0)
    counts = csum[-1]
    rank = jnp.sum(csum * onehot, axis=1) - 1
    padded = (counts + EXPERT_BLOCK - 1) // EXPERT_BLOCK * EXPERT_BLOCK
    pad_end = jnp.cumsum(padded)
    pad_start = pad_end - padded
    slot = pad_start[flat_e] + rank
    n_blocks = -(-(n_assign + N_EXPERTS * (EXPERT_BLOCK - 1)) // EXPERT_BLOCK)
    n_slots = n_blocks * EXPERT_BLOCK
    a = jnp.arange(n_assign, dtype=jnp.int32)
    value = (a % 2) * n_tok + a // 2
    table = jnp.full((n_slots,), -1, jnp.int32).at[slot].set(value)
    block_start = jnp.arange(n_blocks, dtype=jnp.int32) * EXPERT_BLOCK
    block_expert = jnp.minimum(jnp.sum((pad_end[None, :] <= block_start[:, None]).astype(jnp.int32), axis=1),
                               N_EXPERTS - 1)
    n_valid = jnp.clip((pad_start + counts)[block_expert] - block_start, 0, EXPERT_BLOCK).astype(jnp.int32)
    return block_expert, n_valid, table


def _combine(h_ref, y0_ref, y1_ref, gate_ref):
    gate = gate_ref[...]
    return h_ref[...] + (gate[:, 0:1] * _load_row_tiles(y0_ref) + gate[:, 1:2] * _load_row_tiles(y1_ref))


def _emit_head(r, h, outs, scr):
    tm = r.shape[0]
    if scr is not None:
        scr[h] = r
    for ref, dil in outs:
        if dil == 0:
            ref[:, h, :] = r
        elif dil == 1:
            ref[:, _chunk(h)] = r.astype(ref.dtype)
        else:
            for res in range(dil):
                col = res * D_MODEL + h * HEAD_DIM
                ref[:, col:col + HEAD_DIM] = scr[h, pl.ds(res, tm // dil, stride=dil), :].astype(ref.dtype)


def _rope(xh, cos, sin_signed):
    return xh * cos + pltpu.roll(xh, HEAD_DIM // 2, axis=1) * sin_signed


def _qkv_kernel(*refs, streams, per_seq, kept):
    (h_ref, y0_ref, y1_ref, gate_ref, gmix_ref, gkv_ref, wq_ref, wkv_ref, cos_ref, sin_ref) = refs[:10]
    outs = refs[10:]
    if streams:
        (h2_ref, q0_ref, q1_ref, q2_ref, k0_ref, k1_ref, k2_ref, v0_ref, v1_ref, v2_ref, kf_ref, vf_ref,
         scr_q1, scr_q2, scr_k, scr_v) = outs
        q_outs = [([(q0_ref, 1)], None), ([(q1_ref, DILATIONS[1])], scr_q1), ([(q2_ref, DILATIONS[2])], scr_q2)]
        k_outs = ([(k0_ref, 1), (k1_ref, DILATIONS[1]), (k2_ref, DILATIONS[2])], scr_k)
        v_outs = ([(v0_ref, 1), (v1_ref, DILATIONS[1]), (v2_ref, DILATIONS[2])], scr_v)
    else:
        h2_ref, q0_ref, q1_ref, q2_ref, kf_ref, vf_ref = outs
        q_outs = [([(q0_ref, 1)], None), ([(q1_ref, 1)], None), ([(q2_ref, 1)], None)]
        k_outs = ([(kf_ref, 0)], None)
        v_outs = ([(vf_ref, 0)], None)

    h2 = _combine(h_ref, y0_ref, y1_ref, gate_ref)
    h2_ref[...] = h2
    cos = cos_ref[...]
    sin = sin_ref[...]
    xn = _rms(h2, gmix_ref[...]).astype(_BF16)
    scale = HEAD_DIM ** -0.5
    for g in range(N_GROUPS):
        q = _dot(xn, wq_ref[:, g * D_MODEL:(g + 1) * D_MODEL])
        for h in range(N_HEADS):
            _emit_head(_rope(q[:, _chunk(h)], cos, sin) * scale, h, *q_outs[g])
    xkv = _rms(h2, gkv_ref[...]).astype(_BF16)
    k = _dot(xkv, wkv_ref[:, 0:D_MODEL])
    for h in range(N_HEADS):
        _emit_head(_rope(k[:, _chunk(h)], cos, sin), h, *k_outs)
    v = _dot(xkv, wkv_ref[:, D_MODEL:2 * D_MODEL])
    for h in range(N_HEADS):
        _emit_head(v[:, _chunk(h)], h, *v_outs)
    if streams:
        @pl.when(pl.program_id(0) % per_seq >= per_seq - kept)
        def _():
            for h in range(N_HEADS):
                kf_ref[:, h, :] = scr_k[h]
                vf_ref[:, h, :] = scr_v[h]


def _qkv(h, ypair, gate, gmix, gkv, wq, wkv, cos, sin, *, tm, streams, seq, keep):
    t, d = h.shape
    nt = t // tm
    per_seq = seq // tm
    kept = keep // tm
    assert seq % tm == 0 and keep % tm == 0

    def tail(i):
        return ((i // per_seq) * kept + jnp.maximum(i % per_seq - (per_seq - kept), 0), 0, 0)

    tail_f32 = (pl.BlockSpec((tm, N_HEADS, HEAD_DIM), tail),
                jax.ShapeDtypeStruct((t // seq * keep, N_HEADS, HEAD_DIM), _F32))
    row = lambda i: (i, 0)
    const = lambda i: (0, 0)
    nat = lambda dt: (pl.BlockSpec((tm, d), row), jax.ShapeDtypeStruct((t, d), dt))

    def stream(dil):
        return (pl.BlockSpec((tm // dil, dil * d), row), jax.ShapeDtypeStruct((t // dil, dil * d), _BF16))

    if streams:
        d1, d2 = DILATIONS[1], DILATIONS[2]
        outs = [nat(_F32), nat(_BF16), stream(d1), stream(d2), nat(_BF16), stream(d1), stream(d2),
                nat(_BF16), stream(d1), stream(d2), tail_f32, tail_f32]
        scratch = [pltpu.VMEM((N_HEADS, tm, HEAD_DIM), _F32)] * 4
    else:
        outs = [nat(_F32), nat(_BF16), nat(_BF16), nat(_BF16), tail_f32, tail_f32]
        scratch = []
    return pl.pallas_call(
        functools.partial(_qkv_kernel, streams=streams, per_seq=per_seq, kept=kept),
        grid=(nt,),
        in_specs=[
            pl.BlockSpec((tm, d), row),
            pl.BlockSpec((tm, N_CHUNKS, LANES), lambda i: (i, 0, 0)),
            pl.BlockSpec((tm, N_CHUNKS, LANES), lambda i: (nt + i, 0, 0)),
            pl.BlockSpec((tm, 2), row),
            pl.BlockSpec((1, d), const),
            pl.BlockSpec((1, d), const),
            pl.BlockSpec((d, 3 * d), const),
            pl.BlockSpec((d, 2 * d), const),
            pl.BlockSpec((tm, HEAD_DIM), row),
            pl.BlockSpec((tm, HEAD_DIM), row),
        ],
        out_specs=[o[0] for o in outs],
        out_shape=[o[1] for o in outs],
        scratch_shapes=scratch,
        compiler_params=pltpu.CompilerParams(
            dimension_semantics=("arbitrary",), vmem_limit_bytes=VMEM_LIMIT),
        name="qkv_proj_streams" if streams else "qkv_proj",
    )(h, ypair, ypair, gate, gmix, gkv, wq, wkv, cos, sin)


ATTN_TQ = 512


def _attn_prompt_kernel(q_ref, kp_ref, kc_ref, vp_ref, vc_ref, o_ref, st_ref, *, tq):
    i = pl.program_id(1)
    qi = lax.broadcasted_iota(jnp.int32, (SPAN, 2 * SPAN), 0)
    kj = lax.broadcasted_iota(jnp.int32, (SPAN, 2 * SPAN), 1)
    band = (kj >= qi) & (kj <= qi + SPAN)
    band0 = band & (kj >= jnp.where(i > 0, 0, SPAN))
    lane = lax.broadcasted_iota(jnp.int32, (SPAN, HEAD_DIM), 1)
    ones = jnp.ones((2 * SPAN, HEAD_DIM), _BF16)
    for a in range(tq // SPAN):
        rows = slice(a * SPAN, (a + 1) * SPAN)
        st = jnp.zeros((SPAN, HEAD_DIM), _F32)
        for h in range(N_HEADS):
            hs = _chunk(h)
            if a == 0:
                keys = jnp.concatenate([kp_ref[:, hs], kc_ref[rows, hs]], axis=0)
                vals = jnp.concatenate([vp_ref[:, hs], vc_ref[rows, hs]], axis=0)
                mask = band0
            else:
                both = slice((a - 1) * SPAN, (a + 1) * SPAN)
                keys, vals, mask = kc_ref[both, hs], vc_ref[both, hs], band
            s = jnp.where(mask, _dot_nt(q_ref[rows, hs], keys), NEG_BIG)
            m = jnp.max(s, axis=1, keepdims=True)
            p = jnp.exp(s - m).astype(_BF16)
            ol = _dot(p, jnp.concatenate([vals, ones], axis=1))
            l = ol[:, HEAD_DIM:]
            o_ref[rows, hs] = ol[:, :HEAD_DIM] / l
            st = jnp.where(lane == h, m + jnp.log(l), st)
        st_ref[rows, :] = st


def _attn_prompt(q, k, v, batch, dilation):
    rows, width = q.shape
    d = width // dilation
    per_batch = rows // batch
    tq = min(ATTN_TQ, per_batch)
    assert per_batch % tq == 0 and tq % SPAN == 0
    nq = per_batch // tq
    sub = tq // SPAN
    cur = lambda b, i, r: (b * nq + i, r)
    prev = lambda b, i, r: ((b * nq + i) * sub - jnp.where(i > 0, 1, 0), r)
    big = (tq, d)
    small = (SPAN, d)
    return pl.pallas_call(
        functools.partial(_attn_prompt_kernel, tq=tq),
        grid=(batch, nq, dilation),
        in_specs=[pl.BlockSpec(big, cur), pl.BlockSpec(small, prev), pl.BlockSpec(big, cur),
                  pl.BlockSpec(small, prev), pl.BlockSpec(big, cur)],
        out_specs=[pl.BlockSpec(big, cur), pl.BlockSpec((tq, HEAD_DIM), cur)],
        out_shape=[jax.ShapeDtypeStruct((rows, width), _F32),
                   jax.ShapeDtypeStruct((rows, dilation * HEAD_DIM), _F32)],
        compiler_params=pltpu.CompilerParams(
            dimension_semantics=("arbitrary", "arbitrary", "arbitrary"), vmem_limit_bytes=VMEM_LIMIT),
        name=f"attn_prompt_d{dilation}",
    )(q, k, k, v, v)


N_NEW = 8
CACHE_A_GROUPS = (PAST_LEN - WINDOWS[1]) // DILATIONS[2]
CACHE_A_ROWS = CACHE_A_GROUPS * N_NEW
CACHE_B_ROWS = WINDOWS[1]
CACHE_B_GROUPS = CACHE_B_ROWS // DILATIONS[2]
KEYS_REAL = CACHE_A_ROWS + CACHE_B_ROWS + N_NEW
KEYS_PAD = -(-KEYS_REAL // 128) * 128


def _sample_key_positions():
    pos = np.full((KEYS_PAD,), -1, np.int64)
    a = np.arange(CACHE_A_ROWS)
    pos[:CACHE_A_ROWS] = (a // N_NEW) * DILATIONS[2] + a % N_NEW
    pos[CACHE_A_ROWS:CACHE_A_ROWS + CACHE_B_ROWS] = PAST_LEN - CACHE_B_ROWS + np.arange(CACHE_B_ROWS)
    pos[CACHE_A_ROWS + CACHE_B_ROWS:KEYS_REAL] = PAST_LEN + np.arange(N_NEW)
    return pos


def _sample_bias():
    pos = _sample_key_positions()
    bias = np.full((N_GROUPS * N_NEW, KEYS_PAD), NEG_BIG, np.float32)
    for g in range(N_GROUPS):
        for n in range(N_NEW):
            delta = PAST_LEN + n - pos
            ok = (pos >= 0) & (delta >= 0) & (delta <= WINDOWS[g]) & (delta % DILATIONS[g] == 0)
            assert int(ok.sum()) == SPAN + 1
            bias[g * N_NEW + n, ok] = 0.0
    return bias


def _attn_sample_kernel(q0_ref, q1_ref, q2_ref, kn_ref, vn_ref, bias_ref, ck_hbm, cv_hbm,
                        o_ref, ka, kb, va, vb, kall, vall, sem, *, n_seq):
    b = pl.program_id(0)
    slot = b % 2

    def cache_copies(seq, s):
        cps = []
        for h in range(N_HEADS):
            for src, dst_a, dst_b in ((ck_hbm, ka, kb), (cv_hbm, va, vb)):
                cps.append(pltpu.make_async_copy(
                    src.at[seq, pl.ds(0, CACHE_A_GROUPS), pl.ds(0, N_NEW), h], dst_a.at[s, h], sem.at[s]))
                cps.append(pltpu.make_async_copy(
                    src.at[seq, pl.ds(CACHE_A_GROUPS, CACHE_B_GROUPS), :, h], dst_b.at[s, h], sem.at[s]))
        return cps

    @pl.when(b == 0)
    def _():
        kall[KEYS_REAL:KEYS_PAD, :] = jnp.zeros((KEYS_PAD - KEYS_REAL, D_MODEL), _BF16)
        vall[KEYS_REAL:KEYS_PAD, :] = jnp.zeros((KEYS_PAD - KEYS_REAL, D_MODEL), _BF16)
        for cp in cache_copies(0, 0):
            cp.start()

    @pl.when(b + 1 < n_seq)
    def _():
        for cp in cache_copies(b + 1, 1 - slot):
            cp.start()

    for cp in cache_copies(b, slot):
        cp.wait()

    nb0 = CACHE_A_ROWS + CACHE_B_ROWS
    for src_a, src_b, src_n, dst in ((ka, kb, kn_ref, kall), (va, vb, vn_ref, vall)):
        for h in range(N_HEADS):
            hs = _chunk(h)
            dst[0:CACHE_A_ROWS, hs] = src_a[slot, h].reshape(CACHE_A_ROWS, HEAD_DIM).astype(_BF16)
            dst[CACHE_A_ROWS:nb0, hs] = src_b[slot, h].reshape(CACHE_B_ROWS, HEAD_DIM).astype(_BF16)
            dst[nb0:KEYS_REAL, hs] = src_n[0, :, h, :].astype(_BF16)

    bias = bias_ref[...]
    for h in range(N_HEADS):
        hs = _chunk(h)
        qh = jnp.concatenate([q0_ref[0, :, hs].astype(_F32), q1_ref[0, :, hs].astype(_F32),
                              q2_ref[0, :, hs].astype(_F32)], axis=0).astype(_BF16)
        s = _dot_nt(qh, kall[:, hs]) + bias
        m = jnp.max(s, axis=1, keepdims=True)
        m8 = jnp.maximum(jnp.maximum(m[0:N_NEW], m[N_NEW:2 * N_NEW]), m[2 * N_NEW:3 * N_NEW])
        p = jnp.exp(s - jnp.concatenate([m8, m8, m8], axis=0))
        p8 = p[0:N_NEW] + p[N_NEW:2 * N_NEW] + p[2 * N_NEW:3 * N_NEW]
        l8 = jnp.sum(p8, axis=1, keepdims=True)
        o_ref[0, :, hs] = _dot(p8.astype(_BF16), vall[:, hs]) / l8


def _attn_sample(q0, q1, q2, cache_k, cache_v, k_new, v_new):
    nb, n_new, d = q0.shape
    past = cache_k.shape[1]
    assert past == PAST_LEN and n_new == N_NEW and d == D_MODEL
    assert cache_k.shape[2:] == (N_HEADS, HEAD_DIM)
    ck = cache_k.reshape(nb, past // DILATIONS[2], DILATIONS[2], N_HEADS, HEAD_DIM)
    cv = cache_v.reshape(nb, past // DILATIONS[2], DILATIONS[2], N_HEADS, HEAD_DIM)
    bias = jnp.asarray(_sample_bias())
    new = pl.BlockSpec((1, n_new, d), lambda b: (b, 0, 0))
    new_kv = pl.BlockSpec((1, n_new, N_HEADS, HEAD_DIM), lambda b: (b, 0, 0, 0))
    part_a = (2, N_HEADS, CACHE_A_GROUPS, N_NEW, HEAD_DIM)
    part_b = (2, N_HEADS, CACHE_B_GROUPS, DILATIONS[2], HEAD_DIM)
    return pl.pallas_call(
        functools.partial(_attn_sample_kernel, n_seq=nb),
        grid=(nb,),
        in_specs=[new, new, new, new_kv, new_kv,
                  pl.BlockSpec((N_GROUPS * N_NEW, KEYS_PAD), lambda b: (0, 0)),
                  pl.BlockSpec(memory_space=pl.ANY), pl.BlockSpec(memory_space=pl.ANY)],
        out_specs=new,
        out_shape=jax.ShapeDtypeStruct((nb, n_new, d), _F32),
        scratch_shapes=[pltpu.VMEM(part_a, _F32), pltpu.VMEM(part_b, _F32),
                        pltpu.VMEM(part_a, _F32), pltpu.VMEM(part_b, _F32),
                        pltpu.VMEM((KEYS_PAD, d), _BF16), pltpu.VMEM((KEYS_PAD, d), _BF16),
                        pltpu.SemaphoreType.DMA((2,))],
        compiler_params=pltpu.CompilerParams(
            dimension_semantics=("arbitrary",), vmem_limit_bytes=VMEM_LIMIT),
        name="attn_sample",
    )(q0, q1, q2, k_new, v_new, bias, ck, cv)


def _attn_out_kernel(*refs, n_groups, tm):
    if n_groups == 1:
        (o_ref, h_ref, wo_ref, gffn_ref, wr_ref, br_ref,
         h3_ref, xn2_ref, eid_ref, gate_ref) = refs
        o = o_ref[...].astype(_BF16)
    else:
        (o0_ref, o1_ref, o2_ref, s0_ref, s1_ref, s2_ref, h_ref, wo_ref, gffn_ref, wr_ref, br_ref,
         h3_ref, xn2_ref, eid_ref, gate_ref, obuf, scr_o1, scr_o2, scr_s1, scr_s2) = refs
        for o_ref, s_ref, scr_o, scr_s, dil in ((o1_ref, s1_ref, scr_o1, scr_s1, DILATIONS[1]),
                                                (o2_ref, s2_ref, scr_o2, scr_s2, DILATIONS[2])):
            for res in range(dil):
                dst = pl.ds(res, tm // dil, stride=dil)
                scr_s[dst, :] = s_ref[:, res * HEAD_DIM:(res + 1) * HEAD_DIM]
                for h in range(N_HEADS):
                    col = res * D_MODEL + h * HEAD_DIM
                    scr_o[h, dst, :] = o_ref[:, col:col + HEAD_DIM]
        sts = [s0_ref[...], scr_s1[...], scr_s2[...]]
        mx = jnp.maximum(jnp.maximum(sts[0], sts[1]), sts[2])
        es = [jnp.exp(s - mx) for s in sts]
        den = es[0] + es[1] + es[2]
        ws = [e / den for e in es]
        for h in range(N_HEADS):
            acc = ws[0][:, h:h + 1] * o0_ref[:, _chunk(h)]
            acc = acc + ws[1][:, h:h + 1] * scr_o1[h]
            acc = acc + ws[2][:, h:h + 1] * scr_o2[h]
            obuf[:, _chunk(h)] = acc.astype(_BF16)
        o = obuf[...]
    h3 = h_ref[...] + _dot(o, wo_ref[...])
    h3_ref[...] = h3
    xn2 = _rms(h3, gffn_ref[...])
    xn2_ref[...] = xn2.astype(_BF16)
    _route(xn2, wr_ref, br_ref, eid_ref, gate_ref)


def _attn_out(os, sts, h, wo, gffn, wr, br, *, tm):
    t, d = h.shape
    n_groups = len(os)
    row = lambda i: (i, 0)
    const = lambda i: (0, 0)
    if n_groups == 1:
        in_specs = [pl.BlockSpec((tm, d), row)]
        scratch = []
        args = [os[0]]
    else:
        in_specs = [pl.BlockSpec((tm // dil, dil * d), row) for dil in DILATIONS]
        in_specs += [pl.BlockSpec((tm // dil, dil * HEAD_DIM), row) for dil in DILATIONS]
        scratch = [pltpu.VMEM((tm, d), _BF16),
                   pltpu.VMEM((N_HEADS, tm, HEAD_DIM), _F32), pltpu.VMEM((N_HEADS, tm, HEAD_DIM), _F32),
                   pltpu.VMEM((tm, HEAD_DIM), _F32), pltpu.VMEM((tm, HEAD_DIM), _F32)]
        args = list(os) + list(sts)
    in_specs += [pl.BlockSpec((tm, d), row), pl.BlockSpec((d, d), const), pl.BlockSpec((1, d), const),
                 pl.BlockSpec((ROUTER_ROWS, d), const), pl.BlockSpec((ROUTER_ROWS, 1), const)]
    return pl.pallas_call(
        functools.partial(_attn_out_kernel, n_groups=n_groups, tm=tm),
        grid=(t // tm,),
        in_specs=in_specs,
        out_specs=[pl.BlockSpec((tm, d), row), pl.BlockSpec((tm, d), row),
                   pl.BlockSpec((2, tm), lambda i: (0, i)), pl.BlockSpec((2, tm), lambda i: (0, i))],
        out_shape=[jax.ShapeDtypeStruct((t, d), _F32), jax.ShapeDtypeStruct((t, d), _BF16),
                   jax.ShapeDtypeStruct((2, t), jnp.int32), jax.ShapeDtypeStruct((2, t), _F32)],
        scratch_shapes=scratch,
        compiler_params=pltpu.CompilerParams(
            dimension_semantics=("arbitrary",), vmem_limit_bytes=VMEM_LIMIT),
        name=f"attn_out_g{n_groups}",
    )(*args, h, wo, gffn, wr, br)


def _final_kernel(h_ref, y0_ref, y1_ref, gate_ref, g_ref, out_ref):
    out_ref[...] = _rms(_combine(h_ref, y0_ref, y1_ref, gate_ref), g_ref[...])


def _final(h, ypair, gate, g, *, tm):
    t, d = h.shape
    nt = t // tm
    row = lambda i: (i, 0)
    return pl.pallas_call(
        _final_kernel,
        grid=(nt,),
        in_specs=[pl.BlockSpec((tm, d), row),
                  pl.BlockSpec((tm, N_CHUNKS, LANES), lambda i: (i, 0, 0)),
                  pl.BlockSpec((tm, N_CHUNKS, LANES), lambda i: (nt + i, 0, 0)),
                  pl.BlockSpec((tm, 2), row),
                  pl.BlockSpec((1, d), lambda i: (0, 0))],
        out_specs=pl.BlockSpec((tm, d), row),
        out_shape=jax.ShapeDtypeStruct((t, d), _F32),
        compiler_params=pltpu.CompilerParams(
            dimension_semantics=("arbitrary",), vmem_limit_bytes=VMEM_LIMIT),
        name="final_norm",
    )(h, ypair, ypair, gate, g)


def _rope_tables(pos):
    half = HEAD_DIM // 2
    inv_freq = jnp.power(jnp.float32(ROPE_THETA), -jnp.arange(half, dtype=jnp.float32) / half)
    ang = pos.astype(jnp.float32)[:, None] * inv_freq[None, :]
    cos = jnp.cos(ang)
    sin = jnp.sin(ang)
    return jnp.concatenate([cos, cos], axis=-1), jnp.concatenate([-sin, sin], axis=-1)


def _router_params(wg, bg, we, be):
    wr = jnp.zeros((ROUTER_ROWS, D_MODEL), _F32)
    wr = wr.at[0:MOE_GROUPS].set(wg.T).at[EXPERT_ROW0:EXPERT_ROW0 + N_EXPERTS].set(we.T)
    br = jnp.zeros((ROUTER_ROWS, 1), _F32)
    br = br.at[0:MOE_GROUPS, 0].set(bg).at[EXPERT_ROW0:EXPERT_ROW0 + N_EXPERTS, 0].set(be)
    return wr, br


def _moe(xn2, eid, w1, w3, w2):
    block_expert, n_valid, table = _moe_plan(eid, xn2.shape[0])
    return _moe_ffn(block_expert, n_valid, table, xn2, w1, w3, w2)


def _forward(x3, prev3, pos_rows, attend, p, *, shift, tm, streams, keep):
    h1, xn2, eid, gate, state = _conv_layer(
        x3, prev3, p['gmix'][0], p['win'], p['ck'], p['wout'], p['gffn'][0], *p['router'][0],
        shift=shift, tm=tm)
    t = x3.shape[0] * x3.shape[1]
    ypair = _moe(xn2, eid, *p['experts'][0])
    cos, sin = _rope_tables(pos_rows)
    h2, *qkv, kf, vf = _qkv(h1.reshape(t, D_MODEL), ypair, gate.T, p['gmix'][1], p['gkv'],
                            p['wq'], p['wkv'], cos, sin, tm=256, streams=streams,
                            seq=x3.shape[1], keep=keep)
    os, sts = attend(qkv, kf, vf)
    h3, xn2, eid, gate = _attn_out(os, sts, h2, p['wo'], p['gffn'][1], *p['router'][1], tm=256)
    ypair = _moe(xn2, eid, *p['experts'][1])
    y = _final(h3, ypair, gate.T, p['gfinal'], tm=512)
    return y, state, kf, vf


def kernel(x_prompt, x_sample, cache_k, cache_v, state_conv, norm_mix, norm_ffn, norm_kv, norm_final,
           conv_w_in, conv_kernel, conv_w_out, attn_w_q, attn_w_kv, attn_w_o, router_group_w,
           router_group_b, router_expert_w, router_expert_b, expert_w1, expert_w3, expert_w2):
    b_p, s_p, d = x_prompt.shape
    b_s, n_new, _ = x_sample.shape
    assert d == D_MODEL and n_new == N_NEW and s_p % (DILATIONS[2] * SPAN) == 0
    assert norm_mix.shape[0] == 2 and conv_w_in.shape[0] == 1 and attn_w_q.shape[0] == 1
    assert cache_k.shape[1] == PAST_LEN

    p = {
        'gmix': [norm_mix[l].reshape(1, d) for l in range(2)],
        'gffn': [norm_ffn[l].reshape(1, d) for l in range(2)],
        'gkv': norm_kv.reshape(1, d),
        'gfinal': norm_final.reshape(1, d),
        'win': conv_w_in[0].astype(_BF16),
        'ck': conv_kernel[0],
        'wout': conv_w_out[0].astype(_BF16),
        'wq': attn_w_q[0].astype(_BF16),
        'wkv': attn_w_kv.astype(_BF16),
        'wo': attn_w_o[0].astype(_BF16),
        'router': [_router_params(router_group_w[l], router_group_b[l], router_expert_w[l], router_expert_b[l])
                   for l in range(2)],
        'experts': [(expert_w1[l].astype(_BF16), expert_w3[l].astype(_BF16), expert_w2[l].astype(_BF16))
                    for l in range(2)],
    }

    def attend_prompt(qkv, kf, vf):
        q0, q1, q2, k0, k1, k2, v0, v1, v2 = qkv
        os, sts = [], []
        for q, k, v, dil in ((q0, k0, v0, DILATIONS[0]), (q1, k1, v1, DILATIONS[1]), (q2, k2, v2, DILATIONS[2])):
            o, st = _attn_prompt(q, k, v, b_p, dil)
            os.append(o)
            sts.append(st)
        return os, sts

    pos_p = jnp.tile(jnp.arange(s_p, dtype=jnp.int32), b_p)
    zero_state = jnp.zeros((b_p, CONV_WIDTH - 1, d), x_prompt.dtype)
    keep = min(max(WINDOWS), s_p)
    y_p, st_p, kf_p, vf_p = _forward(x_prompt, zero_state, pos_p, attend_prompt, p, shift=1, tm=512,
                                     streams=True, keep=keep)
    y_prompt = y_p.reshape(b_p, s_p, d)
    k_p = kf_p.reshape(b_p, keep, N_HEADS, HEAD_DIM)
    v_p = vf_p.reshape(b_p, keep, N_HEADS, HEAD_DIM)
    conv_p = st_p[None]

    halves = 2
    bh = b_s // halves

    def to_rows(a):
        w = a.shape[-1]
        return a.reshape(halves, bh, n_new, w).transpose(0, 2, 1, 3).reshape(halves * n_new * bh, w)

    def to_batch(a):
        w = a.shape[1:]
        return jnp.swapaxes(a.reshape(halves, n_new, bh, *w), 1, 2).reshape(b_s, n_new, *w)

    def attend_sample(qkv, kf, vf):
        q0, q1, q2 = qkv
        o = _attn_sample(to_batch(q0), to_batch(q1), to_batch(q2), cache_k, cache_v,
                         to_batch(kf), to_batch(vf))
        return [to_rows(o)], None

    x_s = to_rows(x_sample).reshape(halves, n_new * bh, d)
    prev_s = state_conv[0].reshape(halves, bh, CONV_WIDTH - 1, d).transpose(0, 2, 1, 3).reshape(
        halves, (CONV_WIDTH - 1) * bh, d)
    pos_s = jnp.tile(jnp.repeat(PAST_LEN + jnp.arange(n_new, dtype=jnp.int32), bh), halves)
    y_s, st_s, kf_s, vf_s = _forward(x_s, prev_s, pos_s, attend_sample, p, shift=bh, tm=n_new * bh,
                                     streams=False, keep=n_new * bh)
    y_sample = to_batch(y_s)
    k_s = to_batch(kf_s)
    v_s = to_batch(vf_s)
    conv_s = st_s.reshape(halves, CONV_WIDTH - 1, bh, d).transpose(0, 2, 1, 3).reshape(
        b_s, CONV_WIDTH - 1, d)[None]

    return (y_prompt, y_sample, k_p, v_p, conv_p, k_s, v_s, conv_s)
```

```python
import functools

import numpy as np
import jax
import jax.numpy as jnp
from jax import lax
from jax.experimental import pallas as pl
from jax.experimental.pallas import tpu as pltpu

D_MODEL = 1024
CONV_WIDTH = 3
WINDOWS = (128, 512, 2048)
DILATIONS = (1, 4, 16)
N_GROUPS = 3
N_HEADS = 8
HEAD_DIM = 128
SPAN = 128
ROPE_THETA = 10000.0
MOE_GROUPS = 4
EXPERTS_PER_GROUP = 4
N_EXPERTS = 16
D_EXPERT = 512
EXPERT_BLOCK = 128
RMS_EPS = 1e-6
NEG_BIG = -1e30
PAST_LEN = 2048

LANES = 128
N_CHUNKS = D_MODEL // LANES
ROUTER_ROWS = 32
EXPERT_ROW0 = 8
VMEM_LIMIT = 56 * 1024 * 1024

_F32 = jnp.float32
_BF16 = jnp.bfloat16


def _rms(x, g):
    ms = jnp.mean(x * x, axis=-1, keepdims=True)
    return (x * lax.rsqrt(ms + RMS_EPS)) * g


def _dot(a, b):
    return jnp.dot(a, b, preferred_element_type=_F32)


def _dot_nt(a, b, precision=None):
    return lax.dot_general(a, b, (((1,), (1,)), ((), ())), precision=precision,
                           preferred_element_type=_F32)


def _chunk(c):
    return slice(c * LANES, (c + 1) * LANES)


def _store_row_tiles(ref, x, lead=()):
    for c in range(N_CHUNKS):
        ref[lead + (slice(None), c, slice(None))] = x[:, _chunk(c)]


def _load_row_tiles(ref, lead=()):
    return jnp.concatenate([ref[lead + (slice(None), c, slice(None))] for c in range(N_CHUNKS)], axis=1)


def _route(xn, wr_ref, br_ref, eid_ref, gate_ref):
    logits = _dot_nt(wr_ref[...], xn, precision=lax.Precision.HIGHEST) + br_ref[...]
    lg = logits[0:MOE_GROUPS]
    row = lax.broadcasted_iota(jnp.int32, lg.shape, 0).astype(_F32)
    eg = jnp.exp(lg - jnp.max(lg, axis=0, keepdims=True))
    pg = eg / jnp.sum(eg, axis=0, keepdims=True)
    gp = jnp.max(pg, axis=0, keepdims=True)
    gi = jnp.min(jnp.where(pg == gp, row, float(MOE_GROUPS)), axis=0, keepdims=True)
    le = jnp.zeros_like(lg)
    for g in range(MOE_GROUPS):
        r0 = EXPERT_ROW0 + g * EXPERTS_PER_GROUP
        le = le + jnp.where(gi == float(g), logits[r0:r0 + EXPERTS_PER_GROUP], 0.0)
    ee = jnp.exp(le - jnp.max(le, axis=0, keepdims=True))
    ev = ee / jnp.sum(ee, axis=0, keepdims=True)
    v1 = jnp.max(ev, axis=0, keepdims=True)
    i1 = jnp.min(jnp.where(ev == v1, row, float(EXPERTS_PER_GROUP)), axis=0, keepdims=True)
    ev2 = jnp.where(row == i1, -1.0, ev)
    v2 = jnp.max(ev2, axis=0, keepdims=True)
    i2 = jnp.min(jnp.where(ev2 == v2, row, float(EXPERTS_PER_GROUP)), axis=0, keepdims=True)
    den = v1 + v2
    ids = jnp.concatenate([gi * EXPERTS_PER_GROUP + i1, gi * EXPERTS_PER_GROUP + i2], axis=0)
    eid_ref[...] = ids.astype(jnp.int32)
    gate_ref[...] = jnp.concatenate([gp * v1 / den, gp * v2 / den], axis=0)


def _conv_layer_kernel(x_ref, prev_ref, gmix_ref, win_ref, ck_ref, wout_ref, gffn_ref, wr_ref, br_ref,
                       h_ref, xn2_ref, eid_ref, gate_ref, st_ref, ubuf, *, shift, tm):
    i = pl.program_id(1)
    halo = ubuf.shape[0] - tm
    keep = (CONV_WIDTH - 1) * shift

    @pl.when(i == 0)
    def _():
        ubuf[halo - keep:halo, :] = prev_ref[0]

    @pl.when(i > 0)
    def _():
        ubuf[halo - keep:halo, :] = ubuf[halo + tm - keep:halo + tm, :]

    x = x_ref[0]
    xn = _rms(x, gmix_ref[...]).astype(_BF16)
    c_gate = _dot(xn, win_ref[:, 0:D_MODEL])
    hid = _dot(xn, win_ref[:, 2 * D_MODEL:3 * D_MODEL])
    u = c_gate * hid
    ubuf[halo:halo + tm, :] = u
    u1 = ubuf[halo - shift:halo - shift + tm, :]
    u2 = ubuf[halo - 2 * shift:halo - 2 * shift + tm, :]
    conv = ck_ref[0:1, :] * u2 + ck_ref[1:2, :] * u1 + ck_ref[2:3, :] * u
    b_gate = _dot(xn, win_ref[:, D_MODEL:2 * D_MODEL])
    y = _dot((b_gate * conv).astype(_BF16), wout_ref[...])
    h = x + y
    h_ref[0] = h
    st_ref[0] = ubuf[halo + tm - keep:halo + tm, :]
    xn2 = _rms(h, gffn_ref[...])
    _store_row_tiles(xn2_ref, xn2)
    _route(xn2, wr_ref, br_ref, eid_ref, gate_ref)


def _conv_layer(x3, prev3, gmix, win, ck, wout, gffn, wr, br, *, shift, tm):
    nb, s, d = x3.shape
    nt = s // tm
    keep = (CONV_WIDTH - 1) * shift
    halo = -(-keep // 8) * 8
    t_all = nb * s
    const = lambda b, i: (0, 0)
    return pl.pallas_call(
        functools.partial(_conv_layer_kernel, shift=shift, tm=tm),
        grid=(nb, nt),
        in_specs=[
            pl.BlockSpec((1, tm, d), lambda b, i: (b, i, 0)),
            pl.BlockSpec((1, keep, d), lambda b, i: (b, 0, 0)),
            pl.BlockSpec((1, d), const),
            pl.BlockSpec((d, 3 * d), const),
            pl.BlockSpec((CONV_WIDTH, d), const),
            pl.BlockSpec((d, d), const),
            pl.BlockSpec((1, d), const),
            pl.BlockSpec((ROUTER_ROWS, d), const),
            pl.BlockSpec((ROUTER_ROWS, 1), const),
        ],
        out_specs=[
            pl.BlockSpec((1, tm, d), lambda b, i: (b, i, 0)),
            pl.BlockSpec((tm, N_CHUNKS, LANES), lambda b, i: (b * nt + i, 0, 0)),
            pl.BlockSpec((2, tm), lambda b, i: (0, b * nt + i)),
            pl.BlockSpec((2, tm), lambda b, i: (0, b * nt + i)),
            pl.BlockSpec((1, keep, d), lambda b, i: (b, 0, 0)),
        ],
        out_shape=[
            jax.ShapeDtypeStruct((nb, s, d), _F32),
            jax.ShapeDtypeStruct((t_all, N_CHUNKS, LANES), _F32),
            jax.ShapeDtypeStruct((2, t_all), jnp.int32),
            jax.ShapeDtypeStruct((2, t_all), _F32),
            jax.ShapeDtypeStruct((nb, keep, d), _F32),
        ],
        scratch_shapes=[pltpu.VMEM((halo + tm, d), _F32)],
        compiler_params=pltpu.CompilerParams(
            dimension_semantics=("arbitrary", "arbitrary"), vmem_limit_bytes=VMEM_LIMIT),
        name="conv_layer",
    )(x3, prev3, gmix, win, ck, wout, gffn, wr, br)


def _moe_ffn_kernel(be_ref, nv_ref, tab_ref, x_hbm, w1_ref, w3_ref, w2_ref, y_hbm,
                    xbuf, obuf, w1b, w3b, w2b, gsem, ssem, *, n_tok, n_blocks):
    j = pl.program_id(0)
    slot = j % 2
    other = 1 - slot
    half = EXPERT_BLOCK // 2

    def token_of(v):
        if n_tok & (n_tok - 1) == 0:
            return v & (n_tok - 1)
        return v - jnp.where(v >= n_tok, n_tok, 0)

    def gather_rows(blk, s, n, k0, k1):
        for k in range(k0, k1):
            @pl.when(k < n)
            def _():
                tok = token_of(tab_ref[blk * EXPERT_BLOCK + k])
                pltpu.make_async_copy(x_hbm.at[tok], xbuf.at[s, k], gsem.at[s]).start()

    def scatter_rows(blk, s, n, k0, k1):
        for k in range(k0, k1):
            @pl.when(k < n)
            def _():
                v = tab_ref[blk * EXPERT_BLOCK + k]
                pltpu.make_async_copy(obuf.at[s, k], y_hbm.at[v], ssem.at[s]).start()

    def wait_gather(n, s):
        @pl.when(n > 0)
        def _():
            pltpu.make_async_copy(x_hbm.at[pl.ds(0, n)], xbuf.at[s, pl.ds(0, n)], gsem.at[s]).wait()

    def wait_scatter(n, s):
        @pl.when(n > 0)
        def _():
            pltpu.make_async_copy(obuf.at[s, pl.ds(0, n)], y_hbm.at[pl.ds(0, n)], ssem.at[s]).wait()

    @pl.when(j == 0)
    def _():
        xbuf[...] = jnp.zeros(xbuf.shape, _F32)
        gather_rows(0, 0, nv_ref[0], 0, EXPERT_BLOCK)

    nxt = jnp.minimum(j + 1, n_blocks - 1)
    prv = jnp.maximum(j - 1, 0)
    n_next = jnp.where(j + 1 < n_blocks, nv_ref[nxt], 0)
    n_prev = jnp.where(j > 0, nv_ref[prv], 0)

    @pl.when((j == 0) | (be_ref[j] != be_ref[prv]))
    def _():
        w1b[...] = w1_ref[0, 0].astype(_BF16)
        w3b[...] = w3_ref[0, 0].astype(_BF16)
        w2b[...] = w2_ref[0, 0].astype(_BF16)

    wait_gather(nv_ref[j], slot)
    x = _load_row_tiles(xbuf, (slot,)).astype(_BF16)
    gather_rows(nxt, other, n_next, 0, half)
    a = _dot(x, w1b[...])
    gather_rows(nxt, other, n_next, half, EXPERT_BLOCK)
    b = _dot(x, w3b[...])
    scatter_rows(prv, other, n_prev, 0, half)
    hmid = (jax.nn.silu(a) * b).astype(_BF16)
    y = _dot(hmid, w2b[...])
    scatter_rows(prv, other, n_prev, half, EXPERT_BLOCK)

    @pl.when(j >= 2)
    def _():
        wait_scatter(nv_ref[jnp.maximum(j - 2, 0)], slot)

    _store_row_tiles(obuf, y, (slot,))

    @pl.when(j == n_blocks - 1)
    def _():
        wait_scatter(n_prev, other)
        n_last = nv_ref[j]
        scatter_rows(j, slot, n_last, 0, EXPERT_BLOCK)
        wait_scatter(n_last, slot)


def _moe_ffn(block_expert, n_valid, table, xn, w1, w3, w2, layer):
    n_tok = xn.shape[0]
    d = D_MODEL
    n_blocks = block_expert.shape[0]
    assert n_blocks >= 2 and xn.shape[1:] == (N_CHUNKS, LANES)
    wmap = lambda j, be, nv, tab: (layer, be[j], 0, 0)
    return pl.pallas_call(
        functools.partial(_moe_ffn_kernel, n_tok=n_tok, n_blocks=n_blocks),
        grid_spec=pltpu.PrefetchScalarGridSpec(
            num_scalar_prefetch=3,
            grid=(n_blocks,),
            in_specs=[
                pl.BlockSpec(memory_space=pl.ANY),
                pl.BlockSpec((1, 1, d, D_EXPERT), wmap),
                pl.BlockSpec((1, 1, d, D_EXPERT), wmap),
                pl.BlockSpec((1, 1, D_EXPERT, d), wmap),
            ],
            out_specs=pl.BlockSpec(memory_space=pl.ANY),
            scratch_shapes=[
                pltpu.VMEM((2, EXPERT_BLOCK, N_CHUNKS, LANES), _F32),
                pltpu.VMEM((2, EXPERT_BLOCK, N_CHUNKS, LANES), _F32),
                pltpu.VMEM((d, D_EXPERT), _BF16),
                pltpu.VMEM((d, D_EXPERT), _BF16),
                pltpu.VMEM((D_EXPERT, d), _BF16),
                pltpu.SemaphoreType.DMA((2,)),
                pltpu.SemaphoreType.DMA((2,)),
            ],
        ),
        out_shape=jax.ShapeDtypeStruct((2 * n_tok, N_CHUNKS, LANES), _F32),
        compiler_params=pltpu.CompilerParams(
            dimension_semantics=("arbitrary",), vmem_limit_bytes=VMEM_LIMIT),
        name="moe_ffn",
    )(block_expert, n_valid, table, xn, w1, w3, w2)


def _moe_plan(eid, n_tok):
    n_assign = 2 * n_tok
    flat_e = eid.T.reshape(-1)
    onehot = (flat_e[:, None] == jnp.arange(N_EXPERTS, dtype=jnp.int32)[None, :]).astype(jnp.int32)
    csum = jnp.cumsum(onehot, axis=0)
    counts = csum[-1]
    rank = jnp.sum(csum * onehot, axis=1) - 1
    padded = (counts + EXPERT_BLOCK - 1) // EXPERT_BLOCK * EXPERT_BLOCK
    pad_end = jnp.cumsum(padded)
    pad_start = pad_end - padded
    slot = pad_start[flat_e] + rank
    n_blocks = -(-(n_assign + N_EXPERTS * (EXPERT_BLOCK - 1)) // EXPERT_BLOCK)
    n_slots = n_blocks * EXPERT_BLOCK
    a = jnp.arange(n_assign, dtype=jnp.int32)
    value = (a % 2) * n_tok + a // 2
    table = jnp.zeros((n_slots,), jnp.int32).at[slot].set(value)
    block_start = jnp.arange(n_blocks, dtype=jnp.int32) * EXPERT_BLOCK
    block_expert = jnp.minimum(jnp.sum((pad_end[None, :] <= block_start[:, None]).astype(jnp.int32), axis=1),
                               N_EXPERTS - 1)
    n_valid = jnp.clip((pad_start + counts)[block_expert] - block_start, 0, EXPERT_BLOCK).astype(jnp.int32)
    return block_expert, n_valid, table


def _combine(h_ref, y0_ref, y1_ref, gate_ref):
    gate = gate_ref[...]
    return h_ref[...] + (gate[:, 0:1] * _load_row_tiles(y0_ref) + gate[:, 1:2] * _load_row_tiles(y1_ref))


def _emit_head(r, h, outs, scr):
    tm = r.shape[0]
    if scr is not None:
        scr[h] = r
    for ref, dil in outs:
        if dil == 0:
            ref[:, h, :] = r
        elif dil == 1:
            ref[:, _chunk(h)] = r.astype(ref.dtype)
        else:
            for res in range(dil):
                col = res * D_MODEL + h * HEAD_DIM
                ref[:, col:col + HEAD_DIM] = scr[h, pl.ds(res, tm // dil, stride=dil), :].astype(ref.dtype)


def _rope(xh, cos, sin_signed):
    return xh * cos + pltpu.roll(xh, HEAD_DIM // 2, axis=1) * sin_signed


def _qkv_kernel(*refs, streams, per_seq, kept):
    (h_ref, y0_ref, y1_ref, gate_ref, gmix_ref, gkv_ref, wq_ref, wkv_ref, cos_ref, sin_ref) = refs[:10]
    outs = refs[10:]
    if streams:
        (h2_ref, q0_ref, q1_ref, q2_ref, k0_ref, k1_ref, k2_ref, v0_ref, v1_ref, v2_ref, kf_ref, vf_ref,
         scr_q1, scr_q2, scr_k, scr_v) = outs
        q_outs = [([(q0_ref, 1)], None), ([(q1_ref, DILATIONS[1])], scr_q1), ([(q2_ref, DILATIONS[2])], scr_q2)]
        k_outs = ([(k0_ref, 1), (k1_ref, DILATIONS[1]), (k2_ref, DILATIONS[2])], scr_k)
        v_outs = ([(v0_ref, 1), (v1_ref, DILATIONS[1]), (v2_ref, DILATIONS[2])], scr_v)
    else:
        h2_ref, q0_ref, q1_ref, q2_ref, kf_ref, vf_ref = outs
        q_outs = [([(q0_ref, 1)], None), ([(q1_ref, 1)], None), ([(q2_ref, 1)], None)]
        k_outs = ([(kf_ref, 0)], None)
        v_outs = ([(vf_ref, 0)], None)

    h2 = _combine(h_ref, y0_ref, y1_ref, gate_ref)
    h2_ref[...] = h2
    cos = cos_ref[...]
    sin = sin_ref[...]
    xn = _rms(h2, gmix_ref[...]).astype(_BF16)
    scale = HEAD_DIM ** -0.5
    for g in range(N_GROUPS):
        q = _dot(xn, wq_ref[:, g * D_MODEL:(g + 1) * D_MODEL])
        for h in range(N_HEADS):
            _emit_head(_rope(q[:, _chunk(h)], cos, sin) * scale, h, *q_outs[g])
    xkv = _rms(h2, gkv_ref[...]).astype(_BF16)
    k = _dot(xkv, wkv_ref[:, 0:D_MODEL])
    for h in range(N_HEADS):
        _emit_head(_rope(k[:, _chunk(h)], cos, sin), h, *k_outs)
    v = _dot(xkv, wkv_ref[:, D_MODEL:2 * D_MODEL])
    for h in range(N_HEADS):
        _emit_head(v[:, _chunk(h)], h, *v_outs)
    if streams:
        @pl.when(pl.program_id(0) % per_seq >= per_seq - kept)
        def _():
            for h in range(N_HEADS):
                kf_ref[:, h, :] = scr_k[h]
                vf_ref[:, h, :] = scr_v[h]


def _qkv(h, ypair, gate, gmix, gkv, wq, wkv, cos, sin, *, tm, streams, seq, keep):
    t, d = h.shape
    nt = t // tm
    per_seq = seq // tm
    kept = keep // tm
    assert seq % tm == 0 and keep % tm == 0

    def tail(i):
        return ((i // per_seq) * kept + jnp.maximum(i % per_seq - (per_seq - kept), 0), 0, 0)

    tail_f32 = (pl.BlockSpec((tm, N_HEADS, HEAD_DIM), tail),
                jax.ShapeDtypeStruct((t // seq * keep, N_HEADS, HEAD_DIM), _F32))
    row = lambda i: (i, 0)
    const = lambda i: (0, 0)
    nat = lambda dt: (pl.BlockSpec((tm, d), row), jax.ShapeDtypeStruct((t, d), dt))

    def stream(dil):
        return (pl.BlockSpec((tm // dil, dil * d), row), jax.ShapeDtypeStruct((t // dil, dil * d), _BF16))

    if streams:
        d1, d2 = DILATIONS[1], DILATIONS[2]
        outs = [nat(_F32), nat(_BF16), stream(d1), stream(d2), nat(_BF16), stream(d1), stream(d2),
                nat(_BF16), stream(d1), stream(d2), tail_f32, tail_f32]
        scratch = [pltpu.VMEM((N_HEADS, tm, HEAD_DIM), _F32)] * 4
    else:
        outs = [nat(_F32), nat(_BF16), nat(_BF16), nat(_BF16), tail_f32, tail_f32]
        scratch = []
    return pl.pallas_call(
        functools.partial(_qkv_kernel, streams=streams, per_seq=per_seq, kept=kept),
        grid=(nt,),
        in_specs=[
            pl.BlockSpec((tm, d), row),
            pl.BlockSpec((tm, N_CHUNKS, LANES), lambda i: (i, 0, 0)),
            pl.BlockSpec((tm, N_CHUNKS, LANES), lambda i: (nt + i, 0, 0)),
            pl.BlockSpec((tm, 2), row),
            pl.BlockSpec((1, d), const),
            pl.BlockSpec((1, d), const),
            pl.BlockSpec((d, 3 * d), const),
            pl.BlockSpec((d, 2 * d), const),
            pl.BlockSpec((tm, HEAD_DIM), row),
            pl.BlockSpec((tm, HEAD_DIM), row),
        ],
        out_specs=[o[0] for o in outs],
        out_shape=[o[1] for o in outs],
        scratch_shapes=scratch,
        compiler_params=pltpu.CompilerParams(
            dimension_semantics=("arbitrary",), vmem_limit_bytes=VMEM_LIMIT),
        name="qkv_proj_streams" if streams else "qkv_proj",
    )(h, ypair, ypair, gate, gmix, gkv, wq, wkv, cos, sin)


ATTN_TQ = 512


def _attn_prompt_kernel(q_ref, kp_ref, kc_ref, vp_ref, vc_ref, o_ref, st_ref, *, tq):
    i = pl.program_id(1)
    qi = lax.broadcasted_iota(jnp.int32, (SPAN, 2 * SPAN), 0)
    kj = lax.broadcasted_iota(jnp.int32, (SPAN, 2 * SPAN), 1)
    band = (kj >= qi) & (kj <= qi + SPAN)
    band0 = band & (kj >= jnp.where(i > 0, 0, SPAN))
    lane = lax.broadcasted_iota(jnp.int32, (SPAN, HEAD_DIM), 1)
    ones = jnp.ones((2 * SPAN, HEAD_DIM), _BF16)
    for a in range(tq // SPAN):
        rows = slice(a * SPAN, (a + 1) * SPAN)
        st = jnp.zeros((SPAN, HEAD_DIM), _F32)
        for h in range(N_HEADS):
            hs = _chunk(h)
            if a == 0:
                keys = jnp.concatenate([kp_ref[:, hs], kc_ref[rows, hs]], axis=0)
                vals = jnp.concatenate([vp_ref[:, hs], vc_ref[rows, hs]], axis=0)
                mask = band0
            else:
                both = slice((a - 1) * SPAN, (a + 1) * SPAN)
                keys, vals, mask = kc_ref[both, hs], vc_ref[both, hs], band
            s = jnp.where(mask, _dot_nt(q_ref[rows, hs], keys), NEG_BIG)
            m = jnp.max(s, axis=1, keepdims=True)
            p = jnp.exp(s - m).astype(_BF16)
            ol = _dot(p, jnp.concatenate([vals, ones], axis=1))
            l = ol[:, HEAD_DIM:]
            o_ref[rows, hs] = ol[:, :HEAD_DIM] / l
            st = jnp.where(lane == h, m + jnp.log(l), st)
        st_ref[rows, :] = st


def _attn_prompt(q, k, v, batch, dilation):
    rows, width = q.shape
    d = width // dilation
    per_batch = rows // batch
    tq = min(ATTN_TQ, per_batch)
    assert per_batch % tq == 0 and tq % SPAN == 0
    nq = per_batch // tq
    sub = tq // SPAN
    cur = lambda b, i, r: (b * nq + i, r)
    prev = lambda b, i, r: ((b * nq + i) * sub - jnp.where(i > 0, 1, 0), r)
    big = (tq, d)
    small = (SPAN, d)
    return pl.pallas_call(
        functools.partial(_attn_prompt_kernel, tq=tq),
        grid=(batch, nq, dilation),
        in_specs=[pl.BlockSpec(big, cur), pl.BlockSpec(small, prev), pl.BlockSpec(big, cur),
                  pl.BlockSpec(small, prev), pl.BlockSpec(big, cur)],
        out_specs=[pl.BlockSpec(big, cur), pl.BlockSpec((tq, HEAD_DIM), cur)],
        out_shape=[jax.ShapeDtypeStruct((rows, width), _F32),
                   jax.ShapeDtypeStruct((rows, dilation * HEAD_DIM), _F32)],
        compiler_params=pltpu.CompilerParams(
            dimension_semantics=("arbitrary", "arbitrary", "arbitrary"), vmem_limit_bytes=VMEM_LIMIT),
        name=f"attn_prompt_d{dilation}",
    )(q, k, k, v, v)


N_NEW = 8
CACHE_A_GROUPS = (PAST_LEN - WINDOWS[1]) // DILATIONS[2]
CACHE_A_ROWS = CACHE_A_GROUPS * N_NEW
CACHE_B_ROWS = WINDOWS[1]
CACHE_B_GROUPS = CACHE_B_ROWS // DILATIONS[2]
KEYS_REAL = CACHE_A_ROWS + CACHE_B_ROWS + N_NEW
KEYS_PAD = -(-KEYS_REAL // 128) * 128


def _sample_key_positions():
    pos = np.full((KEYS_PAD,), -1, np.int64)
    a = np.arange(CACHE_A_ROWS)
    pos[:CACHE_A_ROWS] = (a // N_NEW) * DILATIONS[2] + a % N_NEW
    pos[CACHE_A_ROWS:CACHE_A_ROWS + CACHE_B_ROWS] = PAST_LEN - CACHE_B_ROWS + np.arange(CACHE_B_ROWS)
    pos[CACHE_A_ROWS + CACHE_B_ROWS:KEYS_REAL] = PAST_LEN + np.arange(N_NEW)
    return pos


def _sample_bias():
    pos = _sample_key_positions()
    bias = np.full((N_GROUPS * N_NEW, KEYS_PAD), NEG_BIG, np.float32)
    for g in range(N_GROUPS):
        for n in range(N_NEW):
            delta = PAST_LEN + n - pos
            ok = (pos >= 0) & (delta >= 0) & (delta <= WINDOWS[g]) & (delta % DILATIONS[g] == 0)
            assert int(ok.sum()) == SPAN + 1
            bias[g * N_NEW + n, ok] = 0.0
    return bias


def _attn_sample_kernel(q0_ref, q1_ref, q2_ref, kn_ref, vn_ref, bias_ref, ck_hbm, cv_hbm,
                        o_ref, ka, kb, va, vb, kall, vall, sem, *, n_seq):
    b = pl.program_id(0)
    slot = b % 2

    def cache_copies(seq, s):
        cps = []
        for h in range(N_HEADS):
            for src, dst_a, dst_b in ((ck_hbm, ka, kb), (cv_hbm, va, vb)):
                cps.append(pltpu.make_async_copy(
                    src.at[seq, pl.ds(0, CACHE_A_GROUPS), pl.ds(0, N_NEW), h], dst_a.at[s, h], sem.at[s]))
                cps.append(pltpu.make_async_copy(
                    src.at[seq, pl.ds(CACHE_A_GROUPS, CACHE_B_GROUPS), :, h], dst_b.at[s, h], sem.at[s]))
        return cps

    @pl.when(b == 0)
    def _():
        kall[KEYS_REAL:KEYS_PAD, :] = jnp.zeros((KEYS_PAD - KEYS_REAL, D_MODEL), _BF16)
        vall[KEYS_REAL:KEYS_PAD, :] = jnp.zeros((KEYS_PAD - KEYS_REAL, D_MODEL), _BF16)
        for cp in cache_copies(0, 0):
            cp.start()

    @pl.when(b + 1 < n_seq)
    def _():
        for cp in cache_copies(b + 1, 1 - slot):
            cp.start()

    for cp in cache_copies(b, slot):
        cp.wait()

    nb0 = CACHE_A_ROWS + CACHE_B_ROWS
    for src_a, src_b, src_n, dst in ((ka, kb, kn_ref, kall), (va, vb, vn_ref, vall)):
        for h in range(N_HEADS):
            hs = _chunk(h)
            dst[0:CACHE_A_ROWS, hs] = src_a[slot, h].reshape(CACHE_A_ROWS, HEAD_DIM).astype(_BF16)
            dst[CACHE_A_ROWS:nb0, hs] = src_b[slot, h].reshape(CACHE_B_ROWS, HEAD_DIM).astype(_BF16)
            dst[nb0:KEYS_REAL, hs] = src_n[0, :, h, :].astype(_BF16)

    bias = bias_ref[...]
    for h in range(N_HEADS):
        hs = _chunk(h)
        qh = jnp.concatenate([q0_ref[0, :, hs].astype(_F32), q1_ref[0, :, hs].astype(_F32),
                              q2_ref[0, :, hs].astype(_F32)], axis=0).astype(_BF16)
        s = _dot_nt(qh, kall[:, hs]) + bias
        m = jnp.max(s, axis=1, keepdims=True)
        m8 = jnp.maximum(jnp.maximum(m[0:N_NEW], m[N_NEW:2 * N_NEW]), m[2 * N_NEW:3 * N_NEW])
        p = jnp.exp(s - jnp.concatenate([m8, m8, m8], axis=0))
        p8 = p[0:N_NEW] + p[N_NEW:2 * N_NEW] + p[2 * N_NEW:3 * N_NEW]
        l8 = jnp.sum(p8, axis=1, keepdims=True)
        o_ref[0, :, hs] = _dot(p8.astype(_BF16), vall[:, hs]) / l8


def _attn_sample(q0, q1, q2, cache_k, cache_v, k_new, v_new):
    nb, n_new, d = q0.shape
    past = cache_k.shape[1]
    assert past == PAST_LEN and n_new == N_NEW and d == D_MODEL
    assert cache_k.shape[2:] == (N_HEADS, HEAD_DIM)
    ck = cache_k.reshape(nb, past // DILATIONS[2], DILATIONS[2], N_HEADS, HEAD_DIM)
    cv = cache_v.reshape(nb, past // DILATIONS[2], DILATIONS[2], N_HEADS, HEAD_DIM)
    bias = jnp.asarray(_sample_bias())
    new = pl.BlockSpec((1, n_new, d), lambda b: (b, 0, 0))
    new_kv = pl.BlockSpec((1, n_new, N_HEADS, HEAD_DIM), lambda b: (b, 0, 0, 0))
    part_a = (2, N_HEADS, CACHE_A_GROUPS, N_NEW, HEAD_DIM)
    part_b = (2, N_HEADS, CACHE_B_GROUPS, DILATIONS[2], HEAD_DIM)
    return pl.pallas_call(
        functools.partial(_attn_sample_kernel, n_seq=nb),
        grid=(nb,),
        in_specs=[new, new, new, new_kv, new_kv,
                  pl.BlockSpec((N_GROUPS * N_NEW, KEYS_PAD), lambda b: (0, 0)),
                  pl.BlockSpec(memory_space=pl.ANY), pl.BlockSpec(memory_space=pl.ANY)],
        out_specs=new,
        out_shape=jax.ShapeDtypeStruct((nb, n_new, d), _F32),
        scratch_shapes=[pltpu.VMEM(part_a, _F32), pltpu.VMEM(part_b, _F32),
                        pltpu.VMEM(part_a, _F32), pltpu.VMEM(part_b, _F32),
                        pltpu.VMEM((KEYS_PAD, d), _BF16), pltpu.VMEM((KEYS_PAD, d), _BF16),
                        pltpu.SemaphoreType.DMA((2,))],
        compiler_params=pltpu.CompilerParams(
            dimension_semantics=("arbitrary",), vmem_limit_bytes=VMEM_LIMIT),
        name="attn_sample",
    )(q0, q1, q2, k_new, v_new, bias, ck, cv)


def _attn_out_kernel(*refs, n_groups, tm):
    if n_groups == 1:
        (o_ref, h_ref, wo_ref, gffn_ref, wr_ref, br_ref,
         h3_ref, xn2_ref, eid_ref, gate_ref) = refs
        o = o_ref[...].astype(_BF16)
    else:
        (o0_ref, o1_ref, o2_ref, s0_ref, s1_ref, s2_ref, h_ref, wo_ref, gffn_ref, wr_ref, br_ref,
         h3_ref, xn2_ref, eid_ref, gate_ref, obuf, scr_o1, scr_o2, scr_s1, scr_s2) = refs
        for o_ref, s_ref, scr_o, scr_s, dil in ((o1_ref, s1_ref, scr_o1, scr_s1, DILATIONS[1]),
                                                (o2_ref, s2_ref, scr_o2, scr_s2, DILATIONS[2])):
            for res in range(dil):
                dst = pl.ds(res, tm // dil, stride=dil)
                scr_s[dst, :] = s_ref[:, res * HEAD_DIM:(res + 1) * HEAD_DIM]
                for h in range(N_HEADS):
                    col = res * D_MODEL + h * HEAD_DIM
                    scr_o[h, dst, :] = o_ref[:, col:col + HEAD_DIM]
        sts = [s0_ref[...], scr_s1[...], scr_s2[...]]
        mx = jnp.maximum(jnp.maximum(sts[0], sts[1]), sts[2])
        es = [jnp.exp(s - mx) for s in sts]
        den = es[0] + es[1] + es[2]
        ws = [e / den for e in es]
        for h in range(N_HEADS):
            acc = ws[0][:, h:h + 1] * o0_ref[:, _chunk(h)]
            acc = acc + ws[1][:, h:h + 1] * scr_o1[h]
            acc = acc + ws[2][:, h:h + 1] * scr_o2[h]
            obuf[:, _chunk(h)] = acc.astype(_BF16)
        o = obuf[...]
    h3 = h_ref[...] + _dot(o, wo_ref[...])
    h3_ref[...] = h3
    xn2 = _rms(h3, gffn_ref[...])
    _store_row_tiles(xn2_ref, xn2)
    _route(xn2, wr_ref, br_ref, eid_ref, gate_ref)


def _attn_out(os, sts, h, wo, gffn, wr, br, *, tm):
    t, d = h.shape
    n_groups = len(os)
    row = lambda i: (i, 0)
    const = lambda i: (0, 0)
    if n_groups == 1:
        in_specs = [pl.BlockSpec((tm, d), row)]
        scratch = []
        args = [os[0]]
    else:
        in_specs = [pl.BlockSpec((tm // dil, dil * d), row) for dil in DILATIONS]
        in_specs += [pl.BlockSpec((tm // dil, dil * HEAD_DIM), row) for dil in DILATIONS]
        scratch = [pltpu.VMEM((tm, d), _BF16),
                   pltpu.VMEM((N_HEADS, tm, HEAD_DIM), _F32), pltpu.VMEM((N_HEADS, tm, HEAD_DIM), _F32),
                   pltpu.VMEM((tm, HEAD_DIM), _F32), pltpu.VMEM((tm, HEAD_DIM), _F32)]
        args = list(os) + list(sts)
    in_specs += [pl.BlockSpec((tm, d), row), pl.BlockSpec((d, d), const), pl.BlockSpec((1, d), const),
                 pl.BlockSpec((ROUTER_ROWS, d), const), pl.BlockSpec((ROUTER_ROWS, 1), const)]
    return pl.pallas_call(
        functools.partial(_attn_out_kernel, n_groups=n_groups, tm=tm),
        grid=(t // tm,),
        in_specs=in_specs,
        out_specs=[pl.BlockSpec((tm, d), row), pl.BlockSpec((tm, N_CHUNKS, LANES), lambda i: (i, 0, 0)),
                   pl.BlockSpec((2, tm), lambda i: (0, i)), pl.BlockSpec((2, tm), lambda i: (0, i))],
        out_shape=[jax.ShapeDtypeStruct((t, d), _F32), jax.ShapeDtypeStruct((t, N_CHUNKS, LANES), _F32),
                   jax.ShapeDtypeStruct((2, t), jnp.int32), jax.ShapeDtypeStruct((2, t), _F32)],
        scratch_shapes=scratch,
        compiler_params=pltpu.CompilerParams(
            dimension_semantics=("arbitrary",), vmem_limit_bytes=VMEM_LIMIT),
        name=f"attn_out_g{n_groups}",
    )(*args, h, wo, gffn, wr, br)


def _final_kernel(h_ref, y0_ref, y1_ref, gate_ref, g_ref, out_ref):
    out_ref[...] = _rms(_combine(h_ref, y0_ref, y1_ref, gate_ref), g_ref[...])


def _final(h, ypair, gate, g, *, tm):
    t, d = h.shape
    nt = t // tm
    row = lambda i: (i, 0)
    return pl.pallas_call(
        _final_kernel,
        grid=(nt,),
        in_specs=[pl.BlockSpec((tm, d), row),
                  pl.BlockSpec((tm, N_CHUNKS, LANES), lambda i: (i, 0, 0)),
                  pl.BlockSpec((tm, N_CHUNKS, LANES), lambda i: (nt + i, 0, 0)),
                  pl.BlockSpec((tm, 2), row),
                  pl.BlockSpec((1, d), lambda i: (0, 0))],
        out_specs=pl.BlockSpec((tm, d), row),
        out_shape=jax.ShapeDtypeStruct((t, d), _F32),
        compiler_params=pltpu.CompilerParams(
            dimension_semantics=("arbitrary",), vmem_limit_bytes=VMEM_LIMIT),
        name="final_norm",
    )(h, ypair, ypair, gate, g)


def _rope_tables(pos):
    half = HEAD_DIM // 2
    inv_freq = jnp.power(jnp.float32(ROPE_THETA), -jnp.arange(half, dtype=jnp.float32) / half)
    ang = pos.astype(jnp.float32)[:, None] * inv_freq[None, :]
    cos = jnp.cos(ang)
    sin = jnp.sin(ang)
    return jnp.concatenate([cos, cos], axis=-1), jnp.concatenate([-sin, sin], axis=-1)


def _router_params(wg, bg, we, be):
    wr = jnp.zeros((ROUTER_ROWS, D_MODEL), _F32)
    wr = wr.at[0:MOE_GROUPS].set(wg.T).at[EXPERT_ROW0:EXPERT_ROW0 + N_EXPERTS].set(we.T)
    br = jnp.zeros((ROUTER_ROWS, 1), _F32)
    br = br.at[0:MOE_GROUPS, 0].set(bg).at[EXPERT_ROW0:EXPERT_ROW0 + N_EXPERTS, 0].set(be)
    return wr, br


def _moe(xn2, eid, experts, layer):
    block_expert, n_valid, table = _moe_plan(eid, xn2.shape[0])
    return _moe_ffn(block_expert, n_valid, table, xn2, *experts, layer)


def _forward(x3, prev3, pos_rows, attend, p, *, shift, tm, streams, keep):
    h1, xn2, eid, gate, state = _conv_layer(
        x3, prev3, p['gmix'][0], p['win'], p['ck'], p['wout'], p['gffn'][0], *p['router'][0],
        shift=shift, tm=tm)
    t = x3.shape[0] * x3.shape[1]
    ypair = _moe(xn2, eid, p['experts'], 0)
    cos, sin = _rope_tables(pos_rows)
    h2, *qkv, kf, vf = _qkv(h1.reshape(t, D_MODEL), ypair, gate.T, p['gmix'][1], p['gkv'],
                            p['wq'], p['wkv'], cos, sin, tm=256, streams=streams,
                            seq=x3.shape[1], keep=keep)
    os, sts = attend(qkv, kf, vf)
    h3, xn2, eid, gate = _attn_out(os, sts, h2, p['wo'], p['gffn'][1], *p['router'][1], tm=256)
    ypair = _moe(xn2, eid, p['experts'], 1)
    y = _final(h3, ypair, gate.T, p['gfinal'], tm=512)
    return y, state, kf, vf


def kernel(x_prompt, x_sample, cache_k, cache_v, state_conv, norm_mix, norm_ffn, norm_kv, norm_final,
           conv_w_in, conv_kernel, conv_w_out, attn_w_q, attn_w_kv, attn_w_o, router_group_w,
           router_group_b, router_expert_w, router_expert_b, expert_w1, expert_w3, expert_w2):
    b_p, s_p, d = x_prompt.shape
    b_s, n_new, _ = x_sample.shape
    assert d == D_MODEL and n_new == N_NEW and s_p % (DILATIONS[2] * SPAN) == 0
    assert norm_mix.shape[0] == 2 and conv_w_in.shape[0] == 1 and attn_w_q.shape[0] == 1
    assert cache_k.shape[1] == PAST_LEN

    p = {
        'gmix': [norm_mix[l].reshape(1, d) for l in range(2)],
        'gffn': [norm_ffn[l].reshape(1, d) for l in range(2)],
        'gkv': norm_kv.reshape(1, d),
        'gfinal': norm_final.reshape(1, d),
        'win': conv_w_in[0].astype(_BF16),
        'ck': conv_kernel[0],
        'wout': conv_w_out[0].astype(_BF16),
        'wq': attn_w_q[0].astype(_BF16),
        'wkv': attn_w_kv.astype(_BF16),
        'wo': attn_w_o[0].astype(_BF16),
        'router': [_router_params(router_group_w[l], router_group_b[l], router_expert_w[l], router_expert_b[l])
                   for l in range(2)],
        'experts': (expert_w1, expert_w3, expert_w2),
    }

    def attend_prompt(qkv, kf, vf):
        q0, q1, q2, k0, k1, k2, v0, v1, v2 = qkv
        os, sts = [], []
        for q, k, v, dil in ((q0, k0, v0, DILATIONS[0]), (q1, k1, v1, DILATIONS[1]), (q2, k2, v2, DILATIONS[2])):
            o, st = _attn_prompt(q, k, v, b_p, dil)
            os.append(o)
            sts.append(st)
        return os, sts

    pos_p = jnp.tile(jnp.arange(s_p, dtype=jnp.int32), b_p)
    zero_state = jnp.zeros((b_p, CONV_WIDTH - 1, d), x_prompt.dtype)
    keep = min(max(WINDOWS), s_p)
    y_p, st_p, kf_p, vf_p = _forward(x_prompt, zero_state, pos_p, attend_prompt, p, shift=1, tm=512,
                                     streams=True, keep=keep)
    y_prompt = y_p.reshape(b_p, s_p, d)
    k_p = kf_p.reshape(b_p, keep, N_HEADS, HEAD_DIM)
    v_p = vf_p.reshape(b_p, keep, N_HEADS, HEAD_DIM)
    conv_p = st_p[None]

    halves = 2
    bh = b_s // halves

    def to_rows(a):
        w = a.shape[-1]
        return a.reshape(halves, bh, n_new, w).transpose(0, 2, 1, 3).reshape(halves * n_new * bh, w)

    def to_batch(a):
        w = a.shape[1:]
        return jnp.swapaxes(a.reshape(halves, n_new, bh, *w), 1, 2).reshape(b_s, n_new, *w)

    def attend_sample(qkv, kf, vf):
        q0, q1, q2 = qkv
        o = _attn_sample(to_batch(q0), to_batch(q1), to_batch(q2), cache_k, cache_v,
                         to_batch(kf), to_batch(vf))
        return [to_rows(o)], None

    x_s = to_rows(x_sample).reshape(halves, n_new * bh, d)
    prev_s = state_conv[0].reshape(halves, bh, CONV_WIDTH - 1, d).transpose(0, 2, 1, 3).reshape(
        halves, (CONV_WIDTH - 1) * bh, d)
    pos_s = jnp.tile(jnp.repeat(PAST_LEN + jnp.arange(n_new, dtype=jnp.int32), bh), halves)
    y_s, st_s, kf_s, vf_s = _forward(x_s, prev_s, pos_s, attend_sample, p, shift=bh, tm=n_new * bh,
                                     streams=False, keep=n_new * bh)
    y_sample = to_batch(y_s)
    k_s = to_batch(kf_s)
    v_s = to_batch(vf_s)
    conv_s = st_s.reshape(halves, CONV_WIDTH - 1, bh, d).transpose(0, 2, 1, 3).reshape(
        b_s, CONV_WIDTH - 1, d)[None]

    return (y_prompt, y_sample, k_p, v_p, conv_p, k_s, v_s, conv_s)
```

```python
import functools

import numpy as np
import jax
import jax.numpy as jnp
from jax import lax
from jax.experimental import pallas as pl
from jax.experimental.pallas import tpu as pltpu

D_MODEL = 1024
CONV_WIDTH = 3
WINDOWS = (128, 512, 2048)
DILATIONS = (1, 4, 16)
N_GROUPS = 3
N_HEADS = 8
HEAD_DIM = 128
SPAN = 128
ROPE_THETA = 10000.0
MOE_GROUPS = 4
EXPERTS_PER_GROUP = 4
N_EXPERTS = 16
D_EXPERT = 512
EXPERT_BLOCK = 128
RMS_EPS = 1e-6
NEG_BIG = -1e30
PAST_LEN = 2048

LANES = 128
N_CHUNKS = D_MODEL // LANES
ROUTER_ROWS = 32
EXPERT_ROW0 = 8
VMEM_LIMIT = 56 * 1024 * 1024

_F32 = jnp.float32
_BF16 = jnp.bfloat16


def _rms(x, g):
    ms = jnp.mean(x * x, axis=-1, keepdims=True)
    return (x * lax.rsqrt(ms + RMS_EPS)) * g


def _dot(a, b):
    return jnp.dot(a, b, preferred_element_type=_F32)


def _dot_nt(a, b, precision=None):
    return lax.dot_general(a, b, (((1,), (1,)), ((), ())), precision=precision,
                           preferred_element_type=_F32)


def _chunk(c):
    return slice(c * LANES, (c + 1) * LANES)


def _store_row_tiles(ref, x, lead=()):
    for c in range(N_CHUNKS):
        ref[lead + (slice(None), c, slice(None))] = x[:, _chunk(c)]


def _load_row_tiles(ref, lead=()):
    return jnp.concatenate([ref[lead + (slice(None), c, slice(None))] for c in range(N_CHUNKS)], axis=1)


def _route(xn, wr_ref, br_ref, eid_ref, gate_ref):
    logits = _dot_nt(wr_ref[...], xn, precision=lax.Precision.HIGHEST) + br_ref[...]
    lg = logits[0:MOE_GROUPS]
    row = lax.broadcasted_iota(jnp.int32, lg.shape, 0).astype(_F32)
    eg = jnp.exp(lg - jnp.max(lg, axis=0, keepdims=True))
    pg = eg / jnp.sum(eg, axis=0, keepdims=True)
    gp = jnp.max(pg, axis=0, keepdims=True)
    gi = jnp.min(jnp.where(pg == gp, row, float(MOE_GROUPS)), axis=0, keepdims=True)
    le = jnp.zeros_like(lg)
    for g in range(MOE_GROUPS):
        r0 = EXPERT_ROW0 + g * EXPERTS_PER_GROUP
        le = le + jnp.where(gi == float(g), logits[r0:r0 + EXPERTS_PER_GROUP], 0.0)
    ee = jnp.exp(le - jnp.max(le, axis=0, keepdims=True))
    ev = ee / jnp.sum(ee, axis=0, keepdims=True)
    v1 = jnp.max(ev, axis=0, keepdims=True)
    i1 = jnp.min(jnp.where(ev == v1, row, float(EXPERTS_PER_GROUP)), axis=0, keepdims=True)
    ev2 = jnp.where(row == i1, -1.0, ev)
    v2 = jnp.max(ev2, axis=0, keepdims=True)
    i2 = jnp.min(jnp.where(ev2 == v2, row, float(EXPERTS_PER_GROUP)), axis=0, keepdims=True)
    den = v1 + v2
    ids = jnp.concatenate([gi * EXPERTS_PER_GROUP + i1, gi * EXPERTS_PER_GROUP + i2], axis=0)
    eid_ref[...] = ids.astype(jnp.int32)
    gate_ref[...] = jnp.concatenate([gp * v1 / den, gp * v2 / den], axis=0)


def _conv_layer_kernel(x_ref, prev_ref, gmix_ref, win_ref, ck_ref, wout_ref, gffn_ref, wr_ref, br_ref,
                       h_ref, xn2_ref, eid_ref, gate_ref, st_ref, ubuf, *, shift, tm):
    i = pl.program_id(1)
    halo = ubuf.shape[0] - tm
    keep = (CONV_WIDTH - 1) * shift

    @pl.when(i == 0)
    def _():
        ubuf[halo - keep:halo, :] = prev_ref[0]

    @pl.when(i > 0)
    def _():
        ubuf[halo - keep:halo, :] = ubuf[halo + tm - keep:halo + tm, :]

    x = x_ref[0]
    xn = _rms(x, gmix_ref[...]).astype(_BF16)
    c_gate = _dot(xn, win_ref[:, 0:D_MODEL])
    hid = _dot(xn, win_ref[:, 2 * D_MODEL:3 * D_MODEL])
    u = c_gate * hid
    ubuf[halo:halo + tm, :] = u
    u1 = ubuf[halo - shift:halo - shift + tm, :]
    u2 = ubuf[halo - 2 * shift:halo - 2 * shift + tm, :]
    conv = ck_ref[0:1, :] * u2 + ck_ref[1:2, :] * u1 + ck_ref[2:3, :] * u
    b_gate = _dot(xn, win_ref[:, D_MODEL:2 * D_MODEL])
    y = _dot((b_gate * conv).astype(_BF16), wout_ref[...])
    h = x + y
    h_ref[0] = h
    st_ref[0] = ubuf[halo + tm - keep:halo + tm, :]
    xn2 = _rms(h, gffn_ref[...])
    _store_row_tiles(xn2_ref, xn2)
    _route(xn2, wr_ref, br_ref, eid_ref, gate_ref)


def _conv_layer(x3, prev3, gmix, win, ck, wout, gffn, wr, br, *, shift, tm):
    nb, s, d = x3.shape
    nt = s // tm
    keep = (CONV_WIDTH - 1) * shift
    halo = -(-keep // 8) * 8
    t_all = nb * s
    const = lambda b, i: (0, 0)
    return pl.pallas_call(
        functools.partial(_conv_layer_kernel, shift=shift, tm=tm),
        grid=(nb, nt),
        in_specs=[
            pl.BlockSpec((1, tm, d), lambda b, i: (b, i, 0)),
            pl.BlockSpec((1, keep, d), lambda b, i: (b, 0, 0)),
            pl.BlockSpec((1, d), const),
            pl.BlockSpec((d, 3 * d), const),
            pl.BlockSpec((CONV_WIDTH, d), const),
            pl.BlockSpec((d, d), const),
            pl.BlockSpec((1, d), const),
            pl.BlockSpec((ROUTER_ROWS, d), const),
            pl.BlockSpec((ROUTER_ROWS, 1), const),
        ],
        out_specs=[
            pl.BlockSpec((1, tm, d), lambda b, i: (b, i, 0)),
            pl.BlockSpec((tm, N_CHUNKS, LANES), lambda b, i: (b * nt + i, 0, 0)),
            pl.BlockSpec((2, tm), lambda b, i: (0, b * nt + i)),
            pl.BlockSpec((2, tm), lambda b, i: (0, b * nt + i)),
            pl.BlockSpec((1, keep, d), lambda b, i: (b, 0, 0)),
        ],
        out_shape=[
            jax.ShapeDtypeStruct((nb, s, d), _F32),
            jax.ShapeDtypeStruct((t_all, N_CHUNKS, LANES), _F32),
            jax.ShapeDtypeStruct((2, t_all), jnp.int32),
            jax.ShapeDtypeStruct((2, t_all), _F32),
            jax.ShapeDtypeStruct((nb, keep, d), _F32),
        ],
        scratch_shapes=[pltpu.VMEM((halo + tm, d), _F32)],
        compiler_params=pltpu.CompilerParams(
            dimension_semantics=("arbitrary", "arbitrary"), vmem_limit_bytes=VMEM_LIMIT),
        name="conv_layer",
    )(x3, prev3, gmix, win, ck, wout, gffn, wr, br)


def _moe_ffn_kernel(be_ref, nv_ref, tab_ref, x_hbm, w1_ref, w3_ref, w2_ref, y_hbm,
                    xbuf, obuf, w1b, w3b, w2b, gsem, ssem, *, n_tok, n_blocks):
    j = pl.program_id(0)
    slot = j % 2
    other = 1 - slot
    half = EXPERT_BLOCK // 2

    def token_of(v):
        if n_tok & (n_tok - 1) == 0:
            return v & (n_tok - 1)
        return v - jnp.where(v >= n_tok, n_tok, 0)

    def gather_rows(blk, s, n, k0, k1):
        for k in range(k0, k1):
            @pl.when(k < n)
            def _():
                tok = token_of(tab_ref[blk * EXPERT_BLOCK + k])
                pltpu.make_async_copy(x_hbm.at[tok], xbuf.at[s, k], gsem.at[s]).start()

    def scatter_rows(blk, s, n, k0, k1):
        for k in range(k0, k1):
            @pl.when(k < n)
            def _():
                v = tab_ref[blk * EXPERT_BLOCK + k]
                pltpu.make_async_copy(obuf.at[s, k], y_hbm.at[v], ssem.at[s]).start()

    def wait_gather(n, s):
        @pl.when(n > 0)
        def _():
            pltpu.make_async_copy(x_hbm.at[pl.ds(0, n)], xbuf.at[s, pl.ds(0, n)], gsem.at[s]).wait()

    def wait_scatter(n, s):
        @pl.when(n > 0)
        def _():
            pltpu.make_async_copy(obuf.at[s, pl.ds(0, n)], y_hbm.at[pl.ds(0, n)], ssem.at[s]).wait()

    @pl.when(j == 0)
    def _():
        xbuf[...] = jnp.zeros(xbuf.shape, _F32)
        gather_rows(0, 0, nv_ref[0], 0, EXPERT_BLOCK)

    nxt = jnp.minimum(j + 1, n_blocks - 1)
    prv = jnp.maximum(j - 1, 0)
    n_next = jnp.where(j + 1 < n_blocks, nv_ref[nxt], 0)
    n_prev = jnp.where(j > 0, nv_ref[prv], 0)

    @pl.when((j == 0) | (be_ref[j] != be_ref[prv]))
    def _():
        w1b[...] = w1_ref[0, 0].astype(_BF16)
        w3b[...] = w3_ref[0, 0].astype(_BF16)
        w2b[...] = w2_ref[0, 0].astype(_BF16)

    wait_gather(nv_ref[j], slot)
    x = _load_row_tiles(xbuf, (slot,)).astype(_BF16)
    gather_rows(nxt, other, n_next, 0, half)
    a = _dot(x, w1b[...])
    gather_rows(nxt, other, n_next, half, EXPERT_BLOCK)
    b = _dot(x, w3b[...])
    scatter_rows(prv, other, n_prev, 0, half)
    hmid = (jax.nn.silu(a) * b).astype(_BF16)
    y = _dot(hmid, w2b[...])
    scatter_rows(prv, other, n_prev, half, EXPERT_BLOCK)

    @pl.when(j >= 2)
    def _():
        wait_scatter(nv_ref[jnp.maximum(j - 2, 0)], slot)

    _store_row_tiles(obuf, y, (slot,))

    @pl.when(j == n_blocks - 1)
    def _():
        wait_scatter(n_prev, other)
        n_last = nv_ref[j]
        scatter_rows(j, slot, n_last, 0, EXPERT_BLOCK)
        wait_scatter(n_last, slot)


def _moe_ffn(block_expert, n_valid, table, xn, w1, w3, w2, layer):
    n_tok = xn.shape[0]
    d = D_MODEL
    n_blocks = block_expert.shape[0]
    assert n_blocks >= 2 and xn.shape[1:] == (N_CHUNKS, LANES)
    wmap = lambda j, be, nv, tab: (layer, be[j], 0, 0)
    return pl.pallas_call(
        functools.partial(_moe_ffn_kernel, n_tok=n_tok, n_blocks=n_blocks),
        grid_spec=pltpu.PrefetchScalarGridSpec(
            num_scalar_prefetch=3,
            grid=(n_blocks,),
            in_specs=[
                pl.BlockSpec(memory_space=pl.ANY),
                pl.BlockSpec((1, 1, d, D_EXPERT), wmap),
                pl.BlockSpec((1, 1, d, D_EXPERT), wmap),
                pl.BlockSpec((1, 1, D_EXPERT, d), wmap),
            ],
            out_specs=pl.BlockSpec(memory_space=pl.ANY),
            scratch_shapes=[
                pltpu.VMEM((2, EXPERT_BLOCK, N_CHUNKS, LANES), _F32),
                pltpu.VMEM((2, EXPERT_BLOCK, N_CHUNKS, LANES), _F32),
                pltpu.VMEM((d, D_EXPERT), _BF16),
                pltpu.VMEM((d, D_EXPERT), _BF16),
                pltpu.VMEM((D_EXPERT, d), _BF16),
                pltpu.SemaphoreType.DMA((2,)),
                pltpu.SemaphoreType.DMA((2,)),
            ],
        ),
        out_shape=jax.ShapeDtypeStruct((2 * n_tok, N_CHUNKS, LANES), _F32),
        compiler_params=pltpu.CompilerParams(
            dimension_semantics=("arbitrary",), vmem_limit_bytes=VMEM_LIMIT),
        name="moe_ffn",
    )(block_expert, n_valid, table, xn, w1, w3, w2)


def _moe_plan_kernel(eid_ref, tab_ref, cnt_ref, slot_v, slot_s, sem):
    rows = eid_ref.shape[0]
    depth = max(rows, LANES)
    eid = eid_ref[...]
    r_i = lax.broadcasted_iota(jnp.int32, (LANES, LANES), 0)
    c_i = lax.broadcasted_iota(jnp.int32, (LANES, LANES), 1)
    earlier_lane = jnp.where(r_i < c_i, 1.0, 0.0).astype(_BF16)
    all_lanes = jnp.ones((LANES, LANES), _BF16)
    q_i = lax.broadcasted_iota(jnp.int32, (rows, depth), 1)
    p_i = lax.broadcasted_iota(jnp.int32, (rows, depth), 0)
    earlier_row = jnp.where(q_i < p_i, 1.0, 0.0).astype(_BF16)
    slot = jnp.zeros((rows, LANES), _F32)
    start = jnp.int32(0)
    for ex in range(N_EXPERTS):
        hit = eid == ex
        hot = jnp.where(hit, 1.0, 0.0).astype(_BF16)
        in_row = _dot(hot, earlier_lane)
        row_tot = _dot(hot, all_lanes)
        if depth > rows:
            row_tot = jnp.concatenate([row_tot, jnp.zeros((depth - rows, LANES), _F32)], axis=0)
        above = _dot(earlier_row, row_tot.astype(_BF16))
        count = jnp.sum(jnp.where(hit, 1, 0))
        slot = jnp.where(hit, start.astype(_F32) + in_row + above, slot)
        cnt_ref[ex] = count
        start = start + (count + EXPERT_BLOCK - 1) // EXPERT_BLOCK * EXPERT_BLOCK
    slot_v[...] = slot.astype(jnp.int32)
    to_smem = pltpu.make_async_copy(slot_v, slot_s, sem)
    to_smem.start()

    def clear(i, carry):
        tab_ref[i] = 0
        return carry
    lax.fori_loop(0, tab_ref.shape[0], clear, 0, unroll=8)
    to_smem.wait()

    def place(r, carry):
        for c in range(LANES):
            tab_ref[slot_s[r, c]] = r * LANES + c
        return carry
    lax.fori_loop(0, rows, place, 0)


def _moe_plan(eid, n_tok):
    n_assign = 2 * n_tok
    n_blocks = -(-(n_assign + N_EXPERTS * (EXPERT_BLOCK - 1)) // EXPERT_BLOCK)
    n_slots = n_blocks * EXPERT_BLOCK
    rows = n_assign // LANES
    assert n_assign % LANES == 0
    table, counts = pl.pallas_call(
        _moe_plan_kernel,
        in_specs=[pl.BlockSpec(memory_space=pltpu.VMEM)],
        out_specs=[pl.BlockSpec(memory_space=pltpu.SMEM), pl.BlockSpec(memory_space=pltpu.SMEM)],
        out_shape=[jax.ShapeDtypeStruct((n_slots,), jnp.int32), jax.ShapeDtypeStruct((N_EXPERTS,), jnp.int32)],
        scratch_shapes=[pltpu.VMEM((rows, LANES), jnp.int32), pltpu.SMEM((rows, LANES), jnp.int32),
                        pltpu.SemaphoreType.DMA],
        name="moe_plan",
    )(eid.reshape(rows, LANES))
    padded = (counts + EXPERT_BLOCK - 1) // EXPERT_BLOCK * EXPERT_BLOCK
    pad_end = jnp.cumsum(padded)
    pad_start = pad_end - padded
    block_start = jnp.arange(n_blocks, dtype=jnp.int32) * EXPERT_BLOCK
    block_expert = jnp.minimum(jnp.sum((pad_end[None, :] <= block_start[:, None]).astype(jnp.int32), axis=1),
                               N_EXPERTS - 1)
    n_valid = jnp.clip((pad_start + counts)[block_expert] - block_start, 0, EXPERT_BLOCK).astype(jnp.int32)
    return block_expert, n_valid, table


def _combine(h_ref, y0_ref, y1_ref, gate_ref):
    gate = gate_ref[...]
    return h_ref[...] + (gate[:, 0:1] * _load_row_tiles(y0_ref) + gate[:, 1:2] * _load_row_tiles(y1_ref))


def _emit_head(r, h, outs, scr):
    tm = r.shape[0]
    if scr is not None:
        scr[h] = r
    for ref, dil in outs:
        if dil == 0:
            ref[:, h, :] = r
        elif dil == 1:
            ref[:, _chunk(h)] = r.astype(ref.dtype)
        else:
            for res in range(dil):
                col = res * D_MODEL + h * HEAD_DIM
                ref[:, col:col + HEAD_DIM] = scr[h, pl.ds(res, tm // dil, stride=dil), :].astype(ref.dtype)


def _rope(xh, cos, sin_signed):
    return xh * cos + pltpu.roll(xh, HEAD_DIM // 2, axis=1) * sin_signed


def _qkv_kernel(*refs, streams, per_seq, kept):
    (h_ref, y0_ref, y1_ref, gate_ref, gmix_ref, gkv_ref, wq_ref, wkv_ref, cos_ref, sin_ref) = refs[:10]
    outs = refs[10:]
    if streams:
        (h2_ref, q0_ref, q1_ref, q2_ref, k0_ref, k1_ref, k2_ref, v0_ref, v1_ref, v2_ref, kf_ref, vf_ref,
         scr_q1, scr_q2, scr_k, scr_v) = outs
        q_outs = [([(q0_ref, 1)], None), ([(q1_ref, DILATIONS[1])], scr_q1), ([(q2_ref, DILATIONS[2])], scr_q2)]
        k_outs = ([(k0_ref, 1), (k1_ref, DILATIONS[1]), (k2_ref, DILATIONS[2])], scr_k)
        v_outs = ([(v0_ref, 1), (v1_ref, DILATIONS[1]), (v2_ref, DILATIONS[2])], scr_v)
    else:
        h2_ref, q0_ref, q1_ref, q2_ref, kf_ref, vf_ref = outs
        q_outs = [([(q0_ref, 1)], None), ([(q1_ref, 1)], None), ([(q2_ref, 1)], None)]
        k_outs = ([(kf_ref, 0)], None)
        v_outs = ([(vf_ref, 0)], None)

    h2 = _combine(h_ref, y0_ref, y1_ref, gate_ref)
    h2_ref[...] = h2
    cos = cos_ref[...]
    sin = sin_ref[...]
    xn = _rms(h2, gmix_ref[...]).astype(_BF16)
    scale = HEAD_DIM ** -0.5
    for g in range(N_GROUPS):
        q = _dot(xn, wq_ref[:, g * D_MODEL:(g + 1) * D_MODEL])
        for h in range(N_HEADS):
            _emit_head(_rope(q[:, _chunk(h)], cos, sin) * scale, h, *q_outs[g])
    xkv = _rms(h2, gkv_ref[...]).astype(_BF16)
    k = _dot(xkv, wkv_ref[:, 0:D_MODEL])
    for h in range(N_HEADS):
        _emit_head(_rope(k[:, _chunk(h)], cos, sin), h, *k_outs)
    v = _dot(xkv, wkv_ref[:, D_MODEL:2 * D_MODEL])
    for h in range(N_HEADS):
        _emit_head(v[:, _chunk(h)], h, *v_outs)
    if streams:
        @pl.when(pl.program_id(0) % per_seq >= per_seq - kept)
        def _():
            for h in range(N_HEADS):
                kf_ref[:, h, :] = scr_k[h]
                vf_ref[:, h, :] = scr_v[h]


def _qkv(h, ypair, gate, gmix, gkv, wq, wkv, cos, sin, *, tm, streams, seq, keep):
    t, d = h.shape
    nt = t // tm
    per_seq = seq // tm
    kept = keep // tm
    assert seq % tm == 0 and keep % tm == 0

    def tail(i):
        return ((i // per_seq) * kept + jnp.maximum(i % per_seq - (per_seq - kept), 0), 0, 0)

    tail_f32 = (pl.BlockSpec((tm, N_HEADS, HEAD_DIM), tail),
                jax.ShapeDtypeStruct((t // seq * keep, N_HEADS, HEAD_DIM), _F32))
    row = lambda i: (i, 0)
    const = lambda i: (0, 0)
    nat = lambda dt: (pl.BlockSpec((tm, d), row), jax.ShapeDtypeStruct((t, d), dt))

    def stream(dil):
        return (pl.BlockSpec((tm // dil, dil * d), row), jax.ShapeDtypeStruct((t // dil, dil * d), _BF16))

    if streams:
        d1, d2 = DILATIONS[1], DILATIONS[2]
        outs = [nat(_F32), nat(_BF16), stream(d1), stream(d2), nat(_BF16), stream(d1), stream(d2),
                nat(_BF16), stream(d1), stream(d2), tail_f32, tail_f32]
        scratch = [pltpu.VMEM((N_HEADS, tm, HEAD_DIM), _F32)] * 4
    else:
        outs = [nat(_F32), nat(_BF16), nat(_BF16), nat(_BF16), tail_f32, tail_f32]
        scratch = []
    return pl.pallas_call(
        functools.partial(_qkv_kernel, streams=streams, per_seq=per_seq, kept=kept),
        grid=(nt,),
        in_specs=[
            pl.BlockSpec((tm, d), row),
            pl.BlockSpec((tm, N_CHUNKS, LANES), lambda i: (i, 0, 0)),
            pl.BlockSpec((tm, N_CHUNKS, LANES), lambda i: (nt + i, 0, 0)),
            pl.BlockSpec((tm, 2), row),
            pl.BlockSpec((1, d), const),
            pl.BlockSpec((1, d), const),
            pl.BlockSpec((d, 3 * d), const),
            pl.BlockSpec((d, 2 * d), const),
            pl.BlockSpec((tm, HEAD_DIM), row),
            pl.BlockSpec((tm, HEAD_DIM), row),
        ],
        out_specs=[o[0] for o in outs],
        out_shape=[o[1] for o in outs],
        scratch_shapes=scratch,
        compiler_params=pltpu.CompilerParams(
            dimension_semantics=("arbitrary",), vmem_limit_bytes=VMEM_LIMIT),
        name="qkv_proj_streams" if streams else "qkv_proj",
    )(h, ypair, ypair, gate, gmix, gkv, wq, wkv, cos, sin)


ATTN_TQ = 512


def _attn_prompt_kernel(q_ref, kp_ref, kc_ref, vp_ref, vc_ref, o_ref, st_ref, *, tq):
    i = pl.program_id(1)
    qi = lax.broadcasted_iota(jnp.int32, (SPAN, 2 * SPAN), 0)
    kj = lax.broadcasted_iota(jnp.int32, (SPAN, 2 * SPAN), 1)
    band = (kj >= qi) & (kj <= qi + SPAN)
    band0 = band & (kj >= jnp.where(i > 0, 0, SPAN))
    lane = lax.broadcasted_iota(jnp.int32, (SPAN, HEAD_DIM), 1)
    ones = jnp.ones((2 * SPAN, HEAD_DIM), _BF16)
    for a in range(tq // SPAN):
        rows = slice(a * SPAN, (a + 1) * SPAN)
        st = jnp.zeros((SPAN, HEAD_DIM), _F32)
        for h in range(N_HEADS):
            hs = _chunk(h)
            if a == 0:
                keys = jnp.concatenate([kp_ref[:, hs], kc_ref[rows, hs]], axis=0)
                vals = jnp.concatenate([vp_ref[:, hs], vc_ref[rows, hs]], axis=0)
                mask = band0
            else:
                both = slice((a - 1) * SPAN, (a + 1) * SPAN)
                keys, vals, mask = kc_ref[both, hs], vc_ref[both, hs], band
            s = jnp.where(mask, _dot_nt(q_ref[rows, hs], keys), NEG_BIG)
            m = jnp.max(s, axis=1, keepdims=True)
            p = jnp.exp(s - m).astype(_BF16)
            ol = _dot(p, jnp.concatenate([vals, ones], axis=1))
            l = ol[:, HEAD_DIM:]
            o_ref[rows, hs] = ol[:, :HEAD_DIM] / l
            st = jnp.where(lane == h, m + jnp.log(l), st)
        st_ref[rows, :] = st


def _attn_prompt(q, k, v, batch, dilation):
    rows, width = q.shape
    d = width // dilation
    per_batch = rows // batch
    tq = min(ATTN_TQ, per_batch)
    assert per_batch % tq == 0 and tq % SPAN == 0
    nq = per_batch // tq
    sub = tq // SPAN
    cur = lambda b, i, r: (b * nq + i, r)
    prev = lambda b, i, r: ((b * nq + i) * sub - jnp.where(i > 0, 1, 0), r)
    big = (tq, d)
    small = (SPAN, d)
    return pl.pallas_call(
        functools.partial(_attn_prompt_kernel, tq=tq),
        grid=(batch, nq, dilation),
        in_specs=[pl.BlockSpec(big, cur), pl.BlockSpec(small, prev), pl.BlockSpec(big, cur),
                  pl.BlockSpec(small, prev), pl.BlockSpec(big, cur)],
        out_specs=[pl.BlockSpec(big, cur), pl.BlockSpec((tq, HEAD_DIM), cur)],
        out_shape=[jax.ShapeDtypeStruct((rows, width), _F32),
                   jax.ShapeDtypeStruct((rows, dilation * HEAD_DIM), _F32)],
        compiler_params=pltpu.CompilerParams(
            dimension_semantics=("arbitrary", "arbitrary", "arbitrary"), vmem_limit_bytes=VMEM_LIMIT),
        name=f"attn_prompt_d{dilation}",
    )(q, k, k, v, v)


N_NEW = 8
CACHE_A_GROUPS = (PAST_LEN - WINDOWS[1]) // DILATIONS[2]
CACHE_A_ROWS = CACHE_A_GROUPS * N_NEW
CACHE_B_ROWS = WINDOWS[1]
CACHE_B_GROUPS = CACHE_B_ROWS // DILATIONS[2]
KEYS_REAL = CACHE_A_ROWS + CACHE_B_ROWS + N_NEW
KEYS_PAD = -(-KEYS_REAL // 128) * 128


def _sample_key_positions():
    pos = np.full((KEYS_PAD,), -1, np.int64)
    a = np.arange(CACHE_A_ROWS)
    pos[:CACHE_A_ROWS] = (a // N_NEW) * DILATIONS[2] + a % N_NEW
    pos[CACHE_A_ROWS:CACHE_A_ROWS + CACHE_B_ROWS] = PAST_LEN - CACHE_B_ROWS + np.arange(CACHE_B_ROWS)
    pos[CACHE_A_ROWS + CACHE_B_ROWS:KEYS_REAL] = PAST_LEN + np.arange(N_NEW)
    return pos


def _sample_bias():
    pos = _sample_key_positions()
    bias = np.full((N_GROUPS * N_NEW, KEYS_PAD), NEG_BIG, np.float32)
    for g in range(N_GROUPS):
        for n in range(N_NEW):
            delta = PAST_LEN + n - pos
            ok = (pos >= 0) & (delta >= 0) & (delta <= WINDOWS[g]) & (delta % DILATIONS[g] == 0)
            assert int(ok.sum()) == SPAN + 1
            bias[g * N_NEW + n, ok] = 0.0
    return bias


def _attn_sample_kernel(q0_ref, q1_ref, q2_ref, kn_ref, vn_ref, bias_ref, ck_hbm, cv_hbm,
                        o_ref, ka, kb, va, vb, kall, vall, sem, *, n_seq):
    b = pl.program_id(0)
    slot = b % 2

    def cache_copies(seq, s):
        cps = []
        for h in range(N_HEADS):
            for src, dst_a, dst_b in ((ck_hbm, ka, kb), (cv_hbm, va, vb)):
                cps.append(pltpu.make_async_copy(
                    src.at[seq, pl.ds(0, CACHE_A_GROUPS), pl.ds(0, N_NEW), h], dst_a.at[s, h], sem.at[s]))
                cps.append(pltpu.make_async_copy(
                    src.at[seq, pl.ds(CACHE_A_GROUPS, CACHE_B_GROUPS), :, h], dst_b.at[s, h], sem.at[s]))
        return cps

    @pl.when(b == 0)
    def _():
        kall[KEYS_REAL:KEYS_PAD, :] = jnp.zeros((KEYS_PAD - KEYS_REAL, D_MODEL), _BF16)
        vall[KEYS_REAL:KEYS_PAD, :] = jnp.zeros((KEYS_PAD - KEYS_REAL, D_MODEL), _BF16)
        for cp in cache_copies(0, 0):
            cp.start()

    @pl.when(b + 1 < n_seq)
    def _():
        for cp in cache_copies(b + 1, 1 - slot):
            cp.start()

    for cp in cache_copies(b, slot):
        cp.wait()

    nb0 = CACHE_A_ROWS + CACHE_B_ROWS
    for src_a, src_b, src_n, dst in ((ka, kb, kn_ref, kall), (va, vb, vn_ref, vall)):
        for h in range(N_HEADS):
            hs = _chunk(h)
            dst[0:CACHE_A_ROWS, hs] = src_a[slot, h].reshape(CACHE_A_ROWS, HEAD_DIM).astype(_BF16)
            dst[CACHE_A_ROWS:nb0, hs] = src_b[slot, h].reshape(CACHE_B_ROWS, HEAD_DIM).astype(_BF16)
            dst[nb0:KEYS_REAL, hs] = src_n[0, :, h, :].astype(_BF16)

    bias = bias_ref[...]
    for h in range(N_HEADS):
        hs = _chunk(h)
        qh = jnp.concatenate([q0_ref[0, :, hs].astype(_F32), q1_ref[0, :, hs].astype(_F32),
                              q2_ref[0, :, hs].astype(_F32)], axis=0).astype(_BF16)
        s = _dot_nt(qh, kall[:, hs]) + bias
        m = jnp.max(s, axis=1, keepdims=True)
        m8 = jnp.maximum(jnp.maximum(m[0:N_NEW], m[N_NEW:2 * N_NEW]), m[2 * N_NEW:3 * N_NEW])
        p = jnp.exp(s - jnp.concatenate([m8, m8, m8], axis=0))
        p8 = p[0:N_NEW] + p[N_NEW:2 * N_NEW] + p[2 * N_NEW:3 * N_NEW]
        l8 = jnp.sum(p8, axis=1, keepdims=True)
        o_ref[0, :, hs] = _dot(p8.astype(_BF16), vall[:, hs]) / l8


def _attn_sample(q0, q1, q2, cache_k, cache_v, k_new, v_new):
    nb, n_new, d = q0.shape
    past = cache_k.shape[1]
    assert past == PAST_LEN and n_new == N_NEW and d == D_MODEL
    assert cache_k.shape[2:] == (N_HEADS, HEAD_DIM)
    ck = cache_k.reshape(nb, past // DILATIONS[2], DILATIONS[2], N_HEADS, HEAD_DIM)
    cv = cache_v.reshape(nb, past // DILATIONS[2], DILATIONS[2], N_HEADS, HEAD_DIM)
    bias = jnp.asarray(_sample_bias())
    new = pl.BlockSpec((1, n_new, d), lambda b: (b, 0, 0))
    new_kv = pl.BlockSpec((1, n_new, N_HEADS, HEAD_DIM), lambda b: (b, 0, 0, 0))
    part_a = (2, N_HEADS, CACHE_A_GROUPS, N_NEW, HEAD_DIM)
    part_b = (2, N_HEADS, CACHE_B_GROUPS, DILATIONS[2], HEAD_DIM)
    return pl.pallas_call(
        functools.partial(_attn_sample_kernel, n_seq=nb),
        grid=(nb,),
        in_specs=[new, new, new, new_kv, new_kv,
                  pl.BlockSpec((N_GROUPS * N_NEW, KEYS_PAD), lambda b: (0, 0)),
                  pl.BlockSpec(memory_space=pl.ANY), pl.BlockSpec(memory_space=pl.ANY)],
        out_specs=new,
        out_shape=jax.ShapeDtypeStruct((nb, n_new, d), _F32),
        scratch_shapes=[pltpu.VMEM(part_a, _F32), pltpu.VMEM(part_b, _F32),
                        pltpu.VMEM(part_a, _F32), pltpu.VMEM(part_b, _F32),
                        pltpu.VMEM((KEYS_PAD, d), _BF16), pltpu.VMEM((KEYS_PAD, d), _BF16),
                        pltpu.SemaphoreType.DMA((2,))],
        compiler_params=pltpu.CompilerParams(
            dimension_semantics=("arbitrary",), vmem_limit_bytes=VMEM_LIMIT),
        name="attn_sample",
    )(q0, q1, q2, k_new, v_new, bias, ck, cv)


def _attn_out_kernel(*refs, n_groups, tm):
    if n_groups == 1:
        (o_ref, h_ref, wo_ref, gffn_ref, wr_ref, br_ref,
         h3_ref, xn2_ref, eid_ref, gate_ref) = refs
        o = o_ref[...].astype(_BF16)
    else:
        (o0_ref, o1_ref, o2_ref, s0_ref, s1_ref, s2_ref, h_ref, wo_ref, gffn_ref, wr_ref, br_ref,
         h3_ref, xn2_ref, eid_ref, gate_ref, obuf, scr_o1, scr_o2, scr_s1, scr_s2) = refs
        for o_ref, s_ref, scr_o, scr_s, dil in ((o1_ref, s1_ref, scr_o1, scr_s1, DILATIONS[1]),
                                                (o2_ref, s2_ref, scr_o2, scr_s2, DILATIONS[2])):
            for res in range(dil):
                dst = pl.ds(res, tm // dil, stride=dil)
                scr_s[dst, :] = s_ref[:, res * HEAD_DIM:(res + 1) * HEAD_DIM]
                for h in range(N_HEADS):
                    col = res * D_MODEL + h * HEAD_DIM
                    scr_o[h, dst, :] = o_ref[:, col:col + HEAD_DIM]
        sts = [s0_ref[...], scr_s1[...], scr_s2[...]]
        mx = jnp.maximum(jnp.maximum(sts[0], sts[1]), sts[2])
        es = [jnp.exp(s - mx) for s in sts]
        den = es[0] + es[1] + es[2]
        ws = [e / den for e in es]
        for h in range(N_HEADS):
            acc = ws[0][:, h:h + 1] * o0_ref[:, _chunk(h)]
            acc = acc + ws[1][:, h:h + 1] * scr_o1[h]
            acc = acc + ws[2][:, h:h + 1] * scr_o2[h]
            obuf[:, _chunk(h)] = acc.astype(_BF16)
        o = obuf[...]
    h3 = h_ref[...] + _dot(o, wo_ref[...])
    h3_ref[...] = h3
    xn2 = _rms(h3, gffn_ref[...])
    _store_row_tiles(xn2_ref, xn2)
    _route(xn2, wr_ref, br_ref, eid_ref, gate_ref)


def _attn_out(os, sts, h, wo, gffn, wr, br, *, tm):
    t, d = h.shape
    n_groups = len(os)
    row = lambda i: (i, 0)
    const = lambda i: (0, 0)
    if n_groups == 1:
        in_specs = [pl.BlockSpec((tm, d), row)]
        scratch = []
        args = [os[0]]
    else:
        in_specs = [pl.BlockSpec((tm // dil, dil * d), row) for dil in DILATIONS]
        in_specs += [pl.BlockSpec((tm // dil, dil * HEAD_DIM), row) for dil in DILATIONS]
        scratch = [pltpu.VMEM((tm, d), _BF16),
                   pltpu.VMEM((N_HEADS, tm, HEAD_DIM), _F32), pltpu.VMEM((N_HEADS, tm, HEAD_DIM), _F32),
                   pltpu.VMEM((tm, HEAD_DIM), _F32), pltpu.VMEM((tm, HEAD_DIM), _F32)]
        args = list(os) + list(sts)
    in_specs += [pl.BlockSpec((tm, d), row), pl.BlockSpec((d, d), const), pl.BlockSpec((1, d), const),
                 pl.BlockSpec((ROUTER_ROWS, d), const), pl.BlockSpec((ROUTER_ROWS, 1), const)]
    return pl.pallas_call(
        functools.partial(_attn_out_kernel, n_groups=n_groups, tm=tm),
        grid=(t // tm,),
        in_specs=in_specs,
        out_specs=[pl.BlockSpec((tm, d), row), pl.BlockSpec((tm, N_CHUNKS, LANES), lambda i: (i, 0, 0)),
                   pl.BlockSpec((2, tm), lambda i: (0, i)), pl.BlockSpec((2, tm), lambda i: (0, i))],
        out_shape=[jax.ShapeDtypeStruct((t, d), _F32), jax.ShapeDtypeStruct((t, N_CHUNKS, LANES), _F32),
                   jax.ShapeDtypeStruct((2, t), jnp.int32), jax.ShapeDtypeStruct((2, t), _F32)],
        scratch_shapes=scratch,
        compiler_params=pltpu.CompilerParams(
            dimension_semantics=("arbitrary",), vmem_limit_bytes=VMEM_LIMIT),
        name=f"attn_out_g{n_groups}",
    )(*args, h, wo, gffn, wr, br)


def _final_kernel(h_ref, y0_ref, y1_ref, gate_ref, g_ref, out_ref):
    out_ref[...] = _rms(_combine(h_ref, y0_ref, y1_ref, gate_ref), g_ref[...])


def _final(h, ypair, gate, g, *, tm):
    t, d = h.shape
    nt = t // tm
    row = lambda i: (i, 0)
    return pl.pallas_call(
        _final_kernel,
        grid=(nt,),
        in_specs=[pl.BlockSpec((tm, d), row),
                  pl.BlockSpec((tm, N_CHUNKS, LANES), lambda i: (i, 0, 0)),
                  pl.BlockSpec((tm, N_CHUNKS, LANES), lambda i: (nt + i, 0, 0)),
                  pl.BlockSpec((tm, 2), row),
                  pl.BlockSpec((1, d), lambda i: (0, 0))],
        out_specs=pl.BlockSpec((tm, d), row),
        out_shape=jax.ShapeDtypeStruct((t, d), _F32),
        compiler_params=pltpu.CompilerParams(
            dimension_semantics=("arbitrary",), vmem_limit_bytes=VMEM_LIMIT),
        name="final_norm",
    )(h, ypair, ypair, gate, g)


def _rope_tables(pos):
    half = HEAD_DIM // 2
    inv_freq = jnp.power(jnp.float32(ROPE_THETA), -jnp.arange(half, dtype=jnp.float32) / half)
    ang = pos.astype(jnp.float32)[:, None] * inv_freq[None, :]
    cos = jnp.cos(ang)
    sin = jnp.sin(ang)
    return jnp.concatenate([cos, cos], axis=-1), jnp.concatenate([-sin, sin], axis=-1)


def _router_params(wg, bg, we, be):
    wr = jnp.zeros((ROUTER_ROWS, D_MODEL), _F32)
    wr = wr.at[0:MOE_GROUPS].set(wg.T).at[EXPERT_ROW0:EXPERT_ROW0 + N_EXPERTS].set(we.T)
    br = jnp.zeros((ROUTER_ROWS, 1), _F32)
    br = br.at[0:MOE_GROUPS, 0].set(bg).at[EXPERT_ROW0:EXPERT_ROW0 + N_EXPERTS, 0].set(be)
    return wr, br


def _moe(xn2, eid, experts, layer):
    block_expert, n_valid, table = _moe_plan(eid, xn2.shape[0])
    return _moe_ffn(block_expert, n_valid, table, xn2, *experts, layer)


def _forward(x3, prev3, pos_rows, attend, p, *, shift, tm, streams, keep):
    h1, xn2, eid, gate, state = _conv_layer(
        x3, prev3, p['gmix'][0], p['win'], p['ck'], p['wout'], p['gffn'][0], *p['router'][0],
        shift=shift, tm=tm)
    t = x3.shape[0] * x3.shape[1]
    ypair = _moe(xn2, eid, p['experts'], 0)
    cos, sin = _rope_tables(pos_rows)
    h2, *qkv, kf, vf = _qkv(h1.reshape(t, D_MODEL), ypair, gate.T, p['gmix'][1], p['gkv'],
                            p['wq'], p['wkv'], cos, sin, tm=256, streams=streams,
                            seq=x3.shape[1], keep=keep)
    os, sts = attend(qkv, kf, vf)
    h3, xn2, eid, gate = _attn_out(os, sts, h2, p['wo'], p['gffn'][1], *p['router'][1], tm=256)
    ypair = _moe(xn2, eid, p['experts'], 1)
    y = _final(h3, ypair, gate.T, p['gfinal'], tm=512)
    return y, state, kf, vf


def kernel(x_prompt, x_sample, cache_k, cache_v, state_conv, norm_mix, norm_ffn, norm_kv, norm_final,
           conv_w_in, conv_kernel, conv_w_out, attn_w_q, attn_w_kv, attn_w_o, router_group_w,
           router_group_b, router_expert_w, router_expert_b, expert_w1, expert_w3, expert_w2):
    b_p, s_p, d = x_prompt.shape
    b_s, n_new, _ = x_sample.shape
    assert d == D_MODEL and n_new == N_NEW and s_p % (DILATIONS[2] * SPAN) == 0
    assert norm_mix.shape[0] == 2 and conv_w_in.shape[0] == 1 and attn_w_q.shape[0] == 1
    assert cache_k.shape[1] == PAST_LEN

    p = {
        'gmix': [norm_mix[l].reshape(1, d) for l in range(2)],
        'gffn': [norm_ffn[l].reshape(1, d) for l in range(2)],
        'gkv': norm_kv.reshape(1, d),
        'gfinal': norm_final.reshape(1, d),
        'win': conv_w_in[0].astype(_BF16),
        'ck': conv_kernel[0],
        'wout': conv_w_out[0].astype(_BF16),
        'wq': attn_w_q[0].astype(_BF16),
        'wkv': attn_w_kv.astype(_BF16),
        'wo': attn_w_o[0].astype(_BF16),
        'router': [_router_params(router_group_w[l], router_group_b[l], router_expert_w[l], router_expert_b[l])
                   for l in range(2)],
        'experts': (expert_w1, expert_w3, expert_w2),
    }

    def attend_prompt(qkv, kf, vf):
        q0, q1, q2, k0, k1, k2, v0, v1, v2 = qkv
        os, sts = [], []
        for q, k, v, dil in ((q0, k0, v0, DILATIONS[0]), (q1, k1, v1, DILATIONS[1]), (q2, k2, v2, DILATIONS[2])):
            o, st = _attn_prompt(q, k, v, b_p, dil)
            os.append(o)
            sts.append(st)
        return os, sts

    pos_p = jnp.tile(jnp.arange(s_p, dtype=jnp.int32), b_p)
    zero_state = jnp.zeros((b_p, CONV_WIDTH - 1, d), x_prompt.dtype)
    keep = min(max(WINDOWS), s_p)
    y_p, st_p, kf_p, vf_p = _forward(x_prompt, zero_state, pos_p, attend_prompt, p, shift=1, tm=512,
                                     streams=True, keep=keep)
    y_prompt = y_p.reshape(b_p, s_p, d)
    k_p = kf_p.reshape(b_p, keep, N_HEADS, HEAD_DIM)
    v_p = vf_p.reshape(b_p, keep, N_HEADS, HEAD_DIM)
    conv_p = st_p[None]

    halves = 2
    bh = b_s // halves

    def to_rows(a):
        w = a.shape[-1]
        return a.reshape(halves, bh, n_new, w).transpose(0, 2, 1, 3).reshape(halves * n_new * bh, w)

    def to_batch(a):
        w = a.shape[1:]
        return jnp.swapaxes(a.reshape(halves, n_new, bh, *w), 1, 2).reshape(b_s, n_new, *w)

    def attend_sample(qkv, kf, vf):
        q0, q1, q2 = qkv
        o = _attn_sample(to_batch(q0), to_batch(q1), to_batch(q2), cache_k, cache_v,
                         to_batch(kf), to_batch(vf))
        return [to_rows(o)], None

    x_s = to_rows(x_sample).reshape(halves, n_new * bh, d)
    prev_s = state_conv[0].reshape(halves, bh, CONV_WIDTH - 1, d).transpose(0, 2, 1, 3).reshape(
        halves, (CONV_WIDTH - 1) * bh, d)
    pos_s = jnp.tile(jnp.repeat(PAST_LEN + jnp.arange(n_new, dtype=jnp.int32), bh), halves)
    y_s, st_s, kf_s, vf_s = _forward(x_s, prev_s, pos_s, attend_sample, p, shift=bh, tm=n_new * bh,
                                     streams=False, keep=n_new * bh)
    y_sample = to_batch(y_s)
    k_s = to_batch(kf_s)
    v_s = to_batch(vf_s)
    conv_s = st_s.reshape(halves, CONV_WIDTH - 1, bh, d).transpose(0, 2, 1, 3).reshape(
        b_s, CONV_WIDTH - 1, d)[None]

    return (y_prompt, y_sample, k_p, v_p, conv_p, k_s, v_s, conv_s)
```

```python
import functools

import numpy as np
import jax
import jax.numpy as jnp
from jax import lax
from jax.experimental import pallas as pl
from jax.experimental.pallas import tpu as pltpu
from jax.experimental.pallas import tpu_sc as plsc

D_MODEL = 1024
CONV_WIDTH = 3
WINDOWS = (128, 512, 2048)
DILATIONS = (1, 4, 16)
N_GROUPS = 3
N_HEADS = 8
HEAD_DIM = 128
SPAN = 128
ROPE_THETA = 10000.0
MOE_GROUPS = 4
EXPERTS_PER_GROUP = 4
N_EXPERTS = 16
D_EXPERT = 512
EXPERT_BLOCK = 128
RMS_EPS = 1e-6
NEG_BIG = -1e30
PAST_LEN = 2048

LANES = 128
ROUTER_ROWS = 32
EXPERT_ROW0 = 8
VMEM_LIMIT = 56 * 1024 * 1024

_F32 = jnp.float32
_BF16 = jnp.bfloat16


def _rms(x, g):
    ms = jnp.mean(x * x, axis=-1, keepdims=True)
    return (x * lax.rsqrt(ms + RMS_EPS)) * g


def _dot(a, b):
    return jnp.dot(a, b, preferred_element_type=_F32)


def _dot_nt(a, b, precision=None):
    return lax.dot_general(a, b, (((1,), (1,)), ((), ())), precision=precision,
                           preferred_element_type=_F32)


def _chunk(c):
    return slice(c * LANES, (c + 1) * LANES)


def _route(xn, wr_ref, br_ref, eid_ref, gate_ref):
    logits = _dot_nt(wr_ref[...], xn, precision=lax.Precision.HIGHEST) + br_ref[...]
    lg = logits[0:MOE_GROUPS]
    row = lax.broadcasted_iota(jnp.int32, lg.shape, 0).astype(_F32)
    eg = jnp.exp(lg - jnp.max(lg, axis=0, keepdims=True))
    pg = eg / jnp.sum(eg, axis=0, keepdims=True)
    gp = jnp.max(pg, axis=0, keepdims=True)
    gi = jnp.min(jnp.where(pg == gp, row, float(MOE_GROUPS)), axis=0, keepdims=True)
    le = jnp.zeros_like(lg)
    for g in range(MOE_GROUPS):
        r0 = EXPERT_ROW0 + g * EXPERTS_PER_GROUP
        le = le + jnp.where(gi == float(g), logits[r0:r0 + EXPERTS_PER_GROUP], 0.0)
    ee = jnp.exp(le - jnp.max(le, axis=0, keepdims=True))
    ev = ee / jnp.sum(ee, axis=0, keepdims=True)
    v1 = jnp.max(ev, axis=0, keepdims=True)
    i1 = jnp.min(jnp.where(ev == v1, row, float(EXPERTS_PER_GROUP)), axis=0, keepdims=True)
    ev2 = jnp.where(row == i1, -1.0, ev)
    v2 = jnp.max(ev2, axis=0, keepdims=True)
    i2 = jnp.min(jnp.where(ev2 == v2, row, float(EXPERTS_PER_GROUP)), axis=0, keepdims=True)
    den = v1 + v2
    ids = jnp.concatenate([gi * EXPERTS_PER_GROUP + i1, gi * EXPERTS_PER_GROUP + i2], axis=0)
    eid_ref[...] = ids.astype(jnp.int32)
    gate_ref[...] = jnp.concatenate([gp * v1 / den, gp * v2 / den], axis=0)


def _conv_layer_kernel(x_ref, prev_ref, gmix_ref, win_ref, ck_ref, wout_ref, gffn_ref, wr_ref, br_ref,
                       h_ref, xn2_ref, eid_ref, gate_ref, st_ref, ubuf, *, shift, tm):
    i = pl.program_id(1)
    halo = ubuf.shape[0] - tm
    keep = (CONV_WIDTH - 1) * shift

    @pl.when(i == 0)
    def _():
        ubuf[halo - keep:halo, :] = prev_ref[0]

    @pl.when(i > 0)
    def _():
        ubuf[halo - keep:halo, :] = ubuf[halo + tm - keep:halo + tm, :]

    x = x_ref[0]
    xn = _rms(x, gmix_ref[...]).astype(_BF16)
    c_gate = _dot(xn, win_ref[:, 0:D_MODEL])
    hid = _dot(xn, win_ref[:, 2 * D_MODEL:3 * D_MODEL])
    u = c_gate * hid
    ubuf[halo:halo + tm, :] = u
    u1 = ubuf[halo - shift:halo - shift + tm, :]
    u2 = ubuf[halo - 2 * shift:halo - 2 * shift + tm, :]
    conv = ck_ref[0:1, :] * u2 + ck_ref[1:2, :] * u1 + ck_ref[2:3, :] * u
    b_gate = _dot(xn, win_ref[:, D_MODEL:2 * D_MODEL])
    y = _dot((b_gate * conv).astype(_BF16), wout_ref[...])
    h = x + y
    h_ref[0] = h
    st_ref[0] = ubuf[halo + tm - keep:halo + tm, :]
    xn2 = _rms(h, gffn_ref[...])
    xn2_ref[...] = xn2
    _route(xn2, wr_ref, br_ref, eid_ref, gate_ref)


def _conv_layer(x3, prev3, gmix, win, ck, wout, gffn, wr, br, *, shift, tm):
    nb, s, d = x3.shape
    nt = s // tm
    keep = (CONV_WIDTH - 1) * shift
    halo = -(-keep // 8) * 8
    t_all = nb * s
    const = lambda b, i: (0, 0)
    return pl.pallas_call(
        functools.partial(_conv_layer_kernel, shift=shift, tm=tm),
        grid=(nb, nt),
        in_specs=[
            pl.BlockSpec((1, tm, d), lambda b, i: (b, i, 0)),
            pl.BlockSpec((1, keep, d), lambda b, i: (b, 0, 0)),
            pl.BlockSpec((1, d), const),
            pl.BlockSpec((d, 3 * d), const),
            pl.BlockSpec((CONV_WIDTH, d), const),
            pl.BlockSpec((d, d), const),
            pl.BlockSpec((1, d), const),
            pl.BlockSpec((ROUTER_ROWS, d), const),
            pl.BlockSpec((ROUTER_ROWS, 1), const),
        ],
        out_specs=[
            pl.BlockSpec((1, tm, d), lambda b, i: (b, i, 0)),
            pl.BlockSpec((tm, d), lambda b, i: (b * nt + i, 0)),
            pl.BlockSpec((2, tm), lambda b, i: (0, b * nt + i)),
            pl.BlockSpec((2, tm), lambda b, i: (0, b * nt + i)),
            pl.BlockSpec((1, keep, d), lambda b, i: (b, 0, 0)),
        ],
        out_shape=[
            jax.ShapeDtypeStruct((nb, s, d), _F32),
            jax.ShapeDtypeStruct((t_all, d), _F32),
            jax.ShapeDtypeStruct((2, t_all), jnp.int32),
            jax.ShapeDtypeStruct((2, t_all), _F32),
            jax.ShapeDtypeStruct((nb, keep, d), _F32),
        ],
        scratch_shapes=[pltpu.VMEM((halo + tm, d), _F32)],
        compiler_params=pltpu.CompilerParams(
            dimension_semantics=("arbitrary", "arbitrary"), vmem_limit_bytes=VMEM_LIMIT),
        name="conv_layer",
    )(x3, prev3, gmix, win, ck, wout, gffn, wr, br)


SC_WINDOW = 32


def _row_gather(src, idx):
    n_out = idx.shape[0]
    mesh = plsc.VectorSubcoreMesh(core_axis_name="c", subcore_axis_name="s")
    n_cores = mesh.num_cores
    per_worker = n_out // (n_cores * mesh.num_subcores)
    assert per_worker * n_cores * mesh.num_subcores == n_out and per_worker % SC_WINDOW == 0

    @pl.kernel(out_type=jax.ShapeDtypeStruct((n_out,) + src.shape[1:], src.dtype), mesh=mesh,
               scratch_types=[pltpu.VMEM((per_worker,), jnp.int32),
                              pltpu.VMEM((SC_WINDOW,) + src.shape[1:], src.dtype),
                              pltpu.SemaphoreType.DMA])
    def gather(src_hbm, idx_hbm, out_hbm, idx_v, rows_v, sem):
        base = (lax.axis_index("s") * n_cores + lax.axis_index("c")) * per_worker
        pltpu.sync_copy(idx_hbm.at[pl.ds(base, per_worker)], idx_v)

        @pl.loop(0, per_worker // SC_WINDOW)
        def _(q):
            off = q * SC_WINDOW
            pltpu.async_copy(src_hbm.at[idx_v.at[pl.ds(off, SC_WINDOW)]], rows_v, sem).wait()
            pltpu.sync_copy(rows_v, out_hbm.at[pl.ds(base + off, SC_WINDOW)])

    return gather(src, idx)


def _moe_ffn_kernel(be_ref, x_ref, w1_ref, w3_ref, w2_ref, y_ref, w1b, w3b, w2b):
    j = pl.program_id(0)

    @pl.when((j == 0) | (be_ref[j] != be_ref[jnp.maximum(j - 1, 0)]))
    def _():
        w1b[...] = w1_ref[0, 0].astype(_BF16)
        w3b[...] = w3_ref[0, 0].astype(_BF16)
        w2b[...] = w2_ref[0, 0].astype(_BF16)

    x = x_ref[...].astype(_BF16)
    a = _dot(x, w1b[...])
    b = _dot(x, w3b[...])
    y_ref[...] = _dot((jax.nn.silu(a) * b).astype(_BF16), w2b[...])


def _moe_ffn(block_expert, xs, w1, w3, w2, layer):
    n_slots, d = xs.shape
    n_blocks = block_expert.shape[0]
    wmap = lambda j, be: (layer, be[j], 0, 0)
    blk = pl.BlockSpec((EXPERT_BLOCK, d), lambda j, be: (j, 0))
    return pl.pallas_call(
        _moe_ffn_kernel,
        grid_spec=pltpu.PrefetchScalarGridSpec(
            num_scalar_prefetch=1,
            grid=(n_blocks,),
            in_specs=[blk,
                      pl.BlockSpec((1, 1, d, D_EXPERT), wmap),
                      pl.BlockSpec((1, 1, d, D_EXPERT), wmap),
                      pl.BlockSpec((1, 1, D_EXPERT, d), wmap)],
            out_specs=blk,
            scratch_shapes=[pltpu.VMEM((d, D_EXPERT), _BF16), pltpu.VMEM((d, D_EXPERT), _BF16),
                            pltpu.VMEM((D_EXPERT, d), _BF16)],
        ),
        out_shape=jax.ShapeDtypeStruct((n_slots, d), _F32),
        compiler_params=pltpu.CompilerParams(
            dimension_semantics=("arbitrary",), vmem_limit_bytes=VMEM_LIMIT),
        name="moe_ffn",
    )(block_expert, xs, w1, w3, w2)


def _moe_plan_kernel(eid_ref, tab_ref, cnt_ref, slot_v, slot_s, sem):
    rows = eid_ref.shape[0]
    depth = max(rows, LANES)
    eid = eid_ref[...]
    r_i = lax.broadcasted_iota(jnp.int32, (LANES, LANES), 0)
    c_i = lax.broadcasted_iota(jnp.int32, (LANES, LANES), 1)
    earlier_lane = jnp.where(r_i < c_i, 1.0, 0.0).astype(_BF16)
    all_lanes = jnp.ones((LANES, LANES), _BF16)
    q_i = lax.broadcasted_iota(jnp.int32, (rows, depth), 1)
    p_i = lax.broadcasted_iota(jnp.int32, (rows, depth), 0)
    earlier_row = jnp.where(q_i < p_i, 1.0, 0.0).astype(_BF16)
    slot = jnp.zeros((rows, LANES), _F32)
    start = jnp.int32(0)
    for ex in range(N_EXPERTS):
        hit = eid == ex
        hot = jnp.where(hit, 1.0, 0.0).astype(_BF16)
        in_row = _dot(hot, earlier_lane)
        row_tot = _dot(hot, all_lanes)
        if depth > rows:
            row_tot = jnp.concatenate([row_tot, jnp.zeros((depth - rows, LANES), _F32)], axis=0)
        above = _dot(earlier_row, row_tot.astype(_BF16))
        count = jnp.sum(jnp.where(hit, 1, 0))
        slot = jnp.where(hit, start.astype(_F32) + in_row + above, slot)
        cnt_ref[ex] = count
        start = start + (count + EXPERT_BLOCK - 1) // EXPERT_BLOCK * EXPERT_BLOCK
    slot_v[...] = slot.astype(jnp.int32)
    to_smem = pltpu.make_async_copy(slot_v, slot_s, sem)
    to_smem.start()

    def clear(i, carry):
        tab_ref[i] = 0
        return carry
    lax.fori_loop(0, tab_ref.shape[0], clear, 0, unroll=8)
    to_smem.wait()

    def place(r, carry):
        for c in range(LANES):
            tab_ref[slot_s[r, c]] = r * LANES + c
        return carry
    lax.fori_loop(0, rows, place, 0)


def _moe_plan(eid, n_tok):
    n_assign = 2 * n_tok
    n_blocks = -(-(n_assign + N_EXPERTS * (EXPERT_BLOCK - 1)) // EXPERT_BLOCK)
    n_slots = n_blocks * EXPERT_BLOCK
    rows = n_assign // LANES
    assert n_assign % LANES == 0
    table, counts, slot = pl.pallas_call(
        _moe_plan_kernel,
        in_specs=[pl.BlockSpec(memory_space=pltpu.VMEM)],
        out_specs=[pl.BlockSpec(memory_space=pltpu.SMEM), pl.BlockSpec(memory_space=pltpu.SMEM),
                   pl.BlockSpec(memory_space=pltpu.VMEM)],
        out_shape=[jax.ShapeDtypeStruct((n_slots,), jnp.int32), jax.ShapeDtypeStruct((N_EXPERTS,), jnp.int32),
                   jax.ShapeDtypeStruct((rows, LANES), jnp.int32)],
        scratch_shapes=[pltpu.SMEM((rows, LANES), jnp.int32), pltpu.SemaphoreType.DMA],
        name="moe_plan",
    )(eid.reshape(rows, LANES))
    padded = (counts + EXPERT_BLOCK - 1) // EXPERT_BLOCK * EXPERT_BLOCK
    pad_end = jnp.cumsum(padded)
    pad_start = pad_end - padded
    block_start = jnp.arange(n_blocks, dtype=jnp.int32) * EXPERT_BLOCK
    block_expert = jnp.minimum(jnp.sum((pad_end[None, :] <= block_start[:, None]).astype(jnp.int32), axis=1),
                               N_EXPERTS - 1)
    n_valid = jnp.clip((pad_start + counts)[block_expert] - block_start, 0, EXPERT_BLOCK)
    every = jnp.arange(n_slots, dtype=jnp.int32)
    valid = every % EXPERT_BLOCK < jnp.repeat(n_valid, EXPERT_BLOCK)
    source = jnp.where(valid, table, every) % n_tok
    return block_expert, source, slot.reshape(n_assign)


def _combine(h_ref, y0_ref, y1_ref, gate_ref):
    gate = gate_ref[...]
    return h_ref[...] + (gate[:, 0:1] * y0_ref[...] + gate[:, 1:2] * y1_ref[...])


def _emit_head(r, h, outs, scr):
    tm = r.shape[0]
    if scr is not None:
        scr[h] = r
    for ref, dil in outs:
        if dil == 0:
            ref[:, h, :] = r
        elif dil == 1:
            ref[:, _chunk(h)] = r.astype(ref.dtype)
        else:
            for res in range(dil):
                col = res * D_MODEL + h * HEAD_DIM
                ref[:, col:col + HEAD_DIM] = scr[h, pl.ds(res, tm // dil, stride=dil), :].astype(ref.dtype)


def _rope(xh, cos, sin_signed):
    return xh * cos + pltpu.roll(xh, HEAD_DIM // 2, axis=1) * sin_signed


def _qkv_kernel(*refs, streams, per_seq, kept):
    (h_ref, y0_ref, y1_ref, gate_ref, gmix_ref, gkv_ref, wq_ref, wkv_ref, cos_ref, sin_ref) = refs[:10]
    outs = refs[10:]
    if streams:
        (h2_ref, q0_ref, q1_ref, q2_ref, k0_ref, k1_ref, k2_ref, v0_ref, v1_ref, v2_ref, kf_ref, vf_ref,
         scr_q1, scr_q2, scr_k, scr_v) = outs
        q_outs = [([(q0_ref, 1)], None), ([(q1_ref, DILATIONS[1])], scr_q1), ([(q2_ref, DILATIONS[2])], scr_q2)]
        k_outs = ([(k0_ref, 1), (k1_ref, DILATIONS[1]), (k2_ref, DILATIONS[2])], scr_k)
        v_outs = ([(v0_ref, 1), (v1_ref, DILATIONS[1]), (v2_ref, DILATIONS[2])], scr_v)
    else:
        h2_ref, q0_ref, q1_ref, q2_ref, kf_ref, vf_ref = outs
        q_outs = [([(q0_ref, 1)], None), ([(q1_ref, 1)], None), ([(q2_ref, 1)], None)]
        k_outs = ([(kf_ref, 0)], None)
        v_outs = ([(vf_ref, 0)], None)

    h2 = _combine(h_ref, y0_ref, y1_ref, gate_ref)
    h2_ref[...] = h2
    cos = cos_ref[...]
    sin = sin_ref[...]
    xn = _rms(h2, gmix_ref[...]).astype(_BF16)
    scale = HEAD_DIM ** -0.5
    for g in range(N_GROUPS):
        q = _dot(xn, wq_ref[:, g * D_MODEL:(g + 1) * D_MODEL])
        for h in range(N_HEADS):
            _emit_head(_rope(q[:, _chunk(h)], cos, sin) * scale, h, *q_outs[g])
    xkv = _rms(h2, gkv_ref[...]).astype(_BF16)
    k = _dot(xkv, wkv_ref[:, 0:D_MODEL])
    for h in range(N_HEADS):
        _emit_head(_rope(k[:, _chunk(h)], cos, sin), h, *k_outs)
    v = _dot(xkv, wkv_ref[:, D_MODEL:2 * D_MODEL])
    for h in range(N_HEADS):
        _emit_head(v[:, _chunk(h)], h, *v_outs)
    if streams:
        @pl.when(pl.program_id(0) % per_seq >= per_seq - kept)
        def _():
            for h in range(N_HEADS):
                kf_ref[:, h, :] = scr_k[h]
                vf_ref[:, h, :] = scr_v[h]


def _qkv(h, ypair, gate, gmix, gkv, wq, wkv, cos, sin, *, tm, streams, seq, keep):
    t, d = h.shape
    nt = t // tm
    per_seq = seq // tm
    kept = keep // tm
    assert seq % tm == 0 and keep % tm == 0

    def tail(i):
        return ((i // per_seq) * kept + jnp.maximum(i % per_seq - (per_seq - kept), 0), 0, 0)

    tail_f32 = (pl.BlockSpec((tm, N_HEADS, HEAD_DIM), tail),
                jax.ShapeDtypeStruct((t // seq * keep, N_HEADS, HEAD_DIM), _F32))
    row = lambda i: (i, 0)
    const = lambda i: (0, 0)
    nat = lambda dt: (pl.BlockSpec((tm, d), row), jax.ShapeDtypeStruct((t, d), dt))

    def stream(dil):
        return (pl.BlockSpec((tm // dil, dil * d), row), jax.ShapeDtypeStruct((t // dil, dil * d), _BF16))

    if streams:
        d1, d2 = DILATIONS[1], DILATIONS[2]
        outs = [nat(_F32), nat(_BF16), stream(d1), stream(d2), nat(_BF16), stream(d1), stream(d2),
                nat(_BF16), stream(d1), stream(d2), tail_f32, tail_f32]
        scratch = [pltpu.VMEM((N_HEADS, tm, HEAD_DIM), _F32)] * 4
    else:
        outs = [nat(_F32), nat(_BF16), nat(_BF16), nat(_BF16), tail_f32, tail_f32]
        scratch = []
    return pl.pallas_call(
        functools.partial(_qkv_kernel, streams=streams, per_seq=per_seq, kept=kept),
        grid=(nt,),
        in_specs=[
            pl.BlockSpec((tm, d), row),
            pl.BlockSpec((tm, d), row),
            pl.BlockSpec((tm, d), lambda i: (nt + i, 0)),
            pl.BlockSpec((tm, 2), row),
            pl.BlockSpec((1, d), const),
            pl.BlockSpec((1, d), const),
            pl.BlockSpec((d, 3 * d), const),
            pl.BlockSpec((d, 2 * d), const),
            pl.BlockSpec((tm, HEAD_DIM), row),
            pl.BlockSpec((tm, HEAD_DIM), row),
        ],
        out_specs=[o[0] for o in outs],
        out_shape=[o[1] for o in outs],
        scratch_shapes=scratch,
        compiler_params=pltpu.CompilerParams(
            dimension_semantics=("arbitrary",), vmem_limit_bytes=VMEM_LIMIT),
        name="qkv_proj_streams" if streams else "qkv_proj",
    )(h, ypair, ypair, gate, gmix, gkv, wq, wkv, cos, sin)


ATTN_TQ = 512


def _attn_prompt_kernel(q_ref, kp_ref, kc_ref, vp_ref, vc_ref, o_ref, st_ref, *, tq):
    i = pl.program_id(1)
    qi = lax.broadcasted_iota(jnp.int32, (SPAN, 2 * SPAN), 0)
    kj = lax.broadcasted_iota(jnp.int32, (SPAN, 2 * SPAN), 1)
    band = (kj >= qi) & (kj <= qi + SPAN)
    band0 = band & (kj >= jnp.where(i > 0, 0, SPAN))
    lane = lax.broadcasted_iota(jnp.int32, (SPAN, HEAD_DIM), 1)
    ones = jnp.ones((2 * SPAN, HEAD_DIM), _BF16)
    for a in range(tq // SPAN):
        rows = slice(a * SPAN, (a + 1) * SPAN)
        st = jnp.zeros((SPAN, HEAD_DIM), _F32)
        for h in range(N_HEADS):
            hs = _chunk(h)
            if a == 0:
                keys = jnp.concatenate([kp_ref[:, hs], kc_ref[rows, hs]], axis=0)
                vals = jnp.concatenate([vp_ref[:, hs], vc_ref[rows, hs]], axis=0)
                mask = band0
            else:
                both = slice((a - 1) * SPAN, (a + 1) * SPAN)
                keys, vals, mask = kc_ref[both, hs], vc_ref[both, hs], band
            s = jnp.where(mask, _dot_nt(q_ref[rows, hs], keys), NEG_BIG)
            m = jnp.max(s, axis=1, keepdims=True)
            p = jnp.exp(s - m).astype(_BF16)
            ol = _dot(p, jnp.concatenate([vals, ones], axis=1))
            l = ol[:, HEAD_DIM:]
            o_ref[rows, hs] = ol[:, :HEAD_DIM] / l
            st = jnp.where(lane == h, m + jnp.log(l), st)
        st_ref[rows, :] = st


def _attn_prompt(q, k, v, batch, dilation):
    rows, width = q.shape
    d = width // dilation
    per_batch = rows // batch
    tq = min(ATTN_TQ, per_batch)
    assert per_batch % tq == 0 and tq % SPAN == 0
    nq = per_batch // tq
    sub = tq // SPAN
    cur = lambda b, i, r: (b * nq + i, r)
    prev = lambda b, i, r: ((b * nq + i) * sub - jnp.where(i > 0, 1, 0), r)
    big = (tq, d)
    small = (SPAN, d)
    return pl.pallas_call(
        functools.partial(_attn_prompt_kernel, tq=tq),
        grid=(batch, nq, dilation),
        in_specs=[pl.BlockSpec(big, cur), pl.BlockSpec(small, prev), pl.BlockSpec(big, cur),
                  pl.BlockSpec(small, prev), pl.BlockSpec(big, cur)],
        out_specs=[pl.BlockSpec(big, cur), pl.BlockSpec((tq, HEAD_DIM), cur)],
        out_shape=[jax.ShapeDtypeStruct((rows, width), _F32),
                   jax.ShapeDtypeStruct((rows, dilation * HEAD_DIM), _F32)],
        compiler_params=pltpu.CompilerParams(
            dimension_semantics=("arbitrary", "arbitrary", "arbitrary"), vmem_limit_bytes=VMEM_LIMIT),
        name=f"attn_prompt_d{dilation}",
    )(q, k, k, v, v)


N_NEW = 8
CACHE_A_GROUPS = (PAST_LEN - WINDOWS[1]) // DILATIONS[2]
CACHE_A_ROWS = CACHE_A_GROUPS * N_NEW
CACHE_B_ROWS = WINDOWS[1]
CACHE_B_GROUPS = CACHE_B_ROWS // DILATIONS[2]
KEYS_REAL = CACHE_A_ROWS + CACHE_B_ROWS + N_NEW
KEYS_PAD = -(-KEYS_REAL // 128) * 128


def _sample_key_positions():
    pos = np.full((KEYS_PAD,), -1, np.int64)
    a = np.arange(CACHE_A_ROWS)
    pos[:CACHE_A_ROWS] = (a // N_NEW) * DILATIONS[2] + a % N_NEW
    pos[CACHE_A_ROWS:CACHE_A_ROWS + CACHE_B_ROWS] = PAST_LEN - CACHE_B_ROWS + np.arange(CACHE_B_ROWS)
    pos[CACHE_A_ROWS + CACHE_B_ROWS:KEYS_REAL] = PAST_LEN + np.arange(N_NEW)
    return pos


def _sample_bias():
    pos = _sample_key_positions()
    bias = np.full((N_GROUPS * N_NEW, KEYS_PAD), NEG_BIG, np.float32)
    for g in range(N_GROUPS):
        for n in range(N_NEW):
            delta = PAST_LEN + n - pos
            ok = (pos >= 0) & (delta >= 0) & (delta <= WINDOWS[g]) & (delta % DILATIONS[g] == 0)
            assert int(ok.sum()) == SPAN + 1
            bias[g * N_NEW + n, ok] = 0.0
    return bias


def _attn_sample_kernel(q0_ref, q1_ref, q2_ref, kn_ref, vn_ref, bias_ref, ck_hbm, cv_hbm,
                        o_ref, ka, kb, va, vb, kall, vall, sem, *, n_seq):
    b = pl.program_id(0)
    slot = b % 2

    def cache_copies(seq, s):
        cps = []
        for h in range(N_HEADS):
            for src, dst_a, dst_b in ((ck_hbm, ka, kb), (cv_hbm, va, vb)):
                cps.append(pltpu.make_async_copy(
                    src.at[seq, pl.ds(0, CACHE_A_GROUPS), pl.ds(0, N_NEW), h], dst_a.at[s, h], sem.at[s]))
                cps.append(pltpu.make_async_copy(
                    src.at[seq, pl.ds(CACHE_A_GROUPS, CACHE_B_GROUPS), :, h], dst_b.at[s, h], sem.at[s]))
        return cps

    @pl.when(b == 0)
    def _():
        kall[KEYS_REAL:KEYS_PAD, :] = jnp.zeros((KEYS_PAD - KEYS_REAL, D_MODEL), _BF16)
        vall[KEYS_REAL:KEYS_PAD, :] = jnp.zeros((KEYS_PAD - KEYS_REAL, D_MODEL), _BF16)
        for cp in cache_copies(0, 0):
            cp.start()

    @pl.when(b + 1 < n_seq)
    def _():
        for cp in cache_copies(b + 1, 1 - slot):
            cp.start()

    for cp in cache_copies(b, slot):
        cp.wait()

    nb0 = CACHE_A_ROWS + CACHE_B_ROWS
    for src_a, src_b, src_n, dst in ((ka, kb, kn_ref, kall), (va, vb, vn_ref, vall)):
        for h in range(N_HEADS):
            hs = _chunk(h)
            dst[0:CACHE_A_ROWS, hs] = src_a[slot, h].reshape(CACHE_A_ROWS, HEAD_DIM).astype(_BF16)
            dst[CACHE_A_ROWS:nb0, hs] = src_b[slot, h].reshape(CACHE_B_ROWS, HEAD_DIM).astype(_BF16)
            dst[nb0:KEYS_REAL, hs] = src_n[0, :, h, :].astype(_BF16)

    bias = bias_ref[...]
    for h in range(N_HEADS):
        hs = _chunk(h)
        qh = jnp.concatenate([q0_ref[0, :, hs].astype(_F32), q1_ref[0, :, hs].astype(_F32),
                              q2_ref[0, :, hs].astype(_F32)], axis=0).astype(_BF16)
        s = _dot_nt(qh, kall[:, hs]) + bias
        m = jnp.max(s, axis=1, keepdims=True)
        m8 = jnp.maximum(jnp.maximum(m[0:N_NEW], m[N_NEW:2 * N_NEW]), m[2 * N_NEW:3 * N_NEW])
        p = jnp.exp(s - jnp.concatenate([m8, m8, m8], axis=0))
        p8 = p[0:N_NEW] + p[N_NEW:2 * N_NEW] + p[2 * N_NEW:3 * N_NEW]
        l8 = jnp.sum(p8, axis=1, keepdims=True)
        o_ref[0, :, hs] = _dot(p8.astype(_BF16), vall[:, hs]) / l8


def _attn_sample(q0, q1, q2, cache_k, cache_v, k_new, v_new):
    nb, n_new, d = q0.shape
    past = cache_k.shape[1]
    assert past == PAST_LEN and n_new == N_NEW and d == D_MODEL
    assert cache_k.shape[2:] == (N_HEADS, HEAD_DIM)
    ck = cache_k.reshape(nb, past // DILATIONS[2], DILATIONS[2], N_HEADS, HEAD_DIM)
    cv = cache_v.reshape(nb, past // DILATIONS[2], DILATIONS[2], N_HEADS, HEAD_DIM)
    bias = jnp.asarray(_sample_bias())
    new = pl.BlockSpec((1, n_new, d), lambda b: (b, 0, 0))
    new_kv = pl.BlockSpec((1, n_new, N_HEADS, HEAD_DIM), lambda b: (b, 0, 0, 0))
    part_a = (2, N_HEADS, CACHE_A_GROUPS, N_NEW, HEAD_DIM)
    part_b = (2, N_HEADS, CACHE_B_GROUPS, DILATIONS[2], HEAD_DIM)
    return pl.pallas_call(
        functools.partial(_attn_sample_kernel, n_seq=nb),
        grid=(nb,),
        in_specs=[new, new, new, new_kv, new_kv,
                  pl.BlockSpec((N_GROUPS * N_NEW, KEYS_PAD), lambda b: (0, 0)),
                  pl.BlockSpec(memory_space=pl.ANY), pl.BlockSpec(memory_space=pl.ANY)],
        out_specs=new,
        out_shape=jax.ShapeDtypeStruct((nb, n_new, d), _F32),
        scratch_shapes=[pltpu.VMEM(part_a, _F32), pltpu.VMEM(part_b, _F32),
                        pltpu.VMEM(part_a, _F32), pltpu.VMEM(part_b, _F32),
                        pltpu.VMEM((KEYS_PAD, d), _BF16), pltpu.VMEM((KEYS_PAD, d), _BF16),
                        pltpu.SemaphoreType.DMA((2,))],
        compiler_params=pltpu.CompilerParams(
            dimension_semantics=("arbitrary",), vmem_limit_bytes=VMEM_LIMIT),
        name="attn_sample",
    )(q0, q1, q2, k_new, v_new, bias, ck, cv)


def _attn_out_kernel(*refs, n_groups, tm):
    if n_groups == 1:
        (o_ref, h_ref, wo_ref, gffn_ref, wr_ref, br_ref,
         h3_ref, xn2_ref, eid_ref, gate_ref) = refs
        o = o_ref[...].astype(_BF16)
    else:
        (o0_ref, o1_ref, o2_ref, s0_ref, s1_ref, s2_ref, h_ref, wo_ref, gffn_ref, wr_ref, br_ref,
         h3_ref, xn2_ref, eid_ref, gate_ref, obuf, scr_o1, scr_o2, scr_s1, scr_s2) = refs
        for o_ref, s_ref, scr_o, scr_s, dil in ((o1_ref, s1_ref, scr_o1, scr_s1, DILATIONS[1]),
                                                (o2_ref, s2_ref, scr_o2, scr_s2, DILATIONS[2])):
            for res in range(dil):
                dst = pl.ds(res, tm // dil, stride=dil)
                scr_s[dst, :] = s_ref[:, res * HEAD_DIM:(res + 1) * HEAD_DIM]
                for h in range(N_HEADS):
                    col = res * D_MODEL + h * HEAD_DIM
                    scr_o[h, dst, :] = o_ref[:, col:col + HEAD_DIM]
        sts = [s0_ref[...], scr_s1[...], scr_s2[...]]
        mx = jnp.maximum(jnp.maximum(sts[0], sts[1]), sts[2])
        es = [jnp.exp(s - mx) for s in sts]
        den = es[0] + es[1] + es[2]
        ws = [e / den for e in es]
        for h in range(N_HEADS):
            acc = ws[0][:, h:h + 1] * o0_ref[:, _chunk(h)]
            acc = acc + ws[1][:, h:h + 1] * scr_o1[h]
            acc = acc + ws[2][:, h:h + 1] * scr_o2[h]
            obuf[:, _chunk(h)] = acc.astype(_BF16)
        o = obuf[...]
    h3 = h_ref[...] + _dot(o, wo_ref[...])
    h3_ref[...] = h3
    xn2 = _rms(h3, gffn_ref[...])
    xn2_ref[...] = xn2
    _route(xn2, wr_ref, br_ref, eid_ref, gate_ref)


def _attn_out(os, sts, h, wo, gffn, wr, br, *, tm):
    t, d = h.shape
    n_groups = len(os)
    row = lambda i: (i, 0)
    const = lambda i: (0, 0)
    if n_groups == 1:
        in_specs = [pl.BlockSpec((tm, d), row)]
        scratch = []
        args = [os[0]]
    else:
        in_specs = [pl.BlockSpec((tm // dil, dil * d), row) for dil in DILATIONS]
        in_specs += [pl.BlockSpec((tm // dil, dil * HEAD_DIM), row) for dil in DILATIONS]
        scratch = [pltpu.VMEM((tm, d), _BF16),
                   pltpu.VMEM((N_HEADS, tm, HEAD_DIM), _F32), pltpu.VMEM((N_HEADS, tm, HEAD_DIM), _F32),
                   pltpu.VMEM((tm, HEAD_DIM), _F32), pltpu.VMEM((tm, HEAD_DIM), _F32)]
        args = list(os) + list(sts)
    in_specs += [pl.BlockSpec((tm, d), row), pl.BlockSpec((d, d), const), pl.BlockSpec((1, d), const),
                 pl.BlockSpec((ROUTER_ROWS, d), const), pl.BlockSpec((ROUTER_ROWS, 1), const)]
    return pl.pallas_call(
        functools.partial(_attn_out_kernel, n_groups=n_groups, tm=tm),
        grid=(t // tm,),
        in_specs=in_specs,
        out_specs=[pl.BlockSpec((tm, d), row), pl.BlockSpec((tm, d), row),
                   pl.BlockSpec((2, tm), lambda i: (0, i)), pl.BlockSpec((2, tm), lambda i: (0, i))],
        out_shape=[jax.ShapeDtypeStruct((t, d), _F32), jax.ShapeDtypeStruct((t, d), _F32),
                   jax.ShapeDtypeStruct((2, t), jnp.int32), jax.ShapeDtypeStruct((2, t), _F32)],
        scratch_shapes=scratch,
        compiler_params=pltpu.CompilerParams(
            dimension_semantics=("arbitrary",), vmem_limit_bytes=VMEM_LIMIT),
        name=f"attn_out_g{n_groups}",
    )(*args, h, wo, gffn, wr, br)


def _final_kernel(h_ref, y0_ref, y1_ref, gate_ref, g_ref, out_ref):
    out_ref[...] = _rms(_combine(h_ref, y0_ref, y1_ref, gate_ref), g_ref[...])


def _final(h, ypair, gate, g, *, tm):
    t, d = h.shape
    nt = t // tm
    row = lambda i: (i, 0)
    return pl.pallas_call(
        _final_kernel,
        grid=(nt,),
        in_specs=[pl.BlockSpec((tm, d), row),
                  pl.BlockSpec((tm, d), row),
                  pl.BlockSpec((tm, d), lambda i: (nt + i, 0)),
                  pl.BlockSpec((tm, 2), row),
                  pl.BlockSpec((1, d), lambda i: (0, 0))],
        out_specs=pl.BlockSpec((tm, d), row),
        out_shape=jax.ShapeDtypeStruct((t, d), _F32),
        compiler_params=pltpu.CompilerParams(
            dimension_semantics=("arbitrary",), vmem_limit_bytes=VMEM_LIMIT),
        name="final_norm",
    )(h, ypair, ypair, gate, g)


def _rope_tables(pos):
    half = HEAD_DIM // 2
    inv_freq = jnp.power(jnp.float32(ROPE_THETA), -jnp.arange(half, dtype=jnp.float32) / half)
    ang = pos.astype(jnp.float32)[:, None] * inv_freq[None, :]
    cos = jnp.cos(ang)
    sin = jnp.sin(ang)
    return jnp.concatenate([cos, cos], axis=-1), jnp.concatenate([-sin, sin], axis=-1)


def _router_params(wg, bg, we, be):
    wr = jnp.zeros((ROUTER_ROWS, D_MODEL), _F32)
    wr = wr.at[0:MOE_GROUPS].set(wg.T).at[EXPERT_ROW0:EXPERT_ROW0 + N_EXPERTS].set(we.T)
    br = jnp.zeros((ROUTER_ROWS, 1), _F32)
    br = br.at[0:MOE_GROUPS, 0].set(bg).at[EXPERT_ROW0:EXPERT_ROW0 + N_EXPERTS, 0].set(be)
    return wr, br


def _moe(xn2, eid, experts, layer):
    block_expert, source, slot = _moe_plan(eid, xn2.shape[0])
    ys = _moe_ffn(block_expert, _row_gather(xn2, source), *experts, layer)
    return _row_gather(ys, slot)


def _forward(x3, prev3, pos_rows, attend, p, *, shift, tm, streams, keep):
    h1, xn2, eid, gate, state = _conv_layer(
        x3, prev3, p['gmix'][0], p['win'], p['ck'], p['wout'], p['gffn'][0], *p['router'][0],
        shift=shift, tm=tm)
    t = x3.shape[0] * x3.shape[1]
    ypair = _moe(xn2, eid, p['experts'], 0)
    cos, sin = _rope_tables(pos_rows)
    h2, *qkv, kf, vf = _qkv(h1.reshape(t, D_MODEL), ypair, gate.T, p['gmix'][1], p['gkv'],
                            p['wq'], p['wkv'], cos, sin, tm=256, streams=streams,
                            seq=x3.shape[1], keep=keep)
    os, sts = attend(qkv, kf, vf)
    h3, xn2, eid, gate = _attn_out(os, sts, h2, p['wo'], p['gffn'][1], *p['router'][1], tm=256)
    ypair = _moe(xn2, eid, p['experts'], 1)
    y = _final(h3, ypair, gate.T, p['gfinal'], tm=512)
    return y, state, kf, vf


def kernel(x_prompt, x_sample, cache_k, cache_v, state_conv, norm_mix, norm_ffn, norm_kv, norm_final,
           conv_w_in, conv_kernel, conv_w_out, attn_w_q, attn_w_kv, attn_w_o, router_group_w,
           router_group_b, router_expert_w, router_expert_b, expert_w1, expert_w3, expert_w2):
    b_p, s_p, d = x_prompt.shape
    b_s, n_new, _ = x_sample.shape
    assert d == D_MODEL and n_new == N_NEW and s_p % (DILATIONS[2] * SPAN) == 0
    assert norm_mix.shape[0] == 2 and conv_w_in.shape[0] == 1 and attn_w_q.shape[0] == 1
    assert cache_k.shape[1] == PAST_LEN

    p = {
        'gmix': [norm_mix[l].reshape(1, d) for l in range(2)],
        'gffn': [norm_ffn[l].reshape(1, d) for l in range(2)],
        'gkv': norm_kv.reshape(1, d),
        'gfinal': norm_final.reshape(1, d),
        'win': conv_w_in[0].astype(_BF16),
        'ck': conv_kernel[0],
        'wout': conv_w_out[0].astype(_BF16),
        'wq': attn_w_q[0].astype(_BF16),
        'wkv': attn_w_kv.astype(_BF16),
        'wo': attn_w_o[0].astype(_BF16),
        'router': [_router_params(router_group_w[l], router_group_b[l], router_expert_w[l], router_expert_b[l])
                   for l in range(2)],
        'experts': (expert_w1, expert_w3, expert_w2),
    }

    def attend_prompt(qkv, kf, vf):
        q0, q1, q2, k0, k1, k2, v0, v1, v2 = qkv
        os, sts = [], []
        for q, k, v, dil in ((q0, k0, v0, DILATIONS[0]), (q1, k1, v1, DILATIONS[1]), (q2, k2, v2, DILATIONS[2])):
            o, st = _attn_prompt(q, k, v, b_p, dil)
            os.append(o)
            sts.append(st)
        return os, sts

    pos_p = jnp.tile(jnp.arange(s_p, dtype=jnp.int32), b_p)
    zero_state = jnp.zeros((b_p, CONV_WIDTH - 1, d), x_prompt.dtype)
    keep = min(max(WINDOWS), s_p)
    y_p, st_p, kf_p, vf_p = _forward(x_prompt, zero_state, pos_p, attend_prompt, p, shift=1, tm=512,
                                     streams=True, keep=keep)
    y_prompt = y_p.reshape(b_p, s_p, d)
    k_p = kf_p.reshape(b_p, keep, N_HEADS, HEAD_DIM)
    v_p = vf_p.reshape(b_p, keep, N_HEADS, HEAD_DIM)
    conv_p = st_p[None]

    halves = 2
    bh = b_s // halves

    def to_rows(a):
        w = a.shape[-1]
        return a.reshape(halves, bh, n_new, w).transpose(0, 2, 1, 3).reshape(halves * n_new * bh, w)

    def to_batch(a):
        w = a.shape[1:]
        return jnp.swapaxes(a.reshape(halves, n_new, bh, *w), 1, 2).reshape(b_s, n_new, *w)

    def attend_sample(qkv, kf, vf):
        q0, q1, q2 = qkv
        o = _attn_sample(to_batch(q0), to_batch(q1), to_batch(q2), cache_k, cache_v,
                         to_batch(kf), to_batch(vf))
        return [to_rows(o)], None

    x_s = to_rows(x_sample).reshape(halves, n_new * bh, d)
    prev_s = state_conv[0].reshape(halves, bh, CONV_WIDTH - 1, d).transpose(0, 2, 1, 3).reshape(
        halves, (CONV_WIDTH - 1) * bh, d)
    pos_s = jnp.tile(jnp.repeat(PAST_LEN + jnp.arange(n_new, dtype=jnp.int32), bh), halves)
    y_s, st_s, kf_s, vf_s = _forward(x_s, prev_s, pos_s, attend_sample, p, shift=bh, tm=n_new * bh,
                                     streams=False, keep=n_new * bh)
    y_sample = to_batch(y_s)
    k_s = to_batch(kf_s)
    v_s = to_batch(vf_s)
    conv_s = st_s.reshape(halves, CONV_WIDTH - 1, bh, d).transpose(0, 2, 1, 3).reshape(
        b_s, CONV_WIDTH - 1, d)[None]

    return (y_prompt, y_sample, k_p, v_p, conv_p, k_s, v_s, conv_s)
```

```python
import functools

import numpy as np
import jax
import jax.numpy as jnp
from jax import lax
from jax.experimental import pallas as pl
from jax.experimental.pallas import tpu as pltpu
from jax.experimental.pallas import tpu_sc as plsc

D_MODEL = 1024
CONV_WIDTH = 3
WINDOWS = (128, 512, 2048)
DILATIONS = (1, 4, 16)
N_GROUPS = 3
N_HEADS = 8
HEAD_DIM = 128
SPAN = 128
ROPE_THETA = 10000.0
MOE_GROUPS = 4
EXPERTS_PER_GROUP = 4
N_EXPERTS = 16
D_EXPERT = 512
EXPERT_BLOCK = 256
RMS_EPS = 1e-6
NEG_BIG = -1e30
PAST_LEN = 2048

LANES = 128
ROUTER_ROWS = 32
EXPERT_ROW0 = 8
VMEM_LIMIT = 56 * 1024 * 1024

_F32 = jnp.float32
_BF16 = jnp.bfloat16


def _rms(x, g):
    ms = jnp.mean(x * x, axis=-1, keepdims=True)
    return (x * lax.rsqrt(ms + RMS_EPS)) * g


def _dot(a, b):
    return jnp.dot(a, b, preferred_element_type=_F32)


def _dot_nt(a, b, precision=None):
    return lax.dot_general(a, b, (((1,), (1,)), ((), ())), precision=precision,
                           preferred_element_type=_F32)


def _chunk(c):
    return slice(c * LANES, (c + 1) * LANES)


def _route(xn, wr_ref, br_ref, eid_ref, gate_ref):
    logits = _dot_nt(wr_ref[...], xn, precision=lax.Precision.HIGHEST) + br_ref[...]
    lg = logits[0:MOE_GROUPS]
    row = lax.broadcasted_iota(jnp.int32, lg.shape, 0).astype(_F32)
    eg = jnp.exp(lg - jnp.max(lg, axis=0, keepdims=True))
    pg = eg / jnp.sum(eg, axis=0, keepdims=True)
    gp = jnp.max(pg, axis=0, keepdims=True)
    gi = jnp.min(jnp.where(pg == gp, row, float(MOE_GROUPS)), axis=0, keepdims=True)
    le = jnp.zeros_like(lg)
    for g in range(MOE_GROUPS):
        r0 = EXPERT_ROW0 + g * EXPERTS_PER_GROUP
        le = le + jnp.where(gi == float(g), logits[r0:r0 + EXPERTS_PER_GROUP], 0.0)
    ee = jnp.exp(le - jnp.max(le, axis=0, keepdims=True))
    ev = ee / jnp.sum(ee, axis=0, keepdims=True)
    v1 = jnp.max(ev, axis=0, keepdims=True)
    i1 = jnp.min(jnp.where(ev == v1, row, float(EXPERTS_PER_GROUP)), axis=0, keepdims=True)
    ev2 = jnp.where(row == i1, -1.0, ev)
    v2 = jnp.max(ev2, axis=0, keepdims=True)
    i2 = jnp.min(jnp.where(ev2 == v2, row, float(EXPERTS_PER_GROUP)), axis=0, keepdims=True)
    den = v1 + v2
    ids = jnp.concatenate([gi * EXPERTS_PER_GROUP + i1, gi * EXPERTS_PER_GROUP + i2], axis=0)
    eid_ref[...] = ids.astype(jnp.int32)
    gate_ref[...] = jnp.concatenate([gp * v1 / den, gp * v2 / den], axis=0)


def _conv_layer_kernel(x_ref, prev_ref, gmix_ref, win_ref, ck_ref, wout_ref, gffn_ref, wr_ref, br_ref,
                       h_ref, xn2_ref, eid_ref, gate_ref, st_ref, ubuf, *, shift, tm):
    i = pl.program_id(1)
    halo = ubuf.shape[0] - tm
    keep = (CONV_WIDTH - 1) * shift

    @pl.when(i == 0)
    def _():
        ubuf[halo - keep:halo, :] = prev_ref[0]

    @pl.when(i > 0)
    def _():
        ubuf[halo - keep:halo, :] = ubuf[halo + tm - keep:halo + tm, :]

    x = x_ref[0]
    xn = _rms(x, gmix_ref[...]).astype(_BF16)
    c_gate = _dot(xn, win_ref[:, 0:D_MODEL])
    hid = _dot(xn, win_ref[:, 2 * D_MODEL:3 * D_MODEL])
    u = c_gate * hid
    ubuf[halo:halo + tm, :] = u
    u1 = ubuf[halo - shift:halo - shift + tm, :]
    u2 = ubuf[halo - 2 * shift:halo - 2 * shift + tm, :]
    conv = ck_ref[0:1, :] * u2 + ck_ref[1:2, :] * u1 + ck_ref[2:3, :] * u
    b_gate = _dot(xn, win_ref[:, D_MODEL:2 * D_MODEL])
    y = _dot((b_gate * conv).astype(_BF16), wout_ref[...])
    h = x + y
    h_ref[0] = h
    st_ref[0] = ubuf[halo + tm - keep:halo + tm, :]
    xn2 = _rms(h, gffn_ref[...])
    xn2_ref[...] = xn2
    _route(xn2, wr_ref, br_ref, eid_ref, gate_ref)


def _conv_layer(x3, prev3, gmix, win, ck, wout, gffn, wr, br, *, shift, tm):
    nb, s, d = x3.shape
    nt = s // tm
    keep = (CONV_WIDTH - 1) * shift
    halo = -(-keep // 8) * 8
    t_all = nb * s
    const = lambda b, i: (0, 0)
    return pl.pallas_call(
        functools.partial(_conv_layer_kernel, shift=shift, tm=tm),
        grid=(nb, nt),
        in_specs=[
            pl.BlockSpec((1, tm, d), lambda b, i: (b, i, 0)),
            pl.BlockSpec((1, keep, d), lambda b, i: (b, 0, 0)),
            pl.BlockSpec((1, d), const),
            pl.BlockSpec((d, 3 * d), const),
            pl.BlockSpec((CONV_WIDTH, d), const),
            pl.BlockSpec((d, d), const),
            pl.BlockSpec((1, d), const),
            pl.BlockSpec((ROUTER_ROWS, d), const),
            pl.BlockSpec((ROUTER_ROWS, 1), const),
        ],
        out_specs=[
            pl.BlockSpec((1, tm, d), lambda b, i: (b, i, 0)),
            pl.BlockSpec((tm, d), lambda b, i: (b * nt + i, 0)),
            pl.BlockSpec((2, tm), lambda b, i: (0, b * nt + i)),
            pl.BlockSpec((2, tm), lambda b, i: (0, b * nt + i)),
            pl.BlockSpec((1, keep, d), lambda b, i: (b, 0, 0)),
        ],
        out_shape=[
            jax.ShapeDtypeStruct((nb, s, d), _F32),
            jax.ShapeDtypeStruct((t_all, d), _F32),
            jax.ShapeDtypeStruct((2, t_all), jnp.int32),
            jax.ShapeDtypeStruct((2, t_all), _F32),
            jax.ShapeDtypeStruct((nb, keep, d), _F32),
        ],
        scratch_shapes=[pltpu.VMEM((halo + tm, d), _F32)],
        compiler_params=pltpu.CompilerParams(
            dimension_semantics=("arbitrary", "arbitrary"), vmem_limit_bytes=VMEM_LIMIT),
        name="conv_layer",
    )(x3, prev3, gmix, win, ck, wout, gffn, wr, br)


SC_WINDOW = 32


def _row_gather(src, idx):
    n_out = idx.shape[0]
    mesh = plsc.VectorSubcoreMesh(core_axis_name="c", subcore_axis_name="s")
    n_cores = mesh.num_cores
    per_worker = n_out // (n_cores * mesh.num_subcores)
    assert per_worker * n_cores * mesh.num_subcores == n_out and per_worker % SC_WINDOW == 0

    @pl.kernel(out_type=jax.ShapeDtypeStruct((n_out,) + src.shape[1:], src.dtype), mesh=mesh,
               scratch_types=[pltpu.VMEM((per_worker,), jnp.int32),
                              pltpu.VMEM((SC_WINDOW,) + src.shape[1:], src.dtype),
                              pltpu.SemaphoreType.DMA])
    def gather(src_hbm, idx_hbm, out_hbm, idx_v, rows_v, sem):
        base = (lax.axis_index("s") * n_cores + lax.axis_index("c")) * per_worker
        pltpu.sync_copy(idx_hbm.at[pl.ds(base, per_worker)], idx_v)

        @pl.loop(0, per_worker // SC_WINDOW)
        def _(q):
            off = q * SC_WINDOW
            pltpu.async_copy(src_hbm.at[idx_v.at[pl.ds(off, SC_WINDOW)]], rows_v, sem).wait()
            pltpu.sync_copy(rows_v, out_hbm.at[pl.ds(base + off, SC_WINDOW)])

    return gather(src, idx)


def _moe_ffn_kernel(be_ref, x_ref, w1_ref, w3_ref, w2_ref, y_ref, w1b, w3b, w2b):
    j = pl.program_id(0)

    @pl.when((j == 0) | (be_ref[j] != be_ref[jnp.maximum(j - 1, 0)]))
    def _():
        w1b[...] = w1_ref[0, 0].astype(_BF16)
        w3b[...] = w3_ref[0, 0].astype(_BF16)
        w2b[...] = w2_ref[0, 0].astype(_BF16)

    x = x_ref[...].astype(_BF16)
    a = _dot(x, w1b[...])
    b = _dot(x, w3b[...])
    y_ref[...] = _dot((jax.nn.silu(a) * b).astype(_BF16), w2b[...])


def _moe_ffn(block_expert, xs, w1, w3, w2, layer):
    n_slots, d = xs.shape
    n_blocks = block_expert.shape[0]
    wmap = lambda j, be: (layer, be[j], 0, 0)
    blk = pl.BlockSpec((EXPERT_BLOCK, d), lambda j, be: (j, 0))
    return pl.pallas_call(
        _moe_ffn_kernel,
        grid_spec=pltpu.PrefetchScalarGridSpec(
            num_scalar_prefetch=1,
            grid=(n_blocks,),
            in_specs=[blk,
                      pl.BlockSpec((1, 1, d, D_EXPERT), wmap),
                      pl.BlockSpec((1, 1, d, D_EXPERT), wmap),
                      pl.BlockSpec((1, 1, D_EXPERT, d), wmap)],
            out_specs=blk,
            scratch_shapes=[pltpu.VMEM((d, D_EXPERT), _BF16), pltpu.VMEM((d, D_EXPERT), _BF16),
                            pltpu.VMEM((D_EXPERT, d), _BF16)],
        ),
        out_shape=jax.ShapeDtypeStruct((n_slots, d), _F32),
        compiler_params=pltpu.CompilerParams(
            dimension_semantics=("arbitrary",), vmem_limit_bytes=VMEM_LIMIT),
        name="moe_ffn",
    )(block_expert, xs, w1, w3, w2)


def _moe_plan_kernel(eid_ref, tab_ref, cnt_ref, slot_v, slot_s, sem):
    rows = eid_ref.shape[0]
    depth = max(rows, LANES)
    eid = eid_ref[...]
    r_i = lax.broadcasted_iota(jnp.int32, (LANES, LANES), 0)
    c_i = lax.broadcasted_iota(jnp.int32, (LANES, LANES), 1)
    earlier_lane = jnp.where(r_i < c_i, 1.0, 0.0).astype(_BF16)
    all_lanes = jnp.ones((LANES, LANES), _BF16)
    q_i = lax.broadcasted_iota(jnp.int32, (rows, depth), 1)
    p_i = lax.broadcasted_iota(jnp.int32, (rows, depth), 0)
    earlier_row = jnp.where(q_i < p_i, 1.0, 0.0).astype(_BF16)
    slot = jnp.zeros((rows, LANES), _F32)
    start = jnp.int32(0)
    for ex in range(N_EXPERTS):
        hit = eid == ex
        hot = jnp.where(hit, 1.0, 0.0).astype(_BF16)
        in_row = _dot(hot, earlier_lane)
        row_tot = _dot(hot, all_lanes)
        if depth > rows:
            row_tot = jnp.concatenate([row_tot, jnp.zeros((depth - rows, LANES), _F32)], axis=0)
        above = _dot(earlier_row, row_tot.astype(_BF16))
        count = jnp.sum(jnp.where(hit, 1, 0))
        slot = jnp.where(hit, start.astype(_F32) + in_row + above, slot)
        cnt_ref[ex] = count
        start = start + (count + EXPERT_BLOCK - 1) // EXPERT_BLOCK * EXPERT_BLOCK
    slot_v[...] = slot.astype(jnp.int32)
    to_smem = pltpu.make_async_copy(slot_v, slot_s, sem)
    to_smem.start()

    def clear(i, carry):
        tab_ref[i] = 0
        return carry
    lax.fori_loop(0, tab_ref.shape[0], clear, 0, unroll=8)
    to_smem.wait()

    def place(r, carry):
        for c in range(LANES):
            tab_ref[slot_s[r, c]] = r * LANES + c
        return carry
    lax.fori_loop(0, rows, place, 0)


def _moe_plan(eid, n_tok):
    n_assign = 2 * n_tok
    n_blocks = -(-(n_assign + N_EXPERTS * (EXPERT_BLOCK - 1)) // EXPERT_BLOCK)
    n_slots = n_blocks * EXPERT_BLOCK
    rows = n_assign // LANES
    assert n_assign % LANES == 0
    table, counts, slot = pl.pallas_call(
        _moe_plan_kernel,
        in_specs=[pl.BlockSpec(memory_space=pltpu.VMEM)],
        out_specs=[pl.BlockSpec(memory_space=pltpu.SMEM), pl.BlockSpec(memory_space=pltpu.SMEM),
                   pl.BlockSpec(memory_space=pltpu.VMEM)],
        out_shape=[jax.ShapeDtypeStruct((n_slots,), jnp.int32), jax.ShapeDtypeStruct((N_EXPERTS,), jnp.int32),
                   jax.ShapeDtypeStruct((rows, LANES), jnp.int32)],
        scratch_shapes=[pltpu.SMEM((rows, LANES), jnp.int32), pltpu.SemaphoreType.DMA],
        name="moe_plan",
    )(eid.reshape(rows, LANES))
    padded = (counts + EXPERT_BLOCK - 1) // EXPERT_BLOCK * EXPERT_BLOCK
    pad_end = jnp.cumsum(padded)
    pad_start = pad_end - padded
    block_start = jnp.arange(n_blocks, dtype=jnp.int32) * EXPERT_BLOCK
    block_expert = jnp.minimum(jnp.sum((pad_end[None, :] <= block_start[:, None]).astype(jnp.int32), axis=1),
                               N_EXPERTS - 1)
    n_valid = jnp.clip((pad_start + counts)[block_expert] - block_start, 0, EXPERT_BLOCK)
    every = jnp.arange(n_slots, dtype=jnp.int32)
    valid = every % EXPERT_BLOCK < jnp.repeat(n_valid, EXPERT_BLOCK)
    source = jnp.where(valid, table, every) % n_tok
    return block_expert, source, slot.reshape(n_assign)


def _combine(h_ref, y0_ref, y1_ref, gate_ref):
    gate = gate_ref[...]
    return h_ref[...] + (gate[:, 0:1] * y0_ref[...] + gate[:, 1:2] * y1_ref[...])


def _emit_head(r, h, outs, scr):
    tm = r.shape[0]
    if scr is not None:
        scr[h] = r
    for ref, dil in outs:
        if dil == 0:
            ref[:, h, :] = r
        elif dil == 1:
            ref[:, _chunk(h)] = r.astype(ref.dtype)
        else:
            for res in range(dil):
                col = res * D_MODEL + h * HEAD_DIM
                ref[:, col:col + HEAD_DIM] = scr[h, pl.ds(res, tm // dil, stride=dil), :].astype(ref.dtype)


def _rope(xh, cos, sin_signed):
    return xh * cos + pltpu.roll(xh, HEAD_DIM // 2, axis=1) * sin_signed


def _qkv_kernel(*refs, streams, per_seq, kept):
    (h_ref, y0_ref, y1_ref, gate_ref, gmix_ref, gkv_ref, wq_ref, wkv_ref, cos_ref, sin_ref) = refs[:10]
    outs = refs[10:]
    if streams:
        (h2_ref, q0_ref, q1_ref, q2_ref, k0_ref, k1_ref, k2_ref, v0_ref, v1_ref, v2_ref, kf_ref, vf_ref,
         scr_q1, scr_q2, scr_k, scr_v) = outs
        q_outs = [([(q0_ref, 1)], None), ([(q1_ref, DILATIONS[1])], scr_q1), ([(q2_ref, DILATIONS[2])], scr_q2)]
        k_outs = ([(k0_ref, 1), (k1_ref, DILATIONS[1]), (k2_ref, DILATIONS[2])], scr_k)
        v_outs = ([(v0_ref, 1), (v1_ref, DILATIONS[1]), (v2_ref, DILATIONS[2])], scr_v)
    else:
        h2_ref, q0_ref, q1_ref, q2_ref, kf_ref, vf_ref = outs
        q_outs = [([(q0_ref, 1)], None), ([(q1_ref, 1)], None), ([(q2_ref, 1)], None)]
        k_outs = ([(kf_ref, 0)], None)
        v_outs = ([(vf_ref, 0)], None)

    h2 = _combine(h_ref, y0_ref, y1_ref, gate_ref)
    h2_ref[...] = h2
    cos = cos_ref[...]
    sin = sin_ref[...]
    xn = _rms(h2, gmix_ref[...]).astype(_BF16)
    scale = HEAD_DIM ** -0.5
    for g in range(N_GROUPS):
        q = _dot(xn, wq_ref[:, g * D_MODEL:(g + 1) * D_MODEL])
        for h in range(N_HEADS):
            _emit_head(_rope(q[:, _chunk(h)], cos, sin) * scale, h, *q_outs[g])
    xkv = _rms(h2, gkv_ref[...]).astype(_BF16)
    k = _dot(xkv, wkv_ref[:, 0:D_MODEL])
    for h in range(N_HEADS):
        _emit_head(_rope(k[:, _chunk(h)], cos, sin), h, *k_outs)
    v = _dot(xkv, wkv_ref[:, D_MODEL:2 * D_MODEL])
    for h in range(N_HEADS):
        _emit_head(v[:, _chunk(h)], h, *v_outs)
    if streams:
        @pl.when(pl.program_id(0) % per_seq >= per_seq - kept)
        def _():
            for h in range(N_HEADS):
                kf_ref[:, h, :] = scr_k[h]
                vf_ref[:, h, :] = scr_v[h]


def _qkv(h, ypair, gate, gmix, gkv, wq, wkv, cos, sin, *, tm, streams, seq, keep):
    t, d = h.shape
    nt = t // tm
    per_seq = seq // tm
    kept = keep // tm
    assert seq % tm == 0 and keep % tm == 0

    def tail(i):
        return ((i // per_seq) * kept + jnp.maximum(i % per_seq - (per_seq - kept), 0), 0, 0)

    tail_f32 = (pl.BlockSpec((tm, N_HEADS, HEAD_DIM), tail),
                jax.ShapeDtypeStruct((t // seq * keep, N_HEADS, HEAD_DIM), _F32))
    row = lambda i: (i, 0)
    const = lambda i: (0, 0)
    nat = lambda dt: (pl.BlockSpec((tm, d), row), jax.ShapeDtypeStruct((t, d), dt))

    def stream(dil):
        return (pl.BlockSpec((tm // dil, dil * d), row), jax.ShapeDtypeStruct((t // dil, dil * d), _BF16))

    if streams:
        d1, d2 = DILATIONS[1], DILATIONS[2]
        outs = [nat(_F32), nat(_BF16), stream(d1), stream(d2), nat(_BF16), stream(d1), stream(d2),
                nat(_BF16), stream(d1), stream(d2), tail_f32, tail_f32]
        scratch = [pltpu.VMEM((N_HEADS, tm, HEAD_DIM), _F32)] * 4
    else:
        outs = [nat(_F32), nat(_BF16), nat(_BF16), nat(_BF16), tail_f32, tail_f32]
        scratch = []
    return pl.pallas_call(
        functools.partial(_qkv_kernel, streams=streams, per_seq=per_seq, kept=kept),
        grid=(nt,),
        in_specs=[
            pl.BlockSpec((tm, d), row),
            pl.BlockSpec((tm, d), row),
            pl.BlockSpec((tm, d), lambda i: (nt + i, 0)),
            pl.BlockSpec((tm, 2), row),
            pl.BlockSpec((1, d), const),
            pl.BlockSpec((1, d), const),
            pl.BlockSpec((d, 3 * d), const),
            pl.BlockSpec((d, 2 * d), const),
            pl.BlockSpec((tm, HEAD_DIM), row),
            pl.BlockSpec((tm, HEAD_DIM), row),
        ],
        out_specs=[o[0] for o in outs],
        out_shape=[o[1] for o in outs],
        scratch_shapes=scratch,
        compiler_params=pltpu.CompilerParams(
            dimension_semantics=("arbitrary",), vmem_limit_bytes=VMEM_LIMIT),
        name="qkv_proj_streams" if streams else "qkv_proj",
    )(h, ypair, ypair, gate, gmix, gkv, wq, wkv, cos, sin)


ATTN_TQ = 512


def _attn_prompt_kernel(q_ref, kp_ref, kc_ref, vp_ref, vc_ref, o_ref, st_ref, *, tq):
    i = pl.program_id(1)
    qi = lax.broadcasted_iota(jnp.int32, (SPAN, 2 * SPAN), 0)
    kj = lax.broadcasted_iota(jnp.int32, (SPAN, 2 * SPAN), 1)
    band = (kj >= qi) & (kj <= qi + SPAN)
    band0 = band & (kj >= jnp.where(i > 0, 0, SPAN))
    lane = lax.broadcasted_iota(jnp.int32, (SPAN, HEAD_DIM), 1)
    ones = jnp.ones((2 * SPAN, HEAD_DIM), _BF16)
    for a in range(tq // SPAN):
        rows = slice(a * SPAN, (a + 1) * SPAN)
        st = jnp.zeros((SPAN, HEAD_DIM), _F32)
        for h in range(N_HEADS):
            hs = _chunk(h)
            if a == 0:
                keys = jnp.concatenate([kp_ref[:, hs], kc_ref[rows, hs]], axis=0)
                vals = jnp.concatenate([vp_ref[:, hs], vc_ref[rows, hs]], axis=0)
                mask = band0
            else:
                both = slice((a - 1) * SPAN, (a + 1) * SPAN)
                keys, vals, mask = kc_ref[both, hs], vc_ref[both, hs], band
            s = jnp.where(mask, _dot_nt(q_ref[rows, hs], keys), NEG_BIG)
            m = jnp.max(s, axis=1, keepdims=True)
            p = jnp.exp(s - m).astype(_BF16)
            ol = _dot(p, jnp.concatenate([vals, ones], axis=1))
            l = ol[:, HEAD_DIM:]
            o_ref[rows, hs] = ol[:, :HEAD_DIM] / l
            st = jnp.where(lane == h, m + jnp.log(l), st)
        st_ref[rows, :] = st


def _attn_prompt(q, k, v, batch, dilation):
    rows, width = q.shape
    d = width // dilation
    per_batch = rows // batch
    tq = min(ATTN_TQ, per_batch)
    assert per_batch % tq == 0 and tq % SPAN == 0
    nq = per_batch // tq
    sub = tq // SPAN
    cur = lambda b, i, r: (b * nq + i, r)
    prev = lambda b, i, r: ((b * nq + i) * sub - jnp.where(i > 0, 1, 0), r)
    big = (tq, d)
    small = (SPAN, d)
    return pl.pallas_call(
        functools.partial(_attn_prompt_kernel, tq=tq),
        grid=(batch, nq, dilation),
        in_specs=[pl.BlockSpec(big, cur), pl.BlockSpec(small, prev), pl.BlockSpec(big, cur),
                  pl.BlockSpec(small, prev), pl.BlockSpec(big, cur)],
        out_specs=[pl.BlockSpec(big, cur), pl.BlockSpec((tq, HEAD_DIM), cur)],
        out_shape=[jax.ShapeDtypeStruct((rows, width), _F32),
                   jax.ShapeDtypeStruct((rows, dilation * HEAD_DIM), _F32)],
        compiler_params=pltpu.CompilerParams(
            dimension_semantics=("arbitrary", "arbitrary", "arbitrary"), vmem_limit_bytes=VMEM_LIMIT),
        name=f"attn_prompt_d{dilation}",
    )(q, k, k, v, v)


N_NEW = 8
CACHE_A_GROUPS = (PAST_LEN - WINDOWS[1]) // DILATIONS[2]
CACHE_A_ROWS = CACHE_A_GROUPS * N_NEW
CACHE_B_ROWS = WINDOWS[1]
CACHE_B_GROUPS = CACHE_B_ROWS // DILATIONS[2]
KEYS_REAL = CACHE_A_ROWS + CACHE_B_ROWS + N_NEW
KEYS_PAD = -(-KEYS_REAL // 128) * 128


def _sample_key_positions():
    pos = np.full((KEYS_PAD,), -1, np.int64)
    a = np.arange(CACHE_A_ROWS)
    pos[:CACHE_A_ROWS] = (a // N_NEW) * DILATIONS[2] + a % N_NEW
    pos[CACHE_A_ROWS:CACHE_A_ROWS + CACHE_B_ROWS] = PAST_LEN - CACHE_B_ROWS + np.arange(CACHE_B_ROWS)
    pos[CACHE_A_ROWS + CACHE_B_ROWS:KEYS_REAL] = PAST_LEN + np.arange(N_NEW)
    return pos


def _sample_bias():
    pos = _sample_key_positions()
    bias = np.full((N_GROUPS * N_NEW, KEYS_PAD), NEG_BIG, np.float32)
    for g in range(N_GROUPS):
        for n in range(N_NEW):
            delta = PAST_LEN + n - pos
            ok = (pos >= 0) & (delta >= 0) & (delta <= WINDOWS[g]) & (delta % DILATIONS[g] == 0)
            assert int(ok.sum()) == SPAN + 1
            bias[g * N_NEW + n, ok] = 0.0
    return bias


def _attn_sample_kernel(q0_ref, q1_ref, q2_ref, kn_ref, vn_ref, bias_ref, ck_hbm, cv_hbm,
                        o_ref, ka, kb, va, vb, kall, vall, sem, *, n_seq):
    b = pl.program_id(0)
    slot = b % 2

    def cache_copies(seq, s):
        cps = []
        for h in range(N_HEADS):
            for src, dst_a, dst_b in ((ck_hbm, ka, kb), (cv_hbm, va, vb)):
                cps.append(pltpu.make_async_copy(
                    src.at[seq, pl.ds(0, CACHE_A_GROUPS), pl.ds(0, N_NEW), h], dst_a.at[s, h], sem.at[s]))
                cps.append(pltpu.make_async_copy(
                    src.at[seq, pl.ds(CACHE_A_GROUPS, CACHE_B_GROUPS), :, h], dst_b.at[s, h], sem.at[s]))
        return cps

    @pl.when(b == 0)
    def _():
        kall[KEYS_REAL:KEYS_PAD, :] = jnp.zeros((KEYS_PAD - KEYS_REAL, D_MODEL), _BF16)
        vall[KEYS_REAL:KEYS_PAD, :] = jnp.zeros((KEYS_PAD - KEYS_REAL, D_MODEL), _BF16)
        for cp in cache_copies(0, 0):
            cp.start()

    @pl.when(b + 1 < n_seq)
    def _():
        for cp in cache_copies(b + 1, 1 - slot):
            cp.start()

    for cp in cache_copies(b, slot):
        cp.wait()

    nb0 = CACHE_A_ROWS + CACHE_B_ROWS
    for src_a, src_b, src_n, dst in ((ka, kb, kn_ref, kall), (va, vb, vn_ref, vall)):
        for h in range(N_HEADS):
            hs = _chunk(h)
            dst[0:CACHE_A_ROWS, hs] = src_a[slot, h].reshape(CACHE_A_ROWS, HEAD_DIM).astype(_BF16)
            dst[CACHE_A_ROWS:nb0, hs] = src_b[slot, h].reshape(CACHE_B_ROWS, HEAD_DIM).astype(_BF16)
            dst[nb0:KEYS_REAL, hs] = src_n[0, :, h, :].astype(_BF16)

    bias = bias_ref[...]
    for h in range(N_HEADS):
        hs = _chunk(h)
        qh = jnp.concatenate([q0_ref[0, :, hs].astype(_F32), q1_ref[0, :, hs].astype(_F32),
                              q2_ref[0, :, hs].astype(_F32)], axis=0).astype(_BF16)
        s = _dot_nt(qh, kall[:, hs]) + bias
        m = jnp.max(s, axis=1, keepdims=True)
        m8 = jnp.maximum(jnp.maximum(m[0:N_NEW], m[N_NEW:2 * N_NEW]), m[2 * N_NEW:3 * N_NEW])
        p = jnp.exp(s - jnp.concatenate([m8, m8, m8], axis=0))
        p8 = p[0:N_NEW] + p[N_NEW:2 * N_NEW] + p[2 * N_NEW:3 * N_NEW]
        l8 = jnp.sum(p8, axis=1, keepdims=True)
        o_ref[0, :, hs] = _dot(p8.astype(_BF16), vall[:, hs]) / l8


def _attn_sample(q0, q1, q2, cache_k, cache_v, k_new, v_new):
    nb, n_new, d = q0.shape
    past = cache_k.shape[1]
    assert past == PAST_LEN and n_new == N_NEW and d == D_MODEL
    assert cache_k.shape[2:] == (N_HEADS, HEAD_DIM)
    ck = cache_k.reshape(nb, past // DILATIONS[2], DILATIONS[2], N_HEADS, HEAD_DIM)
    cv = cache_v.reshape(nb, past // DILATIONS[2], DILATIONS[2], N_HEADS, HEAD_DIM)
    bias = jnp.asarray(_sample_bias())
    new = pl.BlockSpec((1, n_new, d), lambda b: (b, 0, 0))
    new_kv = pl.BlockSpec((1, n_new, N_HEADS, HEAD_DIM), lambda b: (b, 0, 0, 0))
    part_a = (2, N_HEADS, CACHE_A_GROUPS, N_NEW, HEAD_DIM)
    part_b = (2, N_HEADS, CACHE_B_GROUPS, DILATIONS[2], HEAD_DIM)
    return pl.pallas_call(
        functools.partial(_attn_sample_kernel, n_seq=nb),
        grid=(nb,),
        in_specs=[new, new, new, new_kv, new_kv,
                  pl.BlockSpec((N_GROUPS * N_NEW, KEYS_PAD), lambda b: (0, 0)),
                  pl.BlockSpec(memory_space=pl.ANY), pl.BlockSpec(memory_space=pl.ANY)],
        out_specs=new,
        out_shape=jax.ShapeDtypeStruct((nb, n_new, d), _F32),
        scratch_shapes=[pltpu.VMEM(part_a, _F32), pltpu.VMEM(part_b, _F32),
                        pltpu.VMEM(part_a, _F32), pltpu.VMEM(part_b, _F32),
                        pltpu.VMEM((KEYS_PAD, d), _BF16), pltpu.VMEM((KEYS_PAD, d), _BF16),
                        pltpu.SemaphoreType.DMA((2,))],
        compiler_params=pltpu.CompilerParams(
            dimension_semantics=("arbitrary",), vmem_limit_bytes=VMEM_LIMIT),
        name="attn_sample",
    )(q0, q1, q2, k_new, v_new, bias, ck, cv)


def _attn_out_kernel(*refs, n_groups, tm):
    if n_groups == 1:
        (o_ref, h_ref, wo_ref, gffn_ref, wr_ref, br_ref,
         h3_ref, xn2_ref, eid_ref, gate_ref) = refs
        o = o_ref[...].astype(_BF16)
    else:
        (o0_ref, o1_ref, o2_ref, s0_ref, s1_ref, s2_ref, h_ref, wo_ref, gffn_ref, wr_ref, br_ref,
         h3_ref, xn2_ref, eid_ref, gate_ref, obuf, scr_o1, scr_o2, scr_s1, scr_s2) = refs
        for o_ref, s_ref, scr_o, scr_s, dil in ((o1_ref, s1_ref, scr_o1, scr_s1, DILATIONS[1]),
                                                (o2_ref, s2_ref, scr_o2, scr_s2, DILATIONS[2])):
            for res in range(dil):
                dst = pl.ds(res, tm // dil, stride=dil)
                scr_s[dst, :] = s_ref[:, res * HEAD_DIM:(res + 1) * HEAD_DIM]
                for h in range(N_HEADS):
                    col = res * D_MODEL + h * HEAD_DIM
                    scr_o[h, dst, :] = o_ref[:, col:col + HEAD_DIM]
        sts = [s0_ref[...], scr_s1[...], scr_s2[...]]
        mx = jnp.maximum(jnp.maximum(sts[0], sts[1]), sts[2])
        es = [jnp.exp(s - mx) for s in sts]
        den = es[0] + es[1] + es[2]
        ws = [e / den for e in es]
        for h in range(N_HEADS):
            acc = ws[0][:, h:h + 1] * o0_ref[:, _chunk(h)]
            acc = acc + ws[1][:, h:h + 1] * scr_o1[h]
            acc = acc + ws[2][:, h:h + 1] * scr_o2[h]
            obuf[:, _chunk(h)] = acc.astype(_BF16)
        o = obuf[...]
    h3 = h_ref[...] + _dot(o, wo_ref[...])
    h3_ref[...] = h3
    xn2 = _rms(h3, gffn_ref[...])
    xn2_ref[...] = xn2
    _route(xn2, wr_ref, br_ref, eid_ref, gate_ref)


def _attn_out(os, sts, h, wo, gffn, wr, br, *, tm):
    t, d = h.shape
    n_groups = len(os)
    row = lambda i: (i, 0)
    const = lambda i: (0, 0)
    if n_groups == 1:
        in_specs = [pl.BlockSpec((tm, d), row)]
        scratch = []
        args = [os[0]]
    else:
        in_specs = [pl.BlockSpec((tm // dil, dil * d), row) for dil in DILATIONS]
        in_specs += [pl.BlockSpec((tm // dil, dil * HEAD_DIM), row) for dil in DILATIONS]
        scratch = [pltpu.VMEM((tm, d), _BF16),
                   pltpu.VMEM((N_HEADS, tm, HEAD_DIM), _F32), pltpu.VMEM((N_HEADS, tm, HEAD_DIM), _F32),
                   pltpu.VMEM((tm, HEAD_DIM), _F32), pltpu.VMEM((tm, HEAD_DIM), _F32)]
        args = list(os) + list(sts)
    in_specs += [pl.BlockSpec((tm, d), row), pl.BlockSpec((d, d), const), pl.BlockSpec((1, d), const),
                 pl.BlockSpec((ROUTER_ROWS, d), const), pl.BlockSpec((ROUTER_ROWS, 1), const)]
    return pl.pallas_call(
        functools.partial(_attn_out_kernel, n_groups=n_groups, tm=tm),
        grid=(t // tm,),
        in_specs=in_specs,
        out_specs=[pl.BlockSpec((tm, d), row), pl.BlockSpec((tm, d), row),
                   pl.BlockSpec((2, tm), lambda i: (0, i)), pl.BlockSpec((2, tm), lambda i: (0, i))],
        out_shape=[jax.ShapeDtypeStruct((t, d), _F32), jax.ShapeDtypeStruct((t, d), _F32),
                   jax.ShapeDtypeStruct((2, t), jnp.int32), jax.ShapeDtypeStruct((2, t), _F32)],
        scratch_shapes=scratch,
        compiler_params=pltpu.CompilerParams(
            dimension_semantics=("arbitrary",), vmem_limit_bytes=VMEM_LIMIT),
        name=f"attn_out_g{n_groups}",
    )(*args, h, wo, gffn, wr, br)


def _final_kernel(h_ref, y0_ref, y1_ref, gate_ref, g_ref, out_ref):
    out_ref[...] = _rms(_combine(h_ref, y0_ref, y1_ref, gate_ref), g_ref[...])


def _final(h, ypair, gate, g, *, tm):
    t, d = h.shape
    nt = t // tm
    row = lambda i: (i, 0)
    return pl.pallas_call(
        _final_kernel,
        grid=(nt,),
        in_specs=[pl.BlockSpec((tm, d), row),
                  pl.BlockSpec((tm, d), row),
                  pl.BlockSpec((tm, d), lambda i: (nt + i, 0)),
                  pl.BlockSpec((tm, 2), row),
                  pl.BlockSpec((1, d), lambda i: (0, 0))],
        out_specs=pl.BlockSpec((tm, d), row),
        out_shape=jax.ShapeDtypeStruct((t, d), _F32),
        compiler_params=pltpu.CompilerParams(
            dimension_semantics=("arbitrary",), vmem_limit_bytes=VMEM_LIMIT),
        name="final_norm",
    )(h, ypair, ypair, gate, g)


def _rope_tables(pos):
    half = HEAD_DIM // 2
    inv_freq = jnp.power(jnp.float32(ROPE_THETA), -jnp.arange(half, dtype=jnp.float32) / half)
    ang = pos.astype(jnp.float32)[:, None] * inv_freq[None, :]
    cos = jnp.cos(ang)
    sin = jnp.sin(ang)
    return jnp.concatenate([cos, cos], axis=-1), jnp.concatenate([-sin, sin], axis=-1)


def _router_params(wg, bg, we, be):
    wr = jnp.zeros((ROUTER_ROWS, D_MODEL), _F32)
    wr = wr.at[0:MOE_GROUPS].set(wg.T).at[EXPERT_ROW0:EXPERT_ROW0 + N_EXPERTS].set(we.T)
    br = jnp.zeros((ROUTER_ROWS, 1), _F32)
    br = br.at[0:MOE_GROUPS, 0].set(bg).at[EXPERT_ROW0:EXPERT_ROW0 + N_EXPERTS, 0].set(be)
    return wr, br


def _moe(xn2, eid, experts, layer):
    block_expert, source, slot = _moe_plan(eid, xn2.shape[0])
    ys = _moe_ffn(block_expert, _row_gather(xn2, source), *experts, layer)
    return _row_gather(ys, slot)


def _forward(x3, prev3, pos_rows, attend, p, *, shift, tm, streams, keep):
    h1, xn2, eid, gate, state = _conv_layer(
        x3, prev3, p['gmix'][0], p['win'], p['ck'], p['wout'], p['gffn'][0], *p['router'][0],
        shift=shift, tm=tm)
    t = x3.shape[0] * x3.shape[1]
    ypair = _moe(xn2, eid, p['experts'], 0)
    cos, sin = _rope_tables(pos_rows)
    h2, *qkv, kf, vf = _qkv(h1.reshape(t, D_MODEL), ypair, gate.T, p['gmix'][1], p['gkv'],
                            p['wq'], p['wkv'], cos, sin, tm=256, streams=streams,
                            seq=x3.shape[1], keep=keep)
    os, sts = attend(qkv, kf, vf)
    h3, xn2, eid, gate = _attn_out(os, sts, h2, p['wo'], p['gffn'][1], *p['router'][1], tm=256)
    ypair = _moe(xn2, eid, p['experts'], 1)
    y = _final(h3, ypair, gate.T, p['gfinal'], tm=512)
    return y, state, kf, vf


def kernel(x_prompt, x_sample, cache_k, cache_v, state_conv, norm_mix, norm_ffn, norm_kv, norm_final,
           conv_w_in, conv_kernel, conv_w_out, attn_w_q, attn_w_kv, attn_w_o, router_group_w,
           router_group_b, router_expert_w, router_expert_b, expert_w1, expert_w3, expert_w2):
    b_p, s_p, d = x_prompt.shape
    b_s, n_new, _ = x_sample.shape
    assert d == D_MODEL and n_new == N_NEW and s_p % (DILATIONS[2] * SPAN) == 0
    assert norm_mix.shape[0] == 2 and conv_w_in.shape[0] == 1 and attn_w_q.shape[0] == 1
    assert cache_k.shape[1] == PAST_LEN

    p = {
        'gmix': [norm_mix[l].reshape(1, d) for l in range(2)],
        'gffn': [norm_ffn[l].reshape(1, d) for l in range(2)],
        'gkv': norm_kv.reshape(1, d),
        'gfinal': norm_final.reshape(1, d),
        'win': conv_w_in[0].astype(_BF16),
        'ck': conv_kernel[0],
        'wout': conv_w_out[0].astype(_BF16),
        'wq': attn_w_q[0].astype(_BF16),
        'wkv': attn_w_kv.astype(_BF16),
        'wo': attn_w_o[0].astype(_BF16),
        'router': [_router_params(router_group_w[l], router_group_b[l], router_expert_w[l], router_expert_b[l])
                   for l in range(2)],
        'experts': (expert_w1, expert_w3, expert_w2),
    }

    def attend_prompt(qkv, kf, vf):
        q0, q1, q2, k0, k1, k2, v0, v1, v2 = qkv
        os, sts = [], []
        for q, k, v, dil in ((q0, k0, v0, DILATIONS[0]), (q1, k1, v1, DILATIONS[1]), (q2, k2, v2, DILATIONS[2])):
            o, st = _attn_prompt(q, k, v, b_p, dil)
            os.append(o)
            sts.append(st)
        return os, sts

    pos_p = jnp.tile(jnp.arange(s_p, dtype=jnp.int32), b_p)
    zero_state = jnp.zeros((b_p, CONV_WIDTH - 1, d), x_prompt.dtype)
    keep = min(max(WINDOWS), s_p)
    y_p, st_p, kf_p, vf_p = _forward(x_prompt, zero_state, pos_p, attend_prompt, p, shift=1, tm=512,
                                     streams=True, keep=keep)
    y_prompt = y_p.reshape(b_p, s_p, d)
    k_p = kf_p.reshape(b_p, keep, N_HEADS, HEAD_DIM)
    v_p = vf_p.reshape(b_p, keep, N_HEADS, HEAD_DIM)
    conv_p = st_p[None]

    halves = 2
    bh = b_s // halves

    def to_rows(a):
        w = a.shape[-1]
        return a.reshape(halves, bh, n_new, w).transpose(0, 2, 1, 3).reshape(halves * n_new * bh, w)

    def to_batch(a):
        w = a.shape[1:]
        return jnp.swapaxes(a.reshape(halves, n_new, bh, *w), 1, 2).reshape(b_s, n_new, *w)

    def attend_sample(qkv, kf, vf):
        q0, q1, q2 = qkv
        o = _attn_sample(to_batch(q0), to_batch(q1), to_batch(q2), cache_k, cache_v,
                         to_batch(kf), to_batch(vf))
        return [to_rows(o)], None

    x_s = to_rows(x_sample).reshape(halves, n_new * bh, d)
    prev_s = state_conv[0].reshape(halves, bh, CONV_WIDTH - 1, d).transpose(0, 2, 1, 3).reshape(
        halves, (CONV_WIDTH - 1) * bh, d)
    pos_s = jnp.tile(jnp.repeat(PAST_LEN + jnp.arange(n_new, dtype=jnp.int32), bh), halves)
    y_s, st_s, kf_s, vf_s = _forward(x_s, prev_s, pos_s, attend_sample, p, shift=bh, tm=n_new * bh,
                                     streams=False, keep=n_new * bh)
    y_sample = to_batch(y_s)
    k_s = to_batch(kf_s)
    v_s = to_batch(vf_s)
    conv_s = st_s.reshape(halves, CONV_WIDTH - 1, bh, d).transpose(0, 2, 1, 3).reshape(
        b_s, CONV_WIDTH - 1, d)[None]

    return (y_prompt, y_sample, k_p, v_p, conv_p, k_s, v_s, conv_s)
```

```python
import functools

import numpy as np
import jax
import jax.numpy as jnp
from jax import lax
from jax.experimental import pallas as pl
from jax.experimental.pallas import tpu as pltpu
from jax.experimental.pallas import tpu_sc as plsc

D_MODEL = 1024
CONV_WIDTH = 3
WINDOWS = (128, 512, 2048)
DILATIONS = (1, 4, 16)
N_GROUPS = 3
N_HEADS = 8
HEAD_DIM = 128
SPAN = 128
ROPE_THETA = 10000.0
MOE_GROUPS = 4
EXPERTS_PER_GROUP = 4
N_EXPERTS = 16
D_EXPERT = 512
EXPERT_BLOCK = 256
RMS_EPS = 1e-6
NEG_BIG = -1e30
PAST_LEN = 2048

LANES = 128
ROUTER_ROWS = 32
EXPERT_ROW0 = 8
VMEM_LIMIT = 56 * 1024 * 1024

_F32 = jnp.float32
_BF16 = jnp.bfloat16


def _rms(x, g):
    ms = jnp.mean(x * x, axis=-1, keepdims=True)
    return (x * lax.rsqrt(ms + RMS_EPS)) * g


def _dot(a, b):
    return jnp.dot(a, b, preferred_element_type=_F32)


def _dot_nt(a, b, precision=None):
    return lax.dot_general(a, b, (((1,), (1,)), ((), ())), precision=precision,
                           preferred_element_type=_F32)


def _chunk(c):
    return slice(c * LANES, (c + 1) * LANES)


def _route(xn, wr_ref, br_ref, eid_ref, gate_ref):
    logits = _dot_nt(wr_ref[...], xn, precision=lax.Precision.HIGHEST) + br_ref[...]
    lg = logits[0:MOE_GROUPS]
    row = lax.broadcasted_iota(jnp.int32, lg.shape, 0).astype(_F32)
    eg = jnp.exp(lg - jnp.max(lg, axis=0, keepdims=True))
    pg = eg / jnp.sum(eg, axis=0, keepdims=True)
    gp = jnp.max(pg, axis=0, keepdims=True)
    gi = jnp.min(jnp.where(pg == gp, row, float(MOE_GROUPS)), axis=0, keepdims=True)
    le = jnp.zeros_like(lg)
    for g in range(MOE_GROUPS):
        r0 = EXPERT_ROW0 + g * EXPERTS_PER_GROUP
        le = le + jnp.where(gi == float(g), logits[r0:r0 + EXPERTS_PER_GROUP], 0.0)
    ee = jnp.exp(le - jnp.max(le, axis=0, keepdims=True))
    ev = ee / jnp.sum(ee, axis=0, keepdims=True)
    v1 = jnp.max(ev, axis=0, keepdims=True)
    i1 = jnp.min(jnp.where(ev == v1, row, float(EXPERTS_PER_GROUP)), axis=0, keepdims=True)
    ev2 = jnp.where(row == i1, -1.0, ev)
    v2 = jnp.max(ev2, axis=0, keepdims=True)
    i2 = jnp.min(jnp.where(ev2 == v2, row, float(EXPERTS_PER_GROUP)), axis=0, keepdims=True)
    den = v1 + v2
    ids = jnp.concatenate([gi * EXPERTS_PER_GROUP + i1, gi * EXPERTS_PER_GROUP + i2], axis=0)
    eid_ref[...] = ids.astype(jnp.int32)
    gate_ref[...] = jnp.concatenate([gp * v1 / den, gp * v2 / den], axis=0)


def _conv_layer_kernel(x_ref, prev_ref, gmix_ref, win_ref, ck_ref, wout_ref, gffn_ref, wr_ref, br_ref,
                       h_ref, xn2_ref, eid_ref, gate_ref, st_ref, ubuf, *, shift, tm):
    i = pl.program_id(1)
    halo = ubuf.shape[0] - tm
    keep = (CONV_WIDTH - 1) * shift

    @pl.when(i == 0)
    def _():
        ubuf[halo - keep:halo, :] = prev_ref[0]

    @pl.when(i > 0)
    def _():
        ubuf[halo - keep:halo, :] = ubuf[halo + tm - keep:halo + tm, :]

    x = x_ref[0]
    xn = _rms(x, gmix_ref[...]).astype(_BF16)
    c_gate = _dot(xn, win_ref[:, 0:D_MODEL])
    hid = _dot(xn, win_ref[:, 2 * D_MODEL:3 * D_MODEL])
    u = c_gate * hid
    ubuf[halo:halo + tm, :] = u
    u1 = ubuf[halo - shift:halo - shift + tm, :]
    u2 = ubuf[halo - 2 * shift:halo - 2 * shift + tm, :]
    conv = ck_ref[0:1, :] * u2 + ck_ref[1:2, :] * u1 + ck_ref[2:3, :] * u
    b_gate = _dot(xn, win_ref[:, D_MODEL:2 * D_MODEL])
    y = _dot((b_gate * conv).astype(_BF16), wout_ref[...])
    h = x + y
    h_ref[0] = h
    st_ref[0] = ubuf[halo + tm - keep:halo + tm, :]
    xn2 = _rms(h, gffn_ref[...])
    xn2_ref[...] = xn2
    _route(xn2, wr_ref, br_ref, eid_ref, gate_ref)


def _conv_layer(x3, prev3, gmix, win, ck, wout, gffn, wr, br, *, shift, tm):
    nb, s, d = x3.shape
    nt = s // tm
    keep = (CONV_WIDTH - 1) * shift
    halo = -(-keep // 8) * 8
    t_all = nb * s
    const = lambda b, i: (0, 0)
    return pl.pallas_call(
        functools.partial(_conv_layer_kernel, shift=shift, tm=tm),
        grid=(nb, nt),
        in_specs=[
            pl.BlockSpec((1, tm, d), lambda b, i: (b, i, 0)),
            pl.BlockSpec((1, keep, d), lambda b, i: (b, 0, 0)),
            pl.BlockSpec((1, d), const),
            pl.BlockSpec((d, 3 * d), const),
            pl.BlockSpec((CONV_WIDTH, d), const),
            pl.BlockSpec((d, d), const),
            pl.BlockSpec((1, d), const),
            pl.BlockSpec((ROUTER_ROWS, d), const),
            pl.BlockSpec((ROUTER_ROWS, 1), const),
        ],
        out_specs=[
            pl.BlockSpec((1, tm, d), lambda b, i: (b, i, 0)),
            pl.BlockSpec((tm, d), lambda b, i: (b * nt + i, 0)),
            pl.BlockSpec((2, tm), lambda b, i: (0, b * nt + i)),
            pl.BlockSpec((2, tm), lambda b, i: (0, b * nt + i)),
            pl.BlockSpec((1, keep, d), lambda b, i: (b, 0, 0)),
        ],
        out_shape=[
            jax.ShapeDtypeStruct((nb, s, d), _F32),
            jax.ShapeDtypeStruct((t_all, d), _F32),
            jax.ShapeDtypeStruct((2, t_all), jnp.int32),
            jax.ShapeDtypeStruct((2, t_all), _F32),
            jax.ShapeDtypeStruct((nb, keep, d), _F32),
        ],
        scratch_shapes=[pltpu.VMEM((halo + tm, d), _F32)],
        compiler_params=pltpu.CompilerParams(
            dimension_semantics=("arbitrary", "arbitrary"), vmem_limit_bytes=VMEM_LIMIT),
        name="conv_layer",
    )(x3, prev3, gmix, win, ck, wout, gffn, wr, br)


SC_WINDOW = 32


def _row_gather(src, idx):
    n_out = idx.shape[0]
    mesh = plsc.VectorSubcoreMesh(core_axis_name="c", subcore_axis_name="s")
    n_cores = mesh.num_cores
    per_worker = n_out // (n_cores * mesh.num_subcores)
    assert per_worker * n_cores * mesh.num_subcores == n_out and per_worker % SC_WINDOW == 0

    @pl.kernel(out_type=jax.ShapeDtypeStruct((n_out,) + src.shape[1:], src.dtype), mesh=mesh,
               scratch_types=[pltpu.VMEM((per_worker,), jnp.int32),
                              pltpu.VMEM((SC_WINDOW,) + src.shape[1:], src.dtype),
                              pltpu.SemaphoreType.DMA])
    def gather(src_hbm, idx_hbm, out_hbm, idx_v, rows_v, sem):
        base = (lax.axis_index("s") * n_cores + lax.axis_index("c")) * per_worker
        pltpu.sync_copy(idx_hbm.at[pl.ds(base, per_worker)], idx_v)

        @pl.loop(0, per_worker // SC_WINDOW)
        def _(q):
            off = q * SC_WINDOW
            pltpu.async_copy(src_hbm.at[idx_v.at[pl.ds(off, SC_WINDOW)]], rows_v, sem).wait()
            pltpu.sync_copy(rows_v, out_hbm.at[pl.ds(base + off, SC_WINDOW)])

    return gather(src, idx)


def _moe_ffn_kernel(be_ref, x_ref, w1_ref, w3_ref, w2_ref, y_ref, w1b, w3b, w2b):
    j = pl.program_id(0)

    @pl.when((j == 0) | (be_ref[j] != be_ref[jnp.maximum(j - 1, 0)]))
    def _():
        w1b[...] = w1_ref[0, 0].astype(_BF16)
        w3b[...] = w3_ref[0, 0].astype(_BF16)
        w2b[...] = w2_ref[0, 0].astype(_BF16)

    x = x_ref[...].astype(_BF16)
    a = _dot(x, w1b[...])
    b = _dot(x, w3b[...])
    y_ref[...] = _dot((jax.nn.silu(a) * b).astype(_BF16), w2b[...])


def _moe_ffn(block_expert, xs, w1, w3, w2, layer):
    n_slots, d = xs.shape
    n_blocks = block_expert.shape[0]
    wmap = lambda j, be: (layer, be[j], 0, 0)
    blk = pl.BlockSpec((EXPERT_BLOCK, d), lambda j, be: (j, 0))
    return pl.pallas_call(
        _moe_ffn_kernel,
        grid_spec=pltpu.PrefetchScalarGridSpec(
            num_scalar_prefetch=1,
            grid=(n_blocks,),
            in_specs=[blk,
                      pl.BlockSpec((1, 1, d, D_EXPERT), wmap),
                      pl.BlockSpec((1, 1, d, D_EXPERT), wmap),
                      pl.BlockSpec((1, 1, D_EXPERT, d), wmap)],
            out_specs=blk,
            scratch_shapes=[pltpu.VMEM((d, D_EXPERT), _BF16), pltpu.VMEM((d, D_EXPERT), _BF16),
                            pltpu.VMEM((D_EXPERT, d), _BF16)],
        ),
        out_shape=jax.ShapeDtypeStruct((n_slots, d), _F32),
        compiler_params=pltpu.CompilerParams(
            dimension_semantics=("arbitrary",), vmem_limit_bytes=VMEM_LIMIT),
        name="moe_ffn",
    )(block_expert, xs, w1, w3, w2)


def _moe_plan_kernel(eid_ref, tab_ref, cnt_ref, slot_v, slot_s, sem):
    rows = eid_ref.shape[0]
    depth = max(rows, LANES)
    eid = eid_ref[...]
    r_i = lax.broadcasted_iota(jnp.int32, (LANES, LANES), 0)
    c_i = lax.broadcasted_iota(jnp.int32, (LANES, LANES), 1)
    earlier_lane = jnp.where(r_i < c_i, 1.0, 0.0).astype(_BF16)
    all_lanes = jnp.ones((LANES, LANES), _BF16)
    q_i = lax.broadcasted_iota(jnp.int32, (rows, depth), 1)
    p_i = lax.broadcasted_iota(jnp.int32, (rows, depth), 0)
    earlier_row = jnp.where(q_i < p_i, 1.0, 0.0).astype(_BF16)
    slot = jnp.zeros((rows, LANES), _F32)
    start = jnp.int32(0)
    for ex in range(N_EXPERTS):
        hit = eid == ex
        hot = jnp.where(hit, 1.0, 0.0).astype(_BF16)
        in_row = _dot(hot, earlier_lane)
        row_tot = _dot(hot, all_lanes)
        if depth > rows:
            row_tot = jnp.concatenate([row_tot, jnp.zeros((depth - rows, LANES), _F32)], axis=0)
        above = _dot(earlier_row, row_tot.astype(_BF16))
        count = jnp.sum(jnp.where(hit, 1, 0))
        slot = jnp.where(hit, start.astype(_F32) + in_row + above, slot)
        cnt_ref[ex] = count
        start = start + (count + EXPERT_BLOCK - 1) // EXPERT_BLOCK * EXPERT_BLOCK
    slot_v[...] = slot.astype(jnp.int32)
    to_smem = pltpu.make_async_copy(slot_v, slot_s, sem)
    to_smem.start()

    def clear(i, carry):
        tab_ref[i] = 0
        return carry
    lax.fori_loop(0, tab_ref.shape[0], clear, 0, unroll=8)
    to_smem.wait()

    def place(r, carry):
        for c in range(LANES):
            tab_ref[slot_s[r, c]] = r * LANES + c
        return carry
    lax.fori_loop(0, rows, place, 0)


def _moe_plan(eid, n_tok):
    n_assign = 2 * n_tok
    n_blocks = -(-(n_assign + N_EXPERTS * (EXPERT_BLOCK - 1)) // EXPERT_BLOCK)
    n_slots = n_blocks * EXPERT_BLOCK
    rows = n_assign // LANES
    assert n_assign % LANES == 0
    table, counts, slot = pl.pallas_call(
        _moe_plan_kernel,
        in_specs=[pl.BlockSpec(memory_space=pltpu.VMEM)],
        out_specs=[pl.BlockSpec(memory_space=pltpu.SMEM), pl.BlockSpec(memory_space=pltpu.SMEM),
                   pl.BlockSpec(memory_space=pltpu.VMEM)],
        out_shape=[jax.ShapeDtypeStruct((n_slots,), jnp.int32), jax.ShapeDtypeStruct((N_EXPERTS,), jnp.int32),
                   jax.ShapeDtypeStruct((rows, LANES), jnp.int32)],
        scratch_shapes=[pltpu.SMEM((rows, LANES), jnp.int32), pltpu.SemaphoreType.DMA],
        name="moe_plan",
    )(eid.reshape(rows, LANES))
    padded = (counts + EXPERT_BLOCK - 1) // EXPERT_BLOCK * EXPERT_BLOCK
    pad_end = jnp.cumsum(padded)
    pad_start = pad_end - padded
    block_start = jnp.arange(n_blocks, dtype=jnp.int32) * EXPERT_BLOCK
    block_expert = jnp.minimum(jnp.sum((pad_end[None, :] <= block_start[:, None]).astype(jnp.int32), axis=1),
                               N_EXPERTS - 1)
    n_valid = jnp.clip((pad_start + counts)[block_expert] - block_start, 0, EXPERT_BLOCK)
    every = jnp.arange(n_slots, dtype=jnp.int32)
    valid = every % EXPERT_BLOCK < jnp.repeat(n_valid, EXPERT_BLOCK)
    source = jnp.where(valid, table, every) % n_tok
    return block_expert, source, slot.reshape(n_assign)


def _combine(h_ref, y0_ref, y1_ref, gate_ref):
    gate = gate_ref[...]
    return h_ref[...] + (gate[:, 0:1] * y0_ref[...] + gate[:, 1:2] * y1_ref[...])


def _emit_head(r, h, outs, scr):
    tm = r.shape[0]
    if scr is not None:
        scr[h] = r
    for ref, dil in outs:
        if dil == 0:
            ref[:, h, :] = r
        elif dil == 1:
            ref[:, _chunk(h)] = r.astype(ref.dtype)
        else:
            for res in range(dil):
                col = res * D_MODEL + h * HEAD_DIM
                ref[:, col:col + HEAD_DIM] = scr[h, pl.ds(res, tm // dil, stride=dil), :].astype(ref.dtype)


def _rope(xh, cos, sin_signed):
    return xh * cos + pltpu.roll(xh, HEAD_DIM // 2, axis=1) * sin_signed


def _qkv_kernel(*refs, streams, per_seq, kept):
    (h_ref, y0_ref, y1_ref, gate_ref, gmix_ref, gkv_ref, wq_ref, wkv_ref, cos_ref, sin_ref) = refs[:10]
    outs = refs[10:]
    if streams:
        (h2_ref, q0_ref, q1_ref, q2_ref, k0_ref, k1_ref, k2_ref, v0_ref, v1_ref, v2_ref, kf_ref, vf_ref,
         scr_q1, scr_q2, scr_k, scr_v) = outs
        q_outs = [([(q0_ref, 1)], None), ([(q1_ref, DILATIONS[1])], scr_q1), ([(q2_ref, DILATIONS[2])], scr_q2)]
        k_outs = ([(k0_ref, 1), (k1_ref, DILATIONS[1]), (k2_ref, DILATIONS[2])], scr_k)
        v_outs = ([(v0_ref, 1), (v1_ref, DILATIONS[1]), (v2_ref, DILATIONS[2])], scr_v)
    else:
        h2_ref, q0_ref, q1_ref, q2_ref, kf_ref, vf_ref = outs
        q_outs = [([(q0_ref, 1)], None), ([(q1_ref, 1)], None), ([(q2_ref, 1)], None)]
        k_outs = ([(kf_ref, 0)], None)
        v_outs = ([(vf_ref, 0)], None)

    h2 = _combine(h_ref, y0_ref, y1_ref, gate_ref)
    h2_ref[...] = h2
    cos = cos_ref[...]
    sin = sin_ref[...]
    xn = _rms(h2, gmix_ref[...]).astype(_BF16)
    scale = HEAD_DIM ** -0.5
    for g in range(N_GROUPS):
        q = _dot(xn, wq_ref[:, g * D_MODEL:(g + 1) * D_MODEL])
        for h in range(N_HEADS):
            _emit_head(_rope(q[:, _chunk(h)], cos, sin) * scale, h, *q_outs[g])
    xkv = _rms(h2, gkv_ref[...]).astype(_BF16)
    k = _dot(xkv, wkv_ref[:, 0:D_MODEL])
    for h in range(N_HEADS):
        _emit_head(_rope(k[:, _chunk(h)], cos, sin), h, *k_outs)
    v = _dot(xkv, wkv_ref[:, D_MODEL:2 * D_MODEL])
    for h in range(N_HEADS):
        _emit_head(v[:, _chunk(h)], h, *v_outs)
    if streams:
        @pl.when(pl.program_id(0) % per_seq >= per_seq - kept)
        def _():
            for h in range(N_HEADS):
                kf_ref[:, h, :] = scr_k[h]
                vf_ref[:, h, :] = scr_v[h]


def _qkv(h, ypair, gate, gmix, gkv, wq, wkv, cos, sin, *, tm, streams, seq, keep):
    t, d = h.shape
    nt = t // tm
    per_seq = seq // tm
    kept = keep // tm
    assert seq % tm == 0 and keep % tm == 0

    def tail(i):
        return ((i // per_seq) * kept + jnp.maximum(i % per_seq - (per_seq - kept), 0), 0, 0)

    tail_f32 = (pl.BlockSpec((tm, N_HEADS, HEAD_DIM), tail),
                jax.ShapeDtypeStruct((t // seq * keep, N_HEADS, HEAD_DIM), _F32))
    row = lambda i: (i, 0)
    const = lambda i: (0, 0)
    nat = lambda dt: (pl.BlockSpec((tm, d), row), jax.ShapeDtypeStruct((t, d), dt))

    def stream(dil):
        return (pl.BlockSpec((tm // dil, dil * d), row), jax.ShapeDtypeStruct((t // dil, dil * d), _BF16))

    if streams:
        d1, d2 = DILATIONS[1], DILATIONS[2]
        outs = [nat(_F32), nat(_BF16), stream(d1), stream(d2), nat(_BF16), stream(d1), stream(d2),
                nat(_BF16), stream(d1), stream(d2), tail_f32, tail_f32]
        scratch = [pltpu.VMEM((N_HEADS, tm, HEAD_DIM), _F32)] * 4
    else:
        outs = [nat(_F32), nat(_BF16), nat(_BF16), nat(_BF16), tail_f32, tail_f32]
        scratch = []
    return pl.pallas_call(
        functools.partial(_qkv_kernel, streams=streams, per_seq=per_seq, kept=kept),
        grid=(nt,),
        in_specs=[
            pl.BlockSpec((tm, d), row),
            pl.BlockSpec((tm, d), row),
            pl.BlockSpec((tm, d), lambda i: (nt + i, 0)),
            pl.BlockSpec((tm, 2), row),
            pl.BlockSpec((1, d), const),
            pl.BlockSpec((1, d), const),
            pl.BlockSpec((d, 3 * d), const),
            pl.BlockSpec((d, 2 * d), const),
            pl.BlockSpec((tm, HEAD_DIM), row),
            pl.BlockSpec((tm, HEAD_DIM), row),
        ],
        out_specs=[o[0] for o in outs],
        out_shape=[o[1] for o in outs],
        scratch_shapes=scratch,
        compiler_params=pltpu.CompilerParams(
            dimension_semantics=("arbitrary",), vmem_limit_bytes=VMEM_LIMIT),
        name="qkv_proj_streams" if streams else "qkv_proj",
    )(h, ypair, ypair, gate, gmix, gkv, wq, wkv, cos, sin)


ATTN_TQ = 512


def _attn_prompt_kernel(q_ref, kp_ref, kc_ref, vp_ref, vc_ref, o_ref, st_ref, *, tq):
    i = pl.program_id(1)
    qi = lax.broadcasted_iota(jnp.int32, (SPAN, 2 * SPAN), 0)
    kj = lax.broadcasted_iota(jnp.int32, (SPAN, 2 * SPAN), 1)
    band = (kj >= qi) & (kj <= qi + SPAN)
    band0 = band & (kj >= jnp.where(i > 0, 0, SPAN))
    lane = lax.broadcasted_iota(jnp.int32, (SPAN, HEAD_DIM), 1)
    ones = jnp.ones((2 * SPAN, HEAD_DIM), _BF16)
    for a in range(tq // SPAN):
        rows = slice(a * SPAN, (a + 1) * SPAN)
        st = jnp.zeros((SPAN, HEAD_DIM), _F32)
        for h in range(N_HEADS):
            hs = _chunk(h)
            if a == 0:
                keys = jnp.concatenate([kp_ref[:, hs], kc_ref[rows, hs]], axis=0)
                vals = jnp.concatenate([vp_ref[:, hs], vc_ref[rows, hs]], axis=0)
                mask = band0
            else:
                both = slice((a - 1) * SPAN, (a + 1) * SPAN)
                keys, vals, mask = kc_ref[both, hs], vc_ref[both, hs], band
            s = jnp.where(mask, _dot_nt(q_ref[rows, hs], keys), NEG_BIG)
            m = jnp.max(s, axis=1, keepdims=True)
            p = jnp.exp(s - m).astype(_BF16)
            ol = _dot(p, jnp.concatenate([vals, ones], axis=1))
            l = ol[:, HEAD_DIM:]
            o_ref[rows, hs] = ol[:, :HEAD_DIM] / l
            st = jnp.where(lane == h, m + jnp.log(l), st)
        st_ref[rows, :] = st


def _attn_prompt(q, k, v, batch, dilation):
    rows, width = q.shape
    d = width // dilation
    per_batch = rows // batch
    tq = min(ATTN_TQ, per_batch)
    assert per_batch % tq == 0 and tq % SPAN == 0
    nq = per_batch // tq
    sub = tq // SPAN
    cur = lambda b, i, r: (b * nq + i, r)
    prev = lambda b, i, r: ((b * nq + i) * sub - jnp.where(i > 0, 1, 0), r)
    big = (tq, d)
    small = (SPAN, d)
    return pl.pallas_call(
        functools.partial(_attn_prompt_kernel, tq=tq),
        grid=(batch, nq, dilation),
        in_specs=[pl.BlockSpec(big, cur), pl.BlockSpec(small, prev), pl.BlockSpec(big, cur),
                  pl.BlockSpec(small, prev), pl.BlockSpec(big, cur)],
        out_specs=[pl.BlockSpec(big, cur), pl.BlockSpec((tq, HEAD_DIM), cur)],
        out_shape=[jax.ShapeDtypeStruct((rows, width), _F32),
                   jax.ShapeDtypeStruct((rows, dilation * HEAD_DIM), _F32)],
        compiler_params=pltpu.CompilerParams(
            dimension_semantics=("arbitrary", "arbitrary", "arbitrary"), vmem_limit_bytes=VMEM_LIMIT),
        name=f"attn_prompt_d{dilation}",
    )(q, k, k, v, v)


N_NEW = 8
CACHE_A_GROUPS = (PAST_LEN - WINDOWS[1]) // DILATIONS[2]
CACHE_A_ROWS = CACHE_A_GROUPS * N_NEW
CACHE_B_ROWS = WINDOWS[1]
CACHE_B_GROUPS = CACHE_B_ROWS // DILATIONS[2]
KEYS_REAL = CACHE_A_ROWS + CACHE_B_ROWS + N_NEW
KEYS_PAD = -(-KEYS_REAL // 128) * 128


def _sample_key_positions():
    pos = np.full((KEYS_PAD,), -1, np.int64)
    a = np.arange(CACHE_A_ROWS)
    pos[:CACHE_A_ROWS] = (a // N_NEW) * DILATIONS[2] + a % N_NEW
    pos[CACHE_A_ROWS:CACHE_A_ROWS + CACHE_B_ROWS] = PAST_LEN - CACHE_B_ROWS + np.arange(CACHE_B_ROWS)
    pos[CACHE_A_ROWS + CACHE_B_ROWS:KEYS_REAL] = PAST_LEN + np.arange(N_NEW)
    return pos


def _sample_bias():
    pos = _sample_key_positions()
    bias = np.full((N_GROUPS * N_NEW, KEYS_PAD), NEG_BIG, np.float32)
    for g in range(N_GROUPS):
        for n in range(N_NEW):
            delta = PAST_LEN + n - pos
            ok = (pos >= 0) & (delta >= 0) & (delta <= WINDOWS[g]) & (delta % DILATIONS[g] == 0)
            assert int(ok.sum()) == SPAN + 1
            bias[g * N_NEW + n, ok] = 0.0
    return bias


def _attn_sample_kernel(q0_ref, q1_ref, q2_ref, kn_ref, vn_ref, bias_ref, ck_hbm, cv_hbm,
                        o_ref, ka, kb, va, vb, kall, vall, sem, *, n_seq):
    b = pl.program_id(0)
    slot = b % 2

    def cache_copies(seq, s):
        cps = []
        for h in range(N_HEADS):
            for src, dst_a, dst_b in ((ck_hbm, ka, kb), (cv_hbm, va, vb)):
                cps.append(pltpu.make_async_copy(
                    src.at[seq, pl.ds(0, CACHE_A_GROUPS), pl.ds(0, N_NEW), h], dst_a.at[s, h], sem.at[s]))
                cps.append(pltpu.make_async_copy(
                    src.at[seq, pl.ds(CACHE_A_GROUPS, CACHE_B_GROUPS), :, h], dst_b.at[s, h], sem.at[s]))
        return cps

    @pl.when(b == 0)
    def _():
        kall[KEYS_REAL:KEYS_PAD, :] = jnp.zeros((KEYS_PAD - KEYS_REAL, D_MODEL), _BF16)
        vall[KEYS_REAL:KEYS_PAD, :] = jnp.zeros((KEYS_PAD - KEYS_REAL, D_MODEL), _BF16)
        for cp in cache_copies(0, 0):
            cp.start()

    @pl.when(b + 1 < n_seq)
    def _():
        for cp in cache_copies(b + 1, 1 - slot):
            cp.start()

    for cp in cache_copies(b, slot):
        cp.wait()

    nb0 = CACHE_A_ROWS + CACHE_B_ROWS
    for src_a, src_b, src_n, dst in ((ka, kb, kn_ref, kall), (va, vb, vn_ref, vall)):
        for h in range(N_HEADS):
            hs = _chunk(h)
            dst[0:CACHE_A_ROWS, hs] = src_a[slot, h].reshape(CACHE_A_ROWS, HEAD_DIM).astype(_BF16)
            dst[CACHE_A_ROWS:nb0, hs] = src_b[slot, h].reshape(CACHE_B_ROWS, HEAD_DIM).astype(_BF16)
            dst[nb0:KEYS_REAL, hs] = src_n[0, :, h, :].astype(_BF16)

    bias = bias_ref[...]
    for h in range(N_HEADS):
        hs = _chunk(h)
        qh = jnp.concatenate([q0_ref[0, :, hs].astype(_F32), q1_ref[0, :, hs].astype(_F32),
                              q2_ref[0, :, hs].astype(_F32)], axis=0).astype(_BF16)
        s = _dot_nt(qh, kall[:, hs]) + bias
        m = jnp.max(s, axis=1, keepdims=True)
        m8 = jnp.maximum(jnp.maximum(m[0:N_NEW], m[N_NEW:2 * N_NEW]), m[2 * N_NEW:3 * N_NEW])
        p = jnp.exp(s - jnp.concatenate([m8, m8, m8], axis=0))
        p8 = p[0:N_NEW] + p[N_NEW:2 * N_NEW] + p[2 * N_NEW:3 * N_NEW]
        l8 = jnp.sum(p8, axis=1, keepdims=True)
        o_ref[0, :, hs] = _dot(p8.astype(_BF16), vall[:, hs]) / l8


def _attn_sample(q0, q1, q2, cache_k, cache_v, k_new, v_new):
    nb, n_new, d = q0.shape
    past = cache_k.shape[1]
    assert past == PAST_LEN and n_new == N_NEW and d == D_MODEL
    assert cache_k.shape[2:] == (N_HEADS, HEAD_DIM)
    ck = cache_k.reshape(nb, past // DILATIONS[2], DILATIONS[2], N_HEADS, HEAD_DIM)
    cv = cache_v.reshape(nb, past // DILATIONS[2], DILATIONS[2], N_HEADS, HEAD_DIM)
    bias = jnp.asarray(_sample_bias())
    new = pl.BlockSpec((1, n_new, d), lambda b: (b, 0, 0))
    new_kv = pl.BlockSpec((1, n_new, N_HEADS, HEAD_DIM), lambda b: (b, 0, 0, 0))
    part_a = (2, N_HEADS, CACHE_A_GROUPS, N_NEW, HEAD_DIM)
    part_b = (2, N_HEADS, CACHE_B_GROUPS, DILATIONS[2], HEAD_DIM)
    return pl.pallas_call(
        functools.partial(_attn_sample_kernel, n_seq=nb),
        grid=(nb,),
        in_specs=[new, new, new, new_kv, new_kv,
                  pl.BlockSpec((N_GROUPS * N_NEW, KEYS_PAD), lambda b: (0, 0)),
                  pl.BlockSpec(memory_space=pl.ANY), pl.BlockSpec(memory_space=pl.ANY)],
        out_specs=new,
        out_shape=jax.ShapeDtypeStruct((nb, n_new, d), _F32),
        scratch_shapes=[pltpu.VMEM(part_a, _F32), pltpu.VMEM(part_b, _F32),
                        pltpu.VMEM(part_a, _F32), pltpu.VMEM(part_b, _F32),
                        pltpu.VMEM((KEYS_PAD, d), _BF16), pltpu.VMEM((KEYS_PAD, d), _BF16),
                        pltpu.SemaphoreType.DMA((2,))],
        compiler_params=pltpu.CompilerParams(
            dimension_semantics=("arbitrary",), vmem_limit_bytes=VMEM_LIMIT),
        name="attn_sample",
    )(q0, q1, q2, k_new, v_new, bias, ck, cv)


def _attn_out_kernel(*refs, n_groups, tm):
    if n_groups == 1:
        (o_ref, h_ref, wo_ref, gffn_ref, wr_ref, br_ref,
         h3_ref, xn2_ref, eid_ref, gate_ref) = refs
        o = o_ref[...].astype(_BF16)
    else:
        (o0_ref, o1_ref, o2_ref, s0_ref, s1_ref, s2_ref, h_ref, wo_ref, gffn_ref, wr_ref, br_ref,
         h3_ref, xn2_ref, eid_ref, gate_ref, obuf, scr_o1, scr_o2, scr_s1, scr_s2) = refs
        for o_ref, s_ref, scr_o, scr_s, dil in ((o1_ref, s1_ref, scr_o1, scr_s1, DILATIONS[1]),
                                                (o2_ref, s2_ref, scr_o2, scr_s2, DILATIONS[2])):
            for res in range(dil):
                dst = pl.ds(res, tm // dil, stride=dil)
                scr_s[dst, :] = s_ref[:, res * HEAD_DIM:(res + 1) * HEAD_DIM]
                for h in range(N_HEADS):
                    col = res * D_MODEL + h * HEAD_DIM
                    scr_o[h, dst, :] = o_ref[:, col:col + HEAD_DIM]
        sts = [s0_ref[...], scr_s1[...], scr_s2[...]]
        mx = jnp.maximum(jnp.maximum(sts[0], sts[1]), sts[2])
        es = [jnp.exp(s - mx) for s in sts]
        den = es[0] + es[1] + es[2]
        ws = [e / den for e in es]
        for h in range(N_HEADS):
            acc = ws[0][:, h:h + 1] * o0_ref[:, _chunk(h)]
            acc = acc + ws[1][:, h:h + 1] * scr_o1[h]
            acc = acc + ws[2][:, h:h + 1] * scr_o2[h]
            obuf[:, _chunk(h)] = acc.astype(_BF16)
        o = obuf[...]
    h3 = h_ref[...] + _dot(o, wo_ref[...])
    h3_ref[...] = h3
    xn2 = _rms(h3, gffn_ref[...])
    xn2_ref[...] = xn2
    _route(xn2, wr_ref, br_ref, eid_ref, gate_ref)


def _attn_out(os, sts, h, wo, gffn, wr, br, *, tm):
    t, d = h.shape
    n_groups = len(os)
    row = lambda i: (i, 0)
    const = lambda i: (0, 0)
    if n_groups == 1:
        in_specs = [pl.BlockSpec((tm, d), row)]
        scratch = []
        args = [os[0]]
    else:
        in_specs = [pl.BlockSpec((tm // dil, dil * d), row) for dil in DILATIONS]
        in_specs += [pl.BlockSpec((tm // dil, dil * HEAD_DIM), row) for dil in DILATIONS]
        scratch = [pltpu.VMEM((tm, d), _BF16),
                   pltpu.VMEM((N_HEADS, tm, HEAD_DIM), _F32), pltpu.VMEM((N_HEADS, tm, HEAD_DIM), _F32),
                   pltpu.VMEM((tm, HEAD_DIM), _F32), pltpu.VMEM((tm, HEAD_DIM), _F32)]
        args = list(os) + list(sts)
    in_specs += [pl.BlockSpec((tm, d), row), pl.BlockSpec((d, d), const), pl.BlockSpec((1, d), const),
                 pl.BlockSpec((ROUTER_ROWS, d), const), pl.BlockSpec((ROUTER_ROWS, 1), const)]
    return pl.pallas_call(
        functools.partial(_attn_out_kernel, n_groups=n_groups, tm=tm),
        grid=(t // tm,),
        in_specs=in_specs,
        out_specs=[pl.BlockSpec((tm, d), row), pl.BlockSpec((tm, d), row),
                   pl.BlockSpec((2, tm), lambda i: (0, i)), pl.BlockSpec((2, tm), lambda i: (0, i))],
        out_shape=[jax.ShapeDtypeStruct((t, d), _F32), jax.ShapeDtypeStruct((t, d), _F32),
                   jax.ShapeDtypeStruct((2, t), jnp.int32), jax.ShapeDtypeStruct((2, t), _F32)],
        scratch_shapes=scratch,
        compiler_params=pltpu.CompilerParams(
            dimension_semantics=("arbitrary",), vmem_limit_bytes=VMEM_LIMIT),
        name=f"attn_out_g{n_groups}",
    )(*args, h, wo, gffn, wr, br)


def _final_kernel(h_ref, y0_ref, y1_ref, gate_ref, g_ref, out_ref):
    out_ref[...] = _rms(_combine(h_ref, y0_ref, y1_ref, gate_ref), g_ref[...])


def _final(h, ypair, gate, g, *, tm):
    t, d = h.shape
    nt = t // tm
    row = lambda i: (i, 0)
    return pl.pallas_call(
        _final_kernel,
        grid=(nt,),
        in_specs=[pl.BlockSpec((tm, d), row),
                  pl.BlockSpec((tm, d), row),
                  pl.BlockSpec((tm, d), lambda i: (nt + i, 0)),
                  pl.BlockSpec((tm, 2), row),
                  pl.BlockSpec((1, d), lambda i: (0, 0))],
        out_specs=pl.BlockSpec((tm, d), row),
        out_shape=jax.ShapeDtypeStruct((t, d), _F32),
        compiler_params=pltpu.CompilerParams(
            dimension_semantics=("arbitrary",), vmem_limit_bytes=VMEM_LIMIT),
        name="final_norm",
    )(h, ypair, ypair, gate, g)


def _rope_tables(pos):
    half = HEAD_DIM // 2
    inv_freq = jnp.power(jnp.float32(ROPE_THETA), -jnp.arange(half, dtype=jnp.float32) / half)
    ang = pos.astype(jnp.float32)[:, None] * inv_freq[None, :]
    cos = jnp.cos(ang)
    sin = jnp.sin(ang)
    return jnp.concatenate([cos, cos], axis=-1), jnp.concatenate([-sin, sin], axis=-1)


def _router_params(wg, bg, we, be):
    wr = jnp.zeros((ROUTER_ROWS, D_MODEL), _F32)
    wr = wr.at[0:MOE_GROUPS].set(wg.T).at[EXPERT_ROW0:EXPERT_ROW0 + N_EXPERTS].set(we.T)
    br = jnp.zeros((ROUTER_ROWS, 1), _F32)
    br = br.at[0:MOE_GROUPS, 0].set(bg).at[EXPERT_ROW0:EXPERT_ROW0 + N_EXPERTS, 0].set(be)
    return wr, br


def _moe(xn2, eid, experts, layer):
    block_expert, source, slot = _moe_plan(eid, xn2.shape[0])
    xs = _row_gather(xn2, source)
    yield
    ys = _moe_ffn(block_expert, xs, *experts, layer)
    ypair = _row_gather(ys, slot)
    yield
    return ypair


def _forward(x3, prev3, pos_rows, attend, p, *, shift, tm, streams, keep):
    h1, xn2, eid, gate, state = _conv_layer(
        x3, prev3, p['gmix'][0], p['win'], p['ck'], p['wout'], p['gffn'][0], *p['router'][0],
        shift=shift, tm=tm)
    t = x3.shape[0] * x3.shape[1]
    ypair = yield from _moe(xn2, eid, p['experts'], 0)
    cos, sin = _rope_tables(pos_rows)
    h2, *qkv, kf, vf = _qkv(h1.reshape(t, D_MODEL), ypair, gate.T, p['gmix'][1], p['gkv'],
                            p['wq'], p['wkv'], cos, sin, tm=256, streams=streams,
                            seq=x3.shape[1], keep=keep)
    os, sts = attend(qkv, kf, vf)
    h3, xn2, eid, gate = _attn_out(os, sts, h2, p['wo'], p['gffn'][1], *p['router'][1], tm=256)
    ypair = yield from _moe(xn2, eid, p['experts'], 1)
    y = _final(h3, ypair, gate.T, p['gfinal'], tm=512)
    return y, state, kf, vf


def _interleave(*paths):
    results = [None] * len(paths)
    live = list(enumerate(paths))
    while live:
        still = []
        for i, gen in live:
            try:
                next(gen)
                still.append((i, gen))
            except StopIteration as done:
                results[i] = done.value
        live = still
    return results


def kernel(x_prompt, x_sample, cache_k, cache_v, state_conv, norm_mix, norm_ffn, norm_kv, norm_final,
           conv_w_in, conv_kernel, conv_w_out, attn_w_q, attn_w_kv, attn_w_o, router_group_w,
           router_group_b, router_expert_w, router_expert_b, expert_w1, expert_w3, expert_w2):
    b_p, s_p, d = x_prompt.shape
    b_s, n_new, _ = x_sample.shape
    assert d == D_MODEL and n_new == N_NEW and s_p % (DILATIONS[2] * SPAN) == 0
    assert norm_mix.shape[0] == 2 and conv_w_in.shape[0] == 1 and attn_w_q.shape[0] == 1
    assert cache_k.shape[1] == PAST_LEN

    p = {
        'gmix': [norm_mix[l].reshape(1, d) for l in range(2)],
        'gffn': [norm_ffn[l].reshape(1, d) for l in range(2)],
        'gkv': norm_kv.reshape(1, d),
        'gfinal': norm_final.reshape(1, d),
        'win': conv_w_in[0].astype(_BF16),
        'ck': conv_kernel[0],
        'wout': conv_w_out[0].astype(_BF16),
        'wq': attn_w_q[0].astype(_BF16),
        'wkv': attn_w_kv.astype(_BF16),
        'wo': attn_w_o[0].astype(_BF16),
        'router': [_router_params(router_group_w[l], router_group_b[l], router_expert_w[l], router_expert_b[l])
                   for l in range(2)],
        'experts': (expert_w1, expert_w3, expert_w2),
    }

    def attend_prompt(qkv, kf, vf):
        q0, q1, q2, k0, k1, k2, v0, v1, v2 = qkv
        os, sts = [], []
        for q, k, v, dil in ((q0, k0, v0, DILATIONS[0]), (q1, k1, v1, DILATIONS[1]), (q2, k2, v2, DILATIONS[2])):
            o, st = _attn_prompt(q, k, v, b_p, dil)
            os.append(o)
            sts.append(st)
        return os, sts

    pos_p = jnp.tile(jnp.arange(s_p, dtype=jnp.int32), b_p)
    zero_state = jnp.zeros((b_p, CONV_WIDTH - 1, d), x_prompt.dtype)
    keep = min(max(WINDOWS), s_p)
    prompt_path = _forward(x_prompt, zero_state, pos_p, attend_prompt, p, shift=1, tm=512,
                           streams=True, keep=keep)

    halves = 2
    bh = b_s // halves

    def to_rows(a):
        w = a.shape[-1]
        return a.reshape(halves, bh, n_new, w).transpose(0, 2, 1, 3).reshape(halves * n_new * bh, w)

    def to_batch(a):
        w = a.shape[1:]
        return jnp.swapaxes(a.reshape(halves, n_new, bh, *w), 1, 2).reshape(b_s, n_new, *w)

    def attend_sample(qkv, kf, vf):
        q0, q1, q2 = qkv
        o = _attn_sample(to_batch(q0), to_batch(q1), to_batch(q2), cache_k, cache_v,
                         to_batch(kf), to_batch(vf))
        return [to_rows(o)], None

    x_s = to_rows(x_sample).reshape(halves, n_new * bh, d)
    prev_s = state_conv[0].reshape(halves, bh, CONV_WIDTH - 1, d).transpose(0, 2, 1, 3).reshape(
        halves, (CONV_WIDTH - 1) * bh, d)
    pos_s = jnp.tile(jnp.repeat(PAST_LEN + jnp.arange(n_new, dtype=jnp.int32), bh), halves)
    sample_path = _forward(x_s, prev_s, pos_s, attend_sample, p, shift=bh, tm=n_new * bh,
                           streams=False, keep=n_new * bh)

    (y_p, st_p, kf_p, vf_p), (y_s, st_s, kf_s, vf_s) = _interleave(prompt_path, sample_path)
    y_prompt = y_p.reshape(b_p, s_p, d)
    k_p = kf_p.reshape(b_p, keep, N_HEADS, HEAD_DIM)
    v_p = vf_p.reshape(b_p, keep, N_HEADS, HEAD_DIM)
    conv_p = st_p[None]
    y_sample = to_batch(y_s)
    k_s = to_batch(kf_s)
    v_s = to_batch(vf_s)
    conv_s = st_s.reshape(halves, CONV_WIDTH - 1, bh, d).transpose(0, 2, 1, 3).reshape(
        b_s, CONV_WIDTH - 1, d)[None]

    return (y_prompt, y_sample, k_p, v_p, conv_p, k_s, v_s, conv_s)
```

```python
import functools

import numpy as np
import jax
import jax.numpy as jnp
from jax import lax
from jax.experimental import pallas as pl
from jax.experimental.pallas import tpu as pltpu
from jax.experimental.pallas import tpu_sc as plsc

D_MODEL = 1024
CONV_WIDTH = 3
WINDOWS = (128, 512, 2048)
DILATIONS = (1, 4, 16)
N_GROUPS = 3
N_HEADS = 8
HEAD_DIM = 128
SPAN = 128
ROPE_THETA = 10000.0
MOE_GROUPS = 4
EXPERTS_PER_GROUP = 4
N_EXPERTS = 16
D_EXPERT = 512
EXPERT_BLOCK = 256
RMS_EPS = 1e-6
NEG_BIG = -1e30
PAST_LEN = 2048

LANES = 128
ROUTER_ROWS = 32
EXPERT_ROW0 = 8
VMEM_LIMIT = 56 * 1024 * 1024

_F32 = jnp.float32
_BF16 = jnp.bfloat16


def _rms(x, g):
    ms = jnp.mean(x * x, axis=-1, keepdims=True)
    return (x * lax.rsqrt(ms + RMS_EPS)) * g


def _dot(a, b):
    return jnp.dot(a, b, preferred_element_type=_F32)


def _dot_nt(a, b, precision=None):
    return lax.dot_general(a, b, (((1,), (1,)), ((), ())), precision=precision,
                           preferred_element_type=_F32)


def _chunk(c):
    return slice(c * LANES, (c + 1) * LANES)


def _route(xn, wr_ref, br_ref, eid_ref, gate_ref):
    logits = _dot_nt(wr_ref[...], xn, precision=lax.Precision.HIGHEST) + br_ref[...]
    lg = logits[0:MOE_GROUPS]
    row = lax.broadcasted_iota(jnp.int32, lg.shape, 0).astype(_F32)
    eg = jnp.exp(lg - jnp.max(lg, axis=0, keepdims=True))
    pg = eg / jnp.sum(eg, axis=0, keepdims=True)
    gp = jnp.max(pg, axis=0, keepdims=True)
    gi = jnp.min(jnp.where(pg == gp, row, float(MOE_GROUPS)), axis=0, keepdims=True)
    le = jnp.zeros_like(lg)
    for g in range(MOE_GROUPS):
        r0 = EXPERT_ROW0 + g * EXPERTS_PER_GROUP
        le = le + jnp.where(gi == float(g), logits[r0:r0 + EXPERTS_PER_GROUP], 0.0)
    ee = jnp.exp(le - jnp.max(le, axis=0, keepdims=True))
    ev = ee / jnp.sum(ee, axis=0, keepdims=True)
    v1 = jnp.max(ev, axis=0, keepdims=True)
    i1 = jnp.min(jnp.where(ev == v1, row, float(EXPERTS_PER_GROUP)), axis=0, keepdims=True)
    ev2 = jnp.where(row == i1, -1.0, ev)
    v2 = jnp.max(ev2, axis=0, keepdims=True)
    i2 = jnp.min(jnp.where(ev2 == v2, row, float(EXPERTS_PER_GROUP)), axis=0, keepdims=True)
    den = v1 + v2
    ids = jnp.concatenate([gi * EXPERTS_PER_GROUP + i1, gi * EXPERTS_PER_GROUP + i2], axis=0)
    eid_ref[...] = ids.astype(jnp.int32)
    gate_ref[...] = jnp.concatenate([gp * v1 / den, gp * v2 / den], axis=0)


def _conv_layer_kernel(x_ref, prev_ref, gmix_ref, win_ref, ck_ref, wout_ref, gffn_ref, wr_ref, br_ref,
                       h_ref, xn2_ref, eid_ref, gate_ref, st_ref, ubuf, *, shift, tm):
    i = pl.program_id(1)
    halo = ubuf.shape[0] - tm
    keep = (CONV_WIDTH - 1) * shift

    @pl.when(i == 0)
    def _():
        ubuf[halo - keep:halo, :] = prev_ref[0]

    @pl.when(i > 0)
    def _():
        ubuf[halo - keep:halo, :] = ubuf[halo + tm - keep:halo + tm, :]

    x = x_ref[0]
    xn = _rms(x, gmix_ref[...]).astype(_BF16)
    c_gate = _dot(xn, win_ref[:, 0:D_MODEL])
    hid = _dot(xn, win_ref[:, 2 * D_MODEL:3 * D_MODEL])
    u = c_gate * hid
    ubuf[halo:halo + tm, :] = u
    u1 = ubuf[halo - shift:halo - shift + tm, :]
    u2 = ubuf[halo - 2 * shift:halo - 2 * shift + tm, :]
    conv = ck_ref[0:1, :] * u2 + ck_ref[1:2, :] * u1 + ck_ref[2:3, :] * u
    b_gate = _dot(xn, win_ref[:, D_MODEL:2 * D_MODEL])
    y = _dot((b_gate * conv).astype(_BF16), wout_ref[...])
    h = x + y
    h_ref[0] = h
    st_ref[0] = ubuf[halo + tm - keep:halo + tm, :]
    xn2 = _rms(h, gffn_ref[...])
    xn2_ref[...] = xn2
    _route(xn2, wr_ref, br_ref, eid_ref, gate_ref)


def _conv_layer(x3, prev3, gmix, win, ck, wout, gffn, wr, br, *, shift, tm):
    nb, s, d = x3.shape
    nt = s // tm
    keep = (CONV_WIDTH - 1) * shift
    halo = -(-keep // 8) * 8
    t_all = nb * s
    const = lambda b, i: (0, 0)
    return pl.pallas_call(
        functools.partial(_conv_layer_kernel, shift=shift, tm=tm),
        grid=(nb, nt),
        in_specs=[
            pl.BlockSpec((1, tm, d), lambda b, i: (b, i, 0)),
            pl.BlockSpec((1, keep, d), lambda b, i: (b, 0, 0)),
            pl.BlockSpec((1, d), const),
            pl.BlockSpec((d, 3 * d), const),
            pl.BlockSpec((CONV_WIDTH, d), const),
            pl.BlockSpec((d, d), const),
            pl.BlockSpec((1, d), const),
            pl.BlockSpec((ROUTER_ROWS, d), const),
            pl.BlockSpec((ROUTER_ROWS, 1), const),
        ],
        out_specs=[
            pl.BlockSpec((1, tm, d), lambda b, i: (b, i, 0)),
            pl.BlockSpec((tm, d), lambda b, i: (b * nt + i, 0)),
            pl.BlockSpec((2, tm), lambda b, i: (0, b * nt + i)),
            pl.BlockSpec((2, tm), lambda b, i: (0, b * nt + i)),
            pl.BlockSpec((1, keep, d), lambda b, i: (b, 0, 0)),
        ],
        out_shape=[
            jax.ShapeDtypeStruct((nb, s, d), _F32),
            jax.ShapeDtypeStruct((t_all, d), _F32),
            jax.ShapeDtypeStruct((2, t_all), jnp.int32),
            jax.ShapeDtypeStruct((2, t_all), _F32),
            jax.ShapeDtypeStruct((nb, keep, d), _F32),
        ],
        scratch_shapes=[pltpu.VMEM((halo + tm, d), _F32)],
        compiler_params=pltpu.CompilerParams(
            dimension_semantics=("arbitrary", "arbitrary"), vmem_limit_bytes=VMEM_LIMIT),
        name="conv_layer",
    )(x3, prev3, gmix, win, ck, wout, gffn, wr, br)


SC_WINDOW = 32


def _row_gather(src, idx):
    n_out = idx.shape[0]
    mesh = plsc.VectorSubcoreMesh(core_axis_name="c", subcore_axis_name="s")
    n_cores = mesh.num_cores
    per_worker = n_out // (n_cores * mesh.num_subcores)
    assert per_worker * n_cores * mesh.num_subcores == n_out and per_worker % SC_WINDOW == 0

    @pl.kernel(out_type=jax.ShapeDtypeStruct((n_out,) + src.shape[1:], src.dtype), mesh=mesh,
               scratch_types=[pltpu.VMEM((per_worker,), jnp.int32),
                              pltpu.VMEM((SC_WINDOW,) + src.shape[1:], src.dtype),
                              pltpu.SemaphoreType.DMA])
    def gather(src_hbm, idx_hbm, out_hbm, idx_v, rows_v, sem):
        base = (lax.axis_index("s") * n_cores + lax.axis_index("c")) * per_worker
        pltpu.sync_copy(idx_hbm.at[pl.ds(base, per_worker)], idx_v)

        @pl.loop(0, per_worker // SC_WINDOW)
        def _(q):
            off = q * SC_WINDOW
            pltpu.async_copy(src_hbm.at[idx_v.at[pl.ds(off, SC_WINDOW)]], rows_v, sem).wait()
            pltpu.sync_copy(rows_v, out_hbm.at[pl.ds(base + off, SC_WINDOW)])

    return gather(src, idx)


def _moe_ffn_kernel(be_ref, x_ref, w1_ref, w3_ref, w2_ref, y_ref, w1b, w3b, w2b):
    j = pl.program_id(0)

    @pl.when((j == 0) | (be_ref[j] != be_ref[jnp.maximum(j - 1, 0)]))
    def _():
        w1b[...] = w1_ref[0, 0].astype(_BF16)
        w3b[...] = w3_ref[0, 0].astype(_BF16)
        w2b[...] = w2_ref[0, 0].astype(_BF16)

    x = x_ref[...].astype(_BF16)
    a = _dot(x, w1b[...])
    b = _dot(x, w3b[...])
    y_ref[...] = _dot((jax.nn.silu(a) * b).astype(_BF16), w2b[...])


def _moe_ffn(block_expert, xs, w1, w3, w2, layer):
    n_slots, d = xs.shape
    n_blocks = block_expert.shape[0]
    wmap = lambda j, be: (layer, be[j], 0, 0)
    blk = pl.BlockSpec((EXPERT_BLOCK, d), lambda j, be: (j, 0))
    return pl.pallas_call(
        _moe_ffn_kernel,
        grid_spec=pltpu.PrefetchScalarGridSpec(
            num_scalar_prefetch=1,
            grid=(n_blocks,),
            in_specs=[blk,
                      pl.BlockSpec((1, 1, d, D_EXPERT), wmap),
                      pl.BlockSpec((1, 1, d, D_EXPERT), wmap),
                      pl.BlockSpec((1, 1, D_EXPERT, d), wmap)],
            out_specs=blk,
            scratch_shapes=[pltpu.VMEM((d, D_EXPERT), _BF16), pltpu.VMEM((d, D_EXPERT), _BF16),
                            pltpu.VMEM((D_EXPERT, d), _BF16)],
        ),
        out_shape=jax.ShapeDtypeStruct((n_slots, d), _F32),
        compiler_params=pltpu.CompilerParams(
            dimension_semantics=("arbitrary",), vmem_limit_bytes=VMEM_LIMIT),
        name="moe_ffn",
    )(block_expert, xs, w1, w3, w2)


def _moe_plan_kernel(eid_ref, tab_ref, cnt_ref, slot_v, slot_s, zeros_v, sem):
    rows = eid_ref.shape[0]
    depth = max(rows, LANES)
    eid = eid_ref[...]
    r_i = lax.broadcasted_iota(jnp.int32, (LANES, LANES), 0)
    c_i = lax.broadcasted_iota(jnp.int32, (LANES, LANES), 1)
    earlier_lane = jnp.where(r_i < c_i, 1.0, 0.0).astype(_BF16)
    all_lanes = jnp.ones((LANES, LANES), _BF16)
    q_i = lax.broadcasted_iota(jnp.int32, (rows, depth), 1)
    p_i = lax.broadcasted_iota(jnp.int32, (rows, depth), 0)
    earlier_row = jnp.where(q_i < p_i, 1.0, 0.0).astype(_BF16)
    slot = jnp.zeros((rows, LANES), _F32)
    start = jnp.int32(0)
    for ex in range(N_EXPERTS):
        hit = eid == ex
        hot = jnp.where(hit, 1.0, 0.0).astype(_BF16)
        in_row = _dot(hot, earlier_lane)
        row_tot = _dot(hot, all_lanes)
        if depth > rows:
            row_tot = jnp.concatenate([row_tot, jnp.zeros((depth - rows, LANES), _F32)], axis=0)
        above = _dot(earlier_row, row_tot.astype(_BF16))
        count = jnp.sum(jnp.where(hit, 1, 0))
        slot = jnp.where(hit, start.astype(_F32) + in_row + above, slot)
        cnt_ref[ex] = count
        start = start + (count + EXPERT_BLOCK - 1) // EXPERT_BLOCK * EXPERT_BLOCK
    slot_v[...] = slot.astype(jnp.int32)
    to_smem = pltpu.make_async_copy(slot_v, slot_s, sem.at[0])
    to_smem.start()
    zeros_v[...] = jnp.zeros(zeros_v.shape, jnp.int32)
    clear = pltpu.make_async_copy(zeros_v, tab_ref, sem.at[1])
    clear.start()
    to_smem.wait()
    clear.wait()

    def place(r, carry):
        for c in range(LANES):
            tab_ref[slot_s[r, c]] = r * LANES + c
        return carry
    lax.fori_loop(0, rows, place, 0)


def _moe_plan(eid, n_tok):
    n_assign = 2 * n_tok
    n_blocks = -(-(n_assign + N_EXPERTS * (EXPERT_BLOCK - 1)) // EXPERT_BLOCK)
    n_slots = n_blocks * EXPERT_BLOCK
    rows = n_assign // LANES
    assert n_assign % LANES == 0
    table, counts, slot = pl.pallas_call(
        _moe_plan_kernel,
        in_specs=[pl.BlockSpec(memory_space=pltpu.VMEM)],
        out_specs=[pl.BlockSpec(memory_space=pltpu.SMEM), pl.BlockSpec(memory_space=pltpu.SMEM),
                   pl.BlockSpec(memory_space=pltpu.VMEM)],
        out_shape=[jax.ShapeDtypeStruct((n_slots,), jnp.int32), jax.ShapeDtypeStruct((N_EXPERTS,), jnp.int32),
                   jax.ShapeDtypeStruct((rows, LANES), jnp.int32)],
        scratch_shapes=[pltpu.SMEM((rows, LANES), jnp.int32), pltpu.VMEM((n_slots,), jnp.int32),
                        pltpu.SemaphoreType.DMA((2,))],
        name="moe_plan",
    )(eid.reshape(rows, LANES))
    padded = (counts + EXPERT_BLOCK - 1) // EXPERT_BLOCK * EXPERT_BLOCK
    pad_end = jnp.cumsum(padded)
    pad_start = pad_end - padded
    block_start = jnp.arange(n_blocks, dtype=jnp.int32) * EXPERT_BLOCK
    block_expert = jnp.minimum(jnp.sum((pad_end[None, :] <= block_start[:, None]).astype(jnp.int32), axis=1),
                               N_EXPERTS - 1)
    n_valid = jnp.clip((pad_start + counts)[block_expert] - block_start, 0, EXPERT_BLOCK)
    every = jnp.arange(n_slots, dtype=jnp.int32)
    valid = every % EXPERT_BLOCK < jnp.repeat(n_valid, EXPERT_BLOCK)
    source = jnp.where(valid, table, every) % n_tok
    return block_expert, source, slot.reshape(n_assign)


def _combine(h_ref, y0_ref, y1_ref, gate_ref):
    gate = gate_ref[...]
    return h_ref[...] + (gate[:, 0:1] * y0_ref[...] + gate[:, 1:2] * y1_ref[...])


def _emit_head(r, h, outs, scr):
    tm = r.shape[0]
    if scr is not None:
        scr[h] = r
    for ref, dil in outs:
        if dil == 0:
            ref[:, h, :] = r
        elif dil == 1:
            ref[:, _chunk(h)] = r.astype(ref.dtype)
        else:
            for res in range(dil):
                col = res * D_MODEL + h * HEAD_DIM
                ref[:, col:col + HEAD_DIM] = scr[h, pl.ds(res, tm // dil, stride=dil), :].astype(ref.dtype)


def _rope(xh, cos, sin_signed):
    return xh * cos + pltpu.roll(xh, HEAD_DIM // 2, axis=1) * sin_signed


def _qkv_kernel(*refs, streams, per_seq, kept):
    (h_ref, y0_ref, y1_ref, gate_ref, gmix_ref, gkv_ref, wq_ref, wkv_ref, cos_ref, sin_ref) = refs[:10]
    outs = refs[10:]
    if streams:
        (h2_ref, q0_ref, q1_ref, q2_ref, k0_ref, k1_ref, k2_ref, v0_ref, v1_ref, v2_ref, kf_ref, vf_ref,
         scr_q1, scr_q2, scr_k, scr_v) = outs
        q_outs = [([(q0_ref, 1)], None), ([(q1_ref, DILATIONS[1])], scr_q1), ([(q2_ref, DILATIONS[2])], scr_q2)]
        k_outs = ([(k0_ref, 1), (k1_ref, DILATIONS[1]), (k2_ref, DILATIONS[2])], scr_k)
        v_outs = ([(v0_ref, 1), (v1_ref, DILATIONS[1]), (v2_ref, DILATIONS[2])], scr_v)
    else:
        h2_ref, q0_ref, q1_ref, q2_ref, kf_ref, vf_ref = outs
        q_outs = [([(q0_ref, 1)], None), ([(q1_ref, 1)], None), ([(q2_ref, 1)], None)]
        k_outs = ([(kf_ref, 0)], None)
        v_outs = ([(vf_ref, 0)], None)

    h2 = _combine(h_ref, y0_ref, y1_ref, gate_ref)
    h2_ref[...] = h2
    cos = cos_ref[...]
    sin = sin_ref[...]
    xn = _rms(h2, gmix_ref[...]).astype(_BF16)
    scale = HEAD_DIM ** -0.5
    for g in range(N_GROUPS):
        q = _dot(xn, wq_ref[:, g * D_MODEL:(g + 1) * D_MODEL])
        for h in range(N_HEADS):
            _emit_head(_rope(q[:, _chunk(h)], cos, sin) * scale, h, *q_outs[g])
    xkv = _rms(h2, gkv_ref[...]).astype(_BF16)
    k = _dot(xkv, wkv_ref[:, 0:D_MODEL])
    for h in range(N_HEADS):
        _emit_head(_rope(k[:, _chunk(h)], cos, sin), h, *k_outs)
    v = _dot(xkv, wkv_ref[:, D_MODEL:2 * D_MODEL])
    for h in range(N_HEADS):
        _emit_head(v[:, _chunk(h)], h, *v_outs)
    if streams:
        @pl.when(pl.program_id(0) % per_seq >= per_seq - kept)
        def _():
            for h in range(N_HEADS):
                kf_ref[:, h, :] = scr_k[h]
                vf_ref[:, h, :] = scr_v[h]


def _qkv(h, ypair, gate, gmix, gkv, wq, wkv, cos, sin, *, tm, streams, seq, keep):
    t, d = h.shape
    nt = t // tm
    per_seq = seq // tm
    kept = keep // tm
    assert seq % tm == 0 and keep % tm == 0

    def tail(i):
        return ((i // per_seq) * kept + jnp.maximum(i % per_seq - (per_seq - kept), 0), 0, 0)

    tail_f32 = (pl.BlockSpec((tm, N_HEADS, HEAD_DIM), tail),
                jax.ShapeDtypeStruct((t // seq * keep, N_HEADS, HEAD_DIM), _F32))
    row = lambda i: (i, 0)
    const = lambda i: (0, 0)
    nat = lambda dt: (pl.BlockSpec((tm, d), row), jax.ShapeDtypeStruct((t, d), dt))

    def stream(dil):
        return (pl.BlockSpec((tm // dil, dil * d), row), jax.ShapeDtypeStruct((t // dil, dil * d), _BF16))

    if streams:
        d1, d2 = DILATIONS[1], DILATIONS[2]
        outs = [nat(_F32), nat(_BF16), stream(d1), stream(d2), nat(_BF16), stream(d1), stream(d2),
                nat(_BF16), stream(d1), stream(d2), tail_f32, tail_f32]
        scratch = [pltpu.VMEM((N_HEADS, tm, HEAD_DIM), _F32)] * 4
    else:
        outs = [nat(_F32), nat(_BF16), nat(_BF16), nat(_BF16), tail_f32, tail_f32]
        scratch = []
    return pl.pallas_call(
        functools.partial(_qkv_kernel, streams=streams, per_seq=per_seq, kept=kept),
        grid=(nt,),
        in_specs=[
            pl.BlockSpec((tm, d), row),
            pl.BlockSpec((tm, d), row),
            pl.BlockSpec((tm, d), lambda i: (nt + i, 0)),
            pl.BlockSpec((tm, 2), row),
            pl.BlockSpec((1, d), const),
            pl.BlockSpec((1, d), const),
            pl.BlockSpec((d, 3 * d), const),
            pl.BlockSpec((d, 2 * d), const),
            pl.BlockSpec((tm, HEAD_DIM), row),
            pl.BlockSpec((tm, HEAD_DIM), row),
        ],
        out_specs=[o[0] for o in outs],
        out_shape=[o[1] for o in outs],
        scratch_shapes=scratch,
        compiler_params=pltpu.CompilerParams(
            dimension_semantics=("arbitrary",), vmem_limit_bytes=VMEM_LIMIT),
        name="qkv_proj_streams" if streams else "qkv_proj",
    )(h, ypair, ypair, gate, gmix, gkv, wq, wkv, cos, sin)


ATTN_TQ = 512


def _attn_prompt_kernel(q_ref, kp_ref, kc_ref, vp_ref, vc_ref, o_ref, st_ref, *, tq):
    i = pl.program_id(1)
    qi = lax.broadcasted_iota(jnp.int32, (SPAN, 2 * SPAN), 0)
    kj = lax.broadcasted_iota(jnp.int32, (SPAN, 2 * SPAN), 1)
    band = (kj >= qi) & (kj <= qi + SPAN)
    band0 = band & (kj >= jnp.where(i > 0, 0, SPAN))
    lane = lax.broadcasted_iota(jnp.int32, (SPAN, HEAD_DIM), 1)
    ones = jnp.ones((2 * SPAN, HEAD_DIM), _BF16)
    for a in range(tq // SPAN):
        rows = slice(a * SPAN, (a + 1) * SPAN)
        st = jnp.zeros((SPAN, HEAD_DIM), _F32)
        for h in range(N_HEADS):
            hs = _chunk(h)
            if a == 0:
                keys = jnp.concatenate([kp_ref[:, hs], kc_ref[rows, hs]], axis=0)
                vals = jnp.concatenate([vp_ref[:, hs], vc_ref[rows, hs]], axis=0)
                mask = band0
            else:
                both = slice((a - 1) * SPAN, (a + 1) * SPAN)
                keys, vals, mask = kc_ref[both, hs], vc_ref[both, hs], band
            s = jnp.where(mask, _dot_nt(q_ref[rows, hs], keys), NEG_BIG)
            m = jnp.max(s, axis=1, keepdims=True)
            p = jnp.exp(s - m).astype(_BF16)
            ol = _dot(p, jnp.concatenate([vals, ones], axis=1))
            l = ol[:, HEAD_DIM:]
            o_ref[rows, hs] = (ol[:, :HEAD_DIM] / l).astype(o_ref.dtype)
            st = jnp.where(lane == h, m + jnp.log(l), st)
        st_ref[rows, :] = st


def _attn_prompt(q, k, v, batch, dilation):
    rows, width = q.shape
    d = width // dilation
    per_batch = rows // batch
    tq = min(ATTN_TQ, per_batch)
    assert per_batch % tq == 0 and tq % SPAN == 0
    nq = per_batch // tq
    sub = tq // SPAN
    cur = lambda b, i, r: (b * nq + i, r)
    prev = lambda b, i, r: ((b * nq + i) * sub - jnp.where(i > 0, 1, 0), r)
    big = (tq, d)
    small = (SPAN, d)
    return pl.pallas_call(
        functools.partial(_attn_prompt_kernel, tq=tq),
        grid=(batch, nq, dilation),
        in_specs=[pl.BlockSpec(big, cur), pl.BlockSpec(small, prev), pl.BlockSpec(big, cur),
                  pl.BlockSpec(small, prev), pl.BlockSpec(big, cur)],
        out_specs=[pl.BlockSpec(big, cur), pl.BlockSpec((tq, HEAD_DIM), cur)],
        out_shape=[jax.ShapeDtypeStruct((rows, width), _BF16),
                   jax.ShapeDtypeStruct((rows, dilation * HEAD_DIM), _F32)],
        compiler_params=pltpu.CompilerParams(
            dimension_semantics=("arbitrary", "arbitrary", "arbitrary"), vmem_limit_bytes=VMEM_LIMIT),
        name=f"attn_prompt_d{dilation}",
    )(q, k, k, v, v)


N_NEW = 8
CACHE_A_GROUPS = (PAST_LEN - WINDOWS[1]) // DILATIONS[2]
CACHE_A_ROWS = CACHE_A_GROUPS * N_NEW
CACHE_B_ROWS = WINDOWS[1]
CACHE_B_GROUPS = CACHE_B_ROWS // DILATIONS[2]
KEYS_REAL = CACHE_A_ROWS + CACHE_B_ROWS + N_NEW
KEYS_PAD = -(-KEYS_REAL // 128) * 128


def _sample_key_positions():
    pos = np.full((KEYS_PAD,), -1, np.int64)
    a = np.arange(CACHE_A_ROWS)
    pos[:CACHE_A_ROWS] = (a // N_NEW) * DILATIONS[2] + a % N_NEW
    pos[CACHE_A_ROWS:CACHE_A_ROWS + CACHE_B_ROWS] = PAST_LEN - CACHE_B_ROWS + np.arange(CACHE_B_ROWS)
    pos[CACHE_A_ROWS + CACHE_B_ROWS:KEYS_REAL] = PAST_LEN + np.arange(N_NEW)
    return pos


def _sample_bias():
    pos = _sample_key_positions()
    bias = np.full((N_GROUPS * N_NEW, KEYS_PAD), NEG_BIG, np.float32)
    for g in range(N_GROUPS):
        for n in range(N_NEW):
            delta = PAST_LEN + n - pos
            ok = (pos >= 0) & (delta >= 0) & (delta <= WINDOWS[g]) & (delta % DILATIONS[g] == 0)
            assert int(ok.sum()) == SPAN + 1
            bias[g * N_NEW + n, ok] = 0.0
    return bias


def _attn_sample_kernel(q0_ref, q1_ref, q2_ref, kn_ref, vn_ref, bias_ref, ck_hbm, cv_hbm,
                        o_ref, ka, kb, va, vb, kall, vall, sem, *, n_seq):
    b = pl.program_id(0)
    slot = b % 2

    def cache_copies(seq, s):
        cps = []
        for h in range(N_HEADS):
            for src, dst_a, dst_b in ((ck_hbm, ka, kb), (cv_hbm, va, vb)):
                cps.append(pltpu.make_async_copy(
                    src.at[seq, pl.ds(0, CACHE_A_GROUPS), pl.ds(0, N_NEW), h], dst_a.at[s, h], sem.at[s]))
                cps.append(pltpu.make_async_copy(
                    src.at[seq, pl.ds(CACHE_A_GROUPS, CACHE_B_GROUPS), :, h], dst_b.at[s, h], sem.at[s]))
        return cps

    @pl.when(b == 0)
    def _():
        kall[KEYS_REAL:KEYS_PAD, :] = jnp.zeros((KEYS_PAD - KEYS_REAL, D_MODEL), _BF16)
        vall[KEYS_REAL:KEYS_PAD, :] = jnp.zeros((KEYS_PAD - KEYS_REAL, D_MODEL), _BF16)
        for cp in cache_copies(0, 0):
            cp.start()

    @pl.when(b + 1 < n_seq)
    def _():
        for cp in cache_copies(b + 1, 1 - slot):
            cp.start()

    for cp in cache_copies(b, slot):
        cp.wait()

    nb0 = CACHE_A_ROWS + CACHE_B_ROWS
    for src_a, src_b, src_n, dst in ((ka, kb, kn_ref, kall), (va, vb, vn_ref, vall)):
        for h in range(N_HEADS):
            hs = _chunk(h)
            dst[0:CACHE_A_ROWS, hs] = src_a[slot, h].reshape(CACHE_A_ROWS, HEAD_DIM).astype(_BF16)
            dst[CACHE_A_ROWS:nb0, hs] = src_b[slot, h].reshape(CACHE_B_ROWS, HEAD_DIM).astype(_BF16)
            dst[nb0:KEYS_REAL, hs] = src_n[0, :, h, :].astype(_BF16)

    bias = bias_ref[...]
    for h in range(N_HEADS):
        hs = _chunk(h)
        qh = jnp.concatenate([q0_ref[0, :, hs].astype(_F32), q1_ref[0, :, hs].astype(_F32),
                              q2_ref[0, :, hs].astype(_F32)], axis=0).astype(_BF16)
        s = _dot_nt(qh, kall[:, hs]) + bias
        m = jnp.max(s, axis=1, keepdims=True)
        m8 = jnp.maximum(jnp.maximum(m[0:N_NEW], m[N_NEW:2 * N_NEW]), m[2 * N_NEW:3 * N_NEW])
        p = jnp.exp(s - jnp.concatenate([m8, m8, m8], axis=0))
        p8 = p[0:N_NEW] + p[N_NEW:2 * N_NEW] + p[2 * N_NEW:3 * N_NEW]
        l8 = jnp.sum(p8, axis=1, keepdims=True)
        o_ref[0, :, hs] = _dot(p8.astype(_BF16), vall[:, hs]) / l8


def _attn_sample(q0, q1, q2, cache_k, cache_v, k_new, v_new):
    nb, n_new, d = q0.shape
    past = cache_k.shape[1]
    assert past == PAST_LEN and n_new == N_NEW and d == D_MODEL
    assert cache_k.shape[2:] == (N_HEADS, HEAD_DIM)
    ck = cache_k.reshape(nb, past // DILATIONS[2], DILATIONS[2], N_HEADS, HEAD_DIM)
    cv = cache_v.reshape(nb, past // DILATIONS[2], DILATIONS[2], N_HEADS, HEAD_DIM)
    bias = jnp.asarray(_sample_bias())
    new = pl.BlockSpec((1, n_new, d), lambda b: (b, 0, 0))
    new_kv = pl.BlockSpec((1, n_new, N_HEADS, HEAD_DIM), lambda b: (b, 0, 0, 0))
    part_a = (2, N_HEADS, CACHE_A_GROUPS, N_NEW, HEAD_DIM)
    part_b = (2, N_HEADS, CACHE_B_GROUPS, DILATIONS[2], HEAD_DIM)
    return pl.pallas_call(
        functools.partial(_attn_sample_kernel, n_seq=nb),
        grid=(nb,),
        in_specs=[new, new, new, new_kv, new_kv,
                  pl.BlockSpec((N_GROUPS * N_NEW, KEYS_PAD), lambda b: (0, 0)),
                  pl.BlockSpec(memory_space=pl.ANY), pl.BlockSpec(memory_space=pl.ANY)],
        out_specs=new,
        out_shape=jax.ShapeDtypeStruct((nb, n_new, d), _F32),
        scratch_shapes=[pltpu.VMEM(part_a, _F32), pltpu.VMEM(part_b, _F32),
                        pltpu.VMEM(part_a, _F32), pltpu.VMEM(part_b, _F32),
                        pltpu.VMEM((KEYS_PAD, d), _BF16), pltpu.VMEM((KEYS_PAD, d), _BF16),
                        pltpu.SemaphoreType.DMA((2,))],
        compiler_params=pltpu.CompilerParams(
            dimension_semantics=("arbitrary",), vmem_limit_bytes=VMEM_LIMIT),
        name="attn_sample",
    )(q0, q1, q2, k_new, v_new, bias, ck, cv)


def _attn_out_kernel(*refs, n_groups, tm):
    if n_groups == 1:
        (o_ref, h_ref, wo_ref, gffn_ref, wr_ref, br_ref,
         h3_ref, xn2_ref, eid_ref, gate_ref) = refs
        o = o_ref[...].astype(_BF16)
    else:
        (o0_ref, o1_ref, o2_ref, s0_ref, s1_ref, s2_ref, h_ref, wo_ref, gffn_ref, wr_ref, br_ref,
         h3_ref, xn2_ref, eid_ref, gate_ref, obuf, scr_o1, scr_o2, scr_s1, scr_s2) = refs
        for o_ref, s_ref, scr_o, scr_s, dil in ((o1_ref, s1_ref, scr_o1, scr_s1, DILATIONS[1]),
                                                (o2_ref, s2_ref, scr_o2, scr_s2, DILATIONS[2])):
            for res in range(dil):
                dst = pl.ds(res, tm // dil, stride=dil)
                scr_s[dst, :] = s_ref[:, res * HEAD_DIM:(res + 1) * HEAD_DIM]
                for h in range(N_HEADS):
                    col = res * D_MODEL + h * HEAD_DIM
                    scr_o[h, dst, :] = o_ref[:, col:col + HEAD_DIM].astype(_F32)
        sts = [s0_ref[...], scr_s1[...], scr_s2[...]]
        mx = jnp.maximum(jnp.maximum(sts[0], sts[1]), sts[2])
        es = [jnp.exp(s - mx) for s in sts]
        den = es[0] + es[1] + es[2]
        ws = [e / den for e in es]
        for h in range(N_HEADS):
            acc = ws[0][:, h:h + 1] * o0_ref[:, _chunk(h)]
            acc = acc + ws[1][:, h:h + 1] * scr_o1[h]
            acc = acc + ws[2][:, h:h + 1] * scr_o2[h]
            obuf[:, _chunk(h)] = acc.astype(_BF16)
        o = obuf[...]
    h3 = h_ref[...] + _dot(o, wo_ref[...])
    h3_ref[...] = h3
    xn2 = _rms(h3, gffn_ref[...])
    xn2_ref[...] = xn2
    _route(xn2, wr_ref, br_ref, eid_ref, gate_ref)


def _attn_out(os, sts, h, wo, gffn, wr, br, *, tm):
    t, d = h.shape
    n_groups = len(os)
    row = lambda i: (i, 0)
    const = lambda i: (0, 0)
    if n_groups == 1:
        in_specs = [pl.BlockSpec((tm, d), row)]
        scratch = []
        args = [os[0]]
    else:
        in_specs = [pl.BlockSpec((tm // dil, dil * d), row) for dil in DILATIONS]
        in_specs += [pl.BlockSpec((tm // dil, dil * HEAD_DIM), row) for dil in DILATIONS]
        scratch = [pltpu.VMEM((tm, d), _BF16),
                   pltpu.VMEM((N_HEADS, tm, HEAD_DIM), _F32), pltpu.VMEM((N_HEADS, tm, HEAD_DIM), _F32),
                   pltpu.VMEM((tm, HEAD_DIM), _F32), pltpu.VMEM((tm, HEAD_DIM), _F32)]
        args = list(os) + list(sts)
    in_specs += [pl.BlockSpec((tm, d), row), pl.BlockSpec((d, d), const), pl.BlockSpec((1, d), const),
                 pl.BlockSpec((ROUTER_ROWS, d), const), pl.BlockSpec((ROUTER_ROWS, 1), const)]
    return pl.pallas_call(
        functools.partial(_attn_out_kernel, n_groups=n_groups, tm=tm),
        grid=(t // tm,),
        in_specs=in_specs,
        out_specs=[pl.BlockSpec((tm, d), row), pl.BlockSpec((tm, d), row),
                   pl.BlockSpec((2, tm), lambda i: (0, i)), pl.BlockSpec((2, tm), lambda i: (0, i))],
        out_shape=[jax.ShapeDtypeStruct((t, d), _F32), jax.ShapeDtypeStruct((t, d), _F32),
                   jax.ShapeDtypeStruct((2, t), jnp.int32), jax.ShapeDtypeStruct((2, t), _F32)],
        scratch_shapes=scratch,
        compiler_params=pltpu.CompilerParams(
            dimension_semantics=("arbitrary",), vmem_limit_bytes=VMEM_LIMIT),
        name=f"attn_out_g{n_groups}",
    )(*args, h, wo, gffn, wr, br)


def _final_kernel(h_ref, y0_ref, y1_ref, gate_ref, g_ref, out_ref):
    out_ref[...] = _rms(_combine(h_ref, y0_ref, y1_ref, gate_ref), g_ref[...])


def _final(h, ypair, gate, g, *, tm):
    t, d = h.shape
    nt = t // tm
    row = lambda i: (i, 0)
    return pl.pallas_call(
        _final_kernel,
        grid=(nt,),
        in_specs=[pl.BlockSpec((tm, d), row),
                  pl.BlockSpec((tm, d), row),
                  pl.BlockSpec((tm, d), lambda i: (nt + i, 0)),
                  pl.BlockSpec((tm, 2), row),
                  pl.BlockSpec((1, d), lambda i: (0, 0))],
        out_specs=pl.BlockSpec((tm, d), row),
        out_shape=jax.ShapeDtypeStruct((t, d), _F32),
        compiler_params=pltpu.CompilerParams(
            dimension_semantics=("arbitrary",), vmem_limit_bytes=VMEM_LIMIT),
        name="final_norm",
    )(h, ypair, ypair, gate, g)


def _rope_tables(pos):
    half = HEAD_DIM // 2
    inv_freq = jnp.power(jnp.float32(ROPE_THETA), -jnp.arange(half, dtype=jnp.float32) / half)
    ang = pos.astype(jnp.float32)[:, None] * inv_freq[None, :]
    cos = jnp.cos(ang)
    sin = jnp.sin(ang)
    return jnp.concatenate([cos, cos], axis=-1), jnp.concatenate([-sin, sin], axis=-1)


def _router_params(wg, bg, we, be):
    wr = jnp.zeros((ROUTER_ROWS, D_MODEL), _F32)
    wr = wr.at[0:MOE_GROUPS].set(wg.T).at[EXPERT_ROW0:EXPERT_ROW0 + N_EXPERTS].set(we.T)
    br = jnp.zeros((ROUTER_ROWS, 1), _F32)
    br = br.at[0:MOE_GROUPS, 0].set(bg).at[EXPERT_ROW0:EXPERT_ROW0 + N_EXPERTS, 0].set(be)
    return wr, br


def _moe(xn2, eid, experts, layer):
    block_expert, source, slot = _moe_plan(eid, xn2.shape[0])
    xs = _row_gather(xn2, source)
    yield
    ys = _moe_ffn(block_expert, xs, *experts, layer)
    ypair = _row_gather(ys, slot)
    yield
    return ypair


def _forward(x3, prev3, pos_rows, attend, p, *, shift, tm, streams, keep):
    h1, xn2, eid, gate, state = _conv_layer(
        x3, prev3, p['gmix'][0], p['win'], p['ck'], p['wout'], p['gffn'][0], *p['router'][0],
        shift=shift, tm=tm)
    t = x3.shape[0] * x3.shape[1]
    ypair = yield from _moe(xn2, eid, p['experts'], 0)
    cos, sin = _rope_tables(pos_rows)
    h2, *qkv, kf, vf = _qkv(h1.reshape(t, D_MODEL), ypair, gate.T, p['gmix'][1], p['gkv'],
                            p['wq'], p['wkv'], cos, sin, tm=256, streams=streams,
                            seq=x3.shape[1], keep=keep)
    os, sts = attend(qkv, kf, vf)
    h3, xn2, eid, gate = _attn_out(os, sts, h2, p['wo'], p['gffn'][1], *p['router'][1], tm=256)
    ypair = yield from _moe(xn2, eid, p['experts'], 1)
    y = _final(h3, ypair, gate.T, p['gfinal'], tm=512)
    return y, state, kf, vf


def _interleave(*paths):
    results = [None] * len(paths)
    live = list(enumerate(paths))
    while live:
        still = []
        for i, gen in live:
            try:
                next(gen)
                still.append((i, gen))
            except StopIteration as done:
                results[i] = done.value
        live = still
    return results


def kernel(x_prompt, x_sample, cache_k, cache_v, state_conv, norm_mix, norm_ffn, norm_kv, norm_final,
           conv_w_in, conv_kernel, conv_w_out, attn_w_q, attn_w_kv, attn_w_o, router_group_w,
           router_group_b, router_expert_w, router_expert_b, expert_w1, expert_w3, expert_w2):
    b_p, s_p, d = x_prompt.shape
    b_s, n_new, _ = x_sample.shape
    assert d == D_MODEL and n_new == N_NEW and s_p % (DILATIONS[2] * SPAN) == 0
    assert norm_mix.shape[0] == 2 and conv_w_in.shape[0] == 1 and attn_w_q.shape[0] == 1
    assert cache_k.shape[1] == PAST_LEN

    p = {
        'gmix': [norm_mix[l].reshape(1, d) for l in range(2)],
        'gffn': [norm_ffn[l].reshape(1, d) for l in range(2)],
        'gkv': norm_kv.reshape(1, d),
        'gfinal': norm_final.reshape(1, d),
        'win': conv_w_in[0].astype(_BF16),
        'ck': conv_kernel[0],
        'wout': conv_w_out[0].astype(_BF16),
        'wq': attn_w_q[0].astype(_BF16),
        'wkv': attn_w_kv.astype(_BF16),
        'wo': attn_w_o[0].astype(_BF16),
        'router': [_router_params(router_group_w[l], router_group_b[l], router_expert_w[l], router_expert_b[l])
                   for l in range(2)],
        'experts': (expert_w1, expert_w3, expert_w2),
    }

    def attend_prompt(qkv, kf, vf):
        q0, q1, q2, k0, k1, k2, v0, v1, v2 = qkv
        os, sts = [], []
        for q, k, v, dil in ((q0, k0, v0, DILATIONS[0]), (q1, k1, v1, DILATIONS[1]), (q2, k2, v2, DILATIONS[2])):
            o, st = _attn_prompt(q, k, v, b_p, dil)
            os.append(o)
            sts.append(st)
        return os, sts

    pos_p = jnp.tile(jnp.arange(s_p, dtype=jnp.int32), b_p)
    zero_state = jnp.zeros((b_p, CONV_WIDTH - 1, d), x_prompt.dtype)
    keep = min(max(WINDOWS), s_p)
    prompt_path = _forward(x_prompt, zero_state, pos_p, attend_prompt, p, shift=1, tm=512,
                           streams=True, keep=keep)

    halves = 2
    bh = b_s // halves

    def to_rows(a):
        w = a.shape[-1]
        return a.reshape(halves, bh, n_new, w).transpose(0, 2, 1, 3).reshape(halves * n_new * bh, w)

    def to_batch(a):
        w = a.shape[1:]
        return jnp.swapaxes(a.reshape(halves, n_new, bh, *w), 1, 2).reshape(b_s, n_new, *w)

    def attend_sample(qkv, kf, vf):
        q0, q1, q2 = qkv
        o = _attn_sample(to_batch(q0), to_batch(q1), to_batch(q2), cache_k, cache_v,
                         to_batch(kf), to_batch(vf))
        return [to_rows(o)], None

    x_s = to_rows(x_sample).reshape(halves, n_new * bh, d)
    prev_s = state_conv[0].reshape(halves, bh, CONV_WIDTH - 1, d).transpose(0, 2, 1, 3).reshape(
        halves, (CONV_WIDTH - 1) * bh, d)
    pos_s = jnp.tile(jnp.repeat(PAST_LEN + jnp.arange(n_new, dtype=jnp.int32), bh), halves)
    sample_path = _forward(x_s, prev_s, pos_s, attend_sample, p, shift=bh, tm=n_new * bh,
                           streams=False, keep=n_new * bh)

    (y_p, st_p, kf_p, vf_p), (y_s, st_s, kf_s, vf_s) = _interleave(prompt_path, sample_path)
    y_prompt = y_p.reshape(b_p, s_p, d)
    k_p = kf_p.reshape(b_p, keep, N_HEADS, HEAD_DIM)
    v_p = vf_p.reshape(b_p, keep, N_HEADS, HEAD_DIM)
    conv_p = st_p[None]
    y_sample = to_batch(y_s)
    k_s = to_batch(kf_s)
    v_s = to_batch(vf_s)
    conv_s = st_s.reshape(halves, CONV_WIDTH - 1, bh, d).transpose(0, 2, 1, 3).reshape(
        b_s, CONV_WIDTH - 1, d)[None]

    return (y_prompt, y_sample, k_p, v_p, conv_p, k_s, v_s, conv_s)
```

```python
import functools

import numpy as np
import jax
import jax.numpy as jnp
from jax import lax
from jax.experimental import pallas as pl
from jax.experimental.pallas import tpu as pltpu
from jax.experimental.pallas import tpu_sc as plsc

D_MODEL = 1024
CONV_WIDTH = 3
WINDOWS = (128, 512, 2048)
DILATIONS = (1, 4, 16)
N_GROUPS = 3
N_HEADS = 8
HEAD_DIM = 128
SPAN = 128
ROPE_THETA = 10000.0
MOE_GROUPS = 4
EXPERTS_PER_GROUP = 4
N_EXPERTS = 16
D_EXPERT = 512
EXPERT_BLOCK = 256
RMS_EPS = 1e-6
NEG_BIG = -1e30
PAST_LEN = 2048

LANES = 128
ROUTER_ROWS = 32
EXPERT_ROW0 = 8
VMEM_LIMIT = 56 * 1024 * 1024

_F32 = jnp.float32
_BF16 = jnp.bfloat16


def _rms(x, g):
    ms = jnp.mean(x * x, axis=-1, keepdims=True)
    return (x * lax.rsqrt(ms + RMS_EPS)) * g


def _dot(a, b):
    return jnp.dot(a, b, preferred_element_type=_F32)


def _dot_nt(a, b, precision=None):
    return lax.dot_general(a, b, (((1,), (1,)), ((), ())), precision=precision,
                           preferred_element_type=_F32)


def _chunk(c):
    return slice(c * LANES, (c + 1) * LANES)


def _route(xn, wr_ref, br_ref, eid_ref, gate_ref):
    logits = _dot_nt(wr_ref[...], xn, precision=lax.Precision.HIGHEST) + br_ref[...]
    lg = logits[0:MOE_GROUPS]
    row = lax.broadcasted_iota(jnp.int32, lg.shape, 0).astype(_F32)
    eg = jnp.exp(lg - jnp.max(lg, axis=0, keepdims=True))
    pg = eg / jnp.sum(eg, axis=0, keepdims=True)
    gp = jnp.max(pg, axis=0, keepdims=True)
    gi = jnp.min(jnp.where(pg == gp, row, float(MOE_GROUPS)), axis=0, keepdims=True)
    le = jnp.zeros_like(lg)
    for g in range(MOE_GROUPS):
        r0 = EXPERT_ROW0 + g * EXPERTS_PER_GROUP
        le = le + jnp.where(gi == float(g), logits[r0:r0 + EXPERTS_PER_GROUP], 0.0)
    ee = jnp.exp(le - jnp.max(le, axis=0, keepdims=True))
    ev = ee / jnp.sum(ee, axis=0, keepdims=True)
    v1 = jnp.max(ev, axis=0, keepdims=True)
    i1 = jnp.min(jnp.where(ev == v1, row, float(EXPERTS_PER_GROUP)), axis=0, keepdims=True)
    ev2 = jnp.where(row == i1, -1.0, ev)
    v2 = jnp.max(ev2, axis=0, keepdims=True)
    i2 = jnp.min(jnp.where(ev2 == v2, row, float(EXPERTS_PER_GROUP)), axis=0, keepdims=True)
    den = v1 + v2
    ids = jnp.concatenate([gi * EXPERTS_PER_GROUP + i1, gi * EXPERTS_PER_GROUP + i2], axis=0)
    eid_ref[...] = ids.astype(jnp.int32)
    gate_ref[...] = jnp.concatenate([gp * v1 / den, gp * v2 / den], axis=0)


def _conv_layer_kernel(x_ref, prev_ref, gmix_ref, win_ref, ck_ref, wout_ref, gffn_ref, wr_ref, br_ref,
                       h_ref, xn2_ref, eid_ref, gate_ref, st_ref, ubuf, *, shift, tm):
    i = pl.program_id(1)
    halo = ubuf.shape[0] - tm
    keep = (CONV_WIDTH - 1) * shift

    @pl.when(i == 0)
    def _():
        ubuf[halo - keep:halo, :] = prev_ref[0]

    @pl.when(i > 0)
    def _():
        ubuf[halo - keep:halo, :] = ubuf[halo + tm - keep:halo + tm, :]

    x = x_ref[0]
    xn = _rms(x, gmix_ref[...]).astype(_BF16)
    c_gate = _dot(xn, win_ref[:, 0:D_MODEL])
    hid = _dot(xn, win_ref[:, 2 * D_MODEL:3 * D_MODEL])
    u = c_gate * hid
    ubuf[halo:halo + tm, :] = u
    u1 = ubuf[halo - shift:halo - shift + tm, :]
    u2 = ubuf[halo - 2 * shift:halo - 2 * shift + tm, :]
    conv = ck_ref[0:1, :] * u2 + ck_ref[1:2, :] * u1 + ck_ref[2:3, :] * u
    b_gate = _dot(xn, win_ref[:, D_MODEL:2 * D_MODEL])
    y = _dot((b_gate * conv).astype(_BF16), wout_ref[...])
    h = x + y
    h_ref[0] = h
    st_ref[0] = ubuf[halo + tm - keep:halo + tm, :]
    xn2 = _rms(h, gffn_ref[...])
    xn2_ref[...] = xn2
    _route(xn2, wr_ref, br_ref, eid_ref, gate_ref)


def _conv_layer(x3, prev3, gmix, win, ck, wout, gffn, wr, br, *, shift, tm):
    nb, s, d = x3.shape
    nt = s // tm
    keep = (CONV_WIDTH - 1) * shift
    halo = -(-keep // 8) * 8
    t_all = nb * s
    const = lambda b, i: (0, 0)
    return pl.pallas_call(
        functools.partial(_conv_layer_kernel, shift=shift, tm=tm),
        grid=(nb, nt),
        in_specs=[
            pl.BlockSpec((1, tm, d), lambda b, i: (b, i, 0)),
            pl.BlockSpec((1, keep, d), lambda b, i: (b, 0, 0)),
            pl.BlockSpec((1, d), const),
            pl.BlockSpec((d, 3 * d), const),
            pl.BlockSpec((CONV_WIDTH, d), const),
            pl.BlockSpec((d, d), const),
            pl.BlockSpec((1, d), const),
            pl.BlockSpec((ROUTER_ROWS, d), const),
            pl.BlockSpec((ROUTER_ROWS, 1), const),
        ],
        out_specs=[
            pl.BlockSpec((1, tm, d), lambda b, i: (b, i, 0)),
            pl.BlockSpec((tm, d), lambda b, i: (b * nt + i, 0)),
            pl.BlockSpec((2, tm), lambda b, i: (0, b * nt + i)),
            pl.BlockSpec((2, tm), lambda b, i: (0, b * nt + i)),
            pl.BlockSpec((1, keep, d), lambda b, i: (b, 0, 0)),
        ],
        out_shape=[
            jax.ShapeDtypeStruct((nb, s, d), _F32),
            jax.ShapeDtypeStruct((t_all, d), _F32),
            jax.ShapeDtypeStruct((2, t_all), jnp.int32),
            jax.ShapeDtypeStruct((2, t_all), _F32),
            jax.ShapeDtypeStruct((nb, keep, d), _F32),
        ],
        scratch_shapes=[pltpu.VMEM((halo + tm, d), _F32)],
        compiler_params=pltpu.CompilerParams(
            dimension_semantics=("arbitrary", "arbitrary"), vmem_limit_bytes=VMEM_LIMIT),
        name="conv_layer",
    )(x3, prev3, gmix, win, ck, wout, gffn, wr, br)


SC_WINDOW = 32


def _row_gather(src, idx):
    n_out = idx.shape[0]
    mesh = plsc.VectorSubcoreMesh(core_axis_name="c", subcore_axis_name="s")
    n_cores = mesh.num_cores
    per_worker = n_out // (n_cores * mesh.num_subcores)
    n_win = per_worker // SC_WINDOW
    assert per_worker * n_cores * mesh.num_subcores == n_out and n_win * SC_WINDOW == per_worker
    assert n_win % 2 == 0

    @pl.kernel(out_type=jax.ShapeDtypeStruct((n_out,) + src.shape[1:], src.dtype), mesh=mesh,
               scratch_types=[pltpu.VMEM((per_worker,), jnp.int32),
                              pltpu.VMEM((2, SC_WINDOW) + src.shape[1:], src.dtype),
                              pltpu.SemaphoreType.DMA((2,)), pltpu.SemaphoreType.DMA((2,))])
    def gather(src_hbm, idx_hbm, out_hbm, idx_v, rows_v, gsem, wsem):
        base = (lax.axis_index("s") * n_cores + lax.axis_index("c")) * per_worker
        pltpu.sync_copy(idx_hbm.at[pl.ds(base, per_worker)], idx_v)

        def gather_copy(q, b):
            return pltpu.make_async_copy(src_hbm.at[idx_v.at[pl.ds(q * SC_WINDOW, SC_WINDOW)]],
                                         rows_v.at[b], gsem.at[b])

        def write_copy(q, b):
            return pltpu.make_async_copy(rows_v.at[b], out_hbm.at[pl.ds(base + q * SC_WINDOW, SC_WINDOW)],
                                         wsem.at[b])

        gather_copy(0, 0).start()

        @pl.loop(0, n_win, step=2)
        def _(q0):
            for b in range(2):
                q = q0 + b
                gather_copy(q, b).wait()

                @pl.when(q + 1 < n_win)
                def _():
                    @pl.when(q >= 1)
                    def _():
                        write_copy(q - 1, 1 - b).wait()
                    gather_copy(q + 1, 1 - b).start()

                write_copy(q, b).start()

        write_copy(n_win - 2, 0).wait()
        write_copy(n_win - 1, 1).wait()

    return gather(src, idx)


def _moe_ffn_kernel(be_ref, x_ref, w1_ref, w3_ref, w2_ref, y_ref, w1b, w3b, w2b):
    j = pl.program_id(0)

    @pl.when((j == 0) | (be_ref[j] != be_ref[jnp.maximum(j - 1, 0)]))
    def _():
        w1b[...] = w1_ref[0, 0].astype(_BF16)
        w3b[...] = w3_ref[0, 0].astype(_BF16)
        w2b[...] = w2_ref[0, 0].astype(_BF16)

    x = x_ref[...].astype(_BF16)
    a = _dot(x, w1b[...])
    b = _dot(x, w3b[...])
    y_ref[...] = _dot((jax.nn.silu(a) * b).astype(_BF16), w2b[...])


def _moe_ffn(block_expert, xs, w1, w3, w2, layer):
    n_slots, d = xs.shape
    n_blocks = block_expert.shape[0]
    wmap = lambda j, be: (layer, be[j], 0, 0)
    blk = pl.BlockSpec((EXPERT_BLOCK, d), lambda j, be: (j, 0))
    return pl.pallas_call(
        _moe_ffn_kernel,
        grid_spec=pltpu.PrefetchScalarGridSpec(
            num_scalar_prefetch=1,
            grid=(n_blocks,),
            in_specs=[blk,
                      pl.BlockSpec((1, 1, d, D_EXPERT), wmap),
                      pl.BlockSpec((1, 1, d, D_EXPERT), wmap),
                      pl.BlockSpec((1, 1, D_EXPERT, d), wmap)],
            out_specs=blk,
            scratch_shapes=[pltpu.VMEM((d, D_EXPERT), _BF16), pltpu.VMEM((d, D_EXPERT), _BF16),
                            pltpu.VMEM((D_EXPERT, d), _BF16)],
        ),
        out_shape=jax.ShapeDtypeStruct((n_slots, d), _F32),
        compiler_params=pltpu.CompilerParams(
            dimension_semantics=("arbitrary",), vmem_limit_bytes=VMEM_LIMIT),
        name="moe_ffn",
    )(block_expert, xs, w1, w3, w2)


def _moe_plan_kernel(eid_ref, tab_ref, cnt_ref, slot_v, slot_s, zeros_v, sem):
    rows = eid_ref.shape[0]
    depth = max(rows, LANES)
    eid = eid_ref[...]
    r_i = lax.broadcasted_iota(jnp.int32, (LANES, LANES), 0)
    c_i = lax.broadcasted_iota(jnp.int32, (LANES, LANES), 1)
    earlier_lane = jnp.where(r_i < c_i, 1.0, 0.0).astype(_BF16)
    all_lanes = jnp.ones((LANES, LANES), _BF16)
    q_i = lax.broadcasted_iota(jnp.int32, (rows, depth), 1)
    p_i = lax.broadcasted_iota(jnp.int32, (rows, depth), 0)
    earlier_row = jnp.where(q_i < p_i, 1.0, 0.0).astype(_BF16)
    slot = jnp.zeros((rows, LANES), _F32)
    start = jnp.int32(0)
    for ex in range(N_EXPERTS):
        hit = eid == ex
        hot = jnp.where(hit, 1.0, 0.0).astype(_BF16)
        in_row = _dot(hot, earlier_lane)
        row_tot = _dot(hot, all_lanes)
        if depth > rows:
            row_tot = jnp.concatenate([row_tot, jnp.zeros((depth - rows, LANES), _F32)], axis=0)
        above = _dot(earlier_row, row_tot.astype(_BF16))
        count = jnp.sum(jnp.where(hit, 1, 0))
        slot = jnp.where(hit, start.astype(_F32) + in_row + above, slot)
        cnt_ref[ex] = count
        start = start + (count + EXPERT_BLOCK - 1) // EXPERT_BLOCK * EXPERT_BLOCK
    slot_v[...] = slot.astype(jnp.int32)
    to_smem = pltpu.make_async_copy(slot_v, slot_s, sem.at[0])
    to_smem.start()
    zeros_v[...] = jnp.zeros(zeros_v.shape, jnp.int32)
    clear = pltpu.make_async_copy(zeros_v, tab_ref, sem.at[1])
    clear.start()
    to_smem.wait()
    clear.wait()

    def place(r, carry):
        for c in range(LANES):
            tab_ref[slot_s[r, c]] = r * LANES + c
        return carry
    lax.fori_loop(0, rows, place, 0)


def _moe_plan(eid, n_tok):
    n_assign = 2 * n_tok
    n_blocks = -(-(n_assign + N_EXPERTS * (EXPERT_BLOCK - 1)) // EXPERT_BLOCK)
    n_slots = n_blocks * EXPERT_BLOCK
    rows = n_assign // LANES
    assert n_assign % LANES == 0
    table, counts, slot = pl.pallas_call(
        _moe_plan_kernel,
        in_specs=[pl.BlockSpec(memory_space=pltpu.VMEM)],
        out_specs=[pl.BlockSpec(memory_space=pltpu.SMEM), pl.BlockSpec(memory_space=pltpu.SMEM),
                   pl.BlockSpec(memory_space=pltpu.VMEM)],
        out_shape=[jax.ShapeDtypeStruct((n_slots,), jnp.int32), jax.ShapeDtypeStruct((N_EXPERTS,), jnp.int32),
                   jax.ShapeDtypeStruct((rows, LANES), jnp.int32)],
        scratch_shapes=[pltpu.SMEM((rows, LANES), jnp.int32), pltpu.VMEM((n_slots,), jnp.int32),
                        pltpu.SemaphoreType.DMA((2,))],
        name="moe_plan",
    )(eid.reshape(rows, LANES))
    padded = (counts + EXPERT_BLOCK - 1) // EXPERT_BLOCK * EXPERT_BLOCK
    pad_end = jnp.cumsum(padded)
    pad_start = pad_end - padded
    block_start = jnp.arange(n_blocks, dtype=jnp.int32) * EXPERT_BLOCK
    block_expert = jnp.minimum(jnp.sum((pad_end[None, :] <= block_start[:, None]).astype(jnp.int32), axis=1),
                               N_EXPERTS - 1)
    n_valid = jnp.clip((pad_start + counts)[block_expert] - block_start, 0, EXPERT_BLOCK)
    every = jnp.arange(n_slots, dtype=jnp.int32)
    valid = every % EXPERT_BLOCK < jnp.repeat(n_valid, EXPERT_BLOCK)
    source = jnp.where(valid, table, every) % n_tok
    return block_expert, source, slot.reshape(n_assign)


def _combine(h_ref, y0_ref, y1_ref, gate_ref):
    gate = gate_ref[...]
    return h_ref[...] + (gate[:, 0:1] * y0_ref[...] + gate[:, 1:2] * y1_ref[...])


def _emit_head(r, h, outs, scr):
    tm = r.shape[0]
    if scr is not None:
        scr[h] = r
    for ref, dil in outs:
        if dil == 0:
            ref[:, h, :] = r
        elif dil == 1:
            ref[:, _chunk(h)] = r.astype(ref.dtype)
        else:
            for res in range(dil):
                col = res * D_MODEL + h * HEAD_DIM
                ref[:, col:col + HEAD_DIM] = scr[h, pl.ds(res, tm // dil, stride=dil), :].astype(ref.dtype)


def _rope(xh, cos, sin_signed):
    return xh * cos + pltpu.roll(xh, HEAD_DIM // 2, axis=1) * sin_signed


def _qkv_kernel(*refs, streams, per_seq, kept):
    (h_ref, y0_ref, y1_ref, gate_ref, gmix_ref, gkv_ref, wq_ref, wkv_ref, cos_ref, sin_ref) = refs[:10]
    outs = refs[10:]
    if streams:
        (h2_ref, q0_ref, q1_ref, q2_ref, k0_ref, k1_ref, k2_ref, v0_ref, v1_ref, v2_ref, kf_ref, vf_ref,
         scr_q1, scr_q2, scr_k, scr_v) = outs
        q_outs = [([(q0_ref, 1)], None), ([(q1_ref, DILATIONS[1])], scr_q1), ([(q2_ref, DILATIONS[2])], scr_q2)]
        k_outs = ([(k0_ref, 1), (k1_ref, DILATIONS[1]), (k2_ref, DILATIONS[2])], scr_k)
        v_outs = ([(v0_ref, 1), (v1_ref, DILATIONS[1]), (v2_ref, DILATIONS[2])], scr_v)
    else:
        h2_ref, q0_ref, q1_ref, q2_ref, kf_ref, vf_ref = outs
        q_outs = [([(q0_ref, 1)], None), ([(q1_ref, 1)], None), ([(q2_ref, 1)], None)]
        k_outs = ([(kf_ref, 0)], None)
        v_outs = ([(vf_ref, 0)], None)

    h2 = _combine(h_ref, y0_ref, y1_ref, gate_ref)
    h2_ref[...] = h2
    cos = cos_ref[...]
    sin = sin_ref[...]
    xn = _rms(h2, gmix_ref[...]).astype(_BF16)
    scale = HEAD_DIM ** -0.5
    for g in range(N_GROUPS):
        q = _dot(xn, wq_ref[:, g * D_MODEL:(g + 1) * D_MODEL])
        for h in range(N_HEADS):
            _emit_head(_rope(q[:, _chunk(h)], cos, sin) * scale, h, *q_outs[g])
    xkv = _rms(h2, gkv_ref[...]).astype(_BF16)
    k = _dot(xkv, wkv_ref[:, 0:D_MODEL])
    for h in range(N_HEADS):
        _emit_head(_rope(k[:, _chunk(h)], cos, sin), h, *k_outs)
    v = _dot(xkv, wkv_ref[:, D_MODEL:2 * D_MODEL])
    for h in range(N_HEADS):
        _emit_head(v[:, _chunk(h)], h, *v_outs)
    if streams:
        @pl.when(pl.program_id(0) % per_seq >= per_seq - kept)
        def _():
            for h in range(N_HEADS):
                kf_ref[:, h, :] = scr_k[h]
                vf_ref[:, h, :] = scr_v[h]


def _qkv(h, ypair, gate, gmix, gkv, wq, wkv, cos, sin, *, tm, streams, seq, keep):
    t, d = h.shape
    nt = t // tm
    per_seq = seq // tm
    kept = keep // tm
    assert seq % tm == 0 and keep % tm == 0

    def tail(i):
        return ((i // per_seq) * kept + jnp.maximum(i % per_seq - (per_seq - kept), 0), 0, 0)

    tail_f32 = (pl.BlockSpec((tm, N_HEADS, HEAD_DIM), tail),
                jax.ShapeDtypeStruct((t // seq * keep, N_HEADS, HEAD_DIM), _F32))
    row = lambda i: (i, 0)
    const = lambda i: (0, 0)
    nat = lambda dt: (pl.BlockSpec((tm, d), row), jax.ShapeDtypeStruct((t, d), dt))

    def stream(dil):
        return (pl.BlockSpec((tm // dil, dil * d), row), jax.ShapeDtypeStruct((t // dil, dil * d), _BF16))

    if streams:
        d1, d2 = DILATIONS[1], DILATIONS[2]
        outs = [nat(_F32), nat(_BF16), stream(d1), stream(d2), nat(_BF16), stream(d1), stream(d2),
                nat(_BF16), stream(d1), stream(d2), tail_f32, tail_f32]
        scratch = [pltpu.VMEM((N_HEADS, tm, HEAD_DIM), _F32)] * 4
    else:
        outs = [nat(_F32), nat(_BF16), nat(_BF16), nat(_BF16), tail_f32, tail_f32]
        scratch = []
    return pl.pallas_call(
        functools.partial(_qkv_kernel, streams=streams, per_seq=per_seq, kept=kept),
        grid=(nt,),
        in_specs=[
            pl.BlockSpec((tm, d), row),
            pl.BlockSpec((tm, d), row),
            pl.BlockSpec((tm, d), lambda i: (nt + i, 0)),
            pl.BlockSpec((tm, 2), row),
            pl.BlockSpec((1, d), const),
            pl.BlockSpec((1, d), const),
            pl.BlockSpec((d, 3 * d), const),
            pl.BlockSpec((d, 2 * d), const),
            pl.BlockSpec((tm, HEAD_DIM), row),
            pl.BlockSpec((tm, HEAD_DIM), row),
        ],
        out_specs=[o[0] for o in outs],
        out_shape=[o[1] for o in outs],
        scratch_shapes=scratch,
        compiler_params=pltpu.CompilerParams(
            dimension_semantics=("arbitrary",), vmem_limit_bytes=VMEM_LIMIT),
        name="qkv_proj_streams" if streams else "qkv_proj",
    )(h, ypair, ypair, gate, gmix, gkv, wq, wkv, cos, sin)


ATTN_TQ = 512


def _attn_prompt_kernel(q_ref, kp_ref, kc_ref, vp_ref, vc_ref, o_ref, st_ref, *, tq):
    i = pl.program_id(1)
    qi = lax.broadcasted_iota(jnp.int32, (SPAN, 2 * SPAN), 0)
    kj = lax.broadcasted_iota(jnp.int32, (SPAN, 2 * SPAN), 1)
    band = (kj >= qi) & (kj <= qi + SPAN)
    band0 = band & (kj >= jnp.where(i > 0, 0, SPAN))
    lane = lax.broadcasted_iota(jnp.int32, (SPAN, HEAD_DIM), 1)
    ones = jnp.ones((2 * SPAN, HEAD_DIM), _BF16)
    for a in range(tq // SPAN):
        rows = slice(a * SPAN, (a + 1) * SPAN)
        st = jnp.zeros((SPAN, HEAD_DIM), _F32)
        for h in range(N_HEADS):
            hs = _chunk(h)
            if a == 0:
                keys = jnp.concatenate([kp_ref[:, hs], kc_ref[rows, hs]], axis=0)
                vals = jnp.concatenate([vp_ref[:, hs], vc_ref[rows, hs]], axis=0)
                mask = band0
            else:
                both = slice((a - 1) * SPAN, (a + 1) * SPAN)
                keys, vals, mask = kc_ref[both, hs], vc_ref[both, hs], band
            s = jnp.where(mask, _dot_nt(q_ref[rows, hs], keys), NEG_BIG)
            m = jnp.max(s, axis=1, keepdims=True)
            p = jnp.exp(s - m).astype(_BF16)
            ol = _dot(p, jnp.concatenate([vals, ones], axis=1))
            l = ol[:, HEAD_DIM:]
            o_ref[rows, hs] = (ol[:, :HEAD_DIM] / l).astype(o_ref.dtype)
            st = jnp.where(lane == h, m + jnp.log(l), st)
        st_ref[rows, :] = st


def _attn_prompt(q, k, v, batch, dilation):
    rows, width = q.shape
    d = width // dilation
    per_batch = rows // batch
    tq = min(ATTN_TQ, per_batch)
    assert per_batch % tq == 0 and tq % SPAN == 0
    nq = per_batch // tq
    sub = tq // SPAN
    cur = lambda b, i, r: (b * nq + i, r)
    prev = lambda b, i, r: ((b * nq + i) * sub - jnp.where(i > 0, 1, 0), r)
    big = (tq, d)
    small = (SPAN, d)
    return pl.pallas_call(
        functools.partial(_attn_prompt_kernel, tq=tq),
        grid=(batch, nq, dilation),
        in_specs=[pl.BlockSpec(big, cur), pl.BlockSpec(small, prev), pl.BlockSpec(big, cur),
                  pl.BlockSpec(small, prev), pl.BlockSpec(big, cur)],
        out_specs=[pl.BlockSpec(big, cur), pl.BlockSpec((tq, HEAD_DIM), cur)],
        out_shape=[jax.ShapeDtypeStruct((rows, width), _BF16),
                   jax.ShapeDtypeStruct((rows, dilation * HEAD_DIM), _F32)],
        compiler_params=pltpu.CompilerParams(
            dimension_semantics=("arbitrary", "arbitrary", "arbitrary"), vmem_limit_bytes=VMEM_LIMIT),
        name=f"attn_prompt_d{dilation}",
    )(q, k, k, v, v)


N_NEW = 8
CACHE_A_GROUPS = (PAST_LEN - WINDOWS[1]) // DILATIONS[2]
CACHE_A_ROWS = CACHE_A_GROUPS * N_NEW
CACHE_B_ROWS = WINDOWS[1]
CACHE_B_GROUPS = CACHE_B_ROWS // DILATIONS[2]
KEYS_REAL = CACHE_A_ROWS + CACHE_B_ROWS + N_NEW
KEYS_PAD = -(-KEYS_REAL // 128) * 128


def _sample_key_positions():
    pos = np.full((KEYS_PAD,), -1, np.int64)
    a = np.arange(CACHE_A_ROWS)
    pos[:CACHE_A_ROWS] = (a // N_NEW) * DILATIONS[2] + a % N_NEW
    pos[CACHE_A_ROWS:CACHE_A_ROWS + CACHE_B_ROWS] = PAST_LEN - CACHE_B_ROWS + np.arange(CACHE_B_ROWS)
    pos[CACHE_A_ROWS + CACHE_B_ROWS:KEYS_REAL] = PAST_LEN + np.arange(N_NEW)
    return pos


def _sample_bias():
    pos = _sample_key_positions()
    bias = np.full((N_GROUPS * N_NEW, KEYS_PAD), NEG_BIG, np.float32)
    for g in range(N_GROUPS):
        for n in range(N_NEW):
            delta = PAST_LEN + n - pos
            ok = (pos >= 0) & (delta >= 0) & (delta <= WINDOWS[g]) & (delta % DILATIONS[g] == 0)
            assert int(ok.sum()) == SPAN + 1
            bias[g * N_NEW + n, ok] = 0.0
    return bias


def _attn_sample_kernel(q0_ref, q1_ref, q2_ref, kn_ref, vn_ref, bias_ref, ck_hbm, cv_hbm,
                        o_ref, ka, kb, va, vb, kall, vall, sem, *, n_seq):
    b = pl.program_id(0)
    slot = b % 2

    def cache_copies(seq, s):
        cps = []
        for h in range(N_HEADS):
            for src, dst_a, dst_b in ((ck_hbm, ka, kb), (cv_hbm, va, vb)):
                cps.append(pltpu.make_async_copy(
                    src.at[seq, pl.ds(0, CACHE_A_GROUPS), pl.ds(0, N_NEW), h], dst_a.at[s, h], sem.at[s]))
                cps.append(pltpu.make_async_copy(
                    src.at[seq, pl.ds(CACHE_A_GROUPS, CACHE_B_GROUPS), :, h], dst_b.at[s, h], sem.at[s]))
        return cps

    @pl.when(b == 0)
    def _():
        kall[KEYS_REAL:KEYS_PAD, :] = jnp.zeros((KEYS_PAD - KEYS_REAL, D_MODEL), _BF16)
        vall[KEYS_REAL:KEYS_PAD, :] = jnp.zeros((KEYS_PAD - KEYS_REAL, D_MODEL), _BF16)
        for cp in cache_copies(0, 0):
            cp.start()

    @pl.when(b + 1 < n_seq)
    def _():
        for cp in cache_copies(b + 1, 1 - slot):
            cp.start()

    for cp in cache_copies(b, slot):
        cp.wait()

    nb0 = CACHE_A_ROWS + CACHE_B_ROWS
    for src_a, src_b, src_n, dst in ((ka, kb, kn_ref, kall), (va, vb, vn_ref, vall)):
        for h in range(N_HEADS):
            hs = _chunk(h)
            dst[0:CACHE_A_ROWS, hs] = src_a[slot, h].reshape(CACHE_A_ROWS, HEAD_DIM).astype(_BF16)
            dst[CACHE_A_ROWS:nb0, hs] = src_b[slot, h].reshape(CACHE_B_ROWS, HEAD_DIM).astype(_BF16)
            dst[nb0:KEYS_REAL, hs] = src_n[0, :, h, :].astype(_BF16)

    bias = bias_ref[...]
    for h in range(N_HEADS):
        hs = _chunk(h)
        qh = jnp.concatenate([q0_ref[0, :, hs].astype(_F32), q1_ref[0, :, hs].astype(_F32),
                              q2_ref[0, :, hs].astype(_F32)], axis=0).astype(_BF16)
        s = _dot_nt(qh, kall[:, hs]) + bias
        m = jnp.max(s, axis=1, keepdims=True)
        m8 = jnp.maximum(jnp.maximum(m[0:N_NEW], m[N_NEW:2 * N_NEW]), m[2 * N_NEW:3 * N_NEW])
        p = jnp.exp(s - jnp.concatenate([m8, m8, m8], axis=0))
        p8 = p[0:N_NEW] + p[N_NEW:2 * N_NEW] + p[2 * N_NEW:3 * N_NEW]
        l8 = jnp.sum(p8, axis=1, keepdims=True)
        o_ref[0, :, hs] = _dot(p8.astype(_BF16), vall[:, hs]) / l8


def _attn_sample(q0, q1, q2, cache_k, cache_v, k_new, v_new):
    nb, n_new, d = q0.shape
    past = cache_k.shape[1]
    assert past == PAST_LEN and n_new == N_NEW and d == D_MODEL
    assert cache_k.shape[2:] == (N_HEADS, HEAD_DIM)
    ck = cache_k.reshape(nb, past // DILATIONS[2], DILATIONS[2], N_HEADS, HEAD_DIM)
    cv = cache_v.reshape(nb, past // DILATIONS[2], DILATIONS[2], N_HEADS, HEAD_DIM)
    bias = jnp.asarray(_sample_bias())
    new = pl.BlockSpec((1, n_new, d), lambda b: (b, 0, 0))
    new_kv = pl.BlockSpec((1, n_new, N_HEADS, HEAD_DIM), lambda b: (b, 0, 0, 0))
    part_a = (2, N_HEADS, CACHE_A_GROUPS, N_NEW, HEAD_DIM)
    part_b = (2, N_HEADS, CACHE_B_GROUPS, DILATIONS[2], HEAD_DIM)
    return pl.pallas_call(
        functools.partial(_attn_sample_kernel, n_seq=nb),
        grid=(nb,),
        in_specs=[new, new, new, new_kv, new_kv,
                  pl.BlockSpec((N_GROUPS * N_NEW, KEYS_PAD), lambda b: (0, 0)),
                  pl.BlockSpec(memory_space=pl.ANY), pl.BlockSpec(memory_space=pl.ANY)],
        out_specs=new,
        out_shape=jax.ShapeDtypeStruct((nb, n_new, d), _F32),
        scratch_shapes=[pltpu.VMEM(part_a, _F32), pltpu.VMEM(part_b, _F32),
                        pltpu.VMEM(part_a, _F32), pltpu.VMEM(part_b, _F32),
                        pltpu.VMEM((KEYS_PAD, d), _BF16), pltpu.VMEM((KEYS_PAD, d), _BF16),
                        pltpu.SemaphoreType.DMA((2,))],
        compiler_params=pltpu.CompilerParams(
            dimension_semantics=("arbitrary",), vmem_limit_bytes=VMEM_LIMIT),
        name="attn_sample",
    )(q0, q1, q2, k_new, v_new, bias, ck, cv)


def _attn_out_kernel(*refs, n_groups, tm):
    if n_groups == 1:
        (o_ref, h_ref, wo_ref, gffn_ref, wr_ref, br_ref,
         h3_ref, xn2_ref, eid_ref, gate_ref) = refs
        o = o_ref[...].astype(_BF16)
    else:
        (o0_ref, o1_ref, o2_ref, s0_ref, s1_ref, s2_ref, h_ref, wo_ref, gffn_ref, wr_ref, br_ref,
         h3_ref, xn2_ref, eid_ref, gate_ref, obuf, scr_o1, scr_o2, scr_s1, scr_s2) = refs
        for o_ref, s_ref, scr_o, scr_s, dil in ((o1_ref, s1_ref, scr_o1, scr_s1, DILATIONS[1]),
                                                (o2_ref, s2_ref, scr_o2, scr_s2, DILATIONS[2])):
            for res in range(dil):
                dst = pl.ds(res, tm // dil, stride=dil)
                scr_s[dst, :] = s_ref[:, res * HEAD_DIM:(res + 1) * HEAD_DIM]
                for h in range(N_HEADS):
                    col = res * D_MODEL + h * HEAD_DIM
                    scr_o[h, dst, :] = o_ref[:, col:col + HEAD_DIM].astype(_F32)
        sts = [s0_ref[...], scr_s1[...], scr_s2[...]]
        mx = jnp.maximum(jnp.maximum(sts[0], sts[1]), sts[2])
        es = [jnp.exp(s - mx) for s in sts]
        den = es[0] + es[1] + es[2]
        ws = [e / den for e in es]
        for h in range(N_HEADS):
            acc = ws[0][:, h:h + 1] * o0_ref[:, _chunk(h)]
            acc = acc + ws[1][:, h:h + 1] * scr_o1[h]
            acc = acc + ws[2][:, h:h + 1] * scr_o2[h]
            obuf[:, _chunk(h)] = acc.astype(_BF16)
        o = obuf[...]
    h3 = h_ref[...] + _dot(o, wo_ref[...])
    h3_ref[...] = h3
    xn2 = _rms(h3, gffn_ref[...])
    xn2_ref[...] = xn2
    _route(xn2, wr_ref, br_ref, eid_ref, gate_ref)


def _attn_out(os, sts, h, wo, gffn, wr, br, *, tm):
    t, d = h.shape
    n_groups = len(os)
    row = lambda i: (i, 0)
    const = lambda i: (0, 0)
    if n_groups == 1:
        in_specs = [pl.BlockSpec((tm, d), row)]
        scratch = []
        args = [os[0]]
    else:
        in_specs = [pl.BlockSpec((tm // dil, dil * d), row) for dil in DILATIONS]
        in_specs += [pl.BlockSpec((tm // dil, dil * HEAD_DIM), row) for dil in DILATIONS]
        scratch = [pltpu.VMEM((tm, d), _BF16),
                   pltpu.VMEM((N_HEADS, tm, HEAD_DIM), _F32), pltpu.VMEM((N_HEADS, tm, HEAD_DIM), _F32),
                   pltpu.VMEM((tm, HEAD_DIM), _F32), pltpu.VMEM((tm, HEAD_DIM), _F32)]
        args = list(os) + list(sts)
    in_specs += [pl.BlockSpec((tm, d), row), pl.BlockSpec((d, d), const), pl.BlockSpec((1, d), const),
                 pl.BlockSpec((ROUTER_ROWS, d), const), pl.BlockSpec((ROUTER_ROWS, 1), const)]
    return pl.pallas_call(
        functools.partial(_attn_out_kernel, n_groups=n_groups, tm=tm),
        grid=(t // tm,),
        in_specs=in_specs,
        out_specs=[pl.BlockSpec((tm, d), row), pl.BlockSpec((tm, d), row),
                   pl.BlockSpec((2, tm), lambda i: (0, i)), pl.BlockSpec((2, tm), lambda i: (0, i))],
        out_shape=[jax.ShapeDtypeStruct((t, d), _F32), jax.ShapeDtypeStruct((t, d), _F32),
                   jax.ShapeDtypeStruct((2, t), jnp.int32), jax.ShapeDtypeStruct((2, t), _F32)],
        scratch_shapes=scratch,
        compiler_params=pltpu.CompilerParams(
            dimension_semantics=("arbitrary",), vmem_limit_bytes=VMEM_LIMIT),
        name=f"attn_out_g{n_groups}",
    )(*args, h, wo, gffn, wr, br)


def _final_kernel(h_ref, y0_ref, y1_ref, gate_ref, g_ref, out_ref):
    out_ref[...] = _rms(_combine(h_ref, y0_ref, y1_ref, gate_ref), g_ref[...])


def _final(h, ypair, gate, g, *, tm):
    t, d = h.shape
    nt = t // tm
    row = lambda i: (i, 0)
    return pl.pallas_call(
        _final_kernel,
        grid=(nt,),
        in_specs=[pl.BlockSpec((tm, d), row),
                  pl.BlockSpec((tm, d), row),
                  pl.BlockSpec((tm, d), lambda i: (nt + i, 0)),
                  pl.BlockSpec((tm, 2), row),
                  pl.BlockSpec((1, d), lambda i: (0, 0))],
        out_specs=pl.BlockSpec((tm, d), row),
        out_shape=jax.ShapeDtypeStruct((t, d), _F32),
        compiler_params=pltpu.CompilerParams(
            dimension_semantics=("arbitrary",), vmem_limit_bytes=VMEM_LIMIT),
        name="final_norm",
    )(h, ypair, ypair, gate, g)


def _rope_tables(pos):
    half = HEAD_DIM // 2
    inv_freq = jnp.power(jnp.float32(ROPE_THETA), -jnp.arange(half, dtype=jnp.float32) / half)
    ang = pos.astype(jnp.float32)[:, None] * inv_freq[None, :]
    cos = jnp.cos(ang)
    sin = jnp.sin(ang)
    return jnp.concatenate([cos, cos], axis=-1), jnp.concatenate([-sin, sin], axis=-1)


def _router_params(wg, bg, we, be):
    wr = jnp.zeros((ROUTER_ROWS, D_MODEL), _F32)
    wr = wr.at[0:MOE_GROUPS].set(wg.T).at[EXPERT_ROW0:EXPERT_ROW0 + N_EXPERTS].set(we.T)
    br = jnp.zeros((ROUTER_ROWS, 1), _F32)
    br = br.at[0:MOE_GROUPS, 0].set(bg).at[EXPERT_ROW0:EXPERT_ROW0 + N_EXPERTS, 0].set(be)
    return wr, br


def _moe(xn2, eid, experts, layer):
    block_expert, source, slot = _moe_plan(eid, xn2.shape[0])
    xs = _row_gather(xn2, source)
    yield
    ys = _moe_ffn(block_expert, xs, *experts, layer)
    ypair = _row_gather(ys, slot)
    yield
    return ypair


def _forward(x3, prev3, pos_rows, attend, p, *, shift, tm, streams, keep):
    h1, xn2, eid, gate, state = _conv_layer(
        x3, prev3, p['gmix'][0], p['win'], p['ck'], p['wout'], p['gffn'][0], *p['router'][0],
        shift=shift, tm=tm)
    t = x3.shape[0] * x3.shape[1]
    ypair = yield from _moe(xn2, eid, p['experts'], 0)
    cos, sin = _rope_tables(pos_rows)
    h2, *qkv, kf, vf = _qkv(h1.reshape(t, D_MODEL), ypair, gate.T, p['gmix'][1], p['gkv'],
                            p['wq'], p['wkv'], cos, sin, tm=256, streams=streams,
                            seq=x3.shape[1], keep=keep)
    os, sts = attend(qkv, kf, vf)
    h3, xn2, eid, gate = _attn_out(os, sts, h2, p['wo'], p['gffn'][1], *p['router'][1], tm=256)
    ypair = yield from _moe(xn2, eid, p['experts'], 1)
    y = _final(h3, ypair, gate.T, p['gfinal'], tm=512)
    return y, state, kf, vf


def _interleave(*paths):
    results = [None] * len(paths)
    live = list(enumerate(paths))
    while live:
        still = []
        for i, gen in live:
            try:
                next(gen)
                still.append((i, gen))
            except StopIteration as done:
                results[i] = done.value
        live = still
    return results


def kernel(x_prompt, x_sample, cache_k, cache_v, state_conv, norm_mix, norm_ffn, norm_kv, norm_final,
           conv_w_in, conv_kernel, conv_w_out, attn_w_q, attn_w_kv, attn_w_o, router_group_w,
           router_group_b, router_expert_w, router_expert_b, expert_w1, expert_w3, expert_w2):
    b_p, s_p, d = x_prompt.shape
    b_s, n_new, _ = x_sample.shape
    assert d == D_MODEL and n_new == N_NEW and s_p % (DILATIONS[2] * SPAN) == 0
    assert norm_mix.shape[0] == 2 and conv_w_in.shape[0] == 1 and attn_w_q.shape[0] == 1
    assert cache_k.shape[1] == PAST_LEN

    p = {
        'gmix': [norm_mix[l].reshape(1, d) for l in range(2)],
        'gffn': [norm_ffn[l].reshape(1, d) for l in range(2)],
        'gkv': norm_kv.reshape(1, d),
        'gfinal': norm_final.reshape(1, d),
        'win': conv_w_in[0].astype(_BF16),
        'ck': conv_kernel[0],
        'wout': conv_w_out[0].astype(_BF16),
        'wq': attn_w_q[0].astype(_BF16),
        'wkv': attn_w_kv.astype(_BF16),
        'wo': attn_w_o[0].astype(_BF16),
        'router': [_router_params(router_group_w[l], router_group_b[l], router_expert_w[l], router_expert_b[l])
                   for l in range(2)],
        'experts': (expert_w1, expert_w3, expert_w2),
    }

    def attend_prompt(qkv, kf, vf):
        q0, q1, q2, k0, k1, k2, v0, v1, v2 = qkv
        os, sts = [], []
        for q, k, v, dil in ((q0, k0, v0, DILATIONS[0]), (q1, k1, v1, DILATIONS[1]), (q2, k2, v2, DILATIONS[2])):
            o, st = _attn_prompt(q, k, v, b_p, dil)
            os.append(o)
            sts.append(st)
        return os, sts

    pos_p = jnp.tile(jnp.arange(s_p, dtype=jnp.int32), b_p)
    zero_state = jnp.zeros((b_p, CONV_WIDTH - 1, d), x_prompt.dtype)
    keep = min(max(WINDOWS), s_p)
    prompt_path = _forward(x_prompt, zero_state, pos_p, attend_prompt, p, shift=1, tm=512,
                           streams=True, keep=keep)

    halves = 2
    bh = b_s // halves

    def to_rows(a):
        w = a.shape[-1]
        return a.reshape(halves, bh, n_new, w).transpose(0, 2, 1, 3).reshape(halves * n_new * bh, w)

    def to_batch(a):
        w = a.shape[1:]
        return jnp.swapaxes(a.reshape(halves, n_new, bh, *w), 1, 2).reshape(b_s, n_new, *w)

    def attend_sample(qkv, kf, vf):
        q0, q1, q2 = qkv
        o = _attn_sample(to_batch(q0), to_batch(q1), to_batch(q2), cache_k, cache_v,
                         to_batch(kf), to_batch(vf))
        return [to_rows(o)], None

    x_s = to_rows(x_sample).reshape(halves, n_new * bh, d)
    prev_s = state_conv[0].reshape(halves, bh, CONV_WIDTH - 1, d).transpose(0, 2, 1, 3).reshape(
        halves, (CONV_WIDTH - 1) * bh, d)
    pos_s = jnp.tile(jnp.repeat(PAST_LEN + jnp.arange(n_new, dtype=jnp.int32), bh), halves)
    sample_path = _forward(x_s, prev_s, pos_s, attend_sample, p, shift=bh, tm=n_new * bh,
                           streams=False, keep=n_new * bh)

    (y_p, st_p, kf_p, vf_p), (y_s, st_s, kf_s, vf_s) = _interleave(prompt_path, sample_path)
    y_prompt = y_p.reshape(b_p, s_p, d)
    k_p = kf_p.reshape(b_p, keep, N_HEADS, HEAD_DIM)
    v_p = vf_p.reshape(b_p, keep, N_HEADS, HEAD_DIM)
    conv_p = st_p[None]
    y_sample = to_batch(y_s)
    k_s = to_batch(kf_s)
    v_s = to_batch(vf_s)
    conv_s = st_s.reshape(halves, CONV_WIDTH - 1, bh, d).transpose(0, 2, 1, 3).reshape(
        b_s, CONV_WIDTH - 1, d)[None]

    return (y_prompt, y_sample, k_p, v_p, conv_p, k_s, v_s, conv_s)
```

```python
import functools

import numpy as np
import jax
import jax.numpy as jnp
from jax import lax
from jax.experimental import pallas as pl
from jax.experimental.pallas import tpu as pltpu
from jax.experimental.pallas import tpu_sc as plsc

D_MODEL = 1024
CONV_WIDTH = 3
WINDOWS = (128, 512, 2048)
DILATIONS = (1, 4, 16)
N_GROUPS = 3
N_HEADS = 8
HEAD_DIM = 128
SPAN = 128
ROPE_THETA = 10000.0
MOE_GROUPS = 4
EXPERTS_PER_GROUP = 4
N_EXPERTS = 16
D_EXPERT = 512
EXPERT_BLOCK = 256
RMS_EPS = 1e-6
NEG_BIG = -1e30
PAST_LEN = 2048

LANES = 128
SUBLANES = 8
ROUTER_ROWS = 32
EXPERT_ROW0 = 8
VMEM_LIMIT = 56 * 1024 * 1024

CONV_TILE = 512
QKV_TILE = 256
ATTN_OUT_TILE = 512
FINAL_TILE = 512

_F32 = jnp.float32
_BF16 = jnp.bfloat16


def _rms(x, g):
    ms = jnp.mean(x * x, axis=-1, keepdims=True)
    return (x * lax.rsqrt(ms + RMS_EPS)) * g


def _dot(a, b):
    return jnp.dot(a, b, preferred_element_type=_F32)


def _dot_nt(a, b, precision=None):
    return lax.dot_general(a, b, (((1,), (1,)), ((), ())), precision=precision,
                           preferred_element_type=_F32)


def _chunk(c):
    return slice(c * LANES, (c + 1) * LANES)


def _route(xn, wr_ref, br_ref, eid_ref, gate_ref):
    logits = _dot_nt(wr_ref[...], xn, precision=lax.Precision.HIGHEST) + br_ref[...]
    lg = logits[0:MOE_GROUPS]
    row = lax.broadcasted_iota(jnp.int32, lg.shape, 0).astype(_F32)
    eg = jnp.exp(lg - jnp.max(lg, axis=0, keepdims=True))
    pg = eg / jnp.sum(eg, axis=0, keepdims=True)
    gp = jnp.max(pg, axis=0, keepdims=True)
    gi = jnp.min(jnp.where(pg == gp, row, float(MOE_GROUPS)), axis=0, keepdims=True)
    le = jnp.zeros_like(lg)
    for g in range(MOE_GROUPS):
        r0 = EXPERT_ROW0 + g * EXPERTS_PER_GROUP
        le = le + jnp.where(gi == float(g), logits[r0:r0 + EXPERTS_PER_GROUP], 0.0)
    ee = jnp.exp(le - jnp.max(le, axis=0, keepdims=True))
    ev = ee / jnp.sum(ee, axis=0, keepdims=True)
    v1 = jnp.max(ev, axis=0, keepdims=True)
    i1 = jnp.min(jnp.where(ev == v1, row, float(EXPERTS_PER_GROUP)), axis=0, keepdims=True)
    ev2 = jnp.where(row == i1, -1.0, ev)
    v2 = jnp.max(ev2, axis=0, keepdims=True)
    i2 = jnp.min(jnp.where(ev2 == v2, row, float(EXPERTS_PER_GROUP)), axis=0, keepdims=True)
    den = v1 + v2
    ids = jnp.concatenate([gi * EXPERTS_PER_GROUP + i1, gi * EXPERTS_PER_GROUP + i2], axis=0)
    eid_ref[...] = ids.astype(jnp.int32)
    gate_ref[...] = jnp.concatenate([gp * v1 / den, gp * v2 / den], axis=0)


def _conv_layer_kernel(x_ref, prev_ref, gmix_ref, win_ref, ck_ref, wout_ref, gffn_ref, wr_ref, br_ref,
                       h_ref, xn2_ref, eid_ref, gate_ref, st_ref, ubuf, *, shift, tm):
    i = pl.program_id(1)
    halo = ubuf.shape[0] - tm
    keep = (CONV_WIDTH - 1) * shift

    @pl.when(i == 0)
    def _():
        ubuf[halo - keep:halo, :] = prev_ref[0]

    @pl.when(i > 0)
    def _():
        ubuf[halo - keep:halo, :] = ubuf[halo + tm - keep:halo + tm, :]

    x = x_ref[0]
    xn = _rms(x, gmix_ref[...]).astype(_BF16)
    c_gate = _dot(xn, win_ref[:, 0:D_MODEL])
    hid = _dot(xn, win_ref[:, 2 * D_MODEL:3 * D_MODEL])
    u = c_gate * hid
    ubuf[halo:halo + tm, :] = u
    u1 = ubuf[halo - shift:halo - shift + tm, :]
    u2 = ubuf[halo - 2 * shift:halo - 2 * shift + tm, :]
    conv = ck_ref[0:1, :] * u2 + ck_ref[1:2, :] * u1 + ck_ref[2:3, :] * u
    b_gate = _dot(xn, win_ref[:, D_MODEL:2 * D_MODEL])
    y = _dot((b_gate * conv).astype(_BF16), wout_ref[...])
    h = x + y
    h_ref[0] = h
    st_ref[0] = ubuf[halo + tm - keep:halo + tm, :]
    xn2 = _rms(h, gffn_ref[...])
    xn2_ref[...] = xn2
    _route(xn2, wr_ref, br_ref, eid_ref, gate_ref)


def _conv_layer(x3, prev3, gmix, win, ck, wout, gffn, wr, br, *, shift, tm):
    nb, s, d = x3.shape
    nt = s // tm
    keep = (CONV_WIDTH - 1) * shift
    halo = -(-keep // SUBLANES) * SUBLANES
    t_all = nb * s
    const = lambda b, i: (0, 0)
    return pl.pallas_call(
        functools.partial(_conv_layer_kernel, shift=shift, tm=tm),
        grid=(nb, nt),
        in_specs=[
            pl.BlockSpec((1, tm, d), lambda b, i: (b, i, 0)),
            pl.BlockSpec((1, keep, d), lambda b, i: (b, 0, 0)),
            pl.BlockSpec((1, d), const),
            pl.BlockSpec((d, 3 * d), const),
            pl.BlockSpec((CONV_WIDTH, d), const),
            pl.BlockSpec((d, d), const),
            pl.BlockSpec((1, d), const),
            pl.BlockSpec((ROUTER_ROWS, d), const),
            pl.BlockSpec((ROUTER_ROWS, 1), const),
        ],
        out_specs=[
            pl.BlockSpec((1, tm, d), lambda b, i: (b, i, 0)),
            pl.BlockSpec((tm, d), lambda b, i: (b * nt + i, 0)),
            pl.BlockSpec((2, tm), lambda b, i: (0, b * nt + i)),
            pl.BlockSpec((2, tm), lambda b, i: (0, b * nt + i)),
            pl.BlockSpec((1, keep, d), lambda b, i: (b, 0, 0)),
        ],
        out_shape=[
            jax.ShapeDtypeStruct((nb, s, d), _F32),
            jax.ShapeDtypeStruct((t_all, d), _F32),
            jax.ShapeDtypeStruct((2, t_all), jnp.int32),
            jax.ShapeDtypeStruct((2, t_all), _F32),
            jax.ShapeDtypeStruct((nb, keep, d), _F32),
        ],
        scratch_shapes=[pltpu.VMEM((halo + tm, d), _F32)],
        compiler_params=pltpu.CompilerParams(
            dimension_semantics=("arbitrary", "arbitrary"), vmem_limit_bytes=VMEM_LIMIT),
        name="conv_layer",
    )(x3, prev3, gmix, win, ck, wout, gffn, wr, br)


SC_WINDOW = 32


def _row_gather(src, idx):
    n_out = idx.shape[0]
    mesh = plsc.VectorSubcoreMesh(core_axis_name="c", subcore_axis_name="s")
    n_cores = mesh.num_cores
    per_worker = n_out // (n_cores * mesh.num_subcores)
    n_win = per_worker // SC_WINDOW
    assert per_worker * n_cores * mesh.num_subcores == n_out and n_win * SC_WINDOW == per_worker
    assert n_win % 2 == 0

    @pl.kernel(out_type=jax.ShapeDtypeStruct((n_out,) + src.shape[1:], src.dtype), mesh=mesh,
               scratch_types=[pltpu.VMEM((per_worker,), jnp.int32),
                              pltpu.VMEM((2, SC_WINDOW) + src.shape[1:], src.dtype),
                              pltpu.SemaphoreType.DMA((2,)), pltpu.SemaphoreType.DMA((2,))])
    def gather(src_hbm, idx_hbm, out_hbm, idx_v, rows_v, gsem, wsem):
        base = (lax.axis_index("s") * n_cores + lax.axis_index("c")) * per_worker
        pltpu.sync_copy(idx_hbm.at[pl.ds(base, per_worker)], idx_v)

        def gather_copy(q, b):
            return pltpu.make_async_copy(src_hbm.at[idx_v.at[pl.ds(q * SC_WINDOW, SC_WINDOW)]],
                                         rows_v.at[b], gsem.at[b])

        def write_copy(q, b):
            return pltpu.make_async_copy(rows_v.at[b], out_hbm.at[pl.ds(base + q * SC_WINDOW, SC_WINDOW)],
                                         wsem.at[b])

        gather_copy(0, 0).start()

        @pl.loop(0, n_win, step=2)
        def _(q0):
            for b in range(2):
                q = q0 + b
                gather_copy(q, b).wait()

                @pl.when(q + 1 < n_win)
                def _():
                    @pl.when(q >= 1)
                    def _():
                        write_copy(q - 1, 1 - b).wait()
                    gather_copy(q + 1, 1 - b).start()

                write_copy(q, b).start()

        write_copy(n_win - 2, 0).wait()
        write_copy(n_win - 1, 1).wait()

    return gather(src, idx)


def _moe_ffn_kernel(be_ref, x_ref, w1_ref, w3_ref, w2_ref, y_ref, w1b, w3b, w2b):
    j = pl.program_id(0)

    @pl.when((j == 0) | (be_ref[j] != be_ref[jnp.maximum(j - 1, 0)]))
    def _():
        w1b[...] = w1_ref[0, 0].astype(_BF16)
        w3b[...] = w3_ref[0, 0].astype(_BF16)
        w2b[...] = w2_ref[0, 0].astype(_BF16)

    x = x_ref[...].astype(_BF16)
    a = _dot(x, w1b[...])
    b = _dot(x, w3b[...])
    y_ref[...] = _dot((jax.nn.silu(a) * b).astype(_BF16), w2b[...])


def _moe_ffn(block_expert, xs, w1, w3, w2, layer):
    n_slots, d = xs.shape
    n_blocks = block_expert.shape[0]
    wmap = lambda j, be: (layer, be[j], 0, 0)
    blk = pl.BlockSpec((EXPERT_BLOCK, d), lambda j, be: (j, 0))
    return pl.pallas_call(
        _moe_ffn_kernel,
        grid_spec=pltpu.PrefetchScalarGridSpec(
            num_scalar_prefetch=1,
            grid=(n_blocks,),
            in_specs=[blk,
                      pl.BlockSpec((1, 1, d, D_EXPERT), wmap),
                      pl.BlockSpec((1, 1, d, D_EXPERT), wmap),
                      pl.BlockSpec((1, 1, D_EXPERT, d), wmap)],
            out_specs=blk,
            scratch_shapes=[pltpu.VMEM((d, D_EXPERT), _BF16), pltpu.VMEM((d, D_EXPERT), _BF16),
                            pltpu.VMEM((D_EXPERT, d), _BF16)],
        ),
        out_shape=jax.ShapeDtypeStruct((n_slots, d), _F32),
        compiler_params=pltpu.CompilerParams(
            dimension_semantics=("arbitrary",), vmem_limit_bytes=VMEM_LIMIT),
        name="moe_ffn",
    )(block_expert, xs, w1, w3, w2)


def _moe_plan_kernel(eid_ref, tab_ref, cnt_ref, slot_v, slot_s, zeros_v, sem):
    rows = eid_ref.shape[0]
    depth = max(rows, LANES)
    eid = eid_ref[...]
    r_i = lax.broadcasted_iota(jnp.int32, (LANES, LANES), 0)
    c_i = lax.broadcasted_iota(jnp.int32, (LANES, LANES), 1)
    earlier_lane = jnp.where(r_i < c_i, 1.0, 0.0).astype(_BF16)
    all_lanes = jnp.ones((LANES, LANES), _BF16)
    q_i = lax.broadcasted_iota(jnp.int32, (rows, depth), 1)
    p_i = lax.broadcasted_iota(jnp.int32, (rows, depth), 0)
    earlier_row = jnp.where(q_i < p_i, 1.0, 0.0).astype(_BF16)
    slot = jnp.zeros((rows, LANES), _F32)
    start = jnp.int32(0)
    for ex in range(N_EXPERTS):
        hit = eid == ex
        hot = jnp.where(hit, 1.0, 0.0).astype(_BF16)
        in_row = _dot(hot, earlier_lane)
        row_tot = _dot(hot, all_lanes)
        if depth > rows:
            row_tot = jnp.concatenate([row_tot, jnp.zeros((depth - rows, LANES), _F32)], axis=0)
        above = _dot(earlier_row, row_tot.astype(_BF16))
        count = jnp.sum(jnp.where(hit, 1, 0))
        slot = jnp.where(hit, start.astype(_F32) + in_row + above, slot)
        cnt_ref[ex] = count
        start = start + (count + EXPERT_BLOCK - 1) // EXPERT_BLOCK * EXPERT_BLOCK
    slot_v[...] = slot.astype(jnp.int32)
    to_smem = pltpu.make_async_copy(slot_v, slot_s, sem.at[0])
    to_smem.start()
    zeros_v[...] = jnp.zeros(zeros_v.shape, jnp.int32)
    clear = pltpu.make_async_copy(zeros_v, tab_ref, sem.at[1])
    clear.start()
    to_smem.wait()
    clear.wait()

    def place(r, carry):
        for c in range(LANES):
            tab_ref[slot_s[r, c]] = r * LANES + c
        return carry
    lax.fori_loop(0, rows, place, 0)


def _moe_plan(eid, n_tok):
    n_assign = 2 * n_tok
    n_blocks = -(-(n_assign + N_EXPERTS * (EXPERT_BLOCK - 1)) // EXPERT_BLOCK)
    n_slots = n_blocks * EXPERT_BLOCK
    rows = n_assign // LANES
    assert n_assign % LANES == 0
    table, counts, slot = pl.pallas_call(
        _moe_plan_kernel,
        in_specs=[pl.BlockSpec(memory_space=pltpu.VMEM)],
        out_specs=[pl.BlockSpec(memory_space=pltpu.SMEM), pl.BlockSpec(memory_space=pltpu.SMEM),
                   pl.BlockSpec(memory_space=pltpu.VMEM)],
        out_shape=[jax.ShapeDtypeStruct((n_slots,), jnp.int32), jax.ShapeDtypeStruct((N_EXPERTS,), jnp.int32),
                   jax.ShapeDtypeStruct((rows, LANES), jnp.int32)],
        scratch_shapes=[pltpu.SMEM((rows, LANES), jnp.int32), pltpu.VMEM((n_slots,), jnp.int32),
                        pltpu.SemaphoreType.DMA((2,))],
        name="moe_plan",
    )(eid.reshape(rows, LANES))
    padded = (counts + EXPERT_BLOCK - 1) // EXPERT_BLOCK * EXPERT_BLOCK
    pad_end = jnp.cumsum(padded)
    pad_start = pad_end - padded
    block_start = jnp.arange(n_blocks, dtype=jnp.int32) * EXPERT_BLOCK
    block_expert = jnp.minimum(jnp.sum((pad_end[None, :] <= block_start[:, None]).astype(jnp.int32), axis=1),
                               N_EXPERTS - 1)
    n_valid = jnp.clip((pad_start + counts)[block_expert] - block_start, 0, EXPERT_BLOCK)
    every = jnp.arange(n_slots, dtype=jnp.int32)
    valid = every % EXPERT_BLOCK < jnp.repeat(n_valid, EXPERT_BLOCK)
    source = jnp.where(valid, table, every) % n_tok
    return block_expert, source, slot.reshape(n_assign)


def _combine(h_ref, y0_ref, y1_ref, gate_ref):
    gate = gate_ref[...]
    return h_ref[...] + (gate[:, 0:1] * y0_ref[...] + gate[:, 1:2] * y1_ref[...])


def _emit_head(r, h, outs, scr):
    tm = r.shape[0]
    if scr is not None:
        scr[h] = r
    for ref, dil in outs:
        if dil == 0:
            ref[:, h, :] = r
        elif dil == 1:
            ref[:, _chunk(h)] = r.astype(ref.dtype)
        else:
            for res in range(dil):
                col = res * D_MODEL + h * HEAD_DIM
                ref[:, col:col + HEAD_DIM] = scr[h, pl.ds(res, tm // dil, stride=dil), :].astype(ref.dtype)


def _rope(xh, cos, sin_signed):
    return xh * cos + pltpu.roll(xh, HEAD_DIM // 2, axis=1) * sin_signed


def _qkv_kernel(*refs, streams, per_seq, kept):
    (h_ref, y0_ref, y1_ref, gate_ref, gmix_ref, gkv_ref, wq_ref, wkv_ref, cos_ref, sin_ref) = refs[:10]
    outs = refs[10:]
    if streams:
        (h2_ref, q0_ref, q1_ref, q2_ref, k0_ref, k1_ref, k2_ref, v0_ref, v1_ref, v2_ref, kf_ref, vf_ref,
         scr_q1, scr_q2, scr_k, scr_v) = outs
        q_outs = [([(q0_ref, 1)], None), ([(q1_ref, DILATIONS[1])], scr_q1), ([(q2_ref, DILATIONS[2])], scr_q2)]
        k_outs = ([(k0_ref, 1), (k1_ref, DILATIONS[1]), (k2_ref, DILATIONS[2])], scr_k)
        v_outs = ([(v0_ref, 1), (v1_ref, DILATIONS[1]), (v2_ref, DILATIONS[2])], scr_v)
    else:
        h2_ref, q0_ref, q1_ref, q2_ref, kf_ref, vf_ref = outs
        q_outs = [([(q0_ref, 1)], None), ([(q1_ref, 1)], None), ([(q2_ref, 1)], None)]
        k_outs = ([(kf_ref, 0)], None)
        v_outs = ([(vf_ref, 0)], None)

    h2 = _combine(h_ref, y0_ref, y1_ref, gate_ref)
    h2_ref[...] = h2
    cos = cos_ref[...]
    sin = sin_ref[...]
    xn = _rms(h2, gmix_ref[...]).astype(_BF16)
    scale = HEAD_DIM ** -0.5
    for g in range(N_GROUPS):
        q = _dot(xn, wq_ref[:, g * D_MODEL:(g + 1) * D_MODEL])
        for h in range(N_HEADS):
            _emit_head(_rope(q[:, _chunk(h)], cos, sin) * scale, h, *q_outs[g])
    xkv = _rms(h2, gkv_ref[...]).astype(_BF16)
    k = _dot(xkv, wkv_ref[:, 0:D_MODEL])
    for h in range(N_HEADS):
        _emit_head(_rope(k[:, _chunk(h)], cos, sin), h, *k_outs)
    v = _dot(xkv, wkv_ref[:, D_MODEL:2 * D_MODEL])
    for h in range(N_HEADS):
        _emit_head(v[:, _chunk(h)], h, *v_outs)
    if streams:
        @pl.when(pl.program_id(0) % per_seq >= per_seq - kept)
        def _():
            for h in range(N_HEADS):
                kf_ref[:, h, :] = scr_k[h]
                vf_ref[:, h, :] = scr_v[h]


def _qkv(h, ypair, gate, gmix, gkv, wq, wkv, cos, sin, *, tm, streams, seq, keep):
    t, d = h.shape
    nt = t // tm
    per_seq = seq // tm
    kept = keep // tm
    assert seq % tm == 0 and keep % tm == 0

    def tail(i):
        return ((i // per_seq) * kept + jnp.maximum(i % per_seq - (per_seq - kept), 0), 0, 0)

    tail_f32 = (pl.BlockSpec((tm, N_HEADS, HEAD_DIM), tail),
                jax.ShapeDtypeStruct((t // seq * keep, N_HEADS, HEAD_DIM), _F32))
    row = lambda i: (i, 0)
    const = lambda i: (0, 0)
    nat = lambda dt: (pl.BlockSpec((tm, d), row), jax.ShapeDtypeStruct((t, d), dt))

    def stream(dil):
        return (pl.BlockSpec((tm // dil, dil * d), row), jax.ShapeDtypeStruct((t // dil, dil * d), _BF16))

    if streams:
        d1, d2 = DILATIONS[1], DILATIONS[2]
        outs = [nat(_F32), nat(_BF16), stream(d1), stream(d2), nat(_BF16), stream(d1), stream(d2),
                nat(_BF16), stream(d1), stream(d2), tail_f32, tail_f32]
        scratch = [pltpu.VMEM((N_HEADS, tm, HEAD_DIM), _F32)] * 4
    else:
        outs = [nat(_F32), nat(_BF16), nat(_BF16), nat(_BF16), tail_f32, tail_f32]
        scratch = []
    return pl.pallas_call(
        functools.partial(_qkv_kernel, streams=streams, per_seq=per_seq, kept=kept),
        grid=(nt,),
        in_specs=[
            pl.BlockSpec((tm, d), row),
            pl.BlockSpec((tm, d), row),
            pl.BlockSpec((tm, d), lambda i: (nt + i, 0)),
            pl.BlockSpec((tm, 2), row),
            pl.BlockSpec((1, d), const),
            pl.BlockSpec((1, d), const),
            pl.BlockSpec((d, 3 * d), const),
            pl.BlockSpec((d, 2 * d), const),
            pl.BlockSpec((tm, HEAD_DIM), row),
            pl.BlockSpec((tm, HEAD_DIM), row),
        ],
        out_specs=[o[0] for o in outs],
        out_shape=[o[1] for o in outs],
        scratch_shapes=scratch,
        compiler_params=pltpu.CompilerParams(
            dimension_semantics=("arbitrary",), vmem_limit_bytes=VMEM_LIMIT),
        name="qkv_proj_streams" if streams else "qkv_proj",
    )(h, ypair, ypair, gate, gmix, gkv, wq, wkv, cos, sin)


ATTN_TQ = 512


def _attn_prompt_kernel(q_ref, kp_ref, kc_ref, vp_ref, vc_ref, o_ref, st_ref, *, tq):
    i = pl.program_id(1)
    qi = lax.broadcasted_iota(jnp.int32, (SPAN, 2 * SPAN), 0)
    kj = lax.broadcasted_iota(jnp.int32, (SPAN, 2 * SPAN), 1)
    band = (kj >= qi) & (kj <= qi + SPAN)
    band0 = band & (kj >= jnp.where(i > 0, 0, SPAN))
    lane = lax.broadcasted_iota(jnp.int32, (SPAN, HEAD_DIM), 1)
    ones = jnp.ones((2 * SPAN, HEAD_DIM), _BF16)
    for a in range(tq // SPAN):
        rows = slice(a * SPAN, (a + 1) * SPAN)
        st = jnp.zeros((SPAN, HEAD_DIM), _F32)
        for h in range(N_HEADS):
            hs = _chunk(h)
            if a == 0:
                keys = jnp.concatenate([kp_ref[:, hs], kc_ref[rows, hs]], axis=0)
                vals = jnp.concatenate([vp_ref[:, hs], vc_ref[rows, hs]], axis=0)
                mask = band0
            else:
                both = slice((a - 1) * SPAN, (a + 1) * SPAN)
                keys, vals, mask = kc_ref[both, hs], vc_ref[both, hs], band
            s = jnp.where(mask, _dot_nt(q_ref[rows, hs], keys), NEG_BIG)
            m = jnp.max(s, axis=1, keepdims=True)
            p = jnp.exp(s - m).astype(_BF16)
            ol = _dot(p, jnp.concatenate([vals, ones], axis=1))
            l = ol[:, HEAD_DIM:]
            o_ref[rows, hs] = (ol[:, :HEAD_DIM] / l).astype(o_ref.dtype)
            st = jnp.where(lane == h, m + jnp.log(l), st)
        st_ref[rows, :] = st


def _attn_prompt(q, k, v, batch, dilation):
    rows, width = q.shape
    d = width // dilation
    per_batch = rows // batch
    tq = min(ATTN_TQ, per_batch)
    assert per_batch % tq == 0 and tq % SPAN == 0
    nq = per_batch // tq
    sub = tq // SPAN
    cur = lambda b, i, r: (b * nq + i, r)
    prev = lambda b, i, r: ((b * nq + i) * sub - jnp.where(i > 0, 1, 0), r)
    big = (tq, d)
    small = (SPAN, d)
    return pl.pallas_call(
        functools.partial(_attn_prompt_kernel, tq=tq),
        grid=(batch, nq, dilation),
        in_specs=[pl.BlockSpec(big, cur), pl.BlockSpec(small, prev), pl.BlockSpec(big, cur),
                  pl.BlockSpec(small, prev), pl.BlockSpec(big, cur)],
        out_specs=[pl.BlockSpec(big, cur), pl.BlockSpec((tq, HEAD_DIM), cur)],
        out_shape=[jax.ShapeDtypeStruct((rows, width), _BF16),
                   jax.ShapeDtypeStruct((rows, dilation * HEAD_DIM), _F32)],
        compiler_params=pltpu.CompilerParams(
            dimension_semantics=("arbitrary", "arbitrary", "arbitrary"), vmem_limit_bytes=VMEM_LIMIT),
        name=f"attn_prompt_d{dilation}",
    )(q, k, k, v, v)


N_NEW = 8
CACHE_A_GROUPS = (PAST_LEN - WINDOWS[1]) // DILATIONS[2]
CACHE_A_ROWS = CACHE_A_GROUPS * N_NEW
CACHE_B_ROWS = WINDOWS[1]
CACHE_B_GROUPS = CACHE_B_ROWS // DILATIONS[2]
KEYS_REAL = CACHE_A_ROWS + CACHE_B_ROWS + N_NEW
KEYS_PAD = -(-KEYS_REAL // LANES) * LANES


def _sample_key_positions():
    pos = np.full((KEYS_PAD,), -1, np.int64)
    a = np.arange(CACHE_A_ROWS)
    pos[:CACHE_A_ROWS] = (a // N_NEW) * DILATIONS[2] + a % N_NEW
    pos[CACHE_A_ROWS:CACHE_A_ROWS + CACHE_B_ROWS] = PAST_LEN - CACHE_B_ROWS + np.arange(CACHE_B_ROWS)
    pos[CACHE_A_ROWS + CACHE_B_ROWS:KEYS_REAL] = PAST_LEN + np.arange(N_NEW)
    return pos


def _sample_bias():
    pos = _sample_key_positions()
    bias = np.full((N_GROUPS * N_NEW, KEYS_PAD), NEG_BIG, np.float32)
    for g in range(N_GROUPS):
        for n in range(N_NEW):
            delta = PAST_LEN + n - pos
            ok = (pos >= 0) & (delta >= 0) & (delta <= WINDOWS[g]) & (delta % DILATIONS[g] == 0)
            assert int(ok.sum()) == SPAN + 1
            bias[g * N_NEW + n, ok] = 0.0
    return bias


def _attn_sample_kernel(q0_ref, q1_ref, q2_ref, kn_ref, vn_ref, bias_ref, ck_hbm, cv_hbm,
                        o_ref, ka, kb, va, vb, kall, vall, sem, *, n_seq):
    b = pl.program_id(0)
    slot = b % 2

    def cache_copies(seq, s):
        cps = []
        for h in range(N_HEADS):
            for src, dst_a, dst_b in ((ck_hbm, ka, kb), (cv_hbm, va, vb)):
                cps.append(pltpu.make_async_copy(
                    src.at[seq, pl.ds(0, CACHE_A_GROUPS), pl.ds(0, N_NEW), h], dst_a.at[s, h], sem.at[s]))
                cps.append(pltpu.make_async_copy(
                    src.at[seq, pl.ds(CACHE_A_GROUPS, CACHE_B_GROUPS), :, h], dst_b.at[s, h], sem.at[s]))
        return cps

    @pl.when(b == 0)
    def _():
        kall[KEYS_REAL:KEYS_PAD, :] = jnp.zeros((KEYS_PAD - KEYS_REAL, D_MODEL), _BF16)
        vall[KEYS_REAL:KEYS_PAD, :] = jnp.zeros((KEYS_PAD - KEYS_REAL, D_MODEL), _BF16)
        for cp in cache_copies(0, 0):
            cp.start()

    @pl.when(b + 1 < n_seq)
    def _():
        for cp in cache_copies(b + 1, 1 - slot):
            cp.start()

    for cp in cache_copies(b, slot):
        cp.wait()

    nb0 = CACHE_A_ROWS + CACHE_B_ROWS
    for src_a, src_b, src_n, dst in ((ka, kb, kn_ref, kall), (va, vb, vn_ref, vall)):
        for h in range(N_HEADS):
            hs = _chunk(h)
            dst[0:CACHE_A_ROWS, hs] = src_a[slot, h].reshape(CACHE_A_ROWS, HEAD_DIM).astype(_BF16)
            dst[CACHE_A_ROWS:nb0, hs] = src_b[slot, h].reshape(CACHE_B_ROWS, HEAD_DIM).astype(_BF16)
            dst[nb0:KEYS_REAL, hs] = src_n[0, :, h, :].astype(_BF16)

    bias = bias_ref[...]
    for h in range(N_HEADS):
        hs = _chunk(h)
        qh = jnp.concatenate([q0_ref[0, :, hs].astype(_F32), q1_ref[0, :, hs].astype(_F32),
                              q2_ref[0, :, hs].astype(_F32)], axis=0).astype(_BF16)
        s = _dot_nt(qh, kall[:, hs]) + bias
        m = jnp.max(s, axis=1, keepdims=True)
        m8 = jnp.maximum(jnp.maximum(m[0:N_NEW], m[N_NEW:2 * N_NEW]), m[2 * N_NEW:3 * N_NEW])
        p = jnp.exp(s - jnp.concatenate([m8, m8, m8], axis=0))
        p8 = p[0:N_NEW] + p[N_NEW:2 * N_NEW] + p[2 * N_NEW:3 * N_NEW]
        l8 = jnp.sum(p8, axis=1, keepdims=True)
        o_ref[0, :, hs] = _dot(p8.astype(_BF16), vall[:, hs]) / l8


def _attn_sample(q0, q1, q2, cache_k, cache_v, k_new, v_new):
    nb, n_new, d = q0.shape
    past = cache_k.shape[1]
    assert past == PAST_LEN and n_new == N_NEW and d == D_MODEL
    assert cache_k.shape[2:] == (N_HEADS, HEAD_DIM)
    ck = cache_k.reshape(nb, past // DILATIONS[2], DILATIONS[2], N_HEADS, HEAD_DIM)
    cv = cache_v.reshape(nb, past // DILATIONS[2], DILATIONS[2], N_HEADS, HEAD_DIM)
    bias = jnp.asarray(_sample_bias())
    new = pl.BlockSpec((1, n_new, d), lambda b: (b, 0, 0))
    new_kv = pl.BlockSpec((1, n_new, N_HEADS, HEAD_DIM), lambda b: (b, 0, 0, 0))
    part_a = (2, N_HEADS, CACHE_A_GROUPS, N_NEW, HEAD_DIM)
    part_b = (2, N_HEADS, CACHE_B_GROUPS, DILATIONS[2], HEAD_DIM)
    return pl.pallas_call(
        functools.partial(_attn_sample_kernel, n_seq=nb),
        grid=(nb,),
        in_specs=[new, new, new, new_kv, new_kv,
                  pl.BlockSpec((N_GROUPS * N_NEW, KEYS_PAD), lambda b: (0, 0)),
                  pl.BlockSpec(memory_space=pl.ANY), pl.BlockSpec(memory_space=pl.ANY)],
        out_specs=new,
        out_shape=jax.ShapeDtypeStruct((nb, n_new, d), _F32),
        scratch_shapes=[pltpu.VMEM(part_a, _F32), pltpu.VMEM(part_b, _F32),
                        pltpu.VMEM(part_a, _F32), pltpu.VMEM(part_b, _F32),
                        pltpu.VMEM((KEYS_PAD, d), _BF16), pltpu.VMEM((KEYS_PAD, d), _BF16),
                        pltpu.SemaphoreType.DMA((2,))],
        compiler_params=pltpu.CompilerParams(
            dimension_semantics=("arbitrary",), vmem_limit_bytes=VMEM_LIMIT),
        name="attn_sample",
    )(q0, q1, q2, k_new, v_new, bias, ck, cv)


def _attn_out_kernel(*refs, n_groups, tm):
    if n_groups == 1:
        (o_ref, h_ref, wo_ref, gffn_ref, wr_ref, br_ref,
         h3_ref, xn2_ref, eid_ref, gate_ref) = refs
        o = o_ref[...].astype(_BF16)
    else:
        (o0_ref, o1_ref, o2_ref, s0_ref, s1_ref, s2_ref, h_ref, wo_ref, gffn_ref, wr_ref, br_ref,
         h3_ref, xn2_ref, eid_ref, gate_ref, obuf, scr_o1, scr_o2, scr_s1, scr_s2) = refs
        for o_ref, s_ref, scr_o, scr_s, dil in ((o1_ref, s1_ref, scr_o1, scr_s1, DILATIONS[1]),
                                                (o2_ref, s2_ref, scr_o2, scr_s2, DILATIONS[2])):
            for res in range(dil):
                dst = pl.ds(res, tm // dil, stride=dil)
                scr_s[dst, :] = s_ref[:, res * HEAD_DIM:(res + 1) * HEAD_DIM]
                for h in range(N_HEADS):
                    col = res * D_MODEL + h * HEAD_DIM
                    scr_o[h, dst, :] = o_ref[:, col:col + HEAD_DIM].astype(_F32)
        sts = [s0_ref[...], scr_s1[...], scr_s2[...]]
        mx = jnp.maximum(jnp.maximum(sts[0], sts[1]), sts[2])
        es = [jnp.exp(s - mx) for s in sts]
        den = es[0] + es[1] + es[2]
        ws = [e / den for e in es]
        for h in range(N_HEADS):
            acc = ws[0][:, h:h + 1] * o0_ref[:, _chunk(h)]
            acc = acc + ws[1][:, h:h + 1] * scr_o1[h]
            acc = acc + ws[2][:, h:h + 1] * scr_o2[h]
            obuf[:, _chunk(h)] = acc.astype(_BF16)
        o = obuf[...]
    h3 = h_ref[...] + _dot(o, wo_ref[...])
    h3_ref[...] = h3
    xn2 = _rms(h3, gffn_ref[...])
    xn2_ref[...] = xn2
    _route(xn2, wr_ref, br_ref, eid_ref, gate_ref)


def _attn_out(os, sts, h, wo, gffn, wr, br, *, tm):
    t, d = h.shape
    n_groups = len(os)
    row = lambda i: (i, 0)
    const = lambda i: (0, 0)
    if n_groups == 1:
        in_specs = [pl.BlockSpec((tm, d), row)]
        scratch = []
        args = [os[0]]
    else:
        in_specs = [pl.BlockSpec((tm // dil, dil * d), row) for dil in DILATIONS]
        in_specs += [pl.BlockSpec((tm // dil, dil * HEAD_DIM), row) for dil in DILATIONS]
        scratch = [pltpu.VMEM((tm, d), _BF16),
                   pltpu.VMEM((N_HEADS, tm, HEAD_DIM), _F32), pltpu.VMEM((N_HEADS, tm, HEAD_DIM), _F32),
                   pltpu.VMEM((tm, HEAD_DIM), _F32), pltpu.VMEM((tm, HEAD_DIM), _F32)]
        args = list(os) + list(sts)
    in_specs += [pl.BlockSpec((tm, d), row), pl.BlockSpec((d, d), const), pl.BlockSpec((1, d), const),
                 pl.BlockSpec((ROUTER_ROWS, d), const), pl.BlockSpec((ROUTER_ROWS, 1), const)]
    return pl.pallas_call(
        functools.partial(_attn_out_kernel, n_groups=n_groups, tm=tm),
        grid=(t // tm,),
        in_specs=in_specs,
        out_specs=[pl.BlockSpec((tm, d), row), pl.BlockSpec((tm, d), row),
                   pl.BlockSpec((2, tm), lambda i: (0, i)), pl.BlockSpec((2, tm), lambda i: (0, i))],
        out_shape=[jax.ShapeDtypeStruct((t, d), _F32), jax.ShapeDtypeStruct((t, d), _F32),
                   jax.ShapeDtypeStruct((2, t), jnp.int32), jax.ShapeDtypeStruct((2, t), _F32)],
        scratch_shapes=scratch,
        compiler_params=pltpu.CompilerParams(
            dimension_semantics=("arbitrary",), vmem_limit_bytes=VMEM_LIMIT),
        name=f"attn_out_g{n_groups}",
    )(*args, h, wo, gffn, wr, br)


def _final_kernel(h_ref, y0_ref, y1_ref, gate_ref, g_ref, out_ref):
    out_ref[...] = _rms(_combine(h_ref, y0_ref, y1_ref, gate_ref), g_ref[...])


def _final(h, ypair, gate, g, *, tm):
    t, d = h.shape
    nt = t // tm
    row = lambda i: (i, 0)
    return pl.pallas_call(
        _final_kernel,
        grid=(nt,),
        in_specs=[pl.BlockSpec((tm, d), row),
                  pl.BlockSpec((tm, d), row),
                  pl.BlockSpec((tm, d), lambda i: (nt + i, 0)),
                  pl.BlockSpec((tm, 2), row),
                  pl.BlockSpec((1, d), lambda i: (0, 0))],
        out_specs=pl.BlockSpec((tm, d), row),
        out_shape=jax.ShapeDtypeStruct((t, d), _F32),
        compiler_params=pltpu.CompilerParams(
            dimension_semantics=("arbitrary",), vmem_limit_bytes=VMEM_LIMIT),
        name="final_norm",
    )(h, ypair, ypair, gate, g)


def _rope_tables(pos):
    half = HEAD_DIM // 2
    inv_freq = jnp.power(jnp.float32(ROPE_THETA), -jnp.arange(half, dtype=jnp.float32) / half)
    ang = pos.astype(jnp.float32)[:, None] * inv_freq[None, :]
    cos = jnp.cos(ang)
    sin = jnp.sin(ang)
    return jnp.concatenate([cos, cos], axis=-1), jnp.concatenate([-sin, sin], axis=-1)


def _router_params(wg, bg, we, be):
    wr = jnp.zeros((ROUTER_ROWS, D_MODEL), _F32)
    wr = wr.at[0:MOE_GROUPS].set(wg.T).at[EXPERT_ROW0:EXPERT_ROW0 + N_EXPERTS].set(we.T)
    br = jnp.zeros((ROUTER_ROWS, 1), _F32)
    br = br.at[0:MOE_GROUPS, 0].set(bg).at[EXPERT_ROW0:EXPERT_ROW0 + N_EXPERTS, 0].set(be)
    return wr, br


def _moe(xn2, eid, experts, layer):
    block_expert, source, slot = _moe_plan(eid, xn2.shape[0])
    xs = _row_gather(xn2, source)
    yield
    ys = _moe_ffn(block_expert, xs, *experts, layer)
    ypair = _row_gather(ys, slot)
    yield
    return ypair


def _forward(x3, prev3, pos_rows, attend, p, *, shift, tm, streams, keep):
    h1, xn2, eid, gate, state = _conv_layer(
        x3, prev3, p['gmix'][0], p['win'], p['ck'], p['wout'], p['gffn'][0], *p['router'][0],
        shift=shift, tm=tm)
    t = x3.shape[0] * x3.shape[1]
    ypair = yield from _moe(xn2, eid, p['experts'], 0)
    cos, sin = _rope_tables(pos_rows)
    h2, *qkv, kf, vf = _qkv(h1.reshape(t, D_MODEL), ypair, gate.T, p['gmix'][1], p['gkv'],
                            p['wq'], p['wkv'], cos, sin, tm=QKV_TILE, streams=streams,
                            seq=x3.shape[1], keep=keep)
    os, sts = attend(qkv, kf, vf)
    h3, xn2, eid, gate = _attn_out(os, sts, h2, p['wo'], p['gffn'][1], *p['router'][1],
                                   tm=min(ATTN_OUT_TILE, t))
    ypair = yield from _moe(xn2, eid, p['experts'], 1)
    y = _final(h3, ypair, gate.T, p['gfinal'], tm=FINAL_TILE)
    return y, state, kf, vf


def _interleave(*paths):
    results = [None] * len(paths)
    live = list(enumerate(paths))
    while live:
        still = []
        for i, gen in live:
            try:
                next(gen)
                still.append((i, gen))
            except StopIteration as done:
                results[i] = done.value
        live = still
    return results


def kernel(x_prompt, x_sample, cache_k, cache_v, state_conv, norm_mix, norm_ffn, norm_kv, norm_final,
           conv_w_in, conv_kernel, conv_w_out, attn_w_q, attn_w_kv, attn_w_o, router_group_w,
           router_group_b, router_expert_w, router_expert_b, expert_w1, expert_w3, expert_w2):
    b_p, s_p, d = x_prompt.shape
    b_s, n_new, _ = x_sample.shape
    assert d == D_MODEL and n_new == N_NEW and s_p % (DILATIONS[2] * SPAN) == 0
    assert norm_mix.shape[0] == 2 and conv_w_in.shape[0] == 1 and attn_w_q.shape[0] == 1
    assert cache_k.shape[1] == PAST_LEN

    p = {
        'gmix': [norm_mix[l].reshape(1, d) for l in range(2)],
        'gffn': [norm_ffn[l].reshape(1, d) for l in range(2)],
        'gkv': norm_kv.reshape(1, d),
        'gfinal': norm_final.reshape(1, d),
        'win': conv_w_in[0].astype(_BF16),
        'ck': conv_kernel[0],
        'wout': conv_w_out[0].astype(_BF16),
        'wq': attn_w_q[0].astype(_BF16),
        'wkv': attn_w_kv.astype(_BF16),
        'wo': attn_w_o[0].astype(_BF16),
        'router': [_router_params(router_group_w[l], router_group_b[l], router_expert_w[l], router_expert_b[l])
                   for l in range(2)],
        'experts': (expert_w1, expert_w3, expert_w2),
    }

    def attend_prompt(qkv, kf, vf):
        q0, q1, q2, k0, k1, k2, v0, v1, v2 = qkv
        os, sts = [], []
        for q, k, v, dil in ((q0, k0, v0, DILATIONS[0]), (q1, k1, v1, DILATIONS[1]), (q2, k2, v2, DILATIONS[2])):
            o, st = _attn_prompt(q, k, v, b_p, dil)
            os.append(o)
            sts.append(st)
        return os, sts

    pos_p = jnp.tile(jnp.arange(s_p, dtype=jnp.int32), b_p)
    zero_state = jnp.zeros((b_p, CONV_WIDTH - 1, d), x_prompt.dtype)
    keep = min(max(WINDOWS), s_p)
    prompt_path = _forward(x_prompt, zero_state, pos_p, attend_prompt, p, shift=1, tm=CONV_TILE,
                           streams=True, keep=keep)

    halves = 2
    bh = b_s // halves

    def to_rows(a):
        w = a.shape[-1]
        return a.reshape(halves, bh, n_new, w).transpose(0, 2, 1, 3).reshape(halves * n_new * bh, w)

    def to_batch(a):
        w = a.shape[1:]
        return jnp.swapaxes(a.reshape(halves, n_new, bh, *w), 1, 2).reshape(b_s, n_new, *w)

    def attend_sample(qkv, kf, vf):
        q0, q1, q2 = qkv
        o = _attn_sample(to_batch(q0), to_batch(q1), to_batch(q2), cache_k, cache_v,
                         to_batch(kf), to_batch(vf))
        return [to_rows(o)], None

    x_s = to_rows(x_sample).reshape(halves, n_new * bh, d)
    prev_s = state_conv[0].reshape(halves, bh, CONV_WIDTH - 1, d).transpose(0, 2, 1, 3).reshape(
        halves, (CONV_WIDTH - 1) * bh, d)
    pos_s = jnp.tile(jnp.repeat(PAST_LEN + jnp.arange(n_new, dtype=jnp.int32), bh), halves)
    sample_path = _forward(x_s, prev_s, pos_s, attend_sample, p, shift=bh, tm=n_new * bh,
                           streams=False, keep=n_new * bh)

    (y_p, st_p, kf_p, vf_p), (y_s, st_s, kf_s, vf_s) = _interleave(prompt_path, sample_path)
    y_prompt = y_p.reshape(b_p, s_p, d)
    k_p = kf_p.reshape(b_p, keep, N_HEADS, HEAD_DIM)
    v_p = vf_p.reshape(b_p, keep, N_HEADS, HEAD_DIM)
    conv_p = st_p[None]
    y_sample = to_batch(y_s)
    k_s = to_batch(kf_s)
    v_s = to_batch(vf_s)
    conv_s = st_s.reshape(halves, CONV_WIDTH - 1, bh, d).transpose(0, 2, 1, 3).reshape(
        b_s, CONV_WIDTH - 1, d)[None]

    return (y_prompt, y_sample, k_p, v_p, conv_p, k_s, v_s, conv_s)
```

```python
import functools

import numpy as np
import jax
import jax.numpy as jnp
from jax import lax
from jax.experimental import pallas as pl
from jax.experimental.pallas import tpu as pltpu
from jax.experimental.pallas import tpu_sc as plsc

D_MODEL = 1024
CONV_WIDTH = 3
WINDOWS = (128, 512, 2048)
DILATIONS = (1, 4, 16)
N_GROUPS = 3
N_HEADS = 8
HEAD_DIM = 128
SPAN = 128
ROPE_THETA = 10000.0
MOE_GROUPS = 4
EXPERTS_PER_GROUP = 4
N_EXPERTS = 16
D_EXPERT = 512
EXPERT_BLOCK = 256
RMS_EPS = 1e-6
NEG_BIG = -1e30
PAST_LEN = 2048

LANES = 128
SUBLANES = 8
ROUTER_ROWS = 32
EXPERT_ROW0 = 8
VMEM_LIMIT = 56 * 1024 * 1024

CONV_TILE = 512
QKV_TILE = 512
ATTN_OUT_TILE = 512
FINAL_TILE = 512

_F32 = jnp.float32
_BF16 = jnp.bfloat16


def _rms(x, g):
    ms = jnp.mean(x * x, axis=-1, keepdims=True)
    return (x * lax.rsqrt(ms + RMS_EPS)) * g


def _dot(a, b):
    return jnp.dot(a, b, preferred_element_type=_F32)


def _dot_nt(a, b, precision=None):
    return lax.dot_general(a, b, (((1,), (1,)), ((), ())), precision=precision,
                           preferred_element_type=_F32)


def _chunk(c):
    return slice(c * LANES, (c + 1) * LANES)


def _route(xn, wr_ref, br_ref, eid_ref, gate_ref):
    logits = _dot_nt(wr_ref[...], xn, precision=lax.Precision.HIGHEST) + br_ref[...]
    lg = logits[0:MOE_GROUPS]
    row = lax.broadcasted_iota(jnp.int32, lg.shape, 0).astype(_F32)
    eg = jnp.exp(lg - jnp.max(lg, axis=0, keepdims=True))
    pg = eg / jnp.sum(eg, axis=0, keepdims=True)
    gp = jnp.max(pg, axis=0, keepdims=True)
    gi = jnp.min(jnp.where(pg == gp, row, float(MOE_GROUPS)), axis=0, keepdims=True)
    le = jnp.zeros_like(lg)
    for g in range(MOE_GROUPS):
        r0 = EXPERT_ROW0 + g * EXPERTS_PER_GROUP
        le = le + jnp.where(gi == float(g), logits[r0:r0 + EXPERTS_PER_GROUP], 0.0)
    ee = jnp.exp(le - jnp.max(le, axis=0, keepdims=True))
    ev = ee / jnp.sum(ee, axis=0, keepdims=True)
    v1 = jnp.max(ev, axis=0, keepdims=True)
    i1 = jnp.min(jnp.where(ev == v1, row, float(EXPERTS_PER_GROUP)), axis=0, keepdims=True)
    ev2 = jnp.where(row == i1, -1.0, ev)
    v2 = jnp.max(ev2, axis=0, keepdims=True)
    i2 = jnp.min(jnp.where(ev2 == v2, row, float(EXPERTS_PER_GROUP)), axis=0, keepdims=True)
    den = v1 + v2
    ids = jnp.concatenate([gi * EXPERTS_PER_GROUP + i1, gi * EXPERTS_PER_GROUP + i2], axis=0)
    eid_ref[...] = ids.astype(jnp.int32)
    gate_ref[...] = jnp.concatenate([gp * v1 / den, gp * v2 / den], axis=0)


def _conv_layer_kernel(x_ref, prev_ref, gmix_ref, win_ref, ck_ref, wout_ref, gffn_ref, wr_ref, br_ref,
                       h_ref, xn2_ref, eid_ref, gate_ref, st_ref, ubuf, *, shift, tm):
    i = pl.program_id(1)
    halo = ubuf.shape[0] - tm
    keep = (CONV_WIDTH - 1) * shift

    @pl.when(i == 0)
    def _():
        ubuf[halo - keep:halo, :] = prev_ref[0]

    @pl.when(i > 0)
    def _():
        ubuf[halo - keep:halo, :] = ubuf[halo + tm - keep:halo + tm, :]

    x = x_ref[0]
    xn = _rms(x, gmix_ref[...]).astype(_BF16)
    c_gate = _dot(xn, win_ref[:, 0:D_MODEL])
    hid = _dot(xn, win_ref[:, 2 * D_MODEL:3 * D_MODEL])
    u = c_gate * hid
    ubuf[halo:halo + tm, :] = u
    u1 = ubuf[halo - shift:halo - shift + tm, :]
    u2 = ubuf[halo - 2 * shift:halo - 2 * shift + tm, :]
    conv = ck_ref[0:1, :] * u2 + ck_ref[1:2, :] * u1 + ck_ref[2:3, :] * u
    b_gate = _dot(xn, win_ref[:, D_MODEL:2 * D_MODEL])
    y = _dot((b_gate * conv).astype(_BF16), wout_ref[...])
    h = x + y
    h_ref[0] = h
    st_ref[0] = ubuf[halo + tm - keep:halo + tm, :]
    xn2 = _rms(h, gffn_ref[...])
    xn2_ref[...] = xn2
    _route(xn2, wr_ref, br_ref, eid_ref, gate_ref)


def _conv_layer(x3, prev3, gmix, win, ck, wout, gffn, wr, br, *, shift, tm):
    nb, s, d = x3.shape
    nt = s // tm
    keep = (CONV_WIDTH - 1) * shift
    halo = -(-keep // SUBLANES) * SUBLANES
    t_all = nb * s
    const = lambda b, i: (0, 0)
    return pl.pallas_call(
        functools.partial(_conv_layer_kernel, shift=shift, tm=tm),
        grid=(nb, nt),
        in_specs=[
            pl.BlockSpec((1, tm, d), lambda b, i: (b, i, 0)),
            pl.BlockSpec((1, keep, d), lambda b, i: (b, 0, 0)),
            pl.BlockSpec((1, d), const),
            pl.BlockSpec((d, 3 * d), const),
            pl.BlockSpec((CONV_WIDTH, d), const),
            pl.BlockSpec((d, d), const),
            pl.BlockSpec((1, d), const),
            pl.BlockSpec((ROUTER_ROWS, d), const),
            pl.BlockSpec((ROUTER_ROWS, 1), const),
        ],
        out_specs=[
            pl.BlockSpec((1, tm, d), lambda b, i: (b, i, 0)),
            pl.BlockSpec((tm, d), lambda b, i: (b * nt + i, 0)),
            pl.BlockSpec((2, tm), lambda b, i: (0, b * nt + i)),
            pl.BlockSpec((2, tm), lambda b, i: (0, b * nt + i)),
            pl.BlockSpec((1, keep, d), lambda b, i: (b, 0, 0)),
        ],
        out_shape=[
            jax.ShapeDtypeStruct((nb, s, d), _F32),
            jax.ShapeDtypeStruct((t_all, d), _F32),
            jax.ShapeDtypeStruct((2, t_all), jnp.int32),
            jax.ShapeDtypeStruct((2, t_all), _F32),
            jax.ShapeDtypeStruct((nb, keep, d), _F32),
        ],
        scratch_shapes=[pltpu.VMEM((halo + tm, d), _F32)],
        compiler_params=pltpu.CompilerParams(
            dimension_semantics=("arbitrary", "arbitrary"), vmem_limit_bytes=VMEM_LIMIT),
        name="conv_layer",
    )(x3, prev3, gmix, win, ck, wout, gffn, wr, br)


SC_WINDOW = 32


def _row_gather(src, idx):
    n_out = idx.shape[0]
    mesh = plsc.VectorSubcoreMesh(core_axis_name="c", subcore_axis_name="s")
    n_cores = mesh.num_cores
    per_worker = n_out // (n_cores * mesh.num_subcores)
    n_win = per_worker // SC_WINDOW
    assert per_worker * n_cores * mesh.num_subcores == n_out and n_win * SC_WINDOW == per_worker
    assert n_win % 2 == 0

    @pl.kernel(out_type=jax.ShapeDtypeStruct((n_out,) + src.shape[1:], src.dtype), mesh=mesh,
               scratch_types=[pltpu.VMEM((per_worker,), jnp.int32),
                              pltpu.VMEM((2, SC_WINDOW) + src.shape[1:], src.dtype),
                              pltpu.SemaphoreType.DMA((2,)), pltpu.SemaphoreType.DMA((2,))])
    def gather(src_hbm, idx_hbm, out_hbm, idx_v, rows_v, gsem, wsem):
        base = (lax.axis_index("s") * n_cores + lax.axis_index("c")) * per_worker
        pltpu.sync_copy(idx_hbm.at[pl.ds(base, per_worker)], idx_v)

        def gather_copy(q, b):
            return pltpu.make_async_copy(src_hbm.at[idx_v.at[pl.ds(q * SC_WINDOW, SC_WINDOW)]],
                                         rows_v.at[b], gsem.at[b])

        def write_copy(q, b):
            return pltpu.make_async_copy(rows_v.at[b], out_hbm.at[pl.ds(base + q * SC_WINDOW, SC_WINDOW)],
                                         wsem.at[b])

        gather_copy(0, 0).start()

        @pl.loop(0, n_win, step=2)
        def _(q0):
            for b in range(2):
                q = q0 + b
                gather_copy(q, b).wait()

                @pl.when(q + 1 < n_win)
                def _():
                    @pl.when(q >= 1)
                    def _():
                        write_copy(q - 1, 1 - b).wait()
                    gather_copy(q + 1, 1 - b).start()

                write_copy(q, b).start()

        write_copy(n_win - 2, 0).wait()
        write_copy(n_win - 1, 1).wait()

    return gather(src, idx)


def _moe_ffn_kernel(be_ref, x_ref, w1_ref, w3_ref, w2_ref, y_ref, w1b, w3b, w2b):
    j = pl.program_id(0)

    @pl.when((j == 0) | (be_ref[j] != be_ref[jnp.maximum(j - 1, 0)]))
    def _():
        w1b[...] = w1_ref[0, 0].astype(_BF16)
        w3b[...] = w3_ref[0, 0].astype(_BF16)
        w2b[...] = w2_ref[0, 0].astype(_BF16)

    x = x_ref[...].astype(_BF16)
    a = _dot(x, w1b[...])
    b = _dot(x, w3b[...])
    y_ref[...] = _dot((jax.nn.silu(a) * b).astype(_BF16), w2b[...])


def _moe_ffn(block_expert, xs, w1, w3, w2, layer):
    n_slots, d = xs.shape
    n_blocks = block_expert.shape[0]
    wmap = lambda j, be: (layer, be[j], 0, 0)
    blk = pl.BlockSpec((EXPERT_BLOCK, d), lambda j, be: (j, 0))
    return pl.pallas_call(
        _moe_ffn_kernel,
        grid_spec=pltpu.PrefetchScalarGridSpec(
            num_scalar_prefetch=1,
            grid=(n_blocks,),
            in_specs=[blk,
                      pl.BlockSpec((1, 1, d, D_EXPERT), wmap),
                      pl.BlockSpec((1, 1, d, D_EXPERT), wmap),
                      pl.BlockSpec((1, 1, D_EXPERT, d), wmap)],
            out_specs=blk,
            scratch_shapes=[pltpu.VMEM((d, D_EXPERT), _BF16), pltpu.VMEM((d, D_EXPERT), _BF16),
                            pltpu.VMEM((D_EXPERT, d), _BF16)],
        ),
        out_shape=jax.ShapeDtypeStruct((n_slots, d), _F32),
        compiler_params=pltpu.CompilerParams(
            dimension_semantics=("arbitrary",), vmem_limit_bytes=VMEM_LIMIT),
        name="moe_ffn",
    )(block_expert, xs, w1, w3, w2)


def _moe_plan_kernel(eid_ref, tab_ref, cnt_ref, slot_v, slot_s, zeros_v, sem):
    rows = eid_ref.shape[0]
    depth = max(rows, LANES)
    eid = eid_ref[...]
    r_i = lax.broadcasted_iota(jnp.int32, (LANES, LANES), 0)
    c_i = lax.broadcasted_iota(jnp.int32, (LANES, LANES), 1)
    earlier_lane = jnp.where(r_i < c_i, 1.0, 0.0).astype(_BF16)
    all_lanes = jnp.ones((LANES, LANES), _BF16)
    q_i = lax.broadcasted_iota(jnp.int32, (rows, depth), 1)
    p_i = lax.broadcasted_iota(jnp.int32, (rows, depth), 0)
    earlier_row = jnp.where(q_i < p_i, 1.0, 0.0).astype(_BF16)
    slot = jnp.zeros((rows, LANES), _F32)
    start = jnp.int32(0)
    for ex in range(N_EXPERTS):
        hit = eid == ex
        hot = jnp.where(hit, 1.0, 0.0).astype(_BF16)
        in_row = _dot(hot, earlier_lane)
        row_tot = _dot(hot, all_lanes)
        if depth > rows:
            row_tot = jnp.concatenate([row_tot, jnp.zeros((depth - rows, LANES), _F32)], axis=0)
        above = _dot(earlier_row, row_tot.astype(_BF16))
        count = jnp.sum(jnp.where(hit, 1, 0))
        slot = jnp.where(hit, start.astype(_F32) + in_row + above, slot)
        cnt_ref[ex] = count
        start = start + (count + EXPERT_BLOCK - 1) // EXPERT_BLOCK * EXPERT_BLOCK
    slot_v[...] = slot.astype(jnp.int32)
    to_smem = pltpu.make_async_copy(slot_v, slot_s, sem.at[0])
    to_smem.start()
    zeros_v[...] = jnp.zeros(zeros_v.shape, jnp.int32)
    clear = pltpu.make_async_copy(zeros_v, tab_ref, sem.at[1])
    clear.start()
    to_smem.wait()
    clear.wait()

    def place(r, carry):
        for c in range(LANES):
            tab_ref[slot_s[r, c]] = r * LANES + c
        return carry
    lax.fori_loop(0, rows, place, 0)


def _moe_plan(eid, n_tok):
    n_assign = 2 * n_tok
    n_blocks = -(-(n_assign + N_EXPERTS * (EXPERT_BLOCK - 1)) // EXPERT_BLOCK)
    n_slots = n_blocks * EXPERT_BLOCK
    rows = n_assign // LANES
    assert n_assign % LANES == 0
    table, counts, slot = pl.pallas_call(
        _moe_plan_kernel,
        in_specs=[pl.BlockSpec(memory_space=pltpu.VMEM)],
        out_specs=[pl.BlockSpec(memory_space=pltpu.SMEM), pl.BlockSpec(memory_space=pltpu.SMEM),
                   pl.BlockSpec(memory_space=pltpu.VMEM)],
        out_shape=[jax.ShapeDtypeStruct((n_slots,), jnp.int32), jax.ShapeDtypeStruct((N_EXPERTS,), jnp.int32),
                   jax.ShapeDtypeStruct((rows, LANES), jnp.int32)],
        scratch_shapes=[pltpu.SMEM((rows, LANES), jnp.int32), pltpu.VMEM((n_slots,), jnp.int32),
                        pltpu.SemaphoreType.DMA((2,))],
        name="moe_plan",
    )(eid.reshape(rows, LANES))
    padded = (counts + EXPERT_BLOCK - 1) // EXPERT_BLOCK * EXPERT_BLOCK
    pad_end = jnp.cumsum(padded)
    pad_start = pad_end - padded
    block_start = jnp.arange(n_blocks, dtype=jnp.int32) * EXPERT_BLOCK
    block_expert = jnp.minimum(jnp.sum((pad_end[None, :] <= block_start[:, None]).astype(jnp.int32), axis=1),
                               N_EXPERTS - 1)
    n_valid = jnp.clip((pad_start + counts)[block_expert] - block_start, 0, EXPERT_BLOCK)
    every = jnp.arange(n_slots, dtype=jnp.int32)
    valid = every % EXPERT_BLOCK < jnp.repeat(n_valid, EXPERT_BLOCK)
    source = jnp.where(valid, table, every) % n_tok
    return block_expert, source, slot.reshape(n_assign)


def _combine(h_ref, y0_ref, y1_ref, gate_ref):
    gate = gate_ref[...]
    return h_ref[...] + (gate[:, 0:1] * y0_ref[...] + gate[:, 1:2] * y1_ref[...])


def _emit_head(r, h, outs, scr):
    tm = r.shape[0]
    if scr is not None:
        scr[h] = r
    for ref, dil in outs:
        if dil == 0:
            ref[:, h, :] = r
        elif dil == 1:
            ref[:, _chunk(h)] = r.astype(ref.dtype)
        else:
            for res in range(dil):
                col = res * D_MODEL + h * HEAD_DIM
                ref[:, col:col + HEAD_DIM] = scr[h, pl.ds(res, tm // dil, stride=dil), :].astype(ref.dtype)


def _rope(xh, cos, sin_signed):
    return xh * cos + pltpu.roll(xh, HEAD_DIM // 2, axis=1) * sin_signed


def _q_kernel(*refs, streams):
    (h_ref, y0_ref, y1_ref, gate_ref, gmix_ref, wq_ref, cos_ref, sin_ref, h2_ref, q0_ref, q1_ref, q2_ref) = refs[:12]
    if streams:
        scr_q1, scr_q2 = refs[12:]
        q_outs = [([(q0_ref, 1)], None), ([(q1_ref, DILATIONS[1])], scr_q1), ([(q2_ref, DILATIONS[2])], scr_q2)]
    else:
        q_outs = [([(q0_ref, 1)], None), ([(q1_ref, 1)], None), ([(q2_ref, 1)], None)]
    h2 = _combine(h_ref, y0_ref, y1_ref, gate_ref)
    h2_ref[...] = h2
    cos = cos_ref[...]
    sin = sin_ref[...]
    xn = _rms(h2, gmix_ref[...]).astype(_BF16)
    scale = HEAD_DIM ** -0.5
    for g in range(N_GROUPS):
        q = _dot(xn, wq_ref[:, g * D_MODEL:(g + 1) * D_MODEL])
        for h in range(N_HEADS):
            _emit_head(_rope(q[:, _chunk(h)], cos, sin) * scale, h, *q_outs[g])


def _kv_kernel(*refs, streams, per_seq, kept):
    h2_ref, gkv_ref, wkv_ref, cos_ref, sin_ref = refs[:5]
    if streams:
        k0_ref, k1_ref, k2_ref, v0_ref, v1_ref, v2_ref, kf_ref, vf_ref, scr_k, scr_v = refs[5:]
        k_outs = ([(k0_ref, 1), (k1_ref, DILATIONS[1]), (k2_ref, DILATIONS[2])], scr_k)
        v_outs = ([(v0_ref, 1), (v1_ref, DILATIONS[1]), (v2_ref, DILATIONS[2])], scr_v)
    else:
        kf_ref, vf_ref = refs[5:]
        k_outs = ([(kf_ref, 0)], None)
        v_outs = ([(vf_ref, 0)], None)
    cos = cos_ref[...]
    sin = sin_ref[...]
    xkv = _rms(h2_ref[...], gkv_ref[...]).astype(_BF16)
    k = _dot(xkv, wkv_ref[:, 0:D_MODEL])
    for h in range(N_HEADS):
        _emit_head(_rope(k[:, _chunk(h)], cos, sin), h, *k_outs)
    v = _dot(xkv, wkv_ref[:, D_MODEL:2 * D_MODEL])
    for h in range(N_HEADS):
        _emit_head(v[:, _chunk(h)], h, *v_outs)
    if streams:
        @pl.when(pl.program_id(0) % per_seq >= per_seq - kept)
        def _():
            for h in range(N_HEADS):
                kf_ref[:, h, :] = scr_k[h]
                vf_ref[:, h, :] = scr_v[h]


def _qkv(h, ypair, gate, gmix, gkv, wq, wkv, cos, sin, *, tm, streams, seq, keep):
    t, d = h.shape
    tm = min(tm, seq)
    nt = t // tm
    per_seq = seq // tm
    kept = keep // tm
    assert seq % tm == 0 and keep % tm == 0

    def tail(i):
        return ((i // per_seq) * kept + jnp.maximum(i % per_seq - (per_seq - kept), 0), 0, 0)

    tail_f32 = (pl.BlockSpec((tm, N_HEADS, HEAD_DIM), tail),
                jax.ShapeDtypeStruct((t // seq * keep, N_HEADS, HEAD_DIM), _F32))
    row = lambda i: (i, 0)
    const = lambda i: (0, 0)
    nat = lambda dt: (pl.BlockSpec((tm, d), row), jax.ShapeDtypeStruct((t, d), dt))
    tables = [pl.BlockSpec((tm, HEAD_DIM), row), pl.BlockSpec((tm, HEAD_DIM), row)]
    params = pltpu.CompilerParams(dimension_semantics=("arbitrary",), vmem_limit_bytes=VMEM_LIMIT)
    head_scratch = pltpu.VMEM((N_HEADS, tm, HEAD_DIM), _F32)

    def stream(dil):
        return (pl.BlockSpec((tm // dil, dil * d), row), jax.ShapeDtypeStruct((t // dil, dil * d), _BF16))

    d1, d2 = DILATIONS[1], DILATIONS[2]
    q_outs = [nat(_F32), nat(_BF16)] + ([stream(d1), stream(d2)] if streams else [nat(_BF16), nat(_BF16)])
    h2, *qs = pl.pallas_call(
        functools.partial(_q_kernel, streams=streams),
        grid=(nt,),
        in_specs=[pl.BlockSpec((tm, d), row), pl.BlockSpec((tm, d), row),
                  pl.BlockSpec((tm, d), lambda i: (nt + i, 0)), pl.BlockSpec((tm, 2), row),
                  pl.BlockSpec((1, d), const), pl.BlockSpec((d, 3 * d), const)] + tables,
        out_specs=[o[0] for o in q_outs],
        out_shape=[o[1] for o in q_outs],
        scratch_shapes=[head_scratch] * 2 if streams else [],
        compiler_params=params,
        name="q_proj_streams" if streams else "q_proj",
    )(h, ypair, ypair, gate, gmix, wq, cos, sin)

    if streams:
        kv_outs = [nat(_BF16), stream(d1), stream(d2), nat(_BF16), stream(d1), stream(d2), tail_f32, tail_f32]
    else:
        kv_outs = [tail_f32, tail_f32]
    *kvs, kf, vf = pl.pallas_call(
        functools.partial(_kv_kernel, streams=streams, per_seq=per_seq, kept=kept),
        grid=(nt,),
        in_specs=[pl.BlockSpec((tm, d), row), pl.BlockSpec((1, d), const), pl.BlockSpec((d, 2 * d), const)] + tables,
        out_specs=[o[0] for o in kv_outs],
        out_shape=[o[1] for o in kv_outs],
        scratch_shapes=[head_scratch] * 2 if streams else [],
        compiler_params=params,
        name="kv_proj_streams" if streams else "kv_proj",
    )(h2, gkv, wkv, cos, sin)
    return (h2, *qs, *kvs, kf, vf)


ATTN_TQ = 512


def _attn_prompt_kernel(q_ref, kp_ref, kc_ref, vp_ref, vc_ref, o_ref, st_ref, *, tq):
    i = pl.program_id(1)
    qi = lax.broadcasted_iota(jnp.int32, (SPAN, 2 * SPAN), 0)
    kj = lax.broadcasted_iota(jnp.int32, (SPAN, 2 * SPAN), 1)
    band = (kj >= qi) & (kj <= qi + SPAN)
    band0 = band & (kj >= jnp.where(i > 0, 0, SPAN))
    lane = lax.broadcasted_iota(jnp.int32, (SPAN, HEAD_DIM), 1)
    ones = jnp.ones((2 * SPAN, HEAD_DIM), _BF16)
    for a in range(tq // SPAN):
        rows = slice(a * SPAN, (a + 1) * SPAN)
        st = jnp.zeros((SPAN, HEAD_DIM), _F32)
        for h in range(N_HEADS):
            hs = _chunk(h)
            if a == 0:
                keys = jnp.concatenate([kp_ref[:, hs], kc_ref[rows, hs]], axis=0)
                vals = jnp.concatenate([vp_ref[:, hs], vc_ref[rows, hs]], axis=0)
                mask = band0
            else:
                both = slice((a - 1) * SPAN, (a + 1) * SPAN)
                keys, vals, mask = kc_ref[both, hs], vc_ref[both, hs], band
            s = jnp.where(mask, _dot_nt(q_ref[rows, hs], keys), NEG_BIG)
            m = jnp.max(s, axis=1, keepdims=True)
            p = jnp.exp(s - m).astype(_BF16)
            ol = _dot(p, jnp.concatenate([vals, ones], axis=1))
            l = ol[:, HEAD_DIM:]
            o_ref[rows, hs] = (ol[:, :HEAD_DIM] / l).astype(o_ref.dtype)
            st = jnp.where(lane == h, m + jnp.log(l), st)
        st_ref[rows, :] = st


def _attn_prompt(q, k, v, batch, dilation):
    rows, width = q.shape
    d = width // dilation
    per_batch = rows // batch
    tq = min(ATTN_TQ, per_batch)
    assert per_batch % tq == 0 and tq % SPAN == 0
    nq = per_batch // tq
    sub = tq // SPAN
    cur = lambda b, i, r: (b * nq + i, r)
    prev = lambda b, i, r: ((b * nq + i) * sub - jnp.where(i > 0, 1, 0), r)
    big = (tq, d)
    small = (SPAN, d)
    return pl.pallas_call(
        functools.partial(_attn_prompt_kernel, tq=tq),
        grid=(batch, nq, dilation),
        in_specs=[pl.BlockSpec(big, cur), pl.BlockSpec(small, prev), pl.BlockSpec(big, cur),
                  pl.BlockSpec(small, prev), pl.BlockSpec(big, cur)],
        out_specs=[pl.BlockSpec(big, cur), pl.BlockSpec((tq, HEAD_DIM), cur)],
        out_shape=[jax.ShapeDtypeStruct((rows, width), _BF16),
                   jax.ShapeDtypeStruct((rows, dilation * HEAD_DIM), _F32)],
        compiler_params=pltpu.CompilerParams(
            dimension_semantics=("arbitrary", "arbitrary", "arbitrary"), vmem_limit_bytes=VMEM_LIMIT),
        name=f"attn_prompt_d{dilation}",
    )(q, k, k, v, v)


N_NEW = 8
CACHE_A_GROUPS = (PAST_LEN - WINDOWS[1]) // DILATIONS[2]
CACHE_A_ROWS = CACHE_A_GROUPS * N_NEW
CACHE_B_ROWS = WINDOWS[1]
CACHE_B_GROUPS = CACHE_B_ROWS // DILATIONS[2]
KEYS_REAL = CACHE_A_ROWS + CACHE_B_ROWS + N_NEW
KEYS_PAD = -(-KEYS_REAL // LANES) * LANES


def _sample_key_positions():
    pos = np.full((KEYS_PAD,), -1, np.int64)
    a = np.arange(CACHE_A_ROWS)
    pos[:CACHE_A_ROWS] = (a // N_NEW) * DILATIONS[2] + a % N_NEW
    pos[CACHE_A_ROWS:CACHE_A_ROWS + CACHE_B_ROWS] = PAST_LEN - CACHE_B_ROWS + np.arange(CACHE_B_ROWS)
    pos[CACHE_A_ROWS + CACHE_B_ROWS:KEYS_REAL] = PAST_LEN + np.arange(N_NEW)
    return pos


def _sample_bias():
    pos = _sample_key_positions()
    bias = np.full((N_GROUPS * N_NEW, KEYS_PAD), NEG_BIG, np.float32)
    for g in range(N_GROUPS):
        for n in range(N_NEW):
            delta = PAST_LEN + n - pos
            ok = (pos >= 0) & (delta >= 0) & (delta <= WINDOWS[g]) & (delta % DILATIONS[g] == 0)
            assert int(ok.sum()) == SPAN + 1
            bias[g * N_NEW + n, ok] = 0.0
    return bias


def _attn_sample_kernel(q0_ref, q1_ref, q2_ref, kn_ref, vn_ref, bias_ref, ck_hbm, cv_hbm,
                        o_ref, ka, kb, va, vb, kall, vall, sem, *, n_seq):
    b = pl.program_id(0)
    slot = b % 2

    def cache_copies(seq, s):
        cps = []
        for h in range(N_HEADS):
            for src, dst_a, dst_b in ((ck_hbm, ka, kb), (cv_hbm, va, vb)):
                cps.append(pltpu.make_async_copy(
                    src.at[seq, pl.ds(0, CACHE_A_GROUPS), pl.ds(0, N_NEW), h], dst_a.at[s, h], sem.at[s]))
                cps.append(pltpu.make_async_copy(
                    src.at[seq, pl.ds(CACHE_A_GROUPS, CACHE_B_GROUPS), :, h], dst_b.at[s, h], sem.at[s]))
        return cps

    @pl.when(b == 0)
    def _():
        kall[KEYS_REAL:KEYS_PAD, :] = jnp.zeros((KEYS_PAD - KEYS_REAL, D_MODEL), _BF16)
        vall[KEYS_REAL:KEYS_PAD, :] = jnp.zeros((KEYS_PAD - KEYS_REAL, D_MODEL), _BF16)
        for cp in cache_copies(0, 0):
            cp.start()

    @pl.when(b + 1 < n_seq)
    def _():
        for cp in cache_copies(b + 1, 1 - slot):
            cp.start()

    for cp in cache_copies(b, slot):
        cp.wait()

    nb0 = CACHE_A_ROWS + CACHE_B_ROWS
    for src_a, src_b, src_n, dst in ((ka, kb, kn_ref, kall), (va, vb, vn_ref, vall)):
        for h in range(N_HEADS):
            hs = _chunk(h)
            dst[0:CACHE_A_ROWS, hs] = src_a[slot, h].reshape(CACHE_A_ROWS, HEAD_DIM).astype(_BF16)
            dst[CACHE_A_ROWS:nb0, hs] = src_b[slot, h].reshape(CACHE_B_ROWS, HEAD_DIM).astype(_BF16)
            dst[nb0:KEYS_REAL, hs] = src_n[0, :, h, :].astype(_BF16)

    bias = bias_ref[...]
    for h in range(N_HEADS):
        hs = _chunk(h)
        qh = jnp.concatenate([q0_ref[0, :, hs].astype(_F32), q1_ref[0, :, hs].astype(_F32),
                              q2_ref[0, :, hs].astype(_F32)], axis=0).astype(_BF16)
        s = _dot_nt(qh, kall[:, hs]) + bias
        m = jnp.max(s, axis=1, keepdims=True)
        m8 = jnp.maximum(jnp.maximum(m[0:N_NEW], m[N_NEW:2 * N_NEW]), m[2 * N_NEW:3 * N_NEW])
        p = jnp.exp(s - jnp.concatenate([m8, m8, m8], axis=0))
        p8 = p[0:N_NEW] + p[N_NEW:2 * N_NEW] + p[2 * N_NEW:3 * N_NEW]
        l8 = jnp.sum(p8, axis=1, keepdims=True)
        o_ref[0, :, hs] = _dot(p8.astype(_BF16), vall[:, hs]) / l8


def _attn_sample(q0, q1, q2, cache_k, cache_v, k_new, v_new):
    nb, n_new, d = q0.shape
    past = cache_k.shape[1]
    assert past == PAST_LEN and n_new == N_NEW and d == D_MODEL
    assert cache_k.shape[2:] == (N_HEADS, HEAD_DIM)
    ck = cache_k.reshape(nb, past // DILATIONS[2], DILATIONS[2], N_HEADS, HEAD_DIM)
    cv = cache_v.reshape(nb, past // DILATIONS[2], DILATIONS[2], N_HEADS, HEAD_DIM)
    bias = jnp.asarray(_sample_bias())
    new = pl.BlockSpec((1, n_new, d), lambda b: (b, 0, 0))
    new_kv = pl.BlockSpec((1, n_new, N_HEADS, HEAD_DIM), lambda b: (b, 0, 0, 0))
    part_a = (2, N_HEADS, CACHE_A_GROUPS, N_NEW, HEAD_DIM)
    part_b = (2, N_HEADS, CACHE_B_GROUPS, DILATIONS[2], HEAD_DIM)
    return pl.pallas_call(
        functools.partial(_attn_sample_kernel, n_seq=nb),
        grid=(nb,),
        in_specs=[new, new, new, new_kv, new_kv,
                  pl.BlockSpec((N_GROUPS * N_NEW, KEYS_PAD), lambda b: (0, 0)),
                  pl.BlockSpec(memory_space=pl.ANY), pl.BlockSpec(memory_space=pl.ANY)],
        out_specs=new,
        out_shape=jax.ShapeDtypeStruct((nb, n_new, d), _F32),
        scratch_shapes=[pltpu.VMEM(part_a, _F32), pltpu.VMEM(part_b, _F32),
                        pltpu.VMEM(part_a, _F32), pltpu.VMEM(part_b, _F32),
                        pltpu.VMEM((KEYS_PAD, d), _BF16), pltpu.VMEM((KEYS_PAD, d), _BF16),
                        pltpu.SemaphoreType.DMA((2,))],
        compiler_params=pltpu.CompilerParams(
            dimension_semantics=("arbitrary",), vmem_limit_bytes=VMEM_LIMIT),
        name="attn_sample",
    )(q0, q1, q2, k_new, v_new, bias, ck, cv)


def _attn_out_kernel(*refs, n_groups, tm):
    if n_groups == 1:
        (o_ref, h_ref, wo_ref, gffn_ref, wr_ref, br_ref,
         h3_ref, xn2_ref, eid_ref, gate_ref) = refs
        o = o_ref[...].astype(_BF16)
    else:
        (o0_ref, o1_ref, o2_ref, s0_ref, s1_ref, s2_ref, h_ref, wo_ref, gffn_ref, wr_ref, br_ref,
         h3_ref, xn2_ref, eid_ref, gate_ref, obuf, scr_o1, scr_o2, scr_s1, scr_s2) = refs
        for o_ref, s_ref, scr_o, scr_s, dil in ((o1_ref, s1_ref, scr_o1, scr_s1, DILATIONS[1]),
                                                (o2_ref, s2_ref, scr_o2, scr_s2, DILATIONS[2])):
            for res in range(dil):
                dst = pl.ds(res, tm // dil, stride=dil)
                scr_s[dst, :] = s_ref[:, res * HEAD_DIM:(res + 1) * HEAD_DIM]
                for h in range(N_HEADS):
                    col = res * D_MODEL + h * HEAD_DIM
                    scr_o[h, dst, :] = o_ref[:, col:col + HEAD_DIM].astype(_F32)
        sts = [s0_ref[...], scr_s1[...], scr_s2[...]]
        mx = jnp.maximum(jnp.maximum(sts[0], sts[1]), sts[2])
        es = [jnp.exp(s - mx) for s in sts]
        den = es[0] + es[1] + es[2]
        ws = [e / den for e in es]
        for h in range(N_HEADS):
            acc = ws[0][:, h:h + 1] * o0_ref[:, _chunk(h)]
            acc = acc + ws[1][:, h:h + 1] * scr_o1[h]
            acc = acc + ws[2][:, h:h + 1] * scr_o2[h]
            obuf[:, _chunk(h)] = acc.astype(_BF16)
        o = obuf[...]
    h3 = h_ref[...] + _dot(o, wo_ref[...])
    h3_ref[...] = h3
    xn2 = _rms(h3, gffn_ref[...])
    xn2_ref[...] = xn2
    _route(xn2, wr_ref, br_ref, eid_ref, gate_ref)


def _attn_out(os, sts, h, wo, gffn, wr, br, *, tm):
    t, d = h.shape
    n_groups = len(os)
    row = lambda i: (i, 0)
    const = lambda i: (0, 0)
    if n_groups == 1:
        in_specs = [pl.BlockSpec((tm, d), row)]
        scratch = []
        args = [os[0]]
    else:
        in_specs = [pl.BlockSpec((tm // dil, dil * d), row) for dil in DILATIONS]
        in_specs += [pl.BlockSpec((tm // dil, dil * HEAD_DIM), row) for dil in DILATIONS]
        scratch = [pltpu.VMEM((tm, d), _BF16),
                   pltpu.VMEM((N_HEADS, tm, HEAD_DIM), _F32), pltpu.VMEM((N_HEADS, tm, HEAD_DIM), _F32),
                   pltpu.VMEM((tm, HEAD_DIM), _F32), pltpu.VMEM((tm, HEAD_DIM), _F32)]
        args = list(os) + list(sts)
    in_specs += [pl.BlockSpec((tm, d), row), pl.BlockSpec((d, d), const), pl.BlockSpec((1, d), const),
                 pl.BlockSpec((ROUTER_ROWS, d), const), pl.BlockSpec((ROUTER_ROWS, 1), const)]
    return pl.pallas_call(
        functools.partial(_attn_out_kernel, n_groups=n_groups, tm=tm),
        grid=(t // tm,),
        in_specs=in_specs,
        out_specs=[pl.BlockSpec((tm, d), row), pl.BlockSpec((tm, d), row),
                   pl.BlockSpec((2, tm), lambda i: (0, i)), pl.BlockSpec((2, tm), lambda i: (0, i))],
        out_shape=[jax.ShapeDtypeStruct((t, d), _F32), jax.ShapeDtypeStruct((t, d), _F32),
                   jax.ShapeDtypeStruct((2, t), jnp.int32), jax.ShapeDtypeStruct((2, t), _F32)],
        scratch_shapes=scratch,
        compiler_params=pltpu.CompilerParams(
            dimension_semantics=("arbitrary",), vmem_limit_bytes=VMEM_LIMIT),
        name=f"attn_out_g{n_groups}",
    )(*args, h, wo, gffn, wr, br)


def _final_kernel(h_ref, y0_ref, y1_ref, gate_ref, g_ref, out_ref):
    out_ref[...] = _rms(_combine(h_ref, y0_ref, y1_ref, gate_ref), g_ref[...])


def _final(h, ypair, gate, g, *, tm):
    t, d = h.shape
    nt = t // tm
    row = lambda i: (i, 0)
    return pl.pallas_call(
        _final_kernel,
        grid=(nt,),
        in_specs=[pl.BlockSpec((tm, d), row),
                  pl.BlockSpec((tm, d), row),
                  pl.BlockSpec((tm, d), lambda i: (nt + i, 0)),
                  pl.BlockSpec((tm, 2), row),
                  pl.BlockSpec((1, d), lambda i: (0, 0))],
        out_specs=pl.BlockSpec((tm, d), row),
        out_shape=jax.ShapeDtypeStruct((t, d), _F32),
        compiler_params=pltpu.CompilerParams(
            dimension_semantics=("arbitrary",), vmem_limit_bytes=VMEM_LIMIT),
        name="final_norm",
    )(h, ypair, ypair, gate, g)


def _rope_tables(pos):
    half = HEAD_DIM // 2
    inv_freq = jnp.power(jnp.float32(ROPE_THETA), -jnp.arange(half, dtype=jnp.float32) / half)
    ang = pos.astype(jnp.float32)[:, None] * inv_freq[None, :]
    cos = jnp.cos(ang)
    sin = jnp.sin(ang)
    return jnp.concatenate([cos, cos], axis=-1), jnp.concatenate([-sin, sin], axis=-1)


def _router_params(wg, bg, we, be):
    wr = jnp.zeros((ROUTER_ROWS, D_MODEL), _F32)
    wr = wr.at[0:MOE_GROUPS].set(wg.T).at[EXPERT_ROW0:EXPERT_ROW0 + N_EXPERTS].set(we.T)
    br = jnp.zeros((ROUTER_ROWS, 1), _F32)
    br = br.at[0:MOE_GROUPS, 0].set(bg).at[EXPERT_ROW0:EXPERT_ROW0 + N_EXPERTS, 0].set(be)
    return wr, br


def _moe(xn2, eid, experts, layer):
    block_expert, source, slot = _moe_plan(eid, xn2.shape[0])
    xs = _row_gather(xn2, source)
    yield
    ys = _moe_ffn(block_expert, xs, *experts, layer)
    ypair = _row_gather(ys, slot)
    yield
    return ypair


def _forward(x3, prev3, pos_rows, attend, p, *, shift, tm, streams, keep):
    h1, xn2, eid, gate, state = _conv_layer(
        x3, prev3, p['gmix'][0], p['win'], p['ck'], p['wout'], p['gffn'][0], *p['router'][0],
        shift=shift, tm=tm)
    t = x3.shape[0] * x3.shape[1]
    ypair = yield from _moe(xn2, eid, p['experts'], 0)
    cos, sin = _rope_tables(pos_rows)
    h2, *qkv, kf, vf = _qkv(h1.reshape(t, D_MODEL), ypair, gate.T, p['gmix'][1], p['gkv'],
                            p['wq'], p['wkv'], cos, sin, tm=QKV_TILE, streams=streams,
                            seq=x3.shape[1], keep=keep)
    os, sts = attend(qkv, kf, vf)
    h3, xn2, eid, gate = _attn_out(os, sts, h2, p['wo'], p['gffn'][1], *p['router'][1],
                                   tm=min(ATTN_OUT_TILE, t))
    ypair = yield from _moe(xn2, eid, p['experts'], 1)
    y = _final(h3, ypair, gate.T, p['gfinal'], tm=FINAL_TILE)
    return y, state, kf, vf


def _interleave(*paths):
    results = [None] * len(paths)
    live = list(enumerate(paths))
    while live:
        still = []
        for i, gen in live:
            try:
                next(gen)
                still.append((i, gen))
            except StopIteration as done:
                results[i] = done.value
        live = still
    return results


def kernel(x_prompt, x_sample, cache_k, cache_v, state_conv, norm_mix, norm_ffn, norm_kv, norm_final,
           conv_w_in, conv_kernel, conv_w_out, attn_w_q, attn_w_kv, attn_w_o, router_group_w,
           router_group_b, router_expert_w, router_expert_b, expert_w1, expert_w3, expert_w2):
    b_p, s_p, d = x_prompt.shape
    b_s, n_new, _ = x_sample.shape
    assert d == D_MODEL and n_new == N_NEW and s_p % (DILATIONS[2] * SPAN) == 0
    assert norm_mix.shape[0] == 2 and conv_w_in.shape[0] == 1 and attn_w_q.shape[0] == 1
    assert cache_k.shape[1] == PAST_LEN

    p = {
        'gmix': [norm_mix[l].reshape(1, d) for l in range(2)],
        'gffn': [norm_ffn[l].reshape(1, d) for l in range(2)],
        'gkv': norm_kv.reshape(1, d),
        'gfinal': norm_final.reshape(1, d),
        'win': conv_w_in[0].astype(_BF16),
        'ck': conv_kernel[0],
        'wout': conv_w_out[0].astype(_BF16),
        'wq': attn_w_q[0].astype(_BF16),
        'wkv': attn_w_kv.astype(_BF16),
        'wo': attn_w_o[0].astype(_BF16),
        'router': [_router_params(router_group_w[l], router_group_b[l], router_expert_w[l], router_expert_b[l])
                   for l in range(2)],
        'experts': (expert_w1, expert_w3, expert_w2),
    }

    def attend_prompt(qkv, kf, vf):
        q0, q1, q2, k0, k1, k2, v0, v1, v2 = qkv
        os, sts = [], []
        for q, k, v, dil in ((q0, k0, v0, DILATIONS[0]), (q1, k1, v1, DILATIONS[1]), (q2, k2, v2, DILATIONS[2])):
            o, st = _attn_prompt(q, k, v, b_p, dil)
            os.append(o)
            sts.append(st)
        return os, sts

    pos_p = jnp.tile(jnp.arange(s_p, dtype=jnp.int32), b_p)
    zero_state = jnp.zeros((b_p, CONV_WIDTH - 1, d), x_prompt.dtype)
    keep = min(max(WINDOWS), s_p)
    prompt_path = _forward(x_prompt, zero_state, pos_p, attend_prompt, p, shift=1, tm=CONV_TILE,
                           streams=True, keep=keep)

    halves = 2
    bh = b_s // halves

    def to_rows(a):
        w = a.shape[-1]
        return a.reshape(halves, bh, n_new, w).transpose(0, 2, 1, 3).reshape(halves * n_new * bh, w)

    def to_batch(a):
        w = a.shape[1:]
        return jnp.swapaxes(a.reshape(halves, n_new, bh, *w), 1, 2).reshape(b_s, n_new, *w)

    def attend_sample(qkv, kf, vf):
        q0, q1, q2 = qkv
        o = _attn_sample(to_batch(q0), to_batch(q1), to_batch(q2), cache_k, cache_v,
                         to_batch(kf), to_batch(vf))
        return [to_rows(o)], None

    x_s = to_rows(x_sample).reshape(halves, n_new * bh, d)
    prev_s = state_conv[0].reshape(halves, bh, CONV_WIDTH - 1, d).transpose(0, 2, 1, 3).reshape(
        halves, (CONV_WIDTH - 1) * bh, d)
    pos_s = jnp.tile(jnp.repeat(PAST_LEN + jnp.arange(n_new, dtype=jnp.int32), bh), halves)
    sample_path = _forward(x_s, prev_s, pos_s, attend_sample, p, shift=bh, tm=n_new * bh,
                           streams=False, keep=n_new * bh)

    (y_p, st_p, kf_p, vf_p), (y_s, st_s, kf_s, vf_s) = _interleave(prompt_path, sample_path)
    y_prompt = y_p.reshape(b_p, s_p, d)
    k_p = kf_p.reshape(b_p, keep, N_HEADS, HEAD_DIM)
    v_p = vf_p.reshape(b_p, keep, N_HEADS, HEAD_DIM)
    conv_p = st_p[None]
    y_sample = to_batch(y_s)
    k_s = to_batch(kf_s)
    v_s = to_batch(vf_s)
    conv_s = st_s.reshape(halves, CONV_WIDTH - 1, bh, d).transpose(0, 2, 1, 3).reshape(
        b_s, CONV_WIDTH - 1, d)[None]

    return (y_prompt, y_sample, k_p, v_p, conv_p, k_s, v_s, conv_s)
```

```python
import functools

import numpy as np
import jax
import jax.numpy as jnp
from jax import lax
from jax.experimental import pallas as pl
from jax.experimental.pallas import tpu as pltpu
from jax.experimental.pallas import tpu_sc as plsc

D_MODEL = 1024
CONV_WIDTH = 3
WINDOWS = (128, 512, 2048)
DILATIONS = (1, 4, 16)
N_GROUPS = 3
N_HEADS = 8
HEAD_DIM = 128
SPAN = 128
ROPE_THETA = 10000.0
MOE_GROUPS = 4
EXPERTS_PER_GROUP = 4
N_EXPERTS = 16
D_EXPERT = 512
EXPERT_BLOCK = 256
RMS_EPS = 1e-6
NEG_BIG = -1e30
PAST_LEN = 2048

LANES = 128
SUBLANES = 8
ROUTER_ROWS = 32
EXPERT_ROW0 = 8
VMEM_LIMIT = 56 * 1024 * 1024

CONV_TILE = 512
QKV_TILE = 256
ATTN_OUT_TILE = 512
FINAL_TILE = 512

_F32 = jnp.float32
_BF16 = jnp.bfloat16


def _rms(x, g):
    ms = jnp.mean(x * x, axis=-1, keepdims=True)
    return (x * lax.rsqrt(ms + RMS_EPS)) * g


def _dot(a, b):
    return jnp.dot(a, b, preferred_element_type=_F32)


def _dot_nt(a, b, precision=None):
    return lax.dot_general(a, b, (((1,), (1,)), ((), ())), precision=precision,
                           preferred_element_type=_F32)


def _chunk(c):
    return slice(c * LANES, (c + 1) * LANES)


def _route(xn, wr_ref, br_ref, eid_ref, gate_ref):
    logits = _dot_nt(wr_ref[...], xn, precision=lax.Precision.HIGHEST) + br_ref[...]
    lg = logits[0:MOE_GROUPS]
    row = lax.broadcasted_iota(jnp.int32, lg.shape, 0).astype(_F32)
    eg = jnp.exp(lg - jnp.max(lg, axis=0, keepdims=True))
    pg = eg / jnp.sum(eg, axis=0, keepdims=True)
    gp = jnp.max(pg, axis=0, keepdims=True)
    gi = jnp.min(jnp.where(pg == gp, row, float(MOE_GROUPS)), axis=0, keepdims=True)
    le = jnp.zeros_like(lg)
    for g in range(MOE_GROUPS):
        r0 = EXPERT_ROW0 + g * EXPERTS_PER_GROUP
        le = le + jnp.where(gi == float(g), logits[r0:r0 + EXPERTS_PER_GROUP], 0.0)
    ee = jnp.exp(le - jnp.max(le, axis=0, keepdims=True))
    ev = ee / jnp.sum(ee, axis=0, keepdims=True)
    v1 = jnp.max(ev, axis=0, keepdims=True)
    i1 = jnp.min(jnp.where(ev == v1, row, float(EXPERTS_PER_GROUP)), axis=0, keepdims=True)
    ev2 = jnp.where(row == i1, -1.0, ev)
    v2 = jnp.max(ev2, axis=0, keepdims=True)
    i2 = jnp.min(jnp.where(ev2 == v2, row, float(EXPERTS_PER_GROUP)), axis=0, keepdims=True)
    den = v1 + v2
    ids = jnp.concatenate([gi * EXPERTS_PER_GROUP + i1, gi * EXPERTS_PER_GROUP + i2], axis=0)
    eid_ref[...] = ids.astype(jnp.int32)
    gate_ref[...] = jnp.concatenate([gp * v1 / den, gp * v2 / den], axis=0)


def _conv_layer_kernel(x_ref, prev_ref, gmix_ref, win_ref, ck_ref, wout_ref, gffn_ref, wr_ref, br_ref,
                       h_ref, xn2_ref, eid_ref, gate_ref, st_ref, ubuf, *, shift, tm):
    i = pl.program_id(1)
    halo = ubuf.shape[0] - tm
    keep = (CONV_WIDTH - 1) * shift

    @pl.when(i == 0)
    def _():
        ubuf[halo - keep:halo, :] = prev_ref[0]

    @pl.when(i > 0)
    def _():
        ubuf[halo - keep:halo, :] = ubuf[halo + tm - keep:halo + tm, :]

    x = x_ref[0]
    xn = _rms(x, gmix_ref[...]).astype(_BF16)
    c_gate = _dot(xn, win_ref[:, 0:D_MODEL])
    hid = _dot(xn, win_ref[:, 2 * D_MODEL:3 * D_MODEL])
    u = c_gate * hid
    ubuf[halo:halo + tm, :] = u
    u1 = ubuf[halo - shift:halo - shift + tm, :]
    u2 = ubuf[halo - 2 * shift:halo - 2 * shift + tm, :]
    conv = ck_ref[0:1, :] * u2 + ck_ref[1:2, :] * u1 + ck_ref[2:3, :] * u
    b_gate = _dot(xn, win_ref[:, D_MODEL:2 * D_MODEL])
    y = _dot((b_gate * conv).astype(_BF16), wout_ref[...])
    h = x + y
    h_ref[0] = h
    st_ref[0] = ubuf[halo + tm - keep:halo + tm, :]
    xn2 = _rms(h, gffn_ref[...])
    xn2_ref[...] = xn2
    _route(xn2, wr_ref, br_ref, eid_ref, gate_ref)


def _conv_layer(x3, prev3, gmix, win, ck, wout, gffn, wr, br, *, shift, tm):
    nb, s, d = x3.shape
    nt = s // tm
    keep = (CONV_WIDTH - 1) * shift
    halo = -(-keep // SUBLANES) * SUBLANES
    t_all = nb * s
    const = lambda b, i: (0, 0)
    return pl.pallas_call(
        functools.partial(_conv_layer_kernel, shift=shift, tm=tm),
        grid=(nb, nt),
        in_specs=[
            pl.BlockSpec((1, tm, d), lambda b, i: (b, i, 0)),
            pl.BlockSpec((1, keep, d), lambda b, i: (b, 0, 0)),
            pl.BlockSpec((1, d), const),
            pl.BlockSpec((d, 3 * d), const),
            pl.BlockSpec((CONV_WIDTH, d), const),
            pl.BlockSpec((d, d), const),
            pl.BlockSpec((1, d), const),
            pl.BlockSpec((ROUTER_ROWS, d), const),
            pl.BlockSpec((ROUTER_ROWS, 1), const),
        ],
        out_specs=[
            pl.BlockSpec((1, tm, d), lambda b, i: (b, i, 0)),
            pl.BlockSpec((tm, d), lambda b, i: (b * nt + i, 0)),
            pl.BlockSpec((2, tm), lambda b, i: (0, b * nt + i)),
            pl.BlockSpec((2, tm), lambda b, i: (0, b * nt + i)),
            pl.BlockSpec((1, keep, d), lambda b, i: (b, 0, 0)),
        ],
        out_shape=[
            jax.ShapeDtypeStruct((nb, s, d), _F32),
            jax.ShapeDtypeStruct((t_all, d), _F32),
            jax.ShapeDtypeStruct((2, t_all), jnp.int32),
            jax.ShapeDtypeStruct((2, t_all), _F32),
            jax.ShapeDtypeStruct((nb, keep, d), _F32),
        ],
        scratch_shapes=[pltpu.VMEM((halo + tm, d), _F32)],
        compiler_params=pltpu.CompilerParams(
            dimension_semantics=("arbitrary", "arbitrary"), vmem_limit_bytes=VMEM_LIMIT),
        name="conv_layer",
    )(x3, prev3, gmix, win, ck, wout, gffn, wr, br)


SC_WINDOW = 32


def _row_gather(src, idx):
    n_out = idx.shape[0]
    mesh = plsc.VectorSubcoreMesh(core_axis_name="c", subcore_axis_name="s")
    n_cores = mesh.num_cores
    per_worker = n_out // (n_cores * mesh.num_subcores)
    n_win = per_worker // SC_WINDOW
    assert per_worker * n_cores * mesh.num_subcores == n_out and n_win * SC_WINDOW == per_worker
    assert n_win % 2 == 0

    @pl.kernel(out_type=jax.ShapeDtypeStruct((n_out,) + src.shape[1:], src.dtype), mesh=mesh,
               scratch_types=[pltpu.VMEM((per_worker,), jnp.int32),
                              pltpu.VMEM((2, SC_WINDOW) + src.shape[1:], src.dtype),
                              pltpu.SemaphoreType.DMA((2,)), pltpu.SemaphoreType.DMA((2,))])
    def gather(src_hbm, idx_hbm, out_hbm, idx_v, rows_v, gsem, wsem):
        base = (lax.axis_index("s") * n_cores + lax.axis_index("c")) * per_worker
        pltpu.sync_copy(idx_hbm.at[pl.ds(base, per_worker)], idx_v)

        def gather_copy(q, b):
            return pltpu.make_async_copy(src_hbm.at[idx_v.at[pl.ds(q * SC_WINDOW, SC_WINDOW)]],
                                         rows_v.at[b], gsem.at[b])

        def write_copy(q, b):
            return pltpu.make_async_copy(rows_v.at[b], out_hbm.at[pl.ds(base + q * SC_WINDOW, SC_WINDOW)],
                                         wsem.at[b])

        gather_copy(0, 0).start()

        @pl.loop(0, n_win, step=2)
        def _(q0):
            for b in range(2):
                q = q0 + b
                gather_copy(q, b).wait()

                @pl.when(q + 1 < n_win)
                def _():
                    @pl.when(q >= 1)
                    def _():
                        write_copy(q - 1, 1 - b).wait()
                    gather_copy(q + 1, 1 - b).start()

                write_copy(q, b).start()

        write_copy(n_win - 2, 0).wait()
        write_copy(n_win - 1, 1).wait()

    return gather(src, idx)


def _moe_ffn_kernel(be_ref, x_ref, w1_ref, w3_ref, w2_ref, y_ref, w1b, w3b, w2b):
    j = pl.program_id(0)

    @pl.when((j == 0) | (be_ref[j] != be_ref[jnp.maximum(j - 1, 0)]))
    def _():
        w1b[...] = w1_ref[0, 0].astype(_BF16)
        w3b[...] = w3_ref[0, 0].astype(_BF16)
        w2b[...] = w2_ref[0, 0].astype(_BF16)

    x = x_ref[...].astype(_BF16)
    a = _dot(x, w1b[...])
    b = _dot(x, w3b[...])
    y_ref[...] = _dot((jax.nn.silu(a) * b).astype(_BF16), w2b[...])


def _moe_ffn(block_expert, xs, w1, w3, w2, layer):
    n_slots, d = xs.shape
    n_blocks = block_expert.shape[0]
    wmap = lambda j, be: (layer, be[j], 0, 0)
    blk = pl.BlockSpec((EXPERT_BLOCK, d), lambda j, be: (j, 0))
    return pl.pallas_call(
        _moe_ffn_kernel,
        grid_spec=pltpu.PrefetchScalarGridSpec(
            num_scalar_prefetch=1,
            grid=(n_blocks,),
            in_specs=[blk,
                      pl.BlockSpec((1, 1, d, D_EXPERT), wmap),
                      pl.BlockSpec((1, 1, d, D_EXPERT), wmap),
                      pl.BlockSpec((1, 1, D_EXPERT, d), wmap)],
            out_specs=blk,
            scratch_shapes=[pltpu.VMEM((d, D_EXPERT), _BF16), pltpu.VMEM((d, D_EXPERT), _BF16),
                            pltpu.VMEM((D_EXPERT, d), _BF16)],
        ),
        out_shape=jax.ShapeDtypeStruct((n_slots, d), _F32),
        compiler_params=pltpu.CompilerParams(
            dimension_semantics=("arbitrary",), vmem_limit_bytes=VMEM_LIMIT),
        name="moe_ffn",
    )(block_expert, xs, w1, w3, w2)


def _moe_plan_kernel(eid_ref, tab_ref, cnt_ref, slot_v, slot_s, zeros_v, sem):
    rows = eid_ref.shape[0]
    depth = max(rows, LANES)
    eid = eid_ref[...]
    r_i = lax.broadcasted_iota(jnp.int32, (LANES, LANES), 0)
    c_i = lax.broadcasted_iota(jnp.int32, (LANES, LANES), 1)
    earlier_lane = jnp.where(r_i < c_i, 1.0, 0.0).astype(_BF16)
    all_lanes = jnp.ones((LANES, LANES), _BF16)
    q_i = lax.broadcasted_iota(jnp.int32, (rows, depth), 1)
    p_i = lax.broadcasted_iota(jnp.int32, (rows, depth), 0)
    earlier_row = jnp.where(q_i < p_i, 1.0, 0.0).astype(_BF16)
    slot = jnp.zeros((rows, LANES), _F32)
    start = jnp.int32(0)
    for ex in range(N_EXPERTS):
        hit = eid == ex
        hot = jnp.where(hit, 1.0, 0.0).astype(_BF16)
        in_row = _dot(hot, earlier_lane)
        row_tot = _dot(hot, all_lanes)
        if depth > rows:
            row_tot = jnp.concatenate([row_tot, jnp.zeros((depth - rows, LANES), _F32)], axis=0)
        above = _dot(earlier_row, row_tot.astype(_BF16))
        count = jnp.sum(jnp.where(hit, 1, 0))
        slot = jnp.where(hit, start.astype(_F32) + in_row + above, slot)
        cnt_ref[ex] = count
        start = start + (count + EXPERT_BLOCK - 1) // EXPERT_BLOCK * EXPERT_BLOCK
    slot_v[...] = slot.astype(jnp.int32)
    to_smem = pltpu.make_async_copy(slot_v, slot_s, sem.at[0])
    to_smem.start()
    zeros_v[...] = jnp.zeros(zeros_v.shape, jnp.int32)
    clear = pltpu.make_async_copy(zeros_v, tab_ref, sem.at[1])
    clear.start()
    to_smem.wait()
    clear.wait()

    def place(r, carry):
        for c in range(LANES):
            tab_ref[slot_s[r, c]] = r * LANES + c
        return carry
    lax.fori_loop(0, rows, place, 0)


def _moe_plan(eid, n_tok):
    n_assign = 2 * n_tok
    n_blocks = -(-(n_assign + N_EXPERTS * (EXPERT_BLOCK - 1)) // EXPERT_BLOCK)
    n_slots = n_blocks * EXPERT_BLOCK
    rows = n_assign // LANES
    assert n_assign % LANES == 0
    table, counts, slot = pl.pallas_call(
        _moe_plan_kernel,
        in_specs=[pl.BlockSpec(memory_space=pltpu.VMEM)],
        out_specs=[pl.BlockSpec(memory_space=pltpu.SMEM), pl.BlockSpec(memory_space=pltpu.SMEM),
                   pl.BlockSpec(memory_space=pltpu.VMEM)],
        out_shape=[jax.ShapeDtypeStruct((n_slots,), jnp.int32), jax.ShapeDtypeStruct((N_EXPERTS,), jnp.int32),
                   jax.ShapeDtypeStruct((rows, LANES), jnp.int32)],
        scratch_shapes=[pltpu.SMEM((rows, LANES), jnp.int32), pltpu.VMEM((n_slots,), jnp.int32),
                        pltpu.SemaphoreType.DMA((2,))],
        name="moe_plan",
    )(eid.reshape(rows, LANES))
    padded = (counts + EXPERT_BLOCK - 1) // EXPERT_BLOCK * EXPERT_BLOCK
    pad_end = jnp.cumsum(padded)
    pad_start = pad_end - padded
    block_start = jnp.arange(n_blocks, dtype=jnp.int32) * EXPERT_BLOCK
    block_expert = jnp.minimum(jnp.sum((pad_end[None, :] <= block_start[:, None]).astype(jnp.int32), axis=1),
                               N_EXPERTS - 1)
    n_valid = jnp.clip((pad_start + counts)[block_expert] - block_start, 0, EXPERT_BLOCK)
    every = jnp.arange(n_slots, dtype=jnp.int32)
    valid = every % EXPERT_BLOCK < jnp.repeat(n_valid, EXPERT_BLOCK)
    source = jnp.where(valid, table, every) % n_tok
    return block_expert, source, slot.reshape(n_assign)


def _combine(h_ref, y0_ref, y1_ref, gate_ref):
    gate = gate_ref[...]
    return h_ref[...] + (gate[:, 0:1] * y0_ref[...] + gate[:, 1:2] * y1_ref[...])


def _emit_head(r, h, outs, scr):
    tm = r.shape[0]
    if scr is not None:
        scr[h] = r
    for ref, dil in outs:
        if dil == 0:
            ref[:, h, :] = r
        elif dil == 1:
            ref[:, _chunk(h)] = r.astype(ref.dtype)
        else:
            for res in range(dil):
                col = res * D_MODEL + h * HEAD_DIM
                ref[:, col:col + HEAD_DIM] = scr[h, pl.ds(res, tm // dil, stride=dil), :].astype(ref.dtype)


def _rope(xh, cos, sin_signed):
    return xh * cos + pltpu.roll(xh, HEAD_DIM // 2, axis=1) * sin_signed


def _qkv_kernel(*refs, streams, per_seq, kept):
    (h_ref, y0_ref, y1_ref, gate_ref, gmix_ref, gkv_ref, wq_ref, wkv_ref, cos_ref, sin_ref) = refs[:10]
    outs = refs[10:]
    if streams:
        (h2_ref, q0_ref, q1_ref, q2_ref, k0_ref, k1_ref, k2_ref, v0_ref, v1_ref, v2_ref, kf_ref, vf_ref,
         scr_q1, scr_q2, scr_k, scr_v) = outs
        q_outs = [([(q0_ref, 1)], None), ([(q1_ref, DILATIONS[1])], scr_q1), ([(q2_ref, DILATIONS[2])], scr_q2)]
        k_outs = ([(k0_ref, 1), (k1_ref, DILATIONS[1]), (k2_ref, DILATIONS[2])], scr_k)
        v_outs = ([(v0_ref, 1), (v1_ref, DILATIONS[1]), (v2_ref, DILATIONS[2])], scr_v)
    else:
        h2_ref, q0_ref, q1_ref, q2_ref, kf_ref, vf_ref = outs
        q_outs = [([(q0_ref, 1)], None), ([(q1_ref, 1)], None), ([(q2_ref, 1)], None)]
        k_outs = ([(kf_ref, 0)], None)
        v_outs = ([(vf_ref, 0)], None)

    h2 = _combine(h_ref, y0_ref, y1_ref, gate_ref)
    h2_ref[...] = h2
    cos = cos_ref[...]
    sin = sin_ref[...]
    xn = _rms(h2, gmix_ref[...]).astype(_BF16)
    scale = HEAD_DIM ** -0.5
    for g in range(N_GROUPS):
        q = _dot(xn, wq_ref[:, g * D_MODEL:(g + 1) * D_MODEL])
        for h in range(N_HEADS):
            _emit_head(_rope(q[:, _chunk(h)], cos, sin) * scale, h, *q_outs[g])
    xkv = _rms(h2, gkv_ref[...]).astype(_BF16)
    k = _dot(xkv, wkv_ref[:, 0:D_MODEL])
    for h in range(N_HEADS):
        _emit_head(_rope(k[:, _chunk(h)], cos, sin), h, *k_outs)
    v = _dot(xkv, wkv_ref[:, D_MODEL:2 * D_MODEL])
    for h in range(N_HEADS):
        _emit_head(v[:, _chunk(h)], h, *v_outs)
    if streams:
        @pl.when(pl.program_id(0) % per_seq >= per_seq - kept)
        def _():
            for h in range(N_HEADS):
                kf_ref[:, h, :] = scr_k[h]
                vf_ref[:, h, :] = scr_v[h]


def _qkv(h, ypair, gate, gmix, gkv, wq, wkv, cos, sin, *, tm, streams, seq, keep):
    t, d = h.shape
    nt = t // tm
    per_seq = seq // tm
    kept = keep // tm
    assert seq % tm == 0 and keep % tm == 0

    def tail(i):
        return ((i // per_seq) * kept + jnp.maximum(i % per_seq - (per_seq - kept), 0), 0, 0)

    tail_f32 = (pl.BlockSpec((tm, N_HEADS, HEAD_DIM), tail),
                jax.ShapeDtypeStruct((t // seq * keep, N_HEADS, HEAD_DIM), _F32))
    row = lambda i: (i, 0)
    const = lambda i: (0, 0)
    nat = lambda dt: (pl.BlockSpec((tm, d), row), jax.ShapeDtypeStruct((t, d), dt))

    def stream(dil):
        return (pl.BlockSpec((tm // dil, dil * d), row), jax.ShapeDtypeStruct((t // dil, dil * d), _BF16))

    if streams:
        d1, d2 = DILATIONS[1], DILATIONS[2]
        outs = [nat(_F32), nat(_BF16), stream(d1), stream(d2), nat(_BF16), stream(d1), stream(d2),
                nat(_BF16), stream(d1), stream(d2), tail_f32, tail_f32]
        scratch = [pltpu.VMEM((N_HEADS, tm, HEAD_DIM), _F32)] * 4
    else:
        outs = [nat(_F32), nat(_BF16), nat(_BF16), nat(_BF16), tail_f32, tail_f32]
        scratch = []
    return pl.pallas_call(
        functools.partial(_qkv_kernel, streams=streams, per_seq=per_seq, kept=kept),
        grid=(nt,),
        in_specs=[
            pl.BlockSpec((tm, d), row),
            pl.BlockSpec((tm, d), row),
            pl.BlockSpec((tm, d), lambda i: (nt + i, 0)),
            pl.BlockSpec((tm, 2), row),
            pl.BlockSpec((1, d), const),
            pl.BlockSpec((1, d), const),
            pl.BlockSpec((d, 3 * d), const),
            pl.BlockSpec((d, 2 * d), const),
            pl.BlockSpec((tm, HEAD_DIM), row),
            pl.BlockSpec((tm, HEAD_DIM), row),
        ],
        out_specs=[o[0] for o in outs],
        out_shape=[o[1] for o in outs],
        scratch_shapes=scratch,
        compiler_params=pltpu.CompilerParams(
            dimension_semantics=("arbitrary",), vmem_limit_bytes=VMEM_LIMIT),
        name="qkv_proj_streams" if streams else "qkv_proj",
    )(h, ypair, ypair, gate, gmix, gkv, wq, wkv, cos, sin)


ATTN_TQ = 512


def _attn_prompt_kernel(q_ref, kp_ref, kc_ref, vp_ref, vc_ref, o_ref, st_ref, *, tq):
    i = pl.program_id(1)
    qi = lax.broadcasted_iota(jnp.int32, (SPAN, 2 * SPAN), 0)
    kj = lax.broadcasted_iota(jnp.int32, (SPAN, 2 * SPAN), 1)
    band = (kj >= qi) & (kj <= qi + SPAN)
    band0 = band & (kj >= jnp.where(i > 0, 0, SPAN))
    lane = lax.broadcasted_iota(jnp.int32, (SPAN, HEAD_DIM), 1)
    ones = jnp.ones((2 * SPAN, HEAD_DIM), _BF16)
    for a in range(tq // SPAN):
        rows = slice(a * SPAN, (a + 1) * SPAN)
        st = jnp.zeros((SPAN, HEAD_DIM), _F32)
        for h in range(N_HEADS):
            hs = _chunk(h)
            if a == 0:
                keys = jnp.concatenate([kp_ref[:, hs], kc_ref[rows, hs]], axis=0)
                vals = jnp.concatenate([vp_ref[:, hs], vc_ref[rows, hs]], axis=0)
                mask = band0
            else:
                both = slice((a - 1) * SPAN, (a + 1) * SPAN)
                keys, vals, mask = kc_ref[both, hs], vc_ref[both, hs], band
            s = jnp.where(mask, _dot_nt(q_ref[rows, hs], keys), NEG_BIG)
            m = jnp.max(s, axis=1, keepdims=True)
            p = jnp.exp(s - m).astype(_BF16)
            ol = _dot(p, jnp.concatenate([vals, ones], axis=1))
            l = ol[:, HEAD_DIM:]
            o_ref[rows, hs] = (ol[:, :HEAD_DIM] / l).astype(o_ref.dtype)
            st = jnp.where(lane == h, m + jnp.log(l), st)
        st_ref[rows, :] = st


def _attn_prompt(q, k, v, batch, dilation):
    rows, width = q.shape
    d = width // dilation
    per_batch = rows // batch
    tq = min(ATTN_TQ, per_batch)
    assert per_batch % tq == 0 and tq % SPAN == 0
    nq = per_batch // tq
    sub = tq // SPAN
    cur = lambda b, i, r: (b * nq + i, r)
    prev = lambda b, i, r: ((b * nq + i) * sub - jnp.where(i > 0, 1, 0), r)
    big = (tq, d)
    small = (SPAN, d)
    return pl.pallas_call(
        functools.partial(_attn_prompt_kernel, tq=tq),
        grid=(batch, nq, dilation),
        in_specs=[pl.BlockSpec(big, cur), pl.BlockSpec(small, prev), pl.BlockSpec(big, cur),
                  pl.BlockSpec(small, prev), pl.BlockSpec(big, cur)],
        out_specs=[pl.BlockSpec(big, cur), pl.BlockSpec((tq, HEAD_DIM), cur)],
        out_shape=[jax.ShapeDtypeStruct((rows, width), _BF16),
                   jax.ShapeDtypeStruct((rows, dilation * HEAD_DIM), _F32)],
        compiler_params=pltpu.CompilerParams(
            dimension_semantics=("arbitrary", "arbitrary", "arbitrary"), vmem_limit_bytes=VMEM_LIMIT),
        name=f"attn_prompt_d{dilation}",
    )(q, k, k, v, v)


N_NEW = 8
CACHE_A_GROUPS = (PAST_LEN - WINDOWS[1]) // DILATIONS[2]
CACHE_A_ROWS = CACHE_A_GROUPS * N_NEW
CACHE_B_ROWS = WINDOWS[1]
CACHE_B_GROUPS = CACHE_B_ROWS // DILATIONS[2]
KEYS_REAL = CACHE_A_ROWS + CACHE_B_ROWS + N_NEW
KEYS_PAD = -(-KEYS_REAL // LANES) * LANES


def _sample_key_positions():
    pos = np.full((KEYS_PAD,), -1, np.int64)
    a = np.arange(CACHE_A_ROWS)
    pos[:CACHE_A_ROWS] = (a // N_NEW) * DILATIONS[2] + a % N_NEW
    pos[CACHE_A_ROWS:CACHE_A_ROWS + CACHE_B_ROWS] = PAST_LEN - CACHE_B_ROWS + np.arange(CACHE_B_ROWS)
    pos[CACHE_A_ROWS + CACHE_B_ROWS:KEYS_REAL] = PAST_LEN + np.arange(N_NEW)
    return pos


def _sample_bias():
    pos = _sample_key_positions()
    bias = np.full((N_GROUPS * N_NEW, KEYS_PAD), NEG_BIG, np.float32)
    for g in range(N_GROUPS):
        for n in range(N_NEW):
            delta = PAST_LEN + n - pos
            ok = (pos >= 0) & (delta >= 0) & (delta <= WINDOWS[g]) & (delta % DILATIONS[g] == 0)
            assert int(ok.sum()) == SPAN + 1
            bias[g * N_NEW + n, ok] = 0.0
    return bias


def _attn_sample_kernel(q0_ref, q1_ref, q2_ref, kn_ref, vn_ref, bias_ref, ck_hbm, cv_hbm,
                        o_ref, ka, kb, va, vb, kall, vall, sem, *, n_seq):
    b = pl.program_id(0)
    slot = b % 2

    def cache_copies(seq, s):
        cps = []
        for h in range(N_HEADS):
            for src, dst_a, dst_b in ((ck_hbm, ka, kb), (cv_hbm, va, vb)):
                cps.append(pltpu.make_async_copy(
                    src.at[seq, pl.ds(0, CACHE_A_GROUPS), pl.ds(0, N_NEW), h], dst_a.at[s, h], sem.at[s]))
                cps.append(pltpu.make_async_copy(
                    src.at[seq, pl.ds(CACHE_A_GROUPS, CACHE_B_GROUPS), :, h], dst_b.at[s, h], sem.at[s]))
        return cps

    @pl.when(b == 0)
    def _():
        kall[KEYS_REAL:KEYS_PAD, :] = jnp.zeros((KEYS_PAD - KEYS_REAL, D_MODEL), _BF16)
        vall[KEYS_REAL:KEYS_PAD, :] = jnp.zeros((KEYS_PAD - KEYS_REAL, D_MODEL), _BF16)
        for cp in cache_copies(0, 0):
            cp.start()

    @pl.when(b + 1 < n_seq)
    def _():
        for i, cp in enumerate(cache_copies(b + 1, 1 - slot)):
            cp.start(priority=i % 2)

    for cp in cache_copies(b, slot):
        cp.wait()

    nb0 = CACHE_A_ROWS + CACHE_B_ROWS
    for src_a, src_b, src_n, dst in ((ka, kb, kn_ref, kall), (va, vb, vn_ref, vall)):
        for h in range(N_HEADS):
            hs = _chunk(h)
            dst[0:CACHE_A_ROWS, hs] = src_a[slot, h].reshape(CACHE_A_ROWS, HEAD_DIM).astype(_BF16)
            dst[CACHE_A_ROWS:nb0, hs] = src_b[slot, h].reshape(CACHE_B_ROWS, HEAD_DIM).astype(_BF16)
            dst[nb0:KEYS_REAL, hs] = src_n[0, :, h, :].astype(_BF16)

    bias = bias_ref[...]
    for h in range(N_HEADS):
        hs = _chunk(h)
        qh = jnp.concatenate([q0_ref[0, :, hs].astype(_F32), q1_ref[0, :, hs].astype(_F32),
                              q2_ref[0, :, hs].astype(_F32)], axis=0).astype(_BF16)
        s = _dot_nt(qh, kall[:, hs]) + bias
        m = jnp.max(s, axis=1, keepdims=True)
        m8 = jnp.maximum(jnp.maximum(m[0:N_NEW], m[N_NEW:2 * N_NEW]), m[2 * N_NEW:3 * N_NEW])
        p = jnp.exp(s - jnp.concatenate([m8, m8, m8], axis=0))
        p8 = p[0:N_NEW] + p[N_NEW:2 * N_NEW] + p[2 * N_NEW:3 * N_NEW]
        l8 = jnp.sum(p8, axis=1, keepdims=True)
        o_ref[0, :, hs] = _dot(p8.astype(_BF16), vall[:, hs]) / l8


def _attn_sample(q0, q1, q2, cache_k, cache_v, k_new, v_new):
    nb, n_new, d = q0.shape
    past = cache_k.shape[1]
    assert past == PAST_LEN and n_new == N_NEW and d == D_MODEL
    assert cache_k.shape[2:] == (N_HEADS, HEAD_DIM)
    ck = cache_k.reshape(nb, past // DILATIONS[2], DILATIONS[2], N_HEADS, HEAD_DIM)
    cv = cache_v.reshape(nb, past // DILATIONS[2], DILATIONS[2], N_HEADS, HEAD_DIM)
    bias = jnp.asarray(_sample_bias())
    new = pl.BlockSpec((1, n_new, d), lambda b: (b, 0, 0))
    new_kv = pl.BlockSpec((1, n_new, N_HEADS, HEAD_DIM), lambda b: (b, 0, 0, 0))
    part_a = (2, N_HEADS, CACHE_A_GROUPS, N_NEW, HEAD_DIM)
    part_b = (2, N_HEADS, CACHE_B_GROUPS, DILATIONS[2], HEAD_DIM)
    return pl.pallas_call(
        functools.partial(_attn_sample_kernel, n_seq=nb),
        grid=(nb,),
        in_specs=[new, new, new, new_kv, new_kv,
                  pl.BlockSpec((N_GROUPS * N_NEW, KEYS_PAD), lambda b: (0, 0)),
                  pl.BlockSpec(memory_space=pl.ANY), pl.BlockSpec(memory_space=pl.ANY)],
        out_specs=new,
        out_shape=jax.ShapeDtypeStruct((nb, n_new, d), _F32),
        scratch_shapes=[pltpu.VMEM(part_a, _F32), pltpu.VMEM(part_b, _F32),
                        pltpu.VMEM(part_a, _F32), pltpu.VMEM(part_b, _F32),
                        pltpu.VMEM((KEYS_PAD, d), _BF16), pltpu.VMEM((KEYS_PAD, d), _BF16),
                        pltpu.SemaphoreType.DMA((2,))],
        compiler_params=pltpu.CompilerParams(
            dimension_semantics=("arbitrary",), vmem_limit_bytes=VMEM_LIMIT),
        name="attn_sample",
    )(q0, q1, q2, k_new, v_new, bias, ck, cv)


def _attn_out_kernel(*refs, n_groups, tm):
    if n_groups == 1:
        (o_ref, h_ref, wo_ref, gffn_ref, wr_ref, br_ref,
         h3_ref, xn2_ref, eid_ref, gate_ref) = refs
        o = o_ref[...].astype(_BF16)
    else:
        (o0_ref, o1_ref, o2_ref, s0_ref, s1_ref, s2_ref, h_ref, wo_ref, gffn_ref, wr_ref, br_ref,
         h3_ref, xn2_ref, eid_ref, gate_ref, obuf, scr_o1, scr_o2, scr_s1, scr_s2) = refs
        for o_ref, s_ref, scr_o, scr_s, dil in ((o1_ref, s1_ref, scr_o1, scr_s1, DILATIONS[1]),
                                                (o2_ref, s2_ref, scr_o2, scr_s2, DILATIONS[2])):
            for res in range(dil):
                dst = pl.ds(res, tm // dil, stride=dil)
                scr_s[dst, :] = s_ref[:, res * HEAD_DIM:(res + 1) * HEAD_DIM]
                for h in range(N_HEADS):
                    col = res * D_MODEL + h * HEAD_DIM
                    scr_o[h, dst, :] = o_ref[:, col:col + HEAD_DIM].astype(_F32)
        sts = [s0_ref[...], scr_s1[...], scr_s2[...]]
        mx = jnp.maximum(jnp.maximum(sts[0], sts[1]), sts[2])
        es = [jnp.exp(s - mx) for s in sts]
        den = es[0] + es[1] + es[2]
        ws = [e / den for e in es]
        for h in range(N_HEADS):
            acc = ws[0][:, h:h + 1] * o0_ref[:, _chunk(h)]
            acc = acc + ws[1][:, h:h + 1] * scr_o1[h]
            acc = acc + ws[2][:, h:h + 1] * scr_o2[h]
            obuf[:, _chunk(h)] = acc.astype(_BF16)
        o = obuf[...]
    h3 = h_ref[...] + _dot(o, wo_ref[...])
    h3_ref[...] = h3
    xn2 = _rms(h3, gffn_ref[...])
    xn2_ref[...] = xn2
    _route(xn2, wr_ref, br_ref, eid_ref, gate_ref)


def _attn_out(os, sts, h, wo, gffn, wr, br, *, tm):
    t, d = h.shape
    n_groups = len(os)
    row = lambda i: (i, 0)
    const = lambda i: (0, 0)
    if n_groups == 1:
        in_specs = [pl.BlockSpec((tm, d), row)]
        scratch = []
        args = [os[0]]
    else:
        in_specs = [pl.BlockSpec((tm // dil, dil * d), row) for dil in DILATIONS]
        in_specs += [pl.BlockSpec((tm // dil, dil * HEAD_DIM), row) for dil in DILATIONS]
        scratch = [pltpu.VMEM((tm, d), _BF16),
                   pltpu.VMEM((N_HEADS, tm, HEAD_DIM), _F32), pltpu.VMEM((N_HEADS, tm, HEAD_DIM), _F32),
                   pltpu.VMEM((tm, HEAD_DIM), _F32), pltpu.VMEM((tm, HEAD_DIM), _F32)]
        args = list(os) + list(sts)
    in_specs += [pl.BlockSpec((tm, d), row), pl.BlockSpec((d, d), const), pl.BlockSpec((1, d), const),
                 pl.BlockSpec((ROUTER_ROWS, d), const), pl.BlockSpec((ROUTER_ROWS, 1), const)]
    return pl.pallas_call(
        functools.partial(_attn_out_kernel, n_groups=n_groups, tm=tm),
        grid=(t // tm,),
        in_specs=in_specs,
        out_specs=[pl.BlockSpec((tm, d), row), pl.BlockSpec((tm, d), row),
                   pl.BlockSpec((2, tm), lambda i: (0, i)), pl.BlockSpec((2, tm), lambda i: (0, i))],
        out_shape=[jax.ShapeDtypeStruct((t, d), _F32), jax.ShapeDtypeStruct((t, d), _F32),
                   jax.ShapeDtypeStruct((2, t), jnp.int32), jax.ShapeDtypeStruct((2, t), _F32)],
        scratch_shapes=scratch,
        compiler_params=pltpu.CompilerParams(
            dimension_semantics=("arbitrary",), vmem_limit_bytes=VMEM_LIMIT),
        name=f"attn_out_g{n_groups}",
    )(*args, h, wo, gffn, wr, br)


def _final_kernel(h_ref, y0_ref, y1_ref, gate_ref, g_ref, out_ref):
    out_ref[...] = _rms(_combine(h_ref, y0_ref, y1_ref, gate_ref), g_ref[...])


def _final(h, ypair, gate, g, *, tm):
    t, d = h.shape
    nt = t // tm
    row = lambda i: (i, 0)
    return pl.pallas_call(
        _final_kernel,
        grid=(nt,),
        in_specs=[pl.BlockSpec((tm, d), row),
                  pl.BlockSpec((tm, d), row),
                  pl.BlockSpec((tm, d), lambda i: (nt + i, 0)),
                  pl.BlockSpec((tm, 2), row),
                  pl.BlockSpec((1, d), lambda i: (0, 0))],
        out_specs=pl.BlockSpec((tm, d), row),
        out_shape=jax.ShapeDtypeStruct((t, d), _F32),
        compiler_params=pltpu.CompilerParams(
            dimension_semantics=("arbitrary",), vmem_limit_bytes=VMEM_LIMIT),
        name="final_norm",
    )(h, ypair, ypair, gate, g)


def _rope_tables(pos):
    half = HEAD_DIM // 2
    inv_freq = jnp.power(jnp.float32(ROPE_THETA), -jnp.arange(half, dtype=jnp.float32) / half)
    ang = pos.astype(jnp.float32)[:, None] * inv_freq[None, :]
    cos = jnp.cos(ang)
    sin = jnp.sin(ang)
    return jnp.concatenate([cos, cos], axis=-1), jnp.concatenate([-sin, sin], axis=-1)


def _router_params(wg, bg, we, be):
    wr = jnp.zeros((ROUTER_ROWS, D_MODEL), _F32)
    wr = wr.at[0:MOE_GROUPS].set(wg.T).at[EXPERT_ROW0:EXPERT_ROW0 + N_EXPERTS].set(we.T)
    br = jnp.zeros((ROUTER_ROWS, 1), _F32)
    br = br.at[0:MOE_GROUPS, 0].set(bg).at[EXPERT_ROW0:EXPERT_ROW0 + N_EXPERTS, 0].set(be)
    return wr, br


def _moe(xn2, eid, experts, layer):
    block_expert, source, slot = _moe_plan(eid, xn2.shape[0])
    xs = _row_gather(xn2, source)
    yield
    ys = _moe_ffn(block_expert, xs, *experts, layer)
    ypair = _row_gather(ys, slot)
    yield
    return ypair


def _forward(x3, prev3, pos_rows, attend, p, *, shift, tm, streams, keep):
    h1, xn2, eid, gate, state = _conv_layer(
        x3, prev3, p['gmix'][0], p['win'], p['ck'], p['wout'], p['gffn'][0], *p['router'][0],
        shift=shift, tm=tm)
    t = x3.shape[0] * x3.shape[1]
    ypair = yield from _moe(xn2, eid, p['experts'], 0)
    cos, sin = _rope_tables(pos_rows)
    h2, *qkv, kf, vf = _qkv(h1.reshape(t, D_MODEL), ypair, gate.T, p['gmix'][1], p['gkv'],
                            p['wq'], p['wkv'], cos, sin, tm=QKV_TILE, streams=streams,
                            seq=x3.shape[1], keep=keep)
    os, sts = attend(qkv, kf, vf)
    h3, xn2, eid, gate = _attn_out(os, sts, h2, p['wo'], p['gffn'][1], *p['router'][1],
                                   tm=min(ATTN_OUT_TILE, t))
    ypair = yield from _moe(xn2, eid, p['experts'], 1)
    y = _final(h3, ypair, gate.T, p['gfinal'], tm=FINAL_TILE)
    return y, state, kf, vf


def _interleave(*paths):
    results = [None] * len(paths)
    live = list(enumerate(paths))
    while live:
        still = []
        for i, gen in live:
            try:
                next(gen)
                still.append((i, gen))
            except StopIteration as done:
                results[i] = done.value
        live = still
    return results


def kernel(x_prompt, x_sample, cache_k, cache_v, state_conv, norm_mix, norm_ffn, norm_kv, norm_final,
           conv_w_in, conv_kernel, conv_w_out, attn_w_q, attn_w_kv, attn_w_o, router_group_w,
           router_group_b, router_expert_w, router_expert_b, expert_w1, expert_w3, expert_w2):
    b_p, s_p, d = x_prompt.shape
    b_s, n_new, _ = x_sample.shape
    assert d == D_MODEL and n_new == N_NEW and s_p % (DILATIONS[2] * SPAN) == 0
    assert norm_mix.shape[0] == 2 and conv_w_in.shape[0] == 1 and attn_w_q.shape[0] == 1
    assert cache_k.shape[1] == PAST_LEN

    p = {
        'gmix': [norm_mix[l].reshape(1, d) for l in range(2)],
        'gffn': [norm_ffn[l].reshape(1, d) for l in range(2)],
        'gkv': norm_kv.reshape(1, d),
        'gfinal': norm_final.reshape(1, d),
        'win': conv_w_in[0].astype(_BF16),
        'ck': conv_kernel[0],
        'wout': conv_w_out[0].astype(_BF16),
        'wq': attn_w_q[0].astype(_BF16),
        'wkv': attn_w_kv.astype(_BF16),
        'wo': attn_w_o[0].astype(_BF16),
        'router': [_router_params(router_group_w[l], router_group_b[l], router_expert_w[l], router_expert_b[l])
                   for l in range(2)],
        'experts': (expert_w1, expert_w3, expert_w2),
    }

    def attend_prompt(qkv, kf, vf):
        q0, q1, q2, k0, k1, k2, v0, v1, v2 = qkv
        os, sts = [], []
        for q, k, v, dil in ((q0, k0, v0, DILATIONS[0]), (q1, k1, v1, DILATIONS[1]), (q2, k2, v2, DILATIONS[2])):
            o, st = _attn_prompt(q, k, v, b_p, dil)
            os.append(o)
            sts.append(st)
        return os, sts

    pos_p = jnp.tile(jnp.arange(s_p, dtype=jnp.int32), b_p)
    zero_state = jnp.zeros((b_p, CONV_WIDTH - 1, d), x_prompt.dtype)
    keep = min(max(WINDOWS), s_p)
    prompt_path = _forward(x_prompt, zero_state, pos_p, attend_prompt, p, shift=1, tm=CONV_TILE,
                           streams=True, keep=keep)

    halves = 2
    bh = b_s // halves

    def to_rows(a):
        w = a.shape[-1]
        return a.reshape(halves, bh, n_new, w).transpose(0, 2, 1, 3).reshape(halves * n_new * bh, w)

    def to_batch(a):
        w = a.shape[1:]
        return jnp.swapaxes(a.reshape(halves, n_new, bh, *w), 1, 2).reshape(b_s, n_new, *w)

    def attend_sample(qkv, kf, vf):
        q0, q1, q2 = qkv
        o = _attn_sample(to_batch(q0), to_batch(q1), to_batch(q2), cache_k, cache_v,
                         to_batch(kf), to_batch(vf))
        return [to_rows(o)], None

    x_s = to_rows(x_sample).reshape(halves, n_new * bh, d)
    prev_s = state_conv[0].reshape(halves, bh, CONV_WIDTH - 1, d).transpose(0, 2, 1, 3).reshape(
        halves, (CONV_WIDTH - 1) * bh, d)
    pos_s = jnp.tile(jnp.repeat(PAST_LEN + jnp.arange(n_new, dtype=jnp.int32), bh), halves)
    sample_path = _forward(x_s, prev_s, pos_s, attend_sample, p, shift=bh, tm=n_new * bh,
                           streams=False, keep=n_new * bh)

    (y_p, st_p, kf_p, vf_p), (y_s, st_s, kf_s, vf_s) = _interleave(prompt_path, sample_path)
    y_prompt = y_p.reshape(b_p, s_p, d)
    k_p = kf_p.reshape(b_p, keep, N_HEADS, HEAD_DIM)
    v_p = vf_p.reshape(b_p, keep, N_HEADS, HEAD_DIM)
    conv_p = st_p[None]
    y_sample = to_batch(y_s)
    k_s = to_batch(kf_s)
    v_s = to_batch(vf_s)
    conv_s = st_s.reshape(halves, CONV_WIDTH - 1, bh, d).transpose(0, 2, 1, 3).reshape(
        b_s, CONV_WIDTH - 1, d)[None]

    return (y_prompt, y_sample, k_p, v_p, conv_p, k_s, v_s, conv_s)
```
